```python
import math
import jax
import jax.numpy as jnp
from jax import lax
import numpy as np


D_MODEL = 1024
BATCH = 8
SEQ = 2048
DEPTH = 1
DEC_BATCH = 32
DEC_SEQ = 1
PAST_LEN = 16384
PAGE_SIZE = 128

N_META = 16
EPS = 1e-6
GDN_HEADS = 4
GDN_DK = 128
GDN_DV = 128
GDN_CHUNK = 64
CONV_W = 4
GDN_QK_W = GDN_HEADS * GDN_DK
GDN_V_W = GDN_HEADS * GDN_DV
CONV_CH = 2 * GDN_QK_W + GDN_V_W
DA_HEADS = 4
DA_DQK = 64
DA_DV = 2 * DA_DQK
DA_QK_W = DA_HEADS * 2 * DA_DQK
DA_V_W = DA_HEADS * DA_DV
ROPE_DIM = DA_DQK // 4
ROPE_THETA = 500000.0
Q_BLOCK = 128
MIX_W = GDN_V_W + DA_V_W
PROJ_W = CONV_CH + GDN_V_W + 2 * GDN_HEADS + 2 * DA_QK_W + DA_V_W
N_GROUPS = 4
EXPERTS_PER_GROUP = 8
N_EXPERTS = N_GROUPS * EXPERTS_PER_GROUP
TOP_K = 2
D_EXPERT = 256

kernel_name = 'hymba_gdn_diffattn_hmoe_step'


def rmsnorm(x, w):
    xf = x.astype(jnp.float32)
    y = xf * lax.rsqrt(jnp.mean(xf * xf, axis=-1, keepdims=True) + EPS)
    return (y * w.astype(jnp.float32)).astype(x.dtype)


def l2norm(x):
    xf = x.astype(jnp.float32)
    return xf * lax.rsqrt(jnp.sum(xf * xf, axis=-1, keepdims=True) + EPS)


def partial_rope(x, pos):
    half = ROPE_DIM // 2
    inv_freq = ROPE_THETA ** (-jnp.arange(half, dtype=jnp.float32) * 2.0 / ROPE_DIM)
    ang = pos.astype(jnp.float32)[:, None] * inv_freq[None, :]
    cos = jnp.cos(ang)[None, :, None, None, :]
    sin = jnp.sin(ang)[None, :, None, None, :]
    xr = x[..., :ROPE_DIM].astype(jnp.float32)
    x1, x2 = xr[..., :half], xr[..., half:]
    rot = jnp.concatenate([x1 * cos - x2 * sin, x2 * cos + x1 * sin], axis=-1)
    return jnp.concatenate([rot.astype(x.dtype), x[..., ROPE_DIM:]], axis=-1)


def split_proj(p):
    sizes = [CONV_CH, GDN_V_W, GDN_HEADS, GDN_HEADS, DA_QK_W, DA_QK_W, DA_V_W]
    offs = np.cumsum(sizes)[:-1].tolist()
    return jnp.split(p, offs, axis=-1)


def gdn_chunked(q, k, v, g, beta, s0):
    bsz, t = q.shape[0], q.shape[1]
    n = t // GDN_CHUNK

    def to_chunks(a):
        return jnp.moveaxis(a.reshape((bsz, n, GDN_CHUNK) + a.shape[2:]), 1, 0)

    xs = tuple(to_chunks(a) for a in (q, k, v, g, beta))
    idx = jnp.arange(GDN_CHUNK)
    strict = idx[:, None] > idx[None, :]
    causal = idx[:, None] >= idx[None, :]
    eye = jnp.eye(GDN_CHUNK, dtype=jnp.float32)

    def step(S, inp):
        qi, ki, vi, gi, bi = inp
        G = jnp.cumsum(gi, axis=1)
        Gh = jnp.swapaxes(G, 1, 2)
        decay = jnp.exp(jnp.where(causal, Gh[..., :, None] - Gh[..., None, :], -jnp.inf))
        bh = jnp.swapaxes(bi, 1, 2)
        kk = jnp.einsum('bihd,bjhd->bhij', ki, ki)
        A = eye + jnp.where(strict, kk * decay * bh[..., :, None], 0.0)
        vb = jnp.swapaxes(vi * bi[..., None], 1, 2)
        kb = jnp.swapaxes(ki * (bi * jnp.exp(G))[..., None], 1, 2)
        sol = lax.linalg.triangular_solve(A, jnp.concatenate([vb, kb], axis=-1),
                                          left_side=True, lower=True, unit_diagonal=True)
        u = sol[..., :GDN_DV] - jnp.einsum('bhik,bhkv->bhiv', sol[..., GDN_DV:], S)
        qk = jnp.einsum('bihd,bjhd->bhij', qi, ki) * decay
        o = (jnp.einsum('bihd,bhdv->bhiv', qi * jnp.exp(G)[..., None], S)
             + jnp.einsum('bhij,bhjv->bhiv', qk, u))
        g_last = Gh[..., -1]
        kw = jnp.swapaxes(ki, 1, 2) * jnp.exp(g_last[..., None] - Gh)[..., None]
        S_new = S * jnp.exp(g_last)[..., None, None] + jnp.einsum('bhjd,bhjv->bhdv', kw, u)
        return S_new, jnp.swapaxes(o, 1, 2)

    S, o = lax.scan(step, s0, xs)
    o = jnp.moveaxis(o, 0, 1).reshape(bsz, t, GDN_HEADS, GDN_DV)
    return o, S


def gdn_branch(xc, z, a, b, conv_prev, s0, pad_front, pad_back, w_conv, a_log, dt_bias, w_gnorm):
    bsz, t = xc.shape[0], xc.shape[1]
    ext = jnp.concatenate([conv_prev.astype(xc.dtype), xc], axis=1)
    conv = jax.nn.silu(sum(ext[:, i:i + t] * w_conv[i] for i in range(CONV_W)))
    q, k, v = jnp.split(conv, [GDN_QK_W, 2 * GDN_QK_W], axis=-1)
    q = l2norm(q.reshape(bsz, t, GDN_HEADS, GDN_DK)) * (GDN_DK ** -0.5)
    k = l2norm(k.reshape(bsz, t, GDN_HEADS, GDN_DK))
    v = v.reshape(bsz, t, GDN_HEADS, GDN_DV).astype(jnp.float32)
    g = -jnp.exp(a_log.astype(jnp.float32)) * jax.nn.softplus(a.astype(jnp.float32) + dt_bias.astype(jnp.float32))
    beta = jax.nn.sigmoid(b.astype(jnp.float32))

    def pad(arr):
        return jnp.pad(arr, ((0, 0), (pad_front, pad_back)) + ((0, 0),) * (arr.ndim - 2))

    o, s_new = gdn_chunked(pad(q), pad(k), pad(v), pad(g), pad(beta), s0.astype(jnp.float32))
    o = o[:, pad_front:pad_front + t]
    o = rmsnorm(o, w_gnorm) * jax.nn.silu(z.reshape(bsz, t, GDN_HEADS, GDN_DV).astype(jnp.float32))
    return o.reshape(bsz, t, GDN_V_W).astype(xc.dtype), ext[:, -(CONV_W - 1):], s_new.astype(s0.dtype)


def diff_attn_prompt(q, k, v, lam):
    bsz, t = q.shape[0], q.shape[1]
    nb = -(-t // Q_BLOCK)
    tp = nb * Q_BLOCK
    qp = jnp.pad(q, ((0, 0), (0, tp - t), (0, 0), (0, 0), (0, 0)))
    qb = jnp.moveaxis(qp.reshape(bsz, nb, Q_BLOCK, DA_HEADS, 2, DA_DQK), 1, 0)
    kpos = jnp.arange(t)
    scale = DA_DQK ** -0.5

    def block(args):
        qi, start = args
        s = jnp.einsum('bqhcd,bkhcd->bhcqk', qi, k).astype(jnp.float32) * scale
        qpos = start + jnp.arange(Q_BLOCK)
        s = jnp.where(kpos[None, :] <= qpos[:, None], s, -jnp.inf)
        p = jax.nn.softmax(s, axis=-1)
        w = p[:, :, 0] - lam * p[:, :, 1]
        return jnp.einsum('bhqk,bkhv->bqhv', w.astype(v.dtype), v)

    o = lax.map(block, (qb, jnp.arange(nb) * Q_BLOCK))
    o = jnp.moveaxis(o, 0, 1).reshape(bsz, tp, DA_HEADS, DA_DV)
    return o[:, :t]


def diff_attn_sample(q, k_new, v_new, k_past, v_past, lam):
    s_len, p_len = q.shape[1], k_past.shape[1]
    scale = DA_DQK ** -0.5
    s_past = jnp.einsum('bqhcd,bkhcd->bhcqk', q, k_past).astype(jnp.float32) * scale
    s_new = jnp.einsum('bqhcd,bkhcd->bhcqk', q, k_new).astype(jnp.float32) * scale
    causal = jnp.arange(s_len)[None, :] <= jnp.arange(s_len)[:, None]
    s_new = jnp.where(causal, s_new, -jnp.inf)
    p = jax.nn.softmax(jnp.concatenate([s_past, s_new], axis=-1), axis=-1)
    w = (p[:, :, 0] - lam * p[:, :, 1]).astype(v_new.dtype)
    return (jnp.einsum('bhqk,bkhv->bqhv', w[..., :p_len], v_past)
            + jnp.einsum('bhqk,bkhv->bqhv', w[..., p_len:], v_new))


def hier_moe(x, w_group, w_expert, w_gate, w_up, w_down):
    shp = x.shape
    xf = x.reshape(-1, D_MODEL)
    n = xf.shape[0]
    g_logits = (xf @ w_group).astype(jnp.float32)
    g_prob = jax.nn.softmax(g_logits, axis=-1)
    g_sel = jnp.argmax(g_logits, axis=-1)
    g_w = jnp.max(g_prob, axis=-1)
    e_logits = (xf @ w_expert).astype(jnp.float32).reshape(n, N_GROUPS, EXPERTS_PER_GROUP)
    e_sel = e_logits[jnp.arange(n), g_sel]
    top_v, top_i = lax.top_k(e_sel, TOP_K)
    top_w = jax.nn.softmax(top_v, axis=-1) * g_w[:, None]
    eid = g_sel[:, None] * EXPERTS_PER_GROUP + top_i
    combine = jnp.sum(jax.nn.one_hot(eid, N_EXPERTS, dtype=jnp.float32) * top_w[..., None], axis=1)
    hg = jnp.einsum('nd,edf->nef', xf, w_gate)
    hu = jnp.einsum('nd,edf->nef', xf, w_up)
    hh = jax.nn.silu(hg) * hu * combine[:, :, None].astype(xf.dtype)
    return jnp.einsum('nef,efd->nd', hh, w_down).reshape(shp)


def setup_inputs(seed: int = 0) -> dict:
    key = jax.random.key(seed)
    ks = jax.random.split(key, 28)
    f32 = jnp.float32
    n_pages = PAST_LEN // PAGE_SIZE
    n_used = DEC_BATCH * n_pages
    n_pool = n_used + max(1, n_used // 4)

    def nrm(k, shape, scale):
        return jax.random.normal(k, shape, f32) * scale

    page_table = jax.random.permutation(ks[0], n_pool)[:n_used].reshape(DEC_BATCH, n_pages).astype(jnp.int32)
    dt = jnp.exp(jax.random.uniform(ks[1], (DEPTH, GDN_HEADS), f32, math.log(1e-3), math.log(1e-1)))
    return dict(
        x_prompt=nrm(ks[2], (BATCH, SEQ, D_MODEL), 1.0),
        x_sample=nrm(ks[3], (DEC_BATCH, DEC_SEQ, D_MODEL), 1.0),
        cache_k=nrm(ks[4], (DEPTH, n_pool, PAGE_SIZE, DA_HEADS, 2 * DA_DQK), 1.0),
        cache_v=nrm(ks[5], (DEPTH, n_pool, PAGE_SIZE, DA_HEADS, DA_DV), 1.0),
        state_conv=nrm(ks[6], (DEPTH, DEC_BATCH, CONV_W - 1, CONV_CH), 1.0),
        state_gdn=nrm(ks[7], (DEPTH, DEC_BATCH, GDN_HEADS, GDN_DK, GDN_DV), 0.1),
        page_table=page_table,
        meta_tokens=nrm(ks[8], (N_META, D_MODEL), 1.0),
        w_norm_mix=1.0 + nrm(ks[9], (DEPTH, D_MODEL), 0.02),
        w_in=nrm(ks[10], (DEPTH, D_MODEL, PROJ_W), D_MODEL ** -0.5),
        w_conv=nrm(ks[11], (DEPTH, CONV_W, CONV_CH), CONV_W ** -0.5),
        a_log=jnp.log(jax.random.uniform(ks[12], (DEPTH, GDN_HEADS), f32, 1.0, 16.0)),
        dt_bias=dt + jnp.log(-jnp.expm1(-dt)),
        w_gnorm=1.0 + nrm(ks[13], (DEPTH, GDN_DV), 0.02),
        lambda_q1=nrm(ks[14], (DEPTH, DA_DQK), 0.1),
        lambda_k1=nrm(ks[15], (DEPTH, DA_DQK), 0.1),
        lambda_q2=nrm(ks[16], (DEPTH, DA_DQK), 0.1),
        lambda_k2=nrm(ks[17], (DEPTH, DA_DQK), 0.1),
        w_subln=1.0 + nrm(ks[18], (DEPTH, DA_DV), 0.02),
        w_out=nrm(ks[19], (DEPTH, MIX_W, D_MODEL), MIX_W ** -0.5),
        w_norm_ffn=1.0 + nrm(ks[20], (DEPTH, D_MODEL), 0.02),
        w_group=nrm(ks[21], (DEPTH, D_MODEL, N_GROUPS), D_MODEL ** -0.5),
        w_expert=nrm(ks[22], (DEPTH, D_MODEL, N_EXPERTS), D_MODEL ** -0.5),
        w_gate=nrm(ks[23], (DEPTH, N_EXPERTS, D_MODEL, D_EXPERT), D_MODEL ** -0.5),
        w_up=nrm(ks[24], (DEPTH, N_EXPERTS, D_MODEL, D_EXPERT), D_MODEL ** -0.5),
        w_down=nrm(ks[25], (DEPTH, N_EXPERTS, D_EXPERT, D_MODEL), D_EXPERT ** -0.5),
        w_final=1.0 + nrm(ks[26], (D_MODEL,), 0.02),
    )


def reference(x_prompt, x_sample, cache_k, cache_v, state_conv, state_gdn, page_table, meta_tokens,
              w_norm_mix, w_in, w_conv, a_log, dt_bias, w_gnorm, lambda_q1, lambda_k1, lambda_q2, lambda_k2,
              w_subln, w_out, w_norm_ffn, w_group, w_expert, w_gate, w_up, w_down, w_final):
    bp, lp = x_prompt.shape[0], x_prompt.shape[1] + N_META
    bs, ls = x_sample.shape[0], x_sample.shape[1]
    past = page_table.shape[1] * cache_k.shape[2]
    meta = jnp.broadcast_to(meta_tokens[None].astype(x_prompt.dtype), (bp, N_META, D_MODEL))
    xp = jnp.concatenate([meta, x_prompt], axis=1)
    xs = x_sample
    pos_p = jnp.arange(lp)
    pos_s = past + jnp.arange(ls)
    pad_front = (-N_META) % GDN_CHUNK
    pad_back = (-ls) % GDN_CHUNK
    p_conv, p_gdn, p_k, p_v = [], [], [], []
    s_conv, s_gdn, s_k, s_v = [], [], [], []
    for l in range(DEPTH):
        lam_init = 0.8 - 0.6 * math.exp(-0.3 * l)
        lam = (jnp.exp(jnp.sum(lambda_q1[l].astype(jnp.float32) * lambda_k1[l].astype(jnp.float32)))
               - jnp.exp(jnp.sum(lambda_q2[l].astype(jnp.float32) * lambda_k2[l].astype(jnp.float32)))
               + lam_init)
        c, z, a, b, dq, dk, dv = split_proj(rmsnorm(xp, w_norm_mix[l]) @ w_in[l])
        g_out, conv_new, gdn_new = gdn_branch(
            c, z, a, b, jnp.zeros((bp, CONV_W - 1, CONV_CH), xp.dtype),
            jnp.zeros((bp, GDN_HEADS, GDN_DK, GDN_DV), state_gdn.dtype), pad_front, 0,
            w_conv[l], a_log[l], dt_bias[l], w_gnorm[l])
        q = partial_rope(dq.reshape(bp, lp, DA_HEADS, 2, DA_DQK), pos_p)
        k = partial_rope(dk.reshape(bp, lp, DA_HEADS, 2, DA_DQK), pos_p)
        v = dv.reshape(bp, lp, DA_HEADS, DA_DV)
        a_out = diff_attn_prompt(q, k, v, lam)
        a_out = (rmsnorm(a_out, w_subln[l]) * (1.0 - lam_init)).reshape(bp, lp, DA_V_W)
        xp = xp + jnp.concatenate([g_out, a_out], axis=-1) @ w_out[l]
        xp = xp + hier_moe(rmsnorm(xp, w_norm_ffn[l]), w_group[l], w_expert[l], w_gate[l], w_up[l], w_down[l])
        p_conv.append(conv_new)
        p_gdn.append(gdn_new)
        p_k.append(k.reshape(bp, lp, DA_HEADS, 2 * DA_DQK))
        p_v.append(v)
        c, z, a, b, dq, dk, dv = split_proj(rmsnorm(xs, w_norm_mix[l]) @ w_in[l])
        g_out, conv_new, gdn_new = gdn_branch(
            c, z, a, b, state_conv[l], state_gdn[l], 0, pad_back,
            w_conv[l], a_log[l], dt_bias[l], w_gnorm[l])
        q = partial_rope(dq.reshape(bs, ls, DA_HEADS, 2, DA_DQK), pos_s)
        k = partial_rope(dk.reshape(bs, ls, DA_HEADS, 2, DA_DQK), pos_s)
        v = dv.reshape(bs, ls, DA_HEADS, DA_DV)
        k_past = cache_k[l, page_table].reshape(bs, past, DA_HEADS, 2, DA_DQK)
        v_past = cache_v[l, page_table].reshape(bs, past, DA_HEADS, DA_DV)
        a_out = diff_attn_sample(q, k, v, k_past.astype(q.dtype), v_past.astype(v.dtype), lam)
        a_out = (rmsnorm(a_out, w_subln[l]) * (1.0 - lam_init)).reshape(bs, ls, DA_V_W)
        xs = xs + jnp.concatenate([g_out, a_out], axis=-1) @ w_out[l]
        xs = xs + hier_moe(rmsnorm(xs, w_norm_ffn[l]), w_group[l], w_expert[l], w_gate[l], w_up[l], w_down[l])
        s_conv.append(conv_new)
        s_gdn.append(gdn_new)
        s_k.append(k.reshape(bs, ls, DA_HEADS, 2 * DA_DQK))
        s_v.append(v)
    y_prompt = rmsnorm(xp, w_final)[:, N_META:]
    y_sample = rmsnorm(xs, w_final)
    return (y_prompt, y_sample, jnp.stack(p_conv), jnp.stack(p_gdn), jnp.stack(p_k), jnp.stack(p_v),
            jnp.stack(s_conv), jnp.stack(s_gdn), jnp.stack(s_k), jnp.stack(s_v))
```

```python
import functools
import math

import jax
import jax.numpy as jnp
import numpy as np
from jax import lax
from jax.experimental import pallas as pl
from jax.experimental.pallas import tpu as pltpu

F32 = jnp.float32
BF16 = jnp.bfloat16

D_MODEL = 1024
N_META = 16
EPS = 1e-6
GDN_HEADS = 4
GDN_DK = 128
GDN_DV = 128
CONV_W = 4
GDN_QK_W = GDN_HEADS * GDN_DK
GDN_V_W = GDN_HEADS * GDN_DV
CONV_CH = 2 * GDN_QK_W + GDN_V_W
DA_HEADS = 4
DA_DQK = 64
DA_DV = 2 * DA_DQK
DA_QK_W = DA_HEADS * 2 * DA_DQK
DA_V_W = DA_HEADS * DA_DV
ROPE_DIM = DA_DQK // 4
ROPE_HALF = ROPE_DIM // 2
ROPE_THETA = 500000.0
N_GROUPS = 4
EXPERTS_PER_GROUP = 8
N_EXPERTS = N_GROUPS * EXPERTS_PER_GROUP
D_EXPERT = 256
LAM_INIT = 0.8 - 0.6 * math.exp(-0.3 * 0)

LANES = 128
GDN_CHUNK = 64
AB_W = LANES
PROJ_PAD_W = CONV_CH + GDN_V_W + 2 * DA_QK_W + DA_V_W + AB_W
VMEM_LIMIT = 56 * 1024 * 1024


_NN = (((1,), (0,)), ((), ()))
_NT = (((1,), (1,)), ((), ()))
_TN = (((0,), (0,)), ((), ()))


def _mm(a, b, dims=_NN, precise=False):
    if precise:
        return lax.dot_general(a.astype(F32), b.astype(F32), dims, preferred_element_type=F32,
                               precision=lax.Precision.HIGHEST)
    return lax.dot_general(a.astype(BF16), b.astype(BF16), dims, preferred_element_type=F32)


def _dot_nt(a, b):
    return _mm(a, b, _NT)


def _dot_f32(a, b):
    return _mm(a, b, _NN, True)


def _dot_nt_f32(a, b):
    return _mm(a, b, _NT, True)


def _sigmoid(x):
    return 1.0 / (1.0 + jnp.exp(-x))


def _softplus(x):
    return jnp.maximum(x, 0.0) + jnp.log(1.0 + jnp.exp(-jnp.abs(x)))


def _rms(x, w):
    return x * lax.rsqrt(jnp.mean(x * x, axis=-1, keepdims=True) + EPS) * w


def _rope(x, cs, sn_lo, sn_hi):
    w = x.shape[1]
    cs = jnp.concatenate([cs] * DA_HEADS, axis=1)
    sn_lo = jnp.concatenate([sn_lo] * DA_HEADS, axis=1)
    sn_hi = jnp.concatenate([sn_hi] * DA_HEADS, axis=1)
    up = pltpu.roll(x, w - ROPE_HALF, 1)
    dn = pltpu.roll(x, ROPE_HALF, 1)
    return x * cs + up * sn_lo + dn * sn_hi


def _proj_kernel(x_ref, wn_ref, w_ref, cs_ref, snl_ref, snh_ref,
                 c_ref, z_ref, ab_ref, q_ref, k_ref, v_ref, *, precise):
    h = _rms(x_ref[...], wn_ref[...])
    h = h if precise else h.astype(BF16)
    o = 0
    c_ref[...] = _mm(h, w_ref[:, o:o + CONV_CH], _NN, precise)
    o += CONV_CH
    z_ref[...] = _mm(h, w_ref[:, o:o + GDN_V_W], _NN, precise)
    o += GDN_V_W
    q = _mm(h, w_ref[:, o:o + DA_QK_W], _NN, precise)
    o += DA_QK_W
    k = _mm(h, w_ref[:, o:o + DA_QK_W], _NN, precise)
    o += DA_QK_W
    v_ref[...] = _mm(h, w_ref[:, o:o + DA_V_W], _NN, precise)
    o += DA_V_W
    ab_ref[...] = _mm(h, w_ref[:, o:o + AB_W], _NN, precise)
    cs, snl, snh = cs_ref[...], snl_ref[...], snh_ref[...]
    q_ref[...] = _rope(q, cs, snl, snh)
    k_ref[...] = _rope(k, cs, snl, snh)


def _proj(x, wn, w, tabs, tm):
    n = x.shape[0]
    assert n % tm == 0
    nt = tabs[0].shape[0] // tm
    row = lambda i: (i, 0)
    fixed = lambda i: (0, 0)
    tab = lambda i: (i % nt, 0)
    widths = (CONV_CH, GDN_V_W, AB_W, DA_QK_W, DA_QK_W, DA_V_W)
    return pl.pallas_call(
        functools.partial(_proj_kernel, precise=w.dtype == F32),
        grid=(n // tm,),
        in_specs=[pl.BlockSpec((tm, D_MODEL), row),
                  pl.BlockSpec((1, D_MODEL), fixed),
                  pl.BlockSpec((D_MODEL, PROJ_PAD_W), fixed),
                  pl.BlockSpec((tm, LANES), tab),
                  pl.BlockSpec((tm, LANES), tab),
                  pl.BlockSpec((tm, LANES), tab)],
        out_specs=[pl.BlockSpec((tm, wd), row) for wd in widths],
        out_shape=[jax.ShapeDtypeStruct((n, wd), F32) for wd in widths],
        compiler_params=pltpu.CompilerParams(dimension_semantics=("parallel",),
                                             vmem_limit_bytes=VMEM_LIMIT),
        name="proj",
    )(x, wn, w, *tabs)


def _rope_tables(pos):
    inv_freq = ROPE_THETA ** (-jnp.arange(ROPE_HALF, dtype=F32) * 2.0 / ROPE_DIM)
    ang = pos.astype(F32)[:, None] * inv_freq[None, :]
    cos, sin = jnp.cos(ang), jnp.sin(ang)
    t = pos.shape[0]
    one = jnp.ones((t, DA_DQK - ROPE_DIM), F32)
    zero = jnp.zeros((t, DA_DQK - ROPE_DIM), F32)
    zh = jnp.zeros((t, ROPE_HALF), F32)
    cs = jnp.concatenate([cos, cos, one] * 2, axis=1)
    snl = jnp.concatenate([-sin, zh, zero] * 2, axis=1)
    snh = jnp.concatenate([zh, sin, zero] * 2, axis=1)
    return cs, snl, snh


def _prep_w_in(w_in):
    c, z, a, b, dq, dk, dv = jnp.split(
        w_in, np.cumsum([CONV_CH, GDN_V_W, GDN_HEADS, GDN_HEADS, DA_QK_W, DA_QK_W]).tolist(), axis=-1)
    pad = jnp.zeros((D_MODEL, AB_W - 2 * GDN_HEADS), w_in.dtype)
    return jnp.concatenate([c, z, dq, dk, dv, a, b, pad], axis=-1)


def _gdn_kernel(c_ref, z_ref, ab_ref, tail_ref, s0_ref, wconv_ref, alog_ref, dtb_ref, wg_ref,
                o_ref, sfin_ref, cbuf, ybuf, s_scr, *, tt, valid_rows):
    C = GDN_CHUNK
    t = pl.program_id(1)
    mm = functools.partial(_mm, precise=valid_rows is not None)

    @pl.when(t == 0)
    def _():
        cbuf[0:8, :] = tail_ref[0]
        s_scr[...] = s0_ref[0]

    cbuf[8:8 + tt, :] = c_ref[0]
    acc = cbuf[8:8 + tt, :] * wconv_ref[CONV_W - 1:CONV_W, :]
    for i in range(CONV_W - 1):
        off = 8 - (CONV_W - 1) + i
        acc = acc + cbuf[off:off + tt, :] * wconv_ref[i:i + 1, :]
    ybuf[...] = acc * _sigmoid(acc)
    cbuf[0:8, :] = cbuf[tt:tt + 8, :]

    ii = lax.broadcasted_iota(jnp.int32, (C, C), 0)
    jj = lax.broadcasted_iota(jnp.int32, (C, C), 1)
    tri = (ii >= jj).astype(F32)
    strict = ii > jj
    causal = ii >= jj
    eye = (ii == jj).astype(F32)
    lane = lax.broadcasted_iota(jnp.int32, (C, LANES), 1)
    rowi = lax.broadcasted_iota(jnp.int32, (C, LANES), 0)
    merge_masks = []
    s = 1
    while s < C:
        sh = s.bit_length() - 1
        merge_masks.append((jnp.right_shift(ii, sh + 1) == jnp.right_shift(jj, sh + 1))
                           & (jnp.right_shift(ii, sh) != jnp.right_shift(jj, sh)) & strict)
        s *= 2

    def chunk(ci, carry):
        r0 = pl.multiple_of(ci * C, C)
        rows = pl.ds(r0, C)
        ab = ab_ref[0, rows, :]
        gfull = -jnp.exp(alog_ref[...]) * _softplus(ab + dtb_ref[...])
        bfull = _sigmoid(ab)
        if valid_rows is not None:
            live = rowi < valid_rows
            gfull = jnp.where(live, gfull, 0.0)
            bfull = jnp.where(live, bfull, 0.0)
        gcum = _dot_f32(tri, gfull)
        glast = gcum[C - 1:C, :]
        e_g = jnp.exp(gcum)
        e_rest = jnp.exp(glast - gcum)
        e_last = jnp.exp(glast)
        for h in range(GDN_HEADS):
            q = ybuf[rows, h * GDN_DK:(h + 1) * GDN_DK]
            k = ybuf[rows, GDN_QK_W + h * GDN_DK:GDN_QK_W + (h + 1) * GDN_DK]
            v = ybuf[rows, 2 * GDN_QK_W + h * GDN_DV:2 * GDN_QK_W + (h + 1) * GDN_DV]
            q = q * lax.rsqrt(jnp.sum(q * q, axis=-1, keepdims=True) + EPS) * (GDN_DK ** -0.5)
            k = k * lax.rsqrt(jnp.sum(k * k, axis=-1, keepdims=True) + EPS)
            if valid_rows is not None:
                live_c = rowi[:, :1] < valid_rows
                q = jnp.where(live_c, q, 0.0)
                k = jnp.where(live_c, k, 0.0)
                v = jnp.where(live_c, v, 0.0)
            g_col = gcum[:, h:h + 1]
            beta = bfull[:, GDN_HEADS + h:GDN_HEADS + h + 1]
            sel = (lane == h).astype(F32)
            g_row = _dot_nt_f32(sel, gcum)
            decay = jnp.exp(jnp.where(causal, g_col - g_row, -jnp.inf))
            kk = mm(k, k, _NT)
            low = jnp.where(strict, kk * decay * beta, 0.0)
            tinv = eye - jnp.where(merge_masks[0], low, 0.0)
            for m in merge_masks[1:]:
                off = jnp.where(m, low, 0.0)
                tinv = tinv - mm(mm(tinv, off), tinv)
            st = s_scr[h]
            eg = e_g[:, h:h + 1]
            rhs = beta * (v - eg * mm(k, st))
            u = mm(tinv, rhs)
            qk = jnp.where(causal, mm(q, k, _NT) * decay, 0.0)
            o = mm(q * eg, st) + mm(qk, u)
            s_scr[h] = st * e_last[:, h:h + 1] + mm(k * e_rest[:, h:h + 1], u, _TN)
            zz = z_ref[0, rows, h * GDN_DV:(h + 1) * GDN_DV]
            o_ref[0, rows, h * GDN_DV:(h + 1) * GDN_DV] = _rms(o, wg_ref[...]) * (zz * _sigmoid(zz))
        return carry

    lax.fori_loop(0, tt // C, chunk, 0)

    @pl.when(t == pl.num_programs(1) - 1)
    def _():
        sfin_ref[0] = s_scr[...]


def _gdn(c, z, ab, tail, s0, wconv, alog, dtb, wg, tt, valid_rows=None):
    b, t, _ = c.shape
    assert t % tt == 0 and tt % GDN_CHUNK == 0
    seq = lambda i, j: (i, j, 0)
    fixed2 = lambda i, j: (0, 0)
    tail_map = (lambda i, j: (i, 0, 0)) if tail.shape[0] == b else (lambda i, j: (0, 0, 0))
    s0_map = (lambda i, j: (i, 0, 0, 0)) if s0.shape[0] == b else (lambda i, j: (0, 0, 0, 0))
    return pl.pallas_call(
        functools.partial(_gdn_kernel, tt=tt, valid_rows=valid_rows),
        grid=(b, t // tt),
        in_specs=[pl.BlockSpec((1, tt, CONV_CH), seq),
                  pl.BlockSpec((1, tt, GDN_V_W), seq),
                  pl.BlockSpec((1, tt, AB_W), seq),
                  pl.BlockSpec((1, 8, CONV_CH), tail_map),
                  pl.BlockSpec((1, GDN_HEADS, GDN_DK, GDN_DV), s0_map),
                  pl.BlockSpec((CONV_W, CONV_CH), fixed2),
                  pl.BlockSpec((1, LANES), fixed2),
                  pl.BlockSpec((1, LANES), fixed2),
                  pl.BlockSpec((1, GDN_DV), fixed2)],
        out_specs=[pl.BlockSpec((1, tt, GDN_V_W), seq),
                   pl.BlockSpec((1, GDN_HEADS, GDN_DK, GDN_DV), lambda i, j: (i, 0, 0, 0))],
        out_shape=[jax.ShapeDtypeStruct((b, t, GDN_V_W), F32),
                   jax.ShapeDtypeStruct((b, GDN_HEADS, GDN_DK, GDN_DV), F32)],
        scratch_shapes=[pltpu.VMEM((tt + 8, CONV_CH), F32),
                        pltpu.VMEM((tt, CONV_CH), F32),
                        pltpu.VMEM((GDN_HEADS, GDN_DK, GDN_DV), F32)],
        compiler_params=pltpu.CompilerParams(dimension_semantics=("parallel", "arbitrary"),
                                             vmem_limit_bytes=VMEM_LIMIT),
        name="gdn",
    )(c, z, ab, tail, s0, wconv, alog, dtb, wg)


def _lambda(lam_ref):
    lv = lam_ref[...]
    s1 = jnp.sum(lv[0:1, :] * lv[1:2, :], axis=-1, keepdims=True)
    s2 = jnp.sum(lv[2:3, :] * lv[3:4, :], axis=-1, keepdims=True)
    return jnp.exp(s1) - jnp.exp(s2) + LAM_INIT


def _split_components(q):
    lane = lax.broadcasted_iota(jnp.int32, q.shape, 1)
    first = (lane % DA_DV) < DA_DQK
    return jnp.where(first, q, 0.0).astype(BF16), jnp.where(first, 0.0, q).astype(BF16)


def _online_update(state, s, vb):
    m, l, acc = state
    m_new = jnp.maximum(m, jnp.max(s, axis=-1, keepdims=True))
    p = jnp.exp(s - m_new)
    alpha = jnp.exp(m - m_new)
    return (m_new, alpha * l + jnp.sum(p, axis=-1, keepdims=True),
            alpha * acc + jnp.dot(p.astype(BF16), vb, preferred_element_type=F32))


def _attn_prompt_kernel(q_ref, k_ref, v_ref, lam_ref, wsub_ref, o_ref, *, bq):
    qi = pl.program_id(2)
    q1, q2 = _split_components(q_ref[0] * (DA_DQK ** -0.5))
    neg = jnp.full((bq, 1), -jnp.inf, F32)
    zero = jnp.zeros((bq, 1), F32)
    zacc = jnp.zeros((bq, DA_DV), F32)
    init = ((neg, zero, zacc), (neg, zero, zacc))

    def block(kblk, vblk, mask, st):
        kb, vb = kblk.astype(BF16), vblk.astype(BF16)
        s1, s2 = _dot_nt(q1, kb), _dot_nt(q2, kb)
        if mask is not None:
            s1 = jnp.where(mask, s1, -jnp.inf)
            s2 = jnp.where(mask, s2, -jnp.inf)
        return (_online_update(st[0], s1, vb), _online_update(st[1], s2, vb))

    col = lax.broadcasted_iota(jnp.int32, (bq, LANES), 1)
    st = block(k_ref[0, 0:LANES, :], v_ref[0, 0:LANES, :], col < N_META, init)

    def body(kb, st):
        r0 = pl.multiple_of(N_META + kb * bq, N_META)
        return block(k_ref[0, pl.ds(r0, bq), :], v_ref[0, pl.ds(r0, bq), :], None, st)

    st = lax.fori_loop(0, qi, body, st)
    r0 = pl.multiple_of(N_META + qi * bq, N_META)
    ri = lax.broadcasted_iota(jnp.int32, (bq, bq), 0)
    ci = lax.broadcasted_iota(jnp.int32, (bq, bq), 1)
    st = block(k_ref[0, pl.ds(r0, bq), :], v_ref[0, pl.ds(r0, bq), :], ci <= ri, st)
    (m1, l1, a1), (m2, l2, a2) = st
    o = a1 / l1 - _lambda(lam_ref) * (a2 / l2)
    o_ref[0] = _rms(o, wsub_ref[...]) * (1.0 - LAM_INIT)


def _attn_prompt(q, k_ext, v_ext, lam_vecs, wsub, bq):
    b, t, _ = q.shape
    t_ext = k_ext.shape[1]
    assert t % bq == 0 and t_ext == t + N_META and bq >= LANES
    return pl.pallas_call(
        functools.partial(_attn_prompt_kernel, bq=bq),
        grid=(b, DA_HEADS, t // bq),
        in_specs=[pl.BlockSpec((1, bq, DA_DV), lambda i, h, j: (i, j, h)),
                  pl.BlockSpec((1, t_ext, DA_DV), lambda i, h, j: (i, 0, h)),
                  pl.BlockSpec((1, t_ext, DA_DV), lambda i, h, j: (i, 0, h)),
                  pl.BlockSpec((4, DA_DQK), lambda i, h, j: (0, 0)),
                  pl.BlockSpec((1, DA_DV), lambda i, h, j: (0, 0))],
        out_specs=pl.BlockSpec((1, bq, DA_DV), lambda i, h, j: (i, j, h)),
        out_shape=jax.ShapeDtypeStruct((b, t, DA_V_W), F32),
        compiler_params=pltpu.CompilerParams(
            dimension_semantics=("parallel", "parallel", "arbitrary"),
            vmem_limit_bytes=VMEM_LIMIT),
        name="attn_prompt",
    )(q, k_ext, v_ext, lam_vecs, wsub)


def _attn_sample_kernel(pt_ref, q_ref, kn_ref, vn_ref, lam_ref, wsub_ref, *rest, pps):
    k_refs, v_refs = rest[:pps], rest[pps:2 * pps]
    o_ref, m_scr, l_scr, acc_scr = rest[2 * pps:]
    step = pl.program_id(1)
    nrow = 2 * DA_HEADS
    row = lax.broadcasted_iota(jnp.int32, (nrow, DA_V_W), 0)
    lane = lax.broadcasted_iota(jnp.int32, (nrow, DA_V_W), 1)
    own_head = (lane // DA_DV) == (row % DA_HEADS)
    own = own_head & (((lane % DA_DV) // DA_DQK) == (row // DA_HEADS))
    qbd = jnp.where(own, q_ref[0] * (DA_DQK ** -0.5), 0.0)

    def fold(x):
        x = jnp.where(own_head, x, 0.0)
        return (x[:, 0:DA_DV] + x[:, DA_DV:2 * DA_DV]) + (x[:, 2 * DA_DV:3 * DA_DV] + x[:, 3 * DA_DV:])

    def rows3(x):
        hi = x.astype(BF16).astype(F32)
        mid = (x - hi).astype(BF16).astype(F32)
        return jnp.concatenate([hi, mid, x - hi - mid], axis=0).astype(BF16)

    def unrows3(r):
        return (r[0:nrow] + r[nrow:2 * nrow]) + r[2 * nrow:3 * nrow]

    def hi_lo(x):
        hi = x.astype(BF16)
        return hi, (x - hi.astype(F32)).astype(BF16)

    @pl.when(step == 0)
    def _():
        s_new = jnp.sum(qbd * kn_ref[0], axis=-1, keepdims=True)
        m_scr[...] = s_new
        l_scr[...] = jnp.ones_like(s_new)
        acc_scr[...] = fold(jnp.broadcast_to(vn_ref[0], (nrow, DA_V_W)))

    q3 = rows3(qbd)
    scores = []
    for r in k_refs:
        k_hi, k_lo = hi_lo(r[0])
        scores.append(unrows3(_dot_nt(q3, k_hi) + _dot_nt(q3, k_lo)))
    s = jnp.concatenate(scores, axis=1)
    m_old = m_scr[...]
    m_new = jnp.maximum(m_old, jnp.max(s, axis=-1, keepdims=True))
    p = jnp.exp(s - m_new)
    alpha = jnp.exp(m_old - m_new)
    page = k_refs[0].shape[1]
    p3 = rows3(p)
    pv = jnp.zeros((3 * nrow, DA_V_W), F32)
    for j, r in enumerate(v_refs):
        v_hi, v_lo = hi_lo(r[0])
        pj = p3[:, j * page:(j + 1) * page]
        pv = pv + (jnp.dot(pj, v_hi, preferred_element_type=F32)
                   + jnp.dot(pj, v_lo, preferred_element_type=F32))
    pv = unrows3(pv)
    m_scr[...] = m_new
    l_scr[...] = alpha * l_scr[...] + jnp.sum(p, axis=-1, keepdims=True)
    acc_scr[...] = alpha * acc_scr[...] + fold(pv)

    @pl.when(step == pl.num_programs(1) - 1)
    def _():
        w = acc_scr[...] / l_scr[...]
        o = w[0:DA_HEADS, :] - _lambda(lam_ref) * w[DA_HEADS:nrow, :]
        o_ref[0] = _rms(o, wsub_ref[...]) * (1.0 - LAM_INIT)


def _attn_sample(q, k_new, v_new, cache_k, cache_v, page_table, lam_vecs, wsub, pps):
    bs = q.shape[0]
    n_pages = page_table.shape[1]
    page = cache_k.shape[1]
    assert n_pages % pps == 0
    tok = lambda i, s, pt: (i, 0, 0)
    fixed = lambda i, s, pt: (0, 0)

    def page_spec(j):
        return pl.BlockSpec((1, page, DA_QK_W), lambda i, s, pt: (pt[i, s * pps + j], 0, 0))

    grid_spec = pltpu.PrefetchScalarGridSpec(
        num_scalar_prefetch=1,
        grid=(bs, n_pages // pps),
        in_specs=[pl.BlockSpec((1, 1, DA_QK_W), tok),
                  pl.BlockSpec((1, 1, DA_QK_W), tok),
                  pl.BlockSpec((1, 1, DA_V_W), tok),
                  pl.BlockSpec((4, DA_DQK), fixed),
                  pl.BlockSpec((1, DA_DV), fixed)]
                 + [page_spec(j) for j in range(pps)] * 2,
        out_specs=pl.BlockSpec((1, DA_HEADS, DA_DV), tok),
        scratch_shapes=[pltpu.VMEM((2 * DA_HEADS, 1), F32),
                        pltpu.VMEM((2 * DA_HEADS, 1), F32),
                        pltpu.VMEM((2 * DA_HEADS, DA_DV), F32)])
    return pl.pallas_call(
        functools.partial(_attn_sample_kernel, pps=pps),
        grid_spec=grid_spec,
        out_shape=jax.ShapeDtypeStruct((bs, DA_HEADS, DA_DV), F32),
        compiler_params=pltpu.CompilerParams(dimension_semantics=("parallel", "arbitrary"),
                                             vmem_limit_bytes=VMEM_LIMIT),
        name="attn_sample",
    )(page_table, q, k_new, v_new, lam_vecs, wsub, *([cache_k] * pps), *([cache_v] * pps))


def _route(logits):
    lane = lax.broadcasted_iota(jnp.int32, logits.shape, 1).astype(F32)
    big = float(LANES)
    ninf = -jnp.inf

    def top(mask):
        val = jnp.where(mask, logits, ninf)
        mx = jnp.max(val, axis=-1, keepdims=True)
        idx = jnp.min(jnp.where(mask & (val == mx), lane, big), axis=-1, keepdims=True)
        return mx, idx

    gmask = (lane >= N_EXPERTS) & (lane < N_EXPERTS + N_GROUPS)
    gmax, gidx = top(gmask)
    g_w = 1.0 / jnp.sum(jnp.where(gmask, jnp.exp(logits - gmax), 0.0), axis=-1, keepdims=True)
    g_sel = gidx - N_EXPERTS
    emask = (lane >= g_sel * EXPERTS_PER_GROUP) & (lane < (g_sel + 1) * EXPERTS_PER_GROUP)
    v1, i1 = top(emask)
    v2, i2 = top(emask & (lane != i1))
    e21 = jnp.exp(v2 - v1)
    p1 = 1.0 / (1.0 + e21)
    return jnp.where(lane == i1, p1 * g_w, 0.0) + jnp.where(lane == i2, e21 * p1 * g_w, 0.0)


def _ffn_kernel(g_ref, a_ref, x_ref, wo_ref, wnf_ref, wr_ref, wg_ref, wu_ref, wd_ref, wfin_ref,
                y_ref, x1_scr, xn_scr, comb_scr, acc_scr, *, precise):
    e = pl.program_id(1)

    @pl.when(e == 0)
    def _():
        mix = (_mm(g_ref[...], wo_ref[0:GDN_V_W, :], _NN, precise)
               + _mm(a_ref[...], wo_ref[GDN_V_W:, :], _NN, precise))
        x1 = x_ref[...] + mix
        xn = _rms(x1, wnf_ref[...])
        x1_scr[...] = x1
        xn_scr[...] = xn.astype(BF16)
        comb_scr[...] = _route(_dot_f32(xn, wr_ref[...]))
        acc_scr[...] = jnp.zeros_like(acc_scr)

    xn = xn_scr[...]
    hg = jnp.dot(xn, wg_ref[0], preferred_element_type=F32)
    hu = jnp.dot(xn, wu_ref[0], preferred_element_type=F32)
    comb = comb_scr[...]
    lane = lax.broadcasted_iota(jnp.int32, comb.shape, 1)
    cw = jnp.sum(jnp.where(lane == e, comb, 0.0), axis=-1, keepdims=True)
    hh = (hg * _sigmoid(hg)) * hu * cw
    acc_scr[...] += jnp.dot(hh.astype(BF16), wd_ref[0], preferred_element_type=F32)

    @pl.when(e == pl.num_programs(1) - 1)
    def _():
        y_ref[...] = _rms(x1_scr[...] + acc_scr[...], wfin_ref[...])


def _ffn(g_out, a_out, x, wo, wnf, wr, wg, wu, wd, wfin, tm):
    n = x.shape[0]
    assert n % tm == 0
    row = lambda i, e: (i, 0)
    fixed = lambda i, e: (0, 0)
    exp = lambda i, e: (e, 0, 0)
    return pl.pallas_call(
        functools.partial(_ffn_kernel, precise=wo.dtype == F32),
        grid=(n // tm, N_EXPERTS),
        in_specs=[pl.BlockSpec((tm, GDN_V_W), row),
                  pl.BlockSpec((tm, DA_V_W), row),
                  pl.BlockSpec((tm, D_MODEL), row),
                  pl.BlockSpec((GDN_V_W + DA_V_W, D_MODEL), fixed),
                  pl.BlockSpec((1, D_MODEL), fixed),
                  pl.BlockSpec((D_MODEL, LANES), fixed),
                  pl.BlockSpec((1, D_MODEL, D_EXPERT), exp),
                  pl.BlockSpec((1, D_MODEL, D_EXPERT), exp),
                  pl.BlockSpec((1, D_EXPERT, D_MODEL), exp),
                  pl.BlockSpec((1, D_MODEL), fixed)],
        out_specs=pl.BlockSpec((tm, D_MODEL), row),
        out_shape=jax.ShapeDtypeStruct((n, D_MODEL), F32),
        scratch_shapes=[pltpu.VMEM((tm, D_MODEL), F32),
                        pltpu.VMEM((tm, D_MODEL), BF16),
                        pltpu.VMEM((tm, LANES), F32),
                        pltpu.VMEM((tm, D_MODEL), F32)],
        compiler_params=pltpu.CompilerParams(dimension_semantics=("parallel", "arbitrary"),
                                             vmem_limit_bytes=VMEM_LIMIT),
        name="ffn",
    )(g_out, a_out, x, wo, wnf, wr, wg, wu, wd, wfin)


def kernel(x_prompt, x_sample, cache_k, cache_v, state_conv, state_gdn, page_table, meta_tokens,
           w_norm_mix, w_in, w_conv, a_log, dt_bias, w_gnorm, lambda_q1, lambda_k1, lambda_q2,
           lambda_k2, w_subln, w_out, w_norm_ffn, w_group, w_expert, w_gate, w_up, w_down, w_final):
    l = 0
    bp, seq = x_prompt.shape[0], x_prompt.shape[1]
    bs = x_sample.shape[0]
    n_pool, page = cache_k.shape[1], cache_k.shape[2]
    past = page_table.shape[1] * page
    C = GDN_CHUNK

    w_in_p = _prep_w_in(w_in[l])
    wn_mix = w_norm_mix[l][None, :]
    pad_lanes = lambda v: jnp.pad(v, (0, LANES - v.shape[0]))[None, :]
    alog, dtb = pad_lanes(a_log[l]), pad_lanes(dt_bias[l])
    wgn = w_gnorm[l][None, :]
    lam_vecs = jnp.stack([lambda_q1[l], lambda_k1[l], lambda_q2[l], lambda_k2[l]])
    wsub = w_subln[l][None, :]
    w_router = jnp.pad(jnp.concatenate([w_expert[l], w_group[l]], axis=1),
                       ((0, 0), (0, LANES - N_EXPERTS - N_GROUPS)))
    ffn_w = (w_norm_ffn[l][None, :], w_router, w_gate[l].astype(BF16),
             w_up[l].astype(BF16), w_down[l].astype(BF16), w_final[None, :])

    pos_small = jnp.concatenate([jnp.arange(N_META), jnp.full((bs,), past)])
    rows_small = jnp.concatenate([meta_tokens, x_sample[:, 0, :]], axis=0)
    c_s, z_s, ab_s, q_s, k_s, v_s = _proj(rows_small, wn_mix, w_in_p, _rope_tables(pos_small),
                                           N_META + bs)
    xp = x_prompt.reshape(bp * seq, D_MODEL)
    c_p, z_p, ab_p, q_p, k_p, v_p = _proj(xp, wn_mix, w_in_p.astype(BF16),
                                           _rope_tables(N_META + jnp.arange(seq)), 512)

    front = lambda a: jnp.pad(a[:N_META], ((C - N_META, 0), (0, 0)))[None]
    zero_tail = jnp.zeros((1, 8, CONV_CH), F32)
    zero_state = jnp.zeros((1, GDN_HEADS, GDN_DK, GDN_DV), F32)
    _, s_meta = _gdn(front(c_s), front(z_s), front(ab_s), zero_tail, zero_state,
                     w_conv[l], alog, dtb, wgn, C)
    tail_meta = jnp.pad(c_s[N_META - (CONV_W - 1):N_META], ((8 - (CONV_W - 1), 0), (0, 0)))[None]
    c_p3 = c_p.reshape(bp, seq, CONV_CH)
    g_out, p_gdn = _gdn(c_p3, z_p.reshape(bp, seq, GDN_V_W), ab_p.reshape(bp, seq, AB_W),
                        tail_meta, s_meta, w_conv[l], alog, dtb, wgn, 256)
    p_conv = c_p3[:, seq - (CONV_W - 1):, :]

    with_meta = lambda m, p: jnp.concatenate(
        [jnp.broadcast_to(m[None, :N_META], (bp, N_META, m.shape[1])), p.reshape(bp, seq, -1)], axis=1)
    p_k = with_meta(k_s, k_p)
    p_v = with_meta(v_s, v_p)
    a_out = _attn_prompt(q_p.reshape(bp, seq, DA_QK_W), p_k, p_v, lam_vecs, wsub, 256)

    y_prompt = _ffn(g_out.reshape(bp * seq, GDN_V_W), a_out.reshape(bp * seq, DA_V_W), xp,
                    w_out[l].astype(BF16), *ffn_w, 1024).reshape(bp, seq, D_MODEL)

    sm = lambda a: a[N_META:]
    back = lambda a: jnp.pad(sm(a)[:, None, :], ((0, 0), (0, C - 1), (0, 0)))
    tail_s = jnp.pad(state_conv[l], ((0, 0), (8 - (CONV_W - 1), 0), (0, 0)))
    g_s, s_gdn = _gdn(back(c_s), back(z_s), back(ab_s), tail_s, state_gdn[l],
                      w_conv[l], alog, dtb, wgn, C, valid_rows=1)
    s_conv = jnp.concatenate([state_conv[l][:, 1:, :], sm(c_s)[:, None, :]], axis=1)
    a_s = _attn_sample(sm(q_s)[:, None, :], sm(k_s)[:, None, :], sm(v_s)[:, None, :],
                       cache_k[l].reshape(n_pool, page, DA_QK_W),
                       cache_v[l].reshape(n_pool, page, DA_V_W),
                       page_table, lam_vecs, wsub, 16)
    y_sample = _ffn(g_s[:, 0, :], a_s.reshape(bs, DA_V_W), x_sample[:, 0, :], w_out[l], *ffn_w,
                    bs).reshape(bs, 1, D_MODEL)

    return (y_prompt, y_sample, p_conv[None], p_gdn[None],
            p_k.reshape(1, bp, seq + N_META, DA_HEADS, 2 * DA_DQK),
            p_v.reshape(1, bp, seq + N_META, DA_HEADS, DA_DV),
            s_conv[None], s_gdn[None],
            sm(k_s).reshape(1, bs, 1, DA_HEADS, 2 * DA_DQK),
            sm(v_s).reshape(1, bs, 1, DA_HEADS, DA_DV))
```

```python
import functools
import math

import jax
import jax.numpy as jnp
import numpy as np
from jax import lax
from jax.experimental import pallas as pl
from jax.experimental.pallas import tpu as pltpu

F32 = jnp.float32
BF16 = jnp.bfloat16

D_MODEL = 1024
N_META = 16
EPS = 1e-6
GDN_HEADS = 4
GDN_DK = 128
GDN_DV = 128
CONV_W = 4
GDN_QK_W = GDN_HEADS * GDN_DK
GDN_V_W = GDN_HEADS * GDN_DV
CONV_CH = 2 * GDN_QK_W + GDN_V_W
DA_HEADS = 4
DA_DQK = 64
DA_DV = 2 * DA_DQK
DA_QK_W = DA_HEADS * 2 * DA_DQK
DA_V_W = DA_HEADS * DA_DV
ROPE_DIM = DA_DQK // 4
ROPE_HALF = ROPE_DIM // 2
ROPE_THETA = 500000.0
N_GROUPS = 4
EXPERTS_PER_GROUP = 8
N_EXPERTS = N_GROUPS * EXPERTS_PER_GROUP
D_EXPERT = 256
LAM_INIT = 0.8 - 0.6 * math.exp(-0.3 * 0)

LANES = 128
GDN_CHUNK = 64
AB_W = LANES
PROJ_PAD_W = CONV_CH + GDN_V_W + 2 * DA_QK_W + DA_V_W + AB_W
VMEM_LIMIT = 56 * 1024 * 1024


_NN = (((1,), (0,)), ((), ()))
_NT = (((1,), (1,)), ((), ()))
_TN = (((0,), (0,)), ((), ()))


def _mm(a, b, dims=_NN, precise=False):
    if precise:
        return lax.dot_general(a.astype(F32), b.astype(F32), dims, preferred_element_type=F32,
                               precision=lax.Precision.HIGHEST)
    return lax.dot_general(a.astype(BF16), b.astype(BF16), dims, preferred_element_type=F32)


def _dot_nt(a, b):
    return _mm(a, b, _NT)


def _dot_f32(a, b):
    return _mm(a, b, _NN, True)


def _dot_nt_f32(a, b):
    return _mm(a, b, _NT, True)


def _sigmoid(x):
    return 1.0 / (1.0 + jnp.exp(-x))


def _softplus(x):
    return jnp.maximum(x, 0.0) + jnp.log(1.0 + jnp.exp(-jnp.abs(x)))


def _rms(x, w):
    return x * lax.rsqrt(jnp.mean(x * x, axis=-1, keepdims=True) + EPS) * w


def _rope(x, cs, sn_lo, sn_hi):
    w = x.shape[1]
    cs = jnp.concatenate([cs] * DA_HEADS, axis=1)
    sn_lo = jnp.concatenate([sn_lo] * DA_HEADS, axis=1)
    sn_hi = jnp.concatenate([sn_hi] * DA_HEADS, axis=1)
    up = pltpu.roll(x, w - ROPE_HALF, 1)
    dn = pltpu.roll(x, ROPE_HALF, 1)
    return x * cs + up * sn_lo + dn * sn_hi


def _proj_kernel(x_ref, wn_ref, w_ref, cs_ref, snl_ref, snh_ref,
                 c_ref, z_ref, ab_ref, q_ref, k_ref, v_ref, *, precise):
    h = _rms(x_ref[...], wn_ref[...])
    h = h if precise else h.astype(BF16)
    o = 0
    c_ref[...] = _mm(h, w_ref[:, o:o + CONV_CH], _NN, precise)
    o += CONV_CH
    z_ref[...] = _mm(h, w_ref[:, o:o + GDN_V_W], _NN, precise)
    o += GDN_V_W
    q = _mm(h, w_ref[:, o:o + DA_QK_W], _NN, precise)
    o += DA_QK_W
    k = _mm(h, w_ref[:, o:o + DA_QK_W], _NN, precise)
    o += DA_QK_W
    v_ref[...] = _mm(h, w_ref[:, o:o + DA_V_W], _NN, precise)
    o += DA_V_W
    ab_ref[...] = _mm(h, w_ref[:, o:o + AB_W], _NN, precise)
    cs, snl, snh = cs_ref[...], snl_ref[...], snh_ref[...]
    q_ref[...] = _rope(q, cs, snl, snh)
    k_ref[...] = _rope(k, cs, snl, snh)


def _proj(x, wn, w, tabs, tm):
    n = x.shape[0]
    assert n % tm == 0
    nt = tabs[0].shape[0] // tm
    row = lambda i: (i, 0)
    fixed = lambda i: (0, 0)
    tab = lambda i: (i % nt, 0)
    widths = (CONV_CH, GDN_V_W, AB_W, DA_QK_W, DA_QK_W, DA_V_W)
    return pl.pallas_call(
        functools.partial(_proj_kernel, precise=w.dtype == F32),
        grid=(n // tm,),
        in_specs=[pl.BlockSpec((tm, D_MODEL), row),
                  pl.BlockSpec((1, D_MODEL), fixed),
                  pl.BlockSpec((D_MODEL, PROJ_PAD_W), fixed),
                  pl.BlockSpec((tm, LANES), tab),
                  pl.BlockSpec((tm, LANES), tab),
                  pl.BlockSpec((tm, LANES), tab)],
        out_specs=[pl.BlockSpec((tm, wd), row) for wd in widths],
        out_shape=[jax.ShapeDtypeStruct((n, wd), F32) for wd in widths],
        compiler_params=pltpu.CompilerParams(dimension_semantics=("parallel",),
                                             vmem_limit_bytes=VMEM_LIMIT),
        name="proj",
    )(x, wn, w, *tabs)


def _rope_tables(pos):
    inv_freq = ROPE_THETA ** (-jnp.arange(ROPE_HALF, dtype=F32) * 2.0 / ROPE_DIM)
    ang = pos.astype(F32)[:, None] * inv_freq[None, :]
    cos, sin = jnp.cos(ang), jnp.sin(ang)
    t = pos.shape[0]
    one = jnp.ones((t, DA_DQK - ROPE_DIM), F32)
    zero = jnp.zeros((t, DA_DQK - ROPE_DIM), F32)
    zh = jnp.zeros((t, ROPE_HALF), F32)
    cs = jnp.concatenate([cos, cos, one] * 2, axis=1)
    snl = jnp.concatenate([-sin, zh, zero] * 2, axis=1)
    snh = jnp.concatenate([zh, sin, zero] * 2, axis=1)
    return cs, snl, snh


def _prep_w_in(w_in):
    c, z, a, b, dq, dk, dv = jnp.split(
        w_in, np.cumsum([CONV_CH, GDN_V_W, GDN_HEADS, GDN_HEADS, DA_QK_W, DA_QK_W]).tolist(), axis=-1)
    pad = jnp.zeros((D_MODEL, AB_W - 2 * GDN_HEADS), w_in.dtype)
    return jnp.concatenate([c, z, dq, dk, dv, a, b, pad], axis=-1)


def _gdn_kernel(c_ref, z_ref, ab_ref, tail_ref, s0_ref, wconv_ref, alog_ref, dtb_ref, wg_ref,
                o_ref, sfin_ref, cbuf, ybuf, s_scr, u0_scr, wq_scr, kw_scr, qkd_scr, el_scr, *, tt):
    C, H = GDN_CHUNK, GDN_HEADS
    R = H * C
    shift = C.bit_length() - 1
    t = pl.program_id(1)

    @pl.when(t == 0)
    def _():
        cbuf[0:8, :] = tail_ref[0]
        s_scr[...] = s0_ref[0]

    cbuf[8:8 + tt, :] = c_ref[0]
    acc = cbuf[8:8 + tt, :] * wconv_ref[CONV_W - 1:CONV_W, :]
    for i in range(CONV_W - 1):
        off = 8 - (CONV_W - 1) + i
        acc = acc + cbuf[off:off + tt, :] * wconv_ref[i:i + 1, :]
    ybuf[...] = acc * _sigmoid(acc)
    cbuf[0:8, :] = cbuf[tt:tt + 8, :]

    ii = lax.broadcasted_iota(jnp.int32, (R, R), 0)
    jj = lax.broadcasted_iota(jnp.int32, (R, R), 1)
    same_head = jnp.right_shift(ii, shift) == jnp.right_shift(jj, shift)
    strict = same_head & (ii > jj)
    causal = same_head & (ii >= jj)
    eye = (ii == jj).astype(F32)
    merge_masks = []
    s = 1
    while s < C:
        sh = s.bit_length() - 1
        merge_masks.append((jnp.right_shift(ii, sh + 1) == jnp.right_shift(jj, sh + 1))
                           & (jnp.right_shift(ii, sh) != jnp.right_shift(jj, sh)) & (ii > jj))
        s *= 2
    ci_ = lax.broadcasted_iota(jnp.int32, (C, C), 0)
    cj_ = lax.broadcasted_iota(jnp.int32, (C, C), 1)
    tri = (ci_ >= cj_).astype(F32)
    lane = lax.broadcasted_iota(jnp.int32, (R, LANES), 1)
    rhead = jnp.right_shift(lax.broadcasted_iota(jnp.int32, (R, LANES), 0), shift)
    ones8 = jnp.ones((8, LANES), F32)
    rep = lambda a: jnp.concatenate([a] * H, axis=0)
    pick = lambda a, off: jnp.sum(jnp.where(lane == rhead + off, a, 0.0), axis=-1, keepdims=True)

    def stacked(rows, base, scale):
        parts = []
        for h in range(H):
            x = ybuf[rows, base + h * GDN_DK:base + (h + 1) * GDN_DK]
            if scale is not None:
                x = x * lax.rsqrt(jnp.sum(x * x, axis=-1, keepdims=True) + EPS) * scale
            parts.append(x)
        return jnp.concatenate(parts, axis=0)

    for ci in range(tt // C):
        rows = slice(ci * C, (ci + 1) * C)
        ab = ab_ref[0, rows, :]
        gfull = -jnp.exp(alog_ref[...]) * _softplus(ab + dtb_ref[...])
        bfull = _sigmoid(ab)
        gcum = _dot_f32(tri, gfull)
        glast = gcum[C - 1:C, :]
        g_m = jnp.where(lane == rhead, rep(gcum), 0.0)
        g_col = jnp.sum(g_m, axis=-1, keepdims=True)
        g_row = _dot_nt_f32(ones8, g_m)[0:1, :]
        beta = pick(rep(bfull), H)
        gl_col = pick(jnp.broadcast_to(glast, (R, LANES)), 0)
        decay = jnp.exp(jnp.where(causal, g_col - g_row, -jnp.inf))
        q = stacked(rows, 0, GDN_DK ** -0.5)
        k = stacked(rows, GDN_QK_W, 1.0)
        v = stacked(rows, 2 * GDN_QK_W, None)
        kq = _mm(jnp.concatenate([k, q], axis=0), k, _NT)
        low = jnp.where(strict, kq[0:R] * decay * beta, 0.0)
        tinv = eye - jnp.where(merge_masks[0], low, 0.0)
        for m in merge_masks[1:]:
            off = jnp.where(m, low, 0.0)
            tinv = tinv - _mm(_mm(tinv, off), tinv)
        e_g = jnp.exp(g_col)
        w2 = _mm(tinv, jnp.concatenate([beta * v, (beta * e_g) * k], axis=1))
        u0_scr[ci] = w2[:, 0:GDN_DV]
        wk, qe = w2[:, GDN_DV:], q * e_g
        wq_scr[ci] = jnp.concatenate(
            [x[h * C:(h + 1) * C] for h in range(H) for x in (wk, qe)], axis=0).astype(BF16)
        kw_scr[ci] = (k * jnp.exp(gl_col - g_col)).astype(BF16)
        qkd_scr[ci] = jnp.where(causal, kq[R:2 * R] * decay, 0.0).astype(BF16)
        el_scr[ci] = jnp.broadcast_to(jnp.exp(glast), (8, LANES))

    st = [s_scr[h] for h in range(H)]
    for ci in range(tt // C):
        rows = slice(ci * C, (ci + 1) * C)
        us, os_ = [], []
        for h in range(H):
            r = jnp.dot(wq_scr[ci, 2 * h * C:2 * (h + 1) * C, :], st[h].astype(BF16),
                        preferred_element_type=F32)
            us.append(u0_scr[ci, h * C:(h + 1) * C, :] - r[0:C])
            os_.append(r[C:2 * C])
        intra = jnp.dot(qkd_scr[ci], jnp.concatenate(us, axis=0).astype(BF16),
                        preferred_element_type=F32)
        el = el_scr[ci]
        for h in range(H):
            st[h] = st[h] * el[0:1, h:h + 1] + _mm(kw_scr[ci, h * C:(h + 1) * C, :], us[h], _TN)
            o = os_[h] + intra[h * C:(h + 1) * C]
            zz = z_ref[0, rows, h * GDN_DV:(h + 1) * GDN_DV]
            o_ref[0, rows, h * GDN_DV:(h + 1) * GDN_DV] = _rms(o, wg_ref[...]) * (zz * _sigmoid(zz))
    for h in range(H):
        s_scr[h] = st[h]

    @pl.when(t == pl.num_programs(1) - 1)
    def _():
        sfin_ref[0] = s_scr[...]


def _gdn_step_kernel(c_ref, z_ref, ab_ref, cs_ref, s_ref, wconv_ref, alog_ref, dtb_ref, wg_ref,
                     o_ref, snew_ref, *, nb):
    H = GDN_HEADS
    acc = c_ref[...] * wconv_ref[CONV_W - 1:CONV_W, :]
    for i in range(CONV_W - 1):
        acc = acc + cs_ref[i] * wconv_ref[i:i + 1, :]
    y = acc * _sigmoid(acc)
    ab = ab_ref[...]
    e_g = jnp.exp(-jnp.exp(alog_ref[...]) * _softplus(ab + dtb_ref[...]))
    bfull = _sigmoid(ab)
    unit = lambda x: x * lax.rsqrt(jnp.sum(x * x, axis=-1, keepdims=True) + EPS)
    qs = [unit(y[:, h * GDN_DK:(h + 1) * GDN_DK]) * (GDN_DK ** -0.5) for h in range(H)]
    ks = [unit(y[:, GDN_QK_W + h * GDN_DK:GDN_QK_W + (h + 1) * GDN_DK]) for h in range(H)]
    vs = [y[:, 2 * GDN_QK_W + h * GDN_DV:2 * GDN_QK_W + (h + 1) * GDN_DV] for h in range(H)]
    cols = jnp.concatenate(ks + qs, axis=0).T
    for h in range(H):
        ks_rows, qs_rows = [], []
        for s in range(nb):
            state = s_ref[s, h]
            kcol = cols[:, h * nb + s:h * nb + s + 1]
            qcol = cols[:, (H + h) * nb + s:(H + h) * nb + s + 1]
            ks_rows.append(jnp.sum(kcol * state, axis=0, keepdims=True))
            qs_rows.append(jnp.sum(qcol * state, axis=0, keepdims=True))
        eg_h = e_g[:, h:h + 1]
        u = bfull[:, H + h:H + h + 1] * (vs[h] - eg_h * jnp.concatenate(ks_rows, axis=0))
        o = (eg_h * jnp.concatenate(qs_rows, axis=0)
             + jnp.sum(qs[h] * ks[h], axis=-1, keepdims=True) * u)
        eg_b = jnp.broadcast_to(eg_h, (nb, GDN_DV))
        for s in range(nb):
            kcol = cols[:, h * nb + s:h * nb + s + 1]
            snew_ref[s, h] = eg_b[s:s + 1, :] * s_ref[s, h] + kcol * u[s:s + 1, :]
        zz = z_ref[:, h * GDN_DV:(h + 1) * GDN_DV]
        o_ref[:, h * GDN_DV:(h + 1) * GDN_DV] = _rms(o, wg_ref[...]) * (zz * _sigmoid(zz))


def _gdn_step(c, z, ab, conv_state, s, wconv, alog, dtb, wg):
    bs = c.shape[0]
    nb = LANES // (2 * GDN_HEADS)
    assert bs % nb == 0
    row = lambda i: (i, 0)
    fixed = lambda i: (0, 0)
    state = pl.BlockSpec((nb, GDN_HEADS, GDN_DK, GDN_DV), lambda i: (i, 0, 0, 0))
    return pl.pallas_call(
        functools.partial(_gdn_step_kernel, nb=nb),
        grid=(bs // nb,),
        in_specs=[pl.BlockSpec((nb, CONV_CH), row),
                  pl.BlockSpec((nb, GDN_V_W), row),
                  pl.BlockSpec((nb, AB_W), row),
                  pl.BlockSpec((CONV_W - 1, nb, CONV_CH), lambda i: (0, i, 0)),
                  state,
                  pl.BlockSpec((CONV_W, CONV_CH), fixed),
                  pl.BlockSpec((1, LANES), fixed),
                  pl.BlockSpec((1, LANES), fixed),
                  pl.BlockSpec((1, GDN_DV), fixed)],
        out_specs=[pl.BlockSpec((nb, GDN_V_W), row), state],
        out_shape=[jax.ShapeDtypeStruct((bs, GDN_V_W), F32),
                   jax.ShapeDtypeStruct(s.shape, F32)],
        compiler_params=pltpu.CompilerParams(dimension_semantics=("parallel",),
                                             vmem_limit_bytes=VMEM_LIMIT),
        name="gdn_step",
    )(c, z, ab, conv_state, s, wconv, alog, dtb, wg)


def _gdn(c, z, ab, tail, s0, wconv, alog, dtb, wg, tt):
    b, t, _ = c.shape
    assert t % tt == 0 and tt % GDN_CHUNK == 0
    nc, rs = tt // GDN_CHUNK, GDN_HEADS * GDN_CHUNK
    seq = lambda i, j: (i, j, 0)
    fixed2 = lambda i, j: (0, 0)
    tail_map = (lambda i, j: (i, 0, 0)) if tail.shape[0] == b else (lambda i, j: (0, 0, 0))
    s0_map = (lambda i, j: (i, 0, 0, 0)) if s0.shape[0] == b else (lambda i, j: (0, 0, 0, 0))
    return pl.pallas_call(
        functools.partial(_gdn_kernel, tt=tt),
        grid=(b, t // tt),
        in_specs=[pl.BlockSpec((1, tt, CONV_CH), seq),
                  pl.BlockSpec((1, tt, GDN_V_W), seq),
                  pl.BlockSpec((1, tt, AB_W), seq),
                  pl.BlockSpec((1, 8, CONV_CH), tail_map),
                  pl.BlockSpec((1, GDN_HEADS, GDN_DK, GDN_DV), s0_map),
                  pl.BlockSpec((CONV_W, CONV_CH), fixed2),
                  pl.BlockSpec((1, LANES), fixed2),
                  pl.BlockSpec((1, LANES), fixed2),
                  pl.BlockSpec((1, GDN_DV), fixed2)],
        out_specs=[pl.BlockSpec((1, tt, GDN_V_W), seq),
                   pl.BlockSpec((1, GDN_HEADS, GDN_DK, GDN_DV), lambda i, j: (i, 0, 0, 0))],
        out_shape=[jax.ShapeDtypeStruct((b, t, GDN_V_W), F32),
                   jax.ShapeDtypeStruct((b, GDN_HEADS, GDN_DK, GDN_DV), F32)],
        scratch_shapes=[pltpu.VMEM((tt + 8, CONV_CH), F32),
                        pltpu.VMEM((tt, CONV_CH), F32),
                        pltpu.VMEM((GDN_HEADS, GDN_DK, GDN_DV), F32),
                        pltpu.VMEM((nc, rs, GDN_DV), F32),
                        pltpu.VMEM((nc, 2 * rs, GDN_DK), BF16),
                        pltpu.VMEM((nc, rs, GDN_DK), BF16),
                        pltpu.VMEM((nc, rs, rs), BF16),
                        pltpu.VMEM((nc, 8, LANES), F32)],
        compiler_params=pltpu.CompilerParams(dimension_semantics=("parallel", "arbitrary"),
                                             vmem_limit_bytes=VMEM_LIMIT),
        name="gdn",
    )(c, z, ab, tail, s0, wconv, alog, dtb, wg)


def _lambda(lam_ref):
    lv = lam_ref[...]
    s1 = jnp.sum(lv[0:1, :] * lv[1:2, :], axis=-1, keepdims=True)
    s2 = jnp.sum(lv[2:3, :] * lv[3:4, :], axis=-1, keepdims=True)
    return jnp.exp(s1) - jnp.exp(s2) + LAM_INIT


def _split_components(q):
    lane = lax.broadcasted_iota(jnp.int32, q.shape, 1)
    first = (lane % DA_DV) < DA_DQK
    return jnp.where(first, q, 0.0).astype(BF16), jnp.where(first, 0.0, q).astype(BF16)


def _online_update(state, s, vb):
    m, l, acc = state
    m_new = jnp.maximum(m, jnp.max(s, axis=-1, keepdims=True))
    p = jnp.exp(s - m_new)
    alpha = jnp.exp(m - m_new)
    return (m_new, alpha * l + jnp.sum(p, axis=-1, keepdims=True),
            alpha * acc + jnp.dot(p.astype(BF16), vb, preferred_element_type=F32))


def _attn_prompt_kernel(q_ref, k_ref, v_ref, lam_ref, wsub_ref, o_ref, *, bq):
    qi = pl.program_id(2)
    q1, q2 = _split_components(q_ref[0] * (DA_DQK ** -0.5))
    neg = jnp.full((bq, 1), -jnp.inf, F32)
    zero = jnp.zeros((bq, 1), F32)
    zacc = jnp.zeros((bq, DA_DV), F32)
    init = ((neg, zero, zacc), (neg, zero, zacc))

    def block(kblk, vblk, mask, st):
        kb, vb = kblk.astype(BF16), vblk.astype(BF16)
        s1, s2 = _dot_nt(q1, kb), _dot_nt(q2, kb)
        if mask is not None:
            s1 = jnp.where(mask, s1, -jnp.inf)
            s2 = jnp.where(mask, s2, -jnp.inf)
        return (_online_update(st[0], s1, vb), _online_update(st[1], s2, vb))

    col = lax.broadcasted_iota(jnp.int32, (bq, LANES), 1)
    st = block(k_ref[0, 0:LANES, :], v_ref[0, 0:LANES, :], col < N_META, init)

    def body(kb, st):
        r0 = pl.multiple_of(N_META + kb * bq, N_META)
        return block(k_ref[0, pl.ds(r0, bq), :], v_ref[0, pl.ds(r0, bq), :], None, st)

    st = lax.fori_loop(0, qi, body, st)
    r0 = pl.multiple_of(N_META + qi * bq, N_META)
    ri = lax.broadcasted_iota(jnp.int32, (bq, bq), 0)
    ci = lax.broadcasted_iota(jnp.int32, (bq, bq), 1)
    st = block(k_ref[0, pl.ds(r0, bq), :], v_ref[0, pl.ds(r0, bq), :], ci <= ri, st)
    (m1, l1, a1), (m2, l2, a2) = st
    o = a1 / l1 - _lambda(lam_ref) * (a2 / l2)
    o_ref[0] = _rms(o, wsub_ref[...]) * (1.0 - LAM_INIT)


def _attn_prompt(q, k_ext, v_ext, lam_vecs, wsub, bq):
    b, t, _ = q.shape
    t_ext = k_ext.shape[1]
    assert t % bq == 0 and t_ext == t + N_META and bq >= LANES
    return pl.pallas_call(
        functools.partial(_attn_prompt_kernel, bq=bq),
        grid=(b, DA_HEADS, t // bq),
        in_specs=[pl.BlockSpec((1, bq, DA_DV), lambda i, h, j: (i, j, h)),
                  pl.BlockSpec((1, t_ext, DA_DV), lambda i, h, j: (i, 0, h)),
                  pl.BlockSpec((1, t_ext, DA_DV), lambda i, h, j: (i, 0, h)),
                  pl.BlockSpec((4, DA_DQK), lambda i, h, j: (0, 0)),
                  pl.BlockSpec((1, DA_DV), lambda i, h, j: (0, 0))],
        out_specs=pl.BlockSpec((1, bq, DA_DV), lambda i, h, j: (i, j, h)),
        out_shape=jax.ShapeDtypeStruct((b, t, DA_V_W), F32),
        compiler_params=pltpu.CompilerParams(
            dimension_semantics=("parallel", "parallel", "arbitrary"),
            vmem_limit_bytes=VMEM_LIMIT),
        name="attn_prompt",
    )(q, k_ext, v_ext, lam_vecs, wsub)


def _attn_sample_kernel(pt_ref, q_ref, kn_ref, vn_ref, lam_ref, wsub_ref, *rest, pps):
    k_refs, v_refs = rest[:pps], rest[pps:2 * pps]
    o_ref, m_scr, l_scr, acc_scr = rest[2 * pps:]
    step = pl.program_id(1)
    nrow = 2 * DA_HEADS
    twice = lambda x: jnp.concatenate([x, x], axis=0)
    rowq = lax.broadcasted_iota(jnp.int32, (nrow, DA_DV), 0)
    laneq = lax.broadcasted_iota(jnp.int32, (nrow, DA_DV), 1)
    own_comp = (laneq // DA_DQK) == (rowq // DA_HEADS)
    qm = jnp.where(own_comp, twice(q_ref[0]) * (DA_DQK ** -0.5), 0.0)
    page_rows = k_refs[0].shape[1]
    rowp = lax.broadcasted_iota(jnp.int32, (nrow, page_rows * pps), 0)
    colp = lax.broadcasted_iota(jnp.int32, (nrow, page_rows * pps), 1)
    own_head = (colp % DA_HEADS) == (rowp % DA_HEADS)

    def rows3(x):
        hi = x.astype(BF16).astype(F32)
        mid = (x - hi).astype(BF16).astype(F32)
        return jnp.concatenate([hi, mid, x - hi - mid], axis=0).astype(BF16)

    def unrows3(r):
        return (r[0:nrow] + r[nrow:2 * nrow]) + r[2 * nrow:3 * nrow]

    def hi_lo(x):
        hi = x.astype(BF16)
        return hi, (x - hi.astype(F32)).astype(BF16)

    @pl.when(step == 0)
    def _():
        s_new = jnp.sum(qm * twice(kn_ref[0]), axis=-1, keepdims=True)
        m_scr[...] = s_new
        l_scr[...] = jnp.ones_like(s_new)
        acc_scr[...] = twice(vn_ref[0])

    q3 = rows3(qm)
    scores = []
    for r in k_refs:
        k_hi, k_lo = hi_lo(r[0])
        scores.append(unrows3(_dot_nt(q3, k_hi) + _dot_nt(q3, k_lo)))
    s = jnp.where(own_head, jnp.concatenate(scores, axis=1), -jnp.inf)
    m_old = m_scr[...]
    m_new = jnp.maximum(m_old, jnp.max(s, axis=-1, keepdims=True))
    p = jnp.exp(s - m_new)
    alpha = jnp.exp(m_old - m_new)
    p3 = rows3(p)
    pv = jnp.zeros((3 * nrow, DA_DV), F32)
    for j, r in enumerate(v_refs):
        v_hi, v_lo = hi_lo(r[0])
        pj = p3[:, j * page_rows:(j + 1) * page_rows]
        pv = pv + (jnp.dot(pj, v_hi, preferred_element_type=F32)
                   + jnp.dot(pj, v_lo, preferred_element_type=F32))
    m_scr[...] = m_new
    l_scr[...] = alpha * l_scr[...] + jnp.sum(p, axis=-1, keepdims=True)
    acc_scr[...] = alpha * acc_scr[...] + unrows3(pv)

    @pl.when(step == pl.num_programs(1) - 1)
    def _():
        w = acc_scr[...] / l_scr[...]
        o = w[0:DA_HEADS, :] - _lambda(lam_ref) * w[DA_HEADS:nrow, :]
        o_ref[0] = _rms(o, wsub_ref[...]) * (1.0 - LAM_INIT)


def _attn_sample(q, k_new, v_new, cache_k, cache_v, page_table, lam_vecs, wsub, pps):
    bs = q.shape[0]
    n_pages = page_table.shape[1]
    page_rows = cache_k.shape[1]
    assert n_pages % pps == 0
    tok = lambda i, s, pt: (i, 0, 0)
    fixed = lambda i, s, pt: (0, 0)

    def page_spec(j):
        return pl.BlockSpec((1, page_rows, DA_DV), lambda i, s, pt: (pt[i, s * pps + j], 0, 0))

    grid_spec = pltpu.PrefetchScalarGridSpec(
        num_scalar_prefetch=1,
        grid=(bs, n_pages // pps),
        in_specs=[pl.BlockSpec((1, DA_HEADS, DA_DV), tok),
                  pl.BlockSpec((1, DA_HEADS, DA_DV), tok),
                  pl.BlockSpec((1, DA_HEADS, DA_DV), tok),
                  pl.BlockSpec((4, DA_DQK), fixed),
                  pl.BlockSpec((1, DA_DV), fixed)]
                 + [page_spec(j) for j in range(pps)] * 2,
        out_specs=pl.BlockSpec((1, DA_HEADS, DA_DV), tok),
        scratch_shapes=[pltpu.VMEM((2 * DA_HEADS, 1), F32),
                        pltpu.VMEM((2 * DA_HEADS, 1), F32),
                        pltpu.VMEM((2 * DA_HEADS, DA_DV), F32)])
    return pl.pallas_call(
        functools.partial(_attn_sample_kernel, pps=pps),
        grid_spec=grid_spec,
        out_shape=jax.ShapeDtypeStruct((bs, DA_HEADS, DA_DV), F32),
        compiler_params=pltpu.CompilerParams(dimension_semantics=("parallel", "arbitrary"),
                                             vmem_limit_bytes=VMEM_LIMIT),
        name="attn_sample",
    )(page_table, q, k_new, v_new, lam_vecs, wsub, *([cache_k] * pps), *([cache_v] * pps))


def _route(logits):
    lane = lax.broadcasted_iota(jnp.int32, logits.shape, 1).astype(F32)
    big = float(LANES)
    ninf = -jnp.inf

    def top(mask):
        val = jnp.where(mask, logits, ninf)
        mx = jnp.max(val, axis=-1, keepdims=True)
        idx = jnp.min(jnp.where(mask & (val == mx), lane, big), axis=-1, keepdims=True)
        return mx, idx

    gmask = (lane >= N_EXPERTS) & (lane < N_EXPERTS + N_GROUPS)
    gmax, gidx = top(gmask)
    g_w = 1.0 / jnp.sum(jnp.where(gmask, jnp.exp(logits - gmax), 0.0), axis=-1, keepdims=True)
    g_sel = gidx - N_EXPERTS
    emask = (lane >= g_sel * EXPERTS_PER_GROUP) & (lane < (g_sel + 1) * EXPERTS_PER_GROUP)
    v1, i1 = top(emask)
    v2, i2 = top(emask & (lane != i1))
    e21 = jnp.exp(v2 - v1)
    p1 = 1.0 / (1.0 + e21)
    return jnp.where(lane == i1, p1 * g_w, 0.0) + jnp.where(lane == i2, e21 * p1 * g_w, 0.0)


def _ffn_kernel(g_ref, a_ref, x_ref, wo_ref, wnf_ref, wr_ref, wg_ref, wu_ref, wd_ref, wfin_ref,
                y_ref, x1_scr, xn_scr, comb_scr, acc_scr, *, precise):
    e = pl.program_id(1)

    @pl.when(e == 0)
    def _():
        mix = (_mm(g_ref[...], wo_ref[0:GDN_V_W, :], _NN, precise)
               + _mm(a_ref[...], wo_ref[GDN_V_W:, :], _NN, precise))
        x1 = x_ref[...] + mix
        xn = _rms(x1, wnf_ref[...])
        x1_scr[...] = x1
        xn_scr[...] = xn.astype(BF16)
        comb_scr[...] = _route(_dot_f32(xn, wr_ref[...]))
        acc_scr[...] = jnp.zeros_like(acc_scr)

    xn = xn_scr[...]
    hg = jnp.dot(xn, wg_ref[0], preferred_element_type=F32)
    hu = jnp.dot(xn, wu_ref[0], preferred_element_type=F32)
    comb = comb_scr[...]
    lane = lax.broadcasted_iota(jnp.int32, comb.shape, 1)
    cw = jnp.sum(jnp.where(lane == e, comb, 0.0), axis=-1, keepdims=True)
    hh = (hg * _sigmoid(hg)) * hu * cw
    acc_scr[...] += jnp.dot(hh.astype(BF16), wd_ref[0], preferred_element_type=F32)

    @pl.when(e == pl.num_programs(1) - 1)
    def _():
        y_ref[...] = _rms(x1_scr[...] + acc_scr[...], wfin_ref[...])


def _ffn(g_out, a_out, x, wo, wnf, wr, wg, wu, wd, wfin, tm):
    n = x.shape[0]
    assert n % tm == 0
    row = lambda i, e: (i, 0)
    fixed = lambda i, e: (0, 0)
    exp = lambda i, e: (e, 0, 0)
    return pl.pallas_call(
        functools.partial(_ffn_kernel, precise=wo.dtype == F32),
        grid=(n // tm, N_EXPERTS),
        in_specs=[pl.BlockSpec((tm, GDN_V_W), row),
                  pl.BlockSpec((tm, DA_V_W), row),
                  pl.BlockSpec((tm, D_MODEL), row),
                  pl.BlockSpec((GDN_V_W + DA_V_W, D_MODEL), fixed),
                  pl.BlockSpec((1, D_MODEL), fixed),
                  pl.BlockSpec((D_MODEL, LANES), fixed),
                  pl.BlockSpec((1, D_MODEL, D_EXPERT), exp),
                  pl.BlockSpec((1, D_MODEL, D_EXPERT), exp),
                  pl.BlockSpec((1, D_EXPERT, D_MODEL), exp),
                  pl.BlockSpec((1, D_MODEL), fixed)],
        out_specs=pl.BlockSpec((tm, D_MODEL), row),
        out_shape=jax.ShapeDtypeStruct((n, D_MODEL), F32),
        scratch_shapes=[pltpu.VMEM((tm, D_MODEL), F32),
                        pltpu.VMEM((tm, D_MODEL), BF16),
                        pltpu.VMEM((tm, LANES), F32),
                        pltpu.VMEM((tm, D_MODEL), F32)],
        compiler_params=pltpu.CompilerParams(dimension_semantics=("parallel", "arbitrary"),
                                             vmem_limit_bytes=VMEM_LIMIT),
        name="ffn",
    )(g_out, a_out, x, wo, wnf, wr, wg, wu, wd, wfin)


def kernel(x_prompt, x_sample, cache_k, cache_v, state_conv, state_gdn, page_table, meta_tokens,
           w_norm_mix, w_in, w_conv, a_log, dt_bias, w_gnorm, lambda_q1, lambda_k1, lambda_q2,
           lambda_k2, w_subln, w_out, w_norm_ffn, w_group, w_expert, w_gate, w_up, w_down, w_final):
    l = 0
    bp, seq = x_prompt.shape[0], x_prompt.shape[1]
    bs = x_sample.shape[0]
    n_pool, page = cache_k.shape[1], cache_k.shape[2]
    past = page_table.shape[1] * page
    C = GDN_CHUNK

    w_in_p = _prep_w_in(w_in[l])
    wn_mix = w_norm_mix[l][None, :]
    pad_lanes = lambda v: jnp.pad(v, (0, LANES - v.shape[0]))[None, :]
    alog, dtb = pad_lanes(a_log[l]), pad_lanes(dt_bias[l])
    wgn = w_gnorm[l][None, :]
    lam_vecs = jnp.stack([lambda_q1[l], lambda_k1[l], lambda_q2[l], lambda_k2[l]])
    wsub = w_subln[l][None, :]
    w_router = jnp.pad(jnp.concatenate([w_expert[l], w_group[l]], axis=1),
                       ((0, 0), (0, LANES - N_EXPERTS - N_GROUPS)))
    ffn_w = (w_norm_ffn[l][None, :], w_router, w_gate[l].astype(BF16),
             w_up[l].astype(BF16), w_down[l].astype(BF16), w_final[None, :])

    pos_small = jnp.concatenate([jnp.arange(N_META), jnp.full((bs,), past)])
    rows_small = jnp.concatenate([meta_tokens, x_sample[:, 0, :]], axis=0)
    c_s, z_s, ab_s, q_s, k_s, v_s = _proj(rows_small, wn_mix, w_in_p, _rope_tables(pos_small),
                                           N_META + bs)
    xp = x_prompt.reshape(bp * seq, D_MODEL)
    c_p, z_p, ab_p, q_p, k_p, v_p = _proj(xp, wn_mix, w_in_p.astype(BF16),
                                           _rope_tables(N_META + jnp.arange(seq)), 512)

    front = lambda a: jnp.pad(a[:N_META], ((C - N_META, 0), (0, 0)))[None]
    zero_tail = jnp.zeros((1, 8, CONV_CH), F32)
    zero_state = jnp.zeros((1, GDN_HEADS, GDN_DK, GDN_DV), F32)
    _, s_meta = _gdn(front(c_s), front(z_s), front(ab_s), zero_tail, zero_state,
                     w_conv[l], alog, dtb, wgn, C)
    tail_meta = jnp.pad(c_s[N_META - (CONV_W - 1):N_META], ((8 - (CONV_W - 1), 0), (0, 0)))[None]
    c_p3 = c_p.reshape(bp, seq, CONV_CH)
    g_out, p_gdn = _gdn(c_p3, z_p.reshape(bp, seq, GDN_V_W), ab_p.reshape(bp, seq, AB_W),
                        tail_meta, s_meta, w_conv[l], alog, dtb, wgn, 256)
    p_conv = c_p3[:, seq - (CONV_W - 1):, :]

    with_meta = lambda m, p: jnp.concatenate(
        [jnp.broadcast_to(m[None, :N_META], (bp, N_META, m.shape[1])), p.reshape(bp, seq, -1)], axis=1)
    p_k = with_meta(k_s, k_p)
    p_v = with_meta(v_s, v_p)
    a_out = _attn_prompt(q_p.reshape(bp, seq, DA_QK_W), p_k, p_v, lam_vecs, wsub, 256)

    y_prompt = _ffn(g_out.reshape(bp * seq, GDN_V_W), a_out.reshape(bp * seq, DA_V_W), xp,
                    w_out[l].astype(BF16), *ffn_w, 1024).reshape(bp, seq, D_MODEL)

    sm = lambda a: a[N_META:]
    g_s, s_gdn = _gdn_step(sm(c_s), sm(z_s), sm(ab_s), jnp.swapaxes(state_conv[l], 0, 1),
                           state_gdn[l], w_conv[l], alog, dtb, wgn)
    s_conv = jnp.concatenate([state_conv[l][:, 1:, :], sm(c_s)[:, None, :]], axis=1)
    heads = lambda a: sm(a).reshape(bs, DA_HEADS, DA_DV)
    a_s = _attn_sample(heads(q_s), heads(k_s), heads(v_s),
                       cache_k[l].reshape(n_pool, page * DA_HEADS, 2 * DA_DQK),
                       cache_v[l].reshape(n_pool, page * DA_HEADS, DA_DV),
                       page_table, lam_vecs, wsub, 16)
    y_sample = _ffn(g_s, a_s.reshape(bs, DA_V_W), x_sample[:, 0, :], w_out[l], *ffn_w,
                    bs).reshape(bs, 1, D_MODEL)

    return (y_prompt, y_sample, p_conv[None], p_gdn[None],
            p_k.reshape(1, bp, seq + N_META, DA_HEADS, 2 * DA_DQK),
            p_v.reshape(1, bp, seq + N_META, DA_HEADS, DA_DV),
            s_conv[None], s_gdn[None],
            sm(k_s).reshape(1, bs, 1, DA_HEADS, 2 * DA_DQK),
            sm(v_s).reshape(1, bs, 1, DA_HEADS, DA_DV))
```

```python
import functools
import math

import jax
import jax.numpy as jnp
import numpy as np
from jax import lax
from jax.experimental import pallas as pl
from jax.experimental.pallas import tpu as pltpu

F32 = jnp.float32
BF16 = jnp.bfloat16

D_MODEL = 1024
N_META = 16
EPS = 1e-6
GDN_HEADS = 4
GDN_DK = 128
GDN_DV = 128
CONV_W = 4
GDN_QK_W = GDN_HEADS * GDN_DK
GDN_V_W = GDN_HEADS * GDN_DV
CONV_CH = 2 * GDN_QK_W + GDN_V_W
DA_HEADS = 4
DA_DQK = 64
DA_DV = 2 * DA_DQK
DA_QK_W = DA_HEADS * 2 * DA_DQK
DA_V_W = DA_HEADS * DA_DV
ROPE_DIM = DA_DQK // 4
ROPE_HALF = ROPE_DIM // 2
ROPE_THETA = 500000.0
N_GROUPS = 4
EXPERTS_PER_GROUP = 8
N_EXPERTS = N_GROUPS * EXPERTS_PER_GROUP
D_EXPERT = 256
LAM_INIT = 0.8 - 0.6 * math.exp(-0.3 * 0)

LANES = 128
GDN_CHUNK = 64
AB_W = LANES
PROJ_PAD_W = CONV_CH + GDN_V_W + 2 * DA_QK_W + DA_V_W + AB_W
VMEM_LIMIT = 56 * 1024 * 1024


_NN = (((1,), (0,)), ((), ()))
_NT = (((1,), (1,)), ((), ()))
_TN = (((0,), (0,)), ((), ()))


def _mm(a, b, dims=_NN, precise=False):
    if precise:
        return lax.dot_general(a.astype(F32), b.astype(F32), dims, preferred_element_type=F32,
                               precision=lax.Precision.HIGHEST)
    return lax.dot_general(a.astype(BF16), b.astype(BF16), dims, preferred_element_type=F32)


def _dot_nt(a, b):
    return _mm(a, b, _NT)


def _dot_f32(a, b):
    return _mm(a, b, _NN, True)


def _dot_nt_f32(a, b):
    return _mm(a, b, _NT, True)


def _sigmoid(x):
    return 1.0 / (1.0 + jnp.exp(-x))


def _softplus(x):
    return jnp.maximum(x, 0.0) + jnp.log(1.0 + jnp.exp(-jnp.abs(x)))


def _rms(x, w):
    return x * lax.rsqrt(jnp.mean(x * x, axis=-1, keepdims=True) + EPS) * w


def _rope(x, cs, sn_lo, sn_hi):
    w = x.shape[1]
    cs = jnp.concatenate([cs] * DA_HEADS, axis=1)
    sn_lo = jnp.concatenate([sn_lo] * DA_HEADS, axis=1)
    sn_hi = jnp.concatenate([sn_hi] * DA_HEADS, axis=1)
    up = pltpu.roll(x, w - ROPE_HALF, 1)
    dn = pltpu.roll(x, ROPE_HALF, 1)
    return x * cs + up * sn_lo + dn * sn_hi


def _proj_kernel(x_ref, wn_ref, w_ref, cs_ref, snl_ref, snh_ref,
                 c_ref, z_ref, ab_ref, q_ref, k_ref, v_ref, *, precise):
    h = _rms(x_ref[...], wn_ref[...])
    h = h if precise else h.astype(BF16)
    o = 0
    c_ref[...] = _mm(h, w_ref[:, o:o + CONV_CH], _NN, precise)
    o += CONV_CH
    z_ref[...] = _mm(h, w_ref[:, o:o + GDN_V_W], _NN, precise)
    o += GDN_V_W
    q = _mm(h, w_ref[:, o:o + DA_QK_W], _NN, precise)
    o += DA_QK_W
    k = _mm(h, w_ref[:, o:o + DA_QK_W], _NN, precise)
    o += DA_QK_W
    v_ref[...] = _mm(h, w_ref[:, o:o + DA_V_W], _NN, precise)
    o += DA_V_W
    ab_ref[...] = _mm(h, w_ref[:, o:o + AB_W], _NN, precise)
    cs, snl, snh = cs_ref[...], snl_ref[...], snh_ref[...]
    q_ref[...] = _rope(q, cs, snl, snh)
    k_ref[...] = _rope(k, cs, snl, snh)


def _proj(x, wn, w, tabs, tm):
    n = x.shape[0]
    assert n % tm == 0
    nt = tabs[0].shape[0] // tm
    row = lambda i: (i, 0)
    fixed = lambda i: (0, 0)
    tab = lambda i: (i % nt, 0)
    widths = (CONV_CH, GDN_V_W, AB_W, DA_QK_W, DA_QK_W, DA_V_W)
    return pl.pallas_call(
        functools.partial(_proj_kernel, precise=w.dtype == F32),
        grid=(n // tm,),
        in_specs=[pl.BlockSpec((tm, D_MODEL), row),
                  pl.BlockSpec((1, D_MODEL), fixed),
                  pl.BlockSpec((D_MODEL, PROJ_PAD_W), fixed),
                  pl.BlockSpec((tm, LANES), tab),
                  pl.BlockSpec((tm, LANES), tab),
                  pl.BlockSpec((tm, LANES), tab)],
        out_specs=[pl.BlockSpec((tm, wd), row) for wd in widths],
        out_shape=[jax.ShapeDtypeStruct((n, wd), F32) for wd in widths],
        compiler_params=pltpu.CompilerParams(dimension_semantics=("parallel",),
                                             vmem_limit_bytes=VMEM_LIMIT),
        name="proj",
    )(x, wn, w, *tabs)


def _rope_tables(pos):
    inv_freq = ROPE_THETA ** (-jnp.arange(ROPE_HALF, dtype=F32) * 2.0 / ROPE_DIM)
    ang = pos.astype(F32)[:, None] * inv_freq[None, :]
    cos, sin = jnp.cos(ang), jnp.sin(ang)
    t = pos.shape[0]
    one = jnp.ones((t, DA_DQK - ROPE_DIM), F32)
    zero = jnp.zeros((t, DA_DQK - ROPE_DIM), F32)
    zh = jnp.zeros((t, ROPE_HALF), F32)
    cs = jnp.concatenate([cos, cos, one] * 2, axis=1)
    snl = jnp.concatenate([-sin, zh, zero] * 2, axis=1)
    snh = jnp.concatenate([zh, sin, zero] * 2, axis=1)
    return cs, snl, snh


def _prep_w_in(w_in):
    c, z, a, b, dq, dk, dv = jnp.split(
        w_in, np.cumsum([CONV_CH, GDN_V_W, GDN_HEADS, GDN_HEADS, DA_QK_W, DA_QK_W]).tolist(), axis=-1)
    pad = jnp.zeros((D_MODEL, AB_W - 2 * GDN_HEADS), w_in.dtype)
    return jnp.concatenate([c, z, dq, dk, dv, a, b, pad], axis=-1)


def _gdn_kernel(c_ref, z_ref, ab_ref, tail_ref, s0_ref, wconv_ref, alog_ref, dtb_ref, wg_ref,
                o_ref, sfin_ref, cbuf, ybuf, s_scr, u0_scr, wq_scr, kw_scr, qkd_scr, el_scr, *, tt):
    C, H = GDN_CHUNK, GDN_HEADS
    R = H * C
    shift = C.bit_length() - 1
    t = pl.program_id(1)

    @pl.when(t == 0)
    def _():
        cbuf[0:8, :] = tail_ref[0]
        s_scr[...] = s0_ref[0]

    cbuf[8:8 + tt, :] = c_ref[0]
    acc = cbuf[8:8 + tt, :] * wconv_ref[CONV_W - 1:CONV_W, :]
    for i in range(CONV_W - 1):
        off = 8 - (CONV_W - 1) + i
        acc = acc + cbuf[off:off + tt, :] * wconv_ref[i:i + 1, :]
    ybuf[...] = acc * _sigmoid(acc)
    cbuf[0:8, :] = cbuf[tt:tt + 8, :]

    ii = lax.broadcasted_iota(jnp.int32, (R, R), 0)
    jj = lax.broadcasted_iota(jnp.int32, (R, R), 1)
    same_head = jnp.right_shift(ii, shift) == jnp.right_shift(jj, shift)
    strict = same_head & (ii > jj)
    causal = same_head & (ii >= jj)
    eye = (ii == jj).astype(F32)
    merge_masks = []
    s = 1
    while s < C:
        sh = s.bit_length() - 1
        merge_masks.append((jnp.right_shift(ii, sh + 1) == jnp.right_shift(jj, sh + 1))
                           & (jnp.right_shift(ii, sh) != jnp.right_shift(jj, sh)) & (ii > jj))
        s *= 2
    ci_ = lax.broadcasted_iota(jnp.int32, (C, C), 0)
    cj_ = lax.broadcasted_iota(jnp.int32, (C, C), 1)
    tri = (ci_ >= cj_).astype(F32)
    lane = lax.broadcasted_iota(jnp.int32, (R, LANES), 1)
    rhead = jnp.right_shift(lax.broadcasted_iota(jnp.int32, (R, LANES), 0), shift)
    ones8 = jnp.ones((8, LANES), F32)
    rep = lambda a: jnp.concatenate([a] * H, axis=0)
    pick = lambda a, off: jnp.sum(jnp.where(lane == rhead + off, a, 0.0), axis=-1, keepdims=True)

    def stacked(rows, base, scale):
        parts = []
        for h in range(H):
            x = ybuf[rows, base + h * GDN_DK:base + (h + 1) * GDN_DK]
            if scale is not None:
                x = x * lax.rsqrt(jnp.sum(x * x, axis=-1, keepdims=True) + EPS) * scale
            parts.append(x)
        return jnp.concatenate(parts, axis=0)

    chunks = range(tt // C)
    low, rhs, qes = [], [], []
    for ci in chunks:
        rows = slice(ci * C, (ci + 1) * C)
        ab = ab_ref[0, rows, :]
        gfull = -jnp.exp(alog_ref[...]) * _softplus(ab + dtb_ref[...])
        bfull = _sigmoid(ab)
        gcum = _dot_f32(tri, gfull)
        glast = gcum[C - 1:C, :]
        g_m = jnp.where(lane == rhead, rep(gcum), 0.0)
        g_col = jnp.sum(g_m, axis=-1, keepdims=True)
        g_row = _dot_nt_f32(ones8, g_m)[0:1, :]
        beta = pick(rep(bfull), H)
        gl_col = pick(jnp.broadcast_to(glast, (R, LANES)), 0)
        decay = jnp.exp(jnp.where(causal, g_col - g_row, -jnp.inf))
        q = stacked(rows, 0, GDN_DK ** -0.5)
        k = stacked(rows, GDN_QK_W, 1.0)
        v = stacked(rows, 2 * GDN_QK_W, None)
        kq = _mm(jnp.concatenate([k, q], axis=0), k, _NT)
        low.append(jnp.where(strict, kq[0:R] * decay * beta, 0.0))
        e_g = jnp.exp(g_col)
        rhs.append(jnp.concatenate([beta * v, (beta * e_g) * k], axis=1).astype(BF16))
        qes.append(q * e_g)
        kw_scr[ci] = (k * jnp.exp(gl_col - g_col)).astype(BF16)
        qkd_scr[ci] = jnp.where(causal, kq[R:2 * R] * decay, 0.0).astype(BF16)
        el_scr[ci] = jnp.broadcast_to(jnp.exp(glast), (8, LANES))
    tinv = [eye - jnp.where(merge_masks[0], low[ci], 0.0) for ci in chunks]
    for m in merge_masks[1:]:
        half = [_mm(tinv[ci], jnp.where(m, low[ci], 0.0)) for ci in chunks]
        tinv = [tinv[ci] - _mm(half[ci], tinv[ci]) for ci in chunks]
    for ci in chunks:
        w2 = _mm(tinv[ci], rhs[ci])
        u0_scr[ci] = w2[:, 0:GDN_DV]
        wk = w2[:, GDN_DV:]
        wq_scr[ci] = jnp.concatenate(
            [x[h * C:(h + 1) * C] for h in range(H) for x in (wk, qes[ci])], axis=0).astype(BF16)

    st = [s_scr[h] for h in range(H)]
    for ci in range(tt // C):
        rows = slice(ci * C, (ci + 1) * C)
        us, os_ = [], []
        for h in range(H):
            r = jnp.dot(wq_scr[ci, 2 * h * C:2 * (h + 1) * C, :], st[h].astype(BF16),
                        preferred_element_type=F32)
            us.append(u0_scr[ci, h * C:(h + 1) * C, :] - r[0:C])
            os_.append(r[C:2 * C])
        intra = jnp.dot(qkd_scr[ci], jnp.concatenate(us, axis=0).astype(BF16),
                        preferred_element_type=F32)
        el = el_scr[ci]
        for h in range(H):
            st[h] = st[h] * el[0:1, h:h + 1] + _mm(kw_scr[ci, h * C:(h + 1) * C, :], us[h], _TN)
            o = os_[h] + intra[h * C:(h + 1) * C]
            zz = z_ref[0, rows, h * GDN_DV:(h + 1) * GDN_DV]
            o_ref[0, rows, h * GDN_DV:(h + 1) * GDN_DV] = _rms(o, wg_ref[...]) * (zz * _sigmoid(zz))
    for h in range(H):
        s_scr[h] = st[h]

    @pl.when(t == pl.num_programs(1) - 1)
    def _():
        sfin_ref[0] = s_scr[...]


def _gdn_step_kernel(c_ref, z_ref, ab_ref, cs_ref, s_ref, wconv_ref, alog_ref, dtb_ref, wg_ref,
                     o_ref, snew_ref, *, nb):
    H = GDN_HEADS
    acc = c_ref[...] * wconv_ref[CONV_W - 1:CONV_W, :]
    for i in range(CONV_W - 1):
        acc = acc + cs_ref[i] * wconv_ref[i:i + 1, :]
    y = acc * _sigmoid(acc)
    ab = ab_ref[...]
    e_g = jnp.exp(-jnp.exp(alog_ref[...]) * _softplus(ab + dtb_ref[...]))
    bfull = _sigmoid(ab)
    unit = lambda x: x * lax.rsqrt(jnp.sum(x * x, axis=-1, keepdims=True) + EPS)
    qs = [unit(y[:, h * GDN_DK:(h + 1) * GDN_DK]) * (GDN_DK ** -0.5) for h in range(H)]
    ks = [unit(y[:, GDN_QK_W + h * GDN_DK:GDN_QK_W + (h + 1) * GDN_DK]) for h in range(H)]
    vs = [y[:, 2 * GDN_QK_W + h * GDN_DV:2 * GDN_QK_W + (h + 1) * GDN_DV] for h in range(H)]
    cols = jnp.concatenate(ks + qs, axis=0).T
    for h in range(H):
        ks_rows, qs_rows = [], []
        for s in range(nb):
            state = s_ref[s, h]
            kcol = cols[:, h * nb + s:h * nb + s + 1]
            qcol = cols[:, (H + h) * nb + s:(H + h) * nb + s + 1]
            ks_rows.append(jnp.sum(kcol * state, axis=0, keepdims=True))
            qs_rows.append(jnp.sum(qcol * state, axis=0, keepdims=True))
        eg_h = e_g[:, h:h + 1]
        u = bfull[:, H + h:H + h + 1] * (vs[h] - eg_h * jnp.concatenate(ks_rows, axis=0))
        o = (eg_h * jnp.concatenate(qs_rows, axis=0)
             + jnp.sum(qs[h] * ks[h], axis=-1, keepdims=True) * u)
        eg_b = jnp.broadcast_to(eg_h, (nb, GDN_DV))
        for s in range(nb):
            kcol = cols[:, h * nb + s:h * nb + s + 1]
            snew_ref[s, h] = eg_b[s:s + 1, :] * s_ref[s, h] + kcol * u[s:s + 1, :]
        zz = z_ref[:, h * GDN_DV:(h + 1) * GDN_DV]
        o_ref[:, h * GDN_DV:(h + 1) * GDN_DV] = _rms(o, wg_ref[...]) * (zz * _sigmoid(zz))


def _gdn_step(c, z, ab, conv_state, s, wconv, alog, dtb, wg):
    bs = c.shape[0]
    nb = LANES // (2 * GDN_HEADS)
    assert bs % nb == 0
    row = lambda i: (i, 0)
    fixed = lambda i: (0, 0)
    state = pl.BlockSpec((nb, GDN_HEADS, GDN_DK, GDN_DV), lambda i: (i, 0, 0, 0))
    return pl.pallas_call(
        functools.partial(_gdn_step_kernel, nb=nb),
        grid=(bs // nb,),
        in_specs=[pl.BlockSpec((nb, CONV_CH), row),
                  pl.BlockSpec((nb, GDN_V_W), row),
                  pl.BlockSpec((nb, AB_W), row),
                  pl.BlockSpec((CONV_W - 1, nb, CONV_CH), lambda i: (0, i, 0)),
                  state,
                  pl.BlockSpec((CONV_W, CONV_CH), fixed),
                  pl.BlockSpec((1, LANES), fixed),
                  pl.BlockSpec((1, LANES), fixed),
                  pl.BlockSpec((1, GDN_DV), fixed)],
        out_specs=[pl.BlockSpec((nb, GDN_V_W), row), state],
        out_shape=[jax.ShapeDtypeStruct((bs, GDN_V_W), F32),
                   jax.ShapeDtypeStruct(s.shape, F32)],
        compiler_params=pltpu.CompilerParams(dimension_semantics=("parallel",),
                                             vmem_limit_bytes=VMEM_LIMIT),
        name="gdn_step",
    )(c, z, ab, conv_state, s, wconv, alog, dtb, wg)


def _gdn(c, z, ab, tail, s0, wconv, alog, dtb, wg, tt):
    b, t, _ = c.shape
    assert t % tt == 0 and tt % GDN_CHUNK == 0
    nc, rs = tt // GDN_CHUNK, GDN_HEADS * GDN_CHUNK
    seq = lambda i, j: (i, j, 0)
    fixed2 = lambda i, j: (0, 0)
    tail_map = (lambda i, j: (i, 0, 0)) if tail.shape[0] == b else (lambda i, j: (0, 0, 0))
    s0_map = (lambda i, j: (i, 0, 0, 0)) if s0.shape[0] == b else (lambda i, j: (0, 0, 0, 0))
    return pl.pallas_call(
        functools.partial(_gdn_kernel, tt=tt),
        grid=(b, t // tt),
        in_specs=[pl.BlockSpec((1, tt, CONV_CH), seq),
                  pl.BlockSpec((1, tt, GDN_V_W), seq),
                  pl.BlockSpec((1, tt, AB_W), seq),
                  pl.BlockSpec((1, 8, CONV_CH), tail_map),
                  pl.BlockSpec((1, GDN_HEADS, GDN_DK, GDN_DV), s0_map),
                  pl.BlockSpec((CONV_W, CONV_CH), fixed2),
                  pl.BlockSpec((1, LANES), fixed2),
                  pl.BlockSpec((1, LANES), fixed2),
                  pl.BlockSpec((1, GDN_DV), fixed2)],
        out_specs=[pl.BlockSpec((1, tt, GDN_V_W), seq),
                   pl.BlockSpec((1, GDN_HEADS, GDN_DK, GDN_DV), lambda i, j: (i, 0, 0, 0))],
        out_shape=[jax.ShapeDtypeStruct((b, t, GDN_V_W), F32),
                   jax.ShapeDtypeStruct((b, GDN_HEADS, GDN_DK, GDN_DV), F32)],
        scratch_shapes=[pltpu.VMEM((tt + 8, CONV_CH), F32),
                        pltpu.VMEM((tt, CONV_CH), F32),
                        pltpu.VMEM((GDN_HEADS, GDN_DK, GDN_DV), F32),
                        pltpu.VMEM((nc, rs, GDN_DV), F32),
                        pltpu.VMEM((nc, 2 * rs, GDN_DK), BF16),
                        pltpu.VMEM((nc, rs, GDN_DK), BF16),
                        pltpu.VMEM((nc, rs, rs), BF16),
                        pltpu.VMEM((nc, 8, LANES), F32)],
        compiler_params=pltpu.CompilerParams(dimension_semantics=("parallel", "arbitrary"),
                                             vmem_limit_bytes=VMEM_LIMIT),
        name="gdn",
    )(c, z, ab, tail, s0, wconv, alog, dtb, wg)


def _lambda(lam_ref):
    lv = lam_ref[...]
    s1 = jnp.sum(lv[0:1, :] * lv[1:2, :], axis=-1, keepdims=True)
    s2 = jnp.sum(lv[2:3, :] * lv[3:4, :], axis=-1, keepdims=True)
    return jnp.exp(s1) - jnp.exp(s2) + LAM_INIT


def _split_components(q):
    lane = lax.broadcasted_iota(jnp.int32, q.shape, 1)
    first = (lane % DA_DV) < DA_DQK
    return jnp.where(first, q, 0.0).astype(BF16), jnp.where(first, 0.0, q).astype(BF16)


def _online_update(state, s, vb):
    m, l, acc = state
    m_new = jnp.maximum(m, jnp.max(s, axis=-1, keepdims=True))
    p = jnp.exp(s - m_new)
    alpha = jnp.exp(m - m_new)
    return (m_new, alpha * l + jnp.sum(p, axis=-1, keepdims=True),
            alpha * acc + jnp.dot(p.astype(BF16), vb, preferred_element_type=F32))


def _attn_prompt_kernel(q_ref, k_ref, v_ref, km_ref, vm_ref, lam_ref, wsub_ref, o_ref, *, bq, bk):
    qi = pl.program_id(2)
    q1, q2 = _split_components(q_ref[0] * (DA_DQK ** -0.5))
    neg = jnp.full((bq, 1), -jnp.inf, F32)
    zero = jnp.zeros((bq, 1), F32)
    zacc = jnp.zeros((bq, DA_DV), F32)
    init = ((neg, zero, zacc), (neg, zero, zacc))

    def block(kblk, vblk, mask, st):
        kb, vb = kblk.astype(BF16), vblk.astype(BF16)
        s1, s2 = _dot_nt(q1, kb), _dot_nt(q2, kb)
        if mask is not None:
            s1 = jnp.where(mask, s1, -jnp.inf)
            s2 = jnp.where(mask, s2, -jnp.inf)
        return (_online_update(st[0], s1, vb), _online_update(st[1], s2, vb))

    n_full = lax.shift_right_logical(qi * bq, bk.bit_length() - 1)

    def body(kb, st):
        r0 = pl.multiple_of(kb * bk, bk)
        return block(k_ref[0, pl.ds(r0, bk), :], v_ref[0, pl.ds(r0, bk), :], None, st)

    st = lax.fori_loop(0, n_full, body, init)
    r0 = pl.multiple_of(n_full * bk, bk)
    kcat = jnp.concatenate([km_ref[...], k_ref[0, pl.ds(r0, bk), :]], axis=0)
    vcat = jnp.concatenate([vm_ref[...], v_ref[0, pl.ds(r0, bk), :]], axis=0)
    ri = lax.broadcasted_iota(jnp.int32, (bq, LANES + bk), 0)
    ci = lax.broadcasted_iota(jnp.int32, (bq, LANES + bk), 1)
    seen = (ci < N_META) | ((ci >= LANES) & (ci - LANES <= ri + (qi * bq - n_full * bk)))
    st = block(kcat, vcat, seen, st)
    (m1, l1, a1), (m2, l2, a2) = st
    o = a1 / l1 - _lambda(lam_ref) * (a2 / l2)
    o_ref[0] = _rms(o, wsub_ref[...]) * (1.0 - LAM_INIT)


def _attn_prompt(q, k, v, k_meta, v_meta, lam_vecs, wsub, bq, bk):
    b, t, _ = q.shape
    assert t % bk == 0 and bk % bq == 0 and k_meta.shape[0] == LANES
    return pl.pallas_call(
        functools.partial(_attn_prompt_kernel, bq=bq, bk=bk),
        grid=(b, DA_HEADS, t // bq),
        in_specs=[pl.BlockSpec((1, bq, DA_DV), lambda i, h, j: (i, j, h)),
                  pl.BlockSpec((1, t, DA_DV), lambda i, h, j: (i, 0, h)),
                  pl.BlockSpec((1, t, DA_DV), lambda i, h, j: (i, 0, h)),
                  pl.BlockSpec((LANES, DA_DV), lambda i, h, j: (0, h)),
                  pl.BlockSpec((LANES, DA_DV), lambda i, h, j: (0, h)),
                  pl.BlockSpec((4, DA_DQK), lambda i, h, j: (0, 0)),
                  pl.BlockSpec((1, DA_DV), lambda i, h, j: (0, 0))],
        out_specs=pl.BlockSpec((1, bq, DA_DV), lambda i, h, j: (i, j, h)),
        out_shape=jax.ShapeDtypeStruct((b, t, DA_V_W), F32),
        compiler_params=pltpu.CompilerParams(
            dimension_semantics=("parallel", "parallel", "arbitrary"),
            vmem_limit_bytes=VMEM_LIMIT),
        name="attn_prompt",
    )(q, k, v, k_meta, v_meta, lam_vecs, wsub)


def _attn_sample_kernel(pt_ref, q_ref, kn_ref, vn_ref, lam_ref, wsub_ref, *rest, pps):
    k_refs, v_refs = rest[:pps], rest[pps:2 * pps]
    o_ref, m_scr, l_scr, acc_scr = rest[2 * pps:]
    step = pl.program_id(1)
    nrow = 2 * DA_HEADS
    twice = lambda x: jnp.concatenate([x, x], axis=0)
    rowq = lax.broadcasted_iota(jnp.int32, (nrow, DA_DV), 0)
    laneq = lax.broadcasted_iota(jnp.int32, (nrow, DA_DV), 1)
    own_comp = (laneq // DA_DQK) == (rowq // DA_HEADS)
    qm = jnp.where(own_comp, twice(q_ref[0]) * (DA_DQK ** -0.5), 0.0)
    page_rows = k_refs[0].shape[1]
    rowp = lax.broadcasted_iota(jnp.int32, (nrow, page_rows * pps), 0)
    colp = lax.broadcasted_iota(jnp.int32, (nrow, page_rows * pps), 1)
    own_head = (colp % DA_HEADS) == (rowp % DA_HEADS)

    def rows3(x):
        hi = x.astype(BF16).astype(F32)
        mid = (x - hi).astype(BF16).astype(F32)
        return jnp.concatenate([hi, mid, x - hi - mid], axis=0).astype(BF16)

    def unrows3(r):
        return (r[0:nrow] + r[nrow:2 * nrow]) + r[2 * nrow:3 * nrow]

    def hi_lo(x):
        hi = x.astype(BF16)
        return hi, (x - hi.astype(F32)).astype(BF16)

    @pl.when(step == 0)
    def _():
        s_new = jnp.sum(qm * twice(kn_ref[0]), axis=-1, keepdims=True)
        m_scr[...] = s_new
        l_scr[...] = jnp.ones_like(s_new)
        acc_scr[...] = twice(vn_ref[0])

    q3 = rows3(qm)
    scores = []
    for r in k_refs:
        k_hi, k_lo = hi_lo(r[0])
        scores.append(unrows3(_dot_nt(q3, k_hi) + _dot_nt(q3, k_lo)))
    s = jnp.where(own_head, jnp.concatenate(scores, axis=1), -jnp.inf)
    m_old = m_scr[...]
    m_new = jnp.maximum(m_old, jnp.max(s, axis=-1, keepdims=True))
    p = jnp.exp(s - m_new)
    alpha = jnp.exp(m_old - m_new)
    p3 = rows3(p)
    pv = jnp.zeros((3 * nrow, DA_DV), F32)
    for j, r in enumerate(v_refs):
        v_hi, v_lo = hi_lo(r[0])
        pj = p3[:, j * page_rows:(j + 1) * page_rows]
        pv = pv + (jnp.dot(pj, v_hi, preferred_element_type=F32)
                   + jnp.dot(pj, v_lo, preferred_element_type=F32))
    m_scr[...] = m_new
    l_scr[...] = alpha * l_scr[...] + jnp.sum(p, axis=-1, keepdims=True)
    acc_scr[...] = alpha * acc_scr[...] + unrows3(pv)

    @pl.when(step == pl.num_programs(1) - 1)
    def _():
        w = acc_scr[...] / l_scr[...]
        o = w[0:DA_HEADS, :] - _lambda(lam_ref) * w[DA_HEADS:nrow, :]
        o_ref[0] = _rms(o, wsub_ref[...]) * (1.0 - LAM_INIT)


def _attn_sample(q, k_new, v_new, cache_k, cache_v, page_table, lam_vecs, wsub, pps):
    bs = q.shape[0]
    n_pages = page_table.shape[1]
    page_rows = cache_k.shape[1]
    assert n_pages % pps == 0
    tok = lambda i, s, pt: (i, 0, 0)
    fixed = lambda i, s, pt: (0, 0)

    def page_spec(j):
        return pl.BlockSpec((1, page_rows, DA_DV), lambda i, s, pt: (pt[i, s * pps + j], 0, 0))

    grid_spec = pltpu.PrefetchScalarGridSpec(
        num_scalar_prefetch=1,
        grid=(bs, n_pages // pps),
        in_specs=[pl.BlockSpec((1, DA_HEADS, DA_DV), tok),
                  pl.BlockSpec((1, DA_HEADS, DA_DV), tok),
                  pl.BlockSpec((1, DA_HEADS, DA_DV), tok),
                  pl.BlockSpec((4, DA_DQK), fixed),
                  pl.BlockSpec((1, DA_DV), fixed)]
                 + [page_spec(j) for j in range(pps)] * 2,
        out_specs=pl.BlockSpec((1, DA_HEADS, DA_DV), tok),
        scratch_shapes=[pltpu.VMEM((2 * DA_HEADS, 1), F32),
                        pltpu.VMEM((2 * DA_HEADS, 1), F32),
                        pltpu.VMEM((2 * DA_HEADS, DA_DV), F32)])
    return pl.pallas_call(
        functools.partial(_attn_sample_kernel, pps=pps),
        grid_spec=grid_spec,
        out_shape=jax.ShapeDtypeStruct((bs, DA_HEADS, DA_DV), F32),
        compiler_params=pltpu.CompilerParams(dimension_semantics=("parallel", "arbitrary"),
                                             vmem_limit_bytes=VMEM_LIMIT),
        name="attn_sample",
    )(page_table, q, k_new, v_new, lam_vecs, wsub, *([cache_k] * pps), *([cache_v] * pps))


def _route(logits):
    lane = lax.broadcasted_iota(jnp.int32, logits.shape, 1).astype(F32)
    big = float(LANES)
    ninf = -jnp.inf

    def top(mask):
        val = jnp.where(mask, logits, ninf)
        mx = jnp.max(val, axis=-1, keepdims=True)
        idx = jnp.min(jnp.where(mask & (val == mx), lane, big), axis=-1, keepdims=True)
        return mx, idx

    gmask = (lane >= N_EXPERTS) & (lane < N_EXPERTS + N_GROUPS)
    gmax, gidx = top(gmask)
    g_w = 1.0 / jnp.sum(jnp.where(gmask, jnp.exp(logits - gmax), 0.0), axis=-1, keepdims=True)
    g_sel = gidx - N_EXPERTS
    emask = (lane >= g_sel * EXPERTS_PER_GROUP) & (lane < (g_sel + 1) * EXPERTS_PER_GROUP)
    v1, i1 = top(emask)
    v2, i2 = top(emask & (lane != i1))
    e21 = jnp.exp(v2 - v1)
    p1 = 1.0 / (1.0 + e21)
    return jnp.where(lane == i1, p1 * g_w, 0.0) + jnp.where(lane == i2, e21 * p1 * g_w, 0.0)


def _ffn_kernel(g_ref, a_ref, x_ref, wo_ref, wnf_ref, wr_ref, wg_ref, wu_ref, wd_ref, wfin_ref,
                y_ref, x1_scr, xn_scr, comb_scr, acc_scr, *, precise):
    e = pl.program_id(1)

    @pl.when(e == 0)
    def _():
        mix = (_mm(g_ref[...], wo_ref[0:GDN_V_W, :], _NN, precise)
               + _mm(a_ref[...], wo_ref[GDN_V_W:, :], _NN, precise))
        x1 = x_ref[...] + mix
        xn = _rms(x1, wnf_ref[...])
        x1_scr[...] = x1
        xn_scr[...] = xn.astype(BF16)
        comb_scr[...] = _route(_dot_f32(xn, wr_ref[...]))
        acc_scr[...] = jnp.zeros_like(acc_scr)

    xn = xn_scr[...]
    hg = jnp.dot(xn, wg_ref[0], preferred_element_type=F32)
    hu = jnp.dot(xn, wu_ref[0], preferred_element_type=F32)
    comb = comb_scr[...]
    lane = lax.broadcasted_iota(jnp.int32, comb.shape, 1)
    cw = jnp.sum(jnp.where(lane == e, comb, 0.0), axis=-1, keepdims=True)
    hh = (hg * _sigmoid(hg)) * hu * cw
    acc_scr[...] += jnp.dot(hh.astype(BF16), wd_ref[0], preferred_element_type=F32)

    @pl.when(e == pl.num_programs(1) - 1)
    def _():
        y_ref[...] = _rms(x1_scr[...] + acc_scr[...], wfin_ref[...])


def _ffn(g_out, a_out, x, wo, wnf, wr, wg, wu, wd, wfin, tm):
    n = x.shape[0]
    assert n % tm == 0
    row = lambda i, e: (i, 0)
    fixed = lambda i, e: (0, 0)
    exp = lambda i, e: (e, 0, 0)
    return pl.pallas_call(
        functools.partial(_ffn_kernel, precise=wo.dtype == F32),
        grid=(n // tm, N_EXPERTS),
        in_specs=[pl.BlockSpec((tm, GDN_V_W), row),
                  pl.BlockSpec((tm, DA_V_W), row),
                  pl.BlockSpec((tm, D_MODEL), row),
                  pl.BlockSpec((GDN_V_W + DA_V_W, D_MODEL), fixed),
                  pl.BlockSpec((1, D_MODEL), fixed),
                  pl.BlockSpec((D_MODEL, LANES), fixed),
                  pl.BlockSpec((1, D_MODEL, D_EXPERT), exp),
                  pl.BlockSpec((1, D_MODEL, D_EXPERT), exp),
                  pl.BlockSpec((1, D_EXPERT, D_MODEL), exp),
                  pl.BlockSpec((1, D_MODEL), fixed)],
        out_specs=pl.BlockSpec((tm, D_MODEL), row),
        out_shape=jax.ShapeDtypeStruct((n, D_MODEL), F32),
        scratch_shapes=[pltpu.VMEM((tm, D_MODEL), F32),
                        pltpu.VMEM((tm, D_MODEL), BF16),
                        pltpu.VMEM((tm, LANES), F32),
                        pltpu.VMEM((tm, D_MODEL), F32)],
        compiler_params=pltpu.CompilerParams(dimension_semantics=("parallel", "arbitrary"),
                                             vmem_limit_bytes=VMEM_LIMIT),
        name="ffn",
    )(g_out, a_out, x, wo, wnf, wr, wg, wu, wd, wfin)


def kernel(x_prompt, x_sample, cache_k, cache_v, state_conv, state_gdn, page_table, meta_tokens,
           w_norm_mix, w_in, w_conv, a_log, dt_bias, w_gnorm, lambda_q1, lambda_k1, lambda_q2,
           lambda_k2, w_subln, w_out, w_norm_ffn, w_group, w_expert, w_gate, w_up, w_down, w_final):
    l = 0
    bp, seq = x_prompt.shape[0], x_prompt.shape[1]
    bs = x_sample.shape[0]
    n_pool, page = cache_k.shape[1], cache_k.shape[2]
    past = page_table.shape[1] * page
    C = GDN_CHUNK

    w_in_p = _prep_w_in(w_in[l])
    wn_mix = w_norm_mix[l][None, :]
    pad_lanes = lambda v: jnp.pad(v, (0, LANES - v.shape[0]))[None, :]
    alog, dtb = pad_lanes(a_log[l]), pad_lanes(dt_bias[l])
    wgn = w_gnorm[l][None, :]
    lam_vecs = jnp.stack([lambda_q1[l], lambda_k1[l], lambda_q2[l], lambda_k2[l]])
    wsub = w_subln[l][None, :]
    w_router = jnp.pad(jnp.concatenate([w_expert[l], w_group[l]], axis=1),
                       ((0, 0), (0, LANES - N_EXPERTS - N_GROUPS)))
    ffn_w = (w_norm_ffn[l][None, :], w_router, w_gate[l].astype(BF16),
             w_up[l].astype(BF16), w_down[l].astype(BF16), w_final[None, :])

    pos_small = jnp.concatenate([jnp.arange(N_META), jnp.full((bs,), past)])
    rows_small = jnp.concatenate([meta_tokens, x_sample[:, 0, :]], axis=0)
    c_s, z_s, ab_s, q_s, k_s, v_s = _proj(rows_small, wn_mix, w_in_p, _rope_tables(pos_small),
                                           N_META + bs)
    xp = x_prompt.reshape(bp * seq, D_MODEL)
    c_p, z_p, ab_p, q_p, k_p, v_p = _proj(xp, wn_mix, w_in_p.astype(BF16),
                                           _rope_tables(N_META + jnp.arange(seq)), 512)

    front = lambda a: jnp.pad(a[:N_META], ((C - N_META, 0), (0, 0)))[None]
    zero_tail = jnp.zeros((1, 8, CONV_CH), F32)
    zero_state = jnp.zeros((1, GDN_HEADS, GDN_DK, GDN_DV), F32)
    _, s_meta = _gdn(front(c_s), front(z_s), front(ab_s), zero_tail, zero_state,
                     w_conv[l], alog, dtb, wgn, C)
    tail_meta = jnp.pad(c_s[N_META - (CONV_W - 1):N_META], ((8 - (CONV_W - 1), 0), (0, 0)))[None]
    c_p3 = c_p.reshape(bp, seq, CONV_CH)
    g_out, p_gdn = _gdn(c_p3, z_p.reshape(bp, seq, GDN_V_W), ab_p.reshape(bp, seq, AB_W),
                        tail_meta, s_meta, w_conv[l], alog, dtb, wgn, 256)
    p_conv = c_p3[:, seq - (CONV_W - 1):, :]

    with_meta = lambda m, p: jnp.concatenate(
        [jnp.broadcast_to(m[None, :N_META], (bp, N_META, m.shape[1])), p.reshape(bp, seq, -1)], axis=1)
    p_k = with_meta(k_s, k_p)
    p_v = with_meta(v_s, v_p)
    meta_blk = lambda m: jnp.pad(m[:N_META], ((0, LANES - N_META), (0, 0)))
    a_out = _attn_prompt(q_p.reshape(bp, seq, DA_QK_W), k_p.reshape(bp, seq, DA_QK_W),
                         v_p.reshape(bp, seq, DA_V_W), meta_blk(k_s), meta_blk(v_s),
                         lam_vecs, wsub, 256, 512)

    y_prompt = _ffn(g_out.reshape(bp * seq, GDN_V_W), a_out.reshape(bp * seq, DA_V_W), xp,
                    w_out[l].astype(BF16), *ffn_w, 1024).reshape(bp, seq, D_MODEL)

    sm = lambda a: a[N_META:]
    g_s, s_gdn = _gdn_step(sm(c_s), sm(z_s), sm(ab_s), jnp.swapaxes(state_conv[l], 0, 1),
                           state_gdn[l], w_conv[l], alog, dtb, wgn)
    s_conv = jnp.concatenate([state_conv[l][:, 1:, :], sm(c_s)[:, None, :]], axis=1)
    heads = lambda a: sm(a).reshape(bs, DA_HEADS, DA_DV)
    a_s = _attn_sample(heads(q_s), heads(k_s), heads(v_s),
                       cache_k[l].reshape(n_pool, page * DA_HEADS, 2 * DA_DQK),
                       cache_v[l].reshape(n_pool, page * DA_HEADS, DA_DV),
                       page_table, lam_vecs, wsub, 32)
    y_sample = _ffn(g_s, a_s.reshape(bs, DA_V_W), x_sample[:, 0, :], w_out[l], *ffn_w,
                    bs).reshape(bs, 1, D_MODEL)

    return (y_prompt, y_sample, p_conv[None], p_gdn[None],
            p_k.reshape(1, bp, seq + N_META, DA_HEADS, 2 * DA_DQK),
            p_v.reshape(1, bp, seq + N_META, DA_HEADS, DA_DV),
            s_conv[None], s_gdn[None],
            sm(k_s).reshape(1, bs, 1, DA_HEADS, 2 * DA_DQK),
            sm(v_s).reshape(1, bs, 1, DA_HEADS, DA_DV))
```

```python
import functools
import math

import jax
import jax.numpy as jnp
import numpy as np
from jax import lax
from jax.experimental import pallas as pl
from jax.experimental.pallas import tpu as pltpu

F32 = jnp.float32
BF16 = jnp.bfloat16

D_MODEL = 1024
N_META = 16
EPS = 1e-6
GDN_HEADS = 4
GDN_DK = 128
GDN_DV = 128
CONV_W = 4
GDN_QK_W = GDN_HEADS * GDN_DK
GDN_V_W = GDN_HEADS * GDN_DV
CONV_CH = 2 * GDN_QK_W + GDN_V_W
DA_HEADS = 4
DA_DQK = 64
DA_DV = 2 * DA_DQK
DA_QK_W = DA_HEADS * 2 * DA_DQK
DA_V_W = DA_HEADS * DA_DV
ROPE_DIM = DA_DQK // 4
ROPE_HALF = ROPE_DIM // 2
ROPE_THETA = 500000.0
N_GROUPS = 4
EXPERTS_PER_GROUP = 8
N_EXPERTS = N_GROUPS * EXPERTS_PER_GROUP
D_EXPERT = 256
LAM_INIT = 0.8 - 0.6 * math.exp(-0.3 * 0)

LANES = 128
GDN_CHUNK = 64
AB_W = LANES
PROJ_PAD_W = CONV_CH + GDN_V_W + 2 * DA_QK_W + DA_V_W + AB_W
VMEM_LIMIT = 56 * 1024 * 1024


_NN = (((1,), (0,)), ((), ()))
_NT = (((1,), (1,)), ((), ()))
_TN = (((0,), (0,)), ((), ()))


def _mm(a, b, dims=_NN, precise=False):
    if precise:
        return lax.dot_general(a.astype(F32), b.astype(F32), dims, preferred_element_type=F32,
                               precision=lax.Precision.HIGHEST)
    return lax.dot_general(a.astype(BF16), b.astype(BF16), dims, preferred_element_type=F32)


def _dot_nt(a, b):
    return _mm(a, b, _NT)


def _dot_f32(a, b):
    return _mm(a, b, _NN, True)


def _dot_nt_f32(a, b):
    return _mm(a, b, _NT, True)


def _sigmoid(x):
    return 1.0 / (1.0 + jnp.exp(-x))


def _softplus(x):
    return jnp.maximum(x, 0.0) + jnp.log(1.0 + jnp.exp(-jnp.abs(x)))


def _rms(x, w):
    return x * lax.rsqrt(jnp.mean(x * x, axis=-1, keepdims=True) + EPS) * w


def _rope(x, cs, sn_lo, sn_hi):
    w = x.shape[1]
    cs = jnp.concatenate([cs] * DA_HEADS, axis=1)
    sn_lo = jnp.concatenate([sn_lo] * DA_HEADS, axis=1)
    sn_hi = jnp.concatenate([sn_hi] * DA_HEADS, axis=1)
    up = pltpu.roll(x, w - ROPE_HALF, 1)
    dn = pltpu.roll(x, ROPE_HALF, 1)
    return x * cs + up * sn_lo + dn * sn_hi


def _proj_kernel(x_ref, wn_ref, w_ref, cs_ref, snl_ref, snh_ref,
                 c_ref, z_ref, ab_ref, q_ref, k_ref, v_ref, *, precise):
    h = _rms(x_ref[...], wn_ref[...])
    h = h if precise else h.astype(BF16)
    o = 0
    c_ref[...] = _mm(h, w_ref[:, o:o + CONV_CH], _NN, precise)
    o += CONV_CH
    z_ref[...] = _mm(h, w_ref[:, o:o + GDN_V_W], _NN, precise)
    o += GDN_V_W
    q = _mm(h, w_ref[:, o:o + DA_QK_W], _NN, precise)
    o += DA_QK_W
    k = _mm(h, w_ref[:, o:o + DA_QK_W], _NN, precise)
    o += DA_QK_W
    v_ref[...] = _mm(h, w_ref[:, o:o + DA_V_W], _NN, precise)
    o += DA_V_W
    ab_ref[...] = _mm(h, w_ref[:, o:o + AB_W], _NN, precise)
    cs, snl, snh = cs_ref[...], snl_ref[...], snh_ref[...]
    q_ref[...] = _rope(q, cs, snl, snh)
    k_ref[...] = _rope(k, cs, snl, snh)


def _proj(x, wn, w, tabs, tm):
    n = x.shape[0]
    assert n % tm == 0
    nt = tabs[0].shape[0] // tm
    row = lambda i: (i, 0)
    fixed = lambda i: (0, 0)
    tab = lambda i: (i % nt, 0)
    widths = (CONV_CH, GDN_V_W, AB_W, DA_QK_W, DA_QK_W, DA_V_W)
    return pl.pallas_call(
        functools.partial(_proj_kernel, precise=w.dtype == F32),
        grid=(n // tm,),
        in_specs=[pl.BlockSpec((tm, D_MODEL), row),
                  pl.BlockSpec((1, D_MODEL), fixed),
                  pl.BlockSpec((D_MODEL, PROJ_PAD_W), fixed),
                  pl.BlockSpec((tm, LANES), tab),
                  pl.BlockSpec((tm, LANES), tab),
                  pl.BlockSpec((tm, LANES), tab)],
        out_specs=[pl.BlockSpec((tm, wd), row) for wd in widths],
        out_shape=[jax.ShapeDtypeStruct((n, wd), F32) for wd in widths],
        compiler_params=pltpu.CompilerParams(dimension_semantics=("parallel",),
                                             vmem_limit_bytes=VMEM_LIMIT),
        name="proj",
    )(x, wn, w, *tabs)


def _rope_tables(pos):
    inv_freq = ROPE_THETA ** (-jnp.arange(ROPE_HALF, dtype=F32) * 2.0 / ROPE_DIM)
    ang = pos.astype(F32)[:, None] * inv_freq[None, :]
    cos, sin = jnp.cos(ang), jnp.sin(ang)
    t = pos.shape[0]
    one = jnp.ones((t, DA_DQK - ROPE_DIM), F32)
    zero = jnp.zeros((t, DA_DQK - ROPE_DIM), F32)
    zh = jnp.zeros((t, ROPE_HALF), F32)
    cs = jnp.concatenate([cos, cos, one] * 2, axis=1)
    snl = jnp.concatenate([-sin, zh, zero] * 2, axis=1)
    snh = jnp.concatenate([zh, sin, zero] * 2, axis=1)
    return cs, snl, snh


def _prep_w_in(w_in):
    c, z, a, b, dq, dk, dv = jnp.split(
        w_in, np.cumsum([CONV_CH, GDN_V_W, GDN_HEADS, GDN_HEADS, DA_QK_W, DA_QK_W]).tolist(), axis=-1)
    pad = jnp.zeros((D_MODEL, AB_W - 2 * GDN_HEADS), w_in.dtype)
    return jnp.concatenate([c, z, dq, dk, dv, a, b, pad], axis=-1)


def _gdn_kernel(c_ref, z_ref, ab_ref, tail_ref, s0_ref, wconv_ref, alog_ref, dtb_ref, wg_ref,
                o_ref, sfin_ref, cbuf, ybuf, s_scr, u0_scr, wq_scr, kw_scr, qkd_scr, el_scr, *, tt):
    C, H = GDN_CHUNK, GDN_HEADS
    R = H * C
    shift = C.bit_length() - 1
    t = pl.program_id(1)

    @pl.when(t == 0)
    def _():
        cbuf[0:8, :] = tail_ref[0]
        s_scr[...] = s0_ref[0]

    cbuf[8:8 + tt, :] = c_ref[0]
    acc = cbuf[8:8 + tt, :] * wconv_ref[CONV_W - 1:CONV_W, :]
    for i in range(CONV_W - 1):
        off = 8 - (CONV_W - 1) + i
        acc = acc + cbuf[off:off + tt, :] * wconv_ref[i:i + 1, :]
    ybuf[...] = acc * _sigmoid(acc)
    cbuf[0:8, :] = cbuf[tt:tt + 8, :]

    ii = lax.broadcasted_iota(jnp.int32, (R, R), 0)
    jj = lax.broadcasted_iota(jnp.int32, (R, R), 1)
    same_head = jnp.right_shift(ii, shift) == jnp.right_shift(jj, shift)
    strict = same_head & (ii > jj)
    causal = same_head & (ii >= jj)
    eye = (ii == jj).astype(F32)
    merge_masks = []
    s = 1
    while s < C:
        sh = s.bit_length() - 1
        merge_masks.append((jnp.right_shift(ii, sh + 1) == jnp.right_shift(jj, sh + 1))
                           & (jnp.right_shift(ii, sh) != jnp.right_shift(jj, sh)) & (ii > jj))
        s *= 2
    ci_ = lax.broadcasted_iota(jnp.int32, (C, C), 0)
    cj_ = lax.broadcasted_iota(jnp.int32, (C, C), 1)
    tri = (ci_ >= cj_).astype(F32)
    lane = lax.broadcasted_iota(jnp.int32, (R, LANES), 1)
    rhead = jnp.right_shift(lax.broadcasted_iota(jnp.int32, (R, LANES), 0), shift)
    ones8 = jnp.ones((8, LANES), F32)
    rep = lambda a: jnp.concatenate([a] * H, axis=0)
    pick = lambda a, off: jnp.sum(jnp.where(lane == rhead + off, a, 0.0), axis=-1, keepdims=True)

    def stacked(rows, base, scale):
        parts = []
        for h in range(H):
            x = ybuf[rows, base + h * GDN_DK:base + (h + 1) * GDN_DK]
            if scale is not None:
                x = x * lax.rsqrt(jnp.sum(x * x, axis=-1, keepdims=True) + EPS) * scale
            parts.append(x)
        return jnp.concatenate(parts, axis=0)

    chunks = range(tt // C)
    low, rhs, qes = [], [], []
    for ci in chunks:
        rows = slice(ci * C, (ci + 1) * C)
        ab = ab_ref[0, rows, :]
        gfull = -jnp.exp(alog_ref[...]) * _softplus(ab + dtb_ref[...])
        bfull = _sigmoid(ab)
        gcum = _dot_f32(tri, gfull)
        glast = gcum[C - 1:C, :]
        g_m = jnp.where(lane == rhead, rep(gcum), 0.0)
        g_col = jnp.sum(g_m, axis=-1, keepdims=True)
        g_row = _dot_nt_f32(ones8, g_m)[0:1, :]
        beta = pick(rep(bfull), H)
        gl_col = pick(jnp.broadcast_to(glast, (R, LANES)), 0)
        decay = jnp.exp(jnp.where(causal, g_col - g_row, -jnp.inf))
        q = stacked(rows, 0, GDN_DK ** -0.5)
        k = stacked(rows, GDN_QK_W, 1.0)
        v = stacked(rows, 2 * GDN_QK_W, None)
        kq = _mm(jnp.concatenate([k, q], axis=0), k, _NT)
        low.append(jnp.where(strict, kq[0:R] * decay * beta, 0.0))
        e_g = jnp.exp(g_col)
        rhs.append(jnp.concatenate([beta * v, (beta * e_g) * k], axis=1).astype(BF16))
        qes.append(q * e_g)
        kw_scr[ci] = (k * jnp.exp(gl_col - g_col)).astype(BF16)
        qkd_scr[ci] = jnp.where(causal, kq[R:2 * R] * decay, 0.0).astype(BF16)
        el_scr[ci] = jnp.broadcast_to(jnp.exp(glast), (8, LANES))
    tinv = [eye - jnp.where(merge_masks[0], low[ci], 0.0) for ci in chunks]
    for m in merge_masks[1:]:
        half = [_mm(tinv[ci], jnp.where(m, low[ci], 0.0)) for ci in chunks]
        tinv = [tinv[ci] - _mm(half[ci], tinv[ci]) for ci in chunks]
    for ci in chunks:
        w2 = _mm(tinv[ci], rhs[ci])
        u0_scr[ci] = w2[:, 0:GDN_DV]
        wk = w2[:, GDN_DV:]
        wq_scr[ci] = jnp.concatenate(
            [x[h * C:(h + 1) * C] for h in range(H) for x in (wk, qes[ci])], axis=0).astype(BF16)

    st = [s_scr[h] for h in range(H)]
    for ci in range(tt // C):
        rows = slice(ci * C, (ci + 1) * C)
        us, os_ = [], []
        for h in range(H):
            r = jnp.dot(wq_scr[ci, 2 * h * C:2 * (h + 1) * C, :], st[h].astype(BF16),
                        preferred_element_type=F32)
            us.append(u0_scr[ci, h * C:(h + 1) * C, :] - r[0:C])
            os_.append(r[C:2 * C])
        intra = jnp.dot(qkd_scr[ci], jnp.concatenate(us, axis=0).astype(BF16),
                        preferred_element_type=F32)
        el = el_scr[ci]
        for h in range(H):
            st[h] = st[h] * el[0:1, h:h + 1] + _mm(kw_scr[ci, h * C:(h + 1) * C, :], us[h], _TN)
            o = os_[h] + intra[h * C:(h + 1) * C]
            zz = z_ref[0, rows, h * GDN_DV:(h + 1) * GDN_DV]
            o_ref[0, rows, h * GDN_DV:(h + 1) * GDN_DV] = _rms(o, wg_ref[...]) * (zz * _sigmoid(zz))
    for h in range(H):
        s_scr[h] = st[h]

    @pl.when(t == pl.num_programs(1) - 1)
    def _():
        sfin_ref[0] = s_scr[...]


def _gdn_step_kernel(c_ref, z_ref, ab_ref, cs_ref, s_ref, wconv_ref, alog_ref, dtb_ref, wg_ref,
                     o_ref, snew_ref, *, nb):
    H = GDN_HEADS
    acc = c_ref[...] * wconv_ref[CONV_W - 1:CONV_W, :]
    for i in range(CONV_W - 1):
        acc = acc + cs_ref[i] * wconv_ref[i:i + 1, :]
    y = acc * _sigmoid(acc)
    ab = ab_ref[...]
    e_g = jnp.exp(-jnp.exp(alog_ref[...]) * _softplus(ab + dtb_ref[...]))
    bfull = _sigmoid(ab)
    unit = lambda x: x * lax.rsqrt(jnp.sum(x * x, axis=-1, keepdims=True) + EPS)
    qs = [unit(y[:, h * GDN_DK:(h + 1) * GDN_DK]) * (GDN_DK ** -0.5) for h in range(H)]
    ks = [unit(y[:, GDN_QK_W + h * GDN_DK:GDN_QK_W + (h + 1) * GDN_DK]) for h in range(H)]
    vs = [y[:, 2 * GDN_QK_W + h * GDN_DV:2 * GDN_QK_W + (h + 1) * GDN_DV] for h in range(H)]
    cols = jnp.concatenate(ks + qs, axis=0).T
    for h in range(H):
        ks_rows, qs_rows = [], []
        for s in range(nb):
            state = s_ref[s, h]
            kcol = cols[:, h * nb + s:h * nb + s + 1]
            qcol = cols[:, (H + h) * nb + s:(H + h) * nb + s + 1]
            ks_rows.append(jnp.sum(kcol * state, axis=0, keepdims=True))
            qs_rows.append(jnp.sum(qcol * state, axis=0, keepdims=True))
        eg_h = e_g[:, h:h + 1]
        u = bfull[:, H + h:H + h + 1] * (vs[h] - eg_h * jnp.concatenate(ks_rows, axis=0))
        o = (eg_h * jnp.concatenate(qs_rows, axis=0)
             + jnp.sum(qs[h] * ks[h], axis=-1, keepdims=True) * u)
        eg_b = jnp.broadcast_to(eg_h, (nb, GDN_DV))
        for s in range(nb):
            kcol = cols[:, h * nb + s:h * nb + s + 1]
            snew_ref[s, h] = eg_b[s:s + 1, :] * s_ref[s, h] + kcol * u[s:s + 1, :]
        zz = z_ref[:, h * GDN_DV:(h + 1) * GDN_DV]
        o_ref[:, h * GDN_DV:(h + 1) * GDN_DV] = _rms(o, wg_ref[...]) * (zz * _sigmoid(zz))


def _gdn_step(c, z, ab, conv_state, s, wconv, alog, dtb, wg):
    bs = c.shape[0]
    nb = LANES // (2 * GDN_HEADS)
    assert bs % nb == 0
    row = lambda i: (i, 0)
    fixed = lambda i: (0, 0)
    state = pl.BlockSpec((nb, GDN_HEADS, GDN_DK, GDN_DV), lambda i: (i, 0, 0, 0))
    return pl.pallas_call(
        functools.partial(_gdn_step_kernel, nb=nb),
        grid=(bs // nb,),
        in_specs=[pl.BlockSpec((nb, CONV_CH), row),
                  pl.BlockSpec((nb, GDN_V_W), row),
                  pl.BlockSpec((nb, AB_W), row),
                  pl.BlockSpec((CONV_W - 1, nb, CONV_CH), lambda i: (0, i, 0)),
                  state,
                  pl.BlockSpec((CONV_W, CONV_CH), fixed),
                  pl.BlockSpec((1, LANES), fixed),
                  pl.BlockSpec((1, LANES), fixed),
                  pl.BlockSpec((1, GDN_DV), fixed)],
        out_specs=[pl.BlockSpec((nb, GDN_V_W), row), state],
        out_shape=[jax.ShapeDtypeStruct((bs, GDN_V_W), F32),
                   jax.ShapeDtypeStruct(s.shape, F32)],
        compiler_params=pltpu.CompilerParams(dimension_semantics=("parallel",),
                                             vmem_limit_bytes=VMEM_LIMIT),
        name="gdn_step",
    )(c, z, ab, conv_state, s, wconv, alog, dtb, wg)


def _gdn(c, z, ab, tail, s0, wconv, alog, dtb, wg, tt):
    b, t, _ = c.shape
    assert t % tt == 0 and tt % GDN_CHUNK == 0
    nc, rs = tt // GDN_CHUNK, GDN_HEADS * GDN_CHUNK
    seq = lambda i, j: (i, j, 0)
    fixed2 = lambda i, j: (0, 0)
    tail_map = (lambda i, j: (i, 0, 0)) if tail.shape[0] == b else (lambda i, j: (0, 0, 0))
    s0_map = (lambda i, j: (i, 0, 0, 0)) if s0.shape[0] == b else (lambda i, j: (0, 0, 0, 0))
    return pl.pallas_call(
        functools.partial(_gdn_kernel, tt=tt),
        grid=(b, t // tt),
        in_specs=[pl.BlockSpec((1, tt, CONV_CH), seq),
                  pl.BlockSpec((1, tt, GDN_V_W), seq),
                  pl.BlockSpec((1, tt, AB_W), seq),
                  pl.BlockSpec((1, 8, CONV_CH), tail_map),
                  pl.BlockSpec((1, GDN_HEADS, GDN_DK, GDN_DV), s0_map),
                  pl.BlockSpec((CONV_W, CONV_CH), fixed2),
                  pl.BlockSpec((1, LANES), fixed2),
                  pl.BlockSpec((1, LANES), fixed2),
                  pl.BlockSpec((1, GDN_DV), fixed2)],
        out_specs=[pl.BlockSpec((1, tt, GDN_V_W), seq),
                   pl.BlockSpec((1, GDN_HEADS, GDN_DK, GDN_DV), lambda i, j: (i, 0, 0, 0))],
        out_shape=[jax.ShapeDtypeStruct((b, t, GDN_V_W), F32),
                   jax.ShapeDtypeStruct((b, GDN_HEADS, GDN_DK, GDN_DV), F32)],
        scratch_shapes=[pltpu.VMEM((tt + 8, CONV_CH), F32),
                        pltpu.VMEM((tt, CONV_CH), F32),
                        pltpu.VMEM((GDN_HEADS, GDN_DK, GDN_DV), F32),
                        pltpu.VMEM((nc, rs, GDN_DV), F32),
                        pltpu.VMEM((nc, 2 * rs, GDN_DK), BF16),
                        pltpu.VMEM((nc, rs, GDN_DK), BF16),
                        pltpu.VMEM((nc, rs, rs), BF16),
                        pltpu.VMEM((nc, 8, LANES), F32)],
        compiler_params=pltpu.CompilerParams(dimension_semantics=("parallel", "arbitrary"),
                                             vmem_limit_bytes=VMEM_LIMIT),
        name="gdn",
    )(c, z, ab, tail, s0, wconv, alog, dtb, wg)


def _lambda(lam_ref):
    lv = lam_ref[...]
    s1 = jnp.sum(lv[0:1, :] * lv[1:2, :], axis=-1, keepdims=True)
    s2 = jnp.sum(lv[2:3, :] * lv[3:4, :], axis=-1, keepdims=True)
    return jnp.exp(s1) - jnp.exp(s2) + LAM_INIT


def _split_components(q):
    lane = lax.broadcasted_iota(jnp.int32, q.shape, 1)
    first = (lane % DA_DV) < DA_DQK
    return jnp.where(first, q, 0.0).astype(BF16), jnp.where(first, 0.0, q).astype(BF16)


def _online_update(state, s, vb):
    m, l, acc = state
    m_new = jnp.maximum(m, jnp.max(s, axis=-1, keepdims=True))
    p = jnp.exp(s - m_new)
    alpha = jnp.exp(m - m_new)
    return (m_new, alpha * l + jnp.sum(p, axis=-1, keepdims=True),
            alpha * acc + jnp.dot(p.astype(BF16), vb, preferred_element_type=F32))


def _attn_prompt_kernel(q_ref, k_ref, v_ref, km_ref, vm_ref, lam_ref, wsub_ref, o_ref, *, bq, bk):
    qi = pl.program_id(2)
    q1, q2 = _split_components(q_ref[0] * (DA_DQK ** -0.5))
    neg = jnp.full((bq, 1), -jnp.inf, F32)
    zero = jnp.zeros((bq, 1), F32)
    zacc = jnp.zeros((bq, DA_DV), F32)
    init = ((neg, zero, zacc), (neg, zero, zacc))

    def block(kblk, vblk, mask, st):
        kb, vb = kblk.astype(BF16), vblk.astype(BF16)
        s1, s2 = _dot_nt(q1, kb), _dot_nt(q2, kb)
        if mask is not None:
            s1 = jnp.where(mask, s1, -jnp.inf)
            s2 = jnp.where(mask, s2, -jnp.inf)
        return (_online_update(st[0], s1, vb), _online_update(st[1], s2, vb))

    n_full = lax.shift_right_logical(qi * bq, bk.bit_length() - 1)

    def body(kb, st):
        r0 = pl.multiple_of(kb * bk, bk)
        return block(k_ref[0, pl.ds(r0, bk), :], v_ref[0, pl.ds(r0, bk), :], None, st)

    st = lax.fori_loop(0, n_full, body, init)
    r0 = pl.multiple_of(n_full * bk, bk)
    kcat = jnp.concatenate([km_ref[...], k_ref[0, pl.ds(r0, bk), :]], axis=0)
    vcat = jnp.concatenate([vm_ref[...], v_ref[0, pl.ds(r0, bk), :]], axis=0)
    ri = lax.broadcasted_iota(jnp.int32, (bq, LANES + bk), 0)
    ci = lax.broadcasted_iota(jnp.int32, (bq, LANES + bk), 1)
    seen = (ci < N_META) | ((ci >= LANES) & (ci - LANES <= ri + (qi * bq - n_full * bk)))
    st = block(kcat, vcat, seen, st)
    (m1, l1, a1), (m2, l2, a2) = st
    o = a1 / l1 - _lambda(lam_ref) * (a2 / l2)
    o_ref[0] = _rms(o, wsub_ref[...]) * (1.0 - LAM_INIT)


def _attn_prompt(q, k, v, k_meta, v_meta, lam_vecs, wsub, bq, bk):
    b, t, _ = q.shape
    assert t % bk == 0 and bk % bq == 0 and k_meta.shape[0] == LANES
    return pl.pallas_call(
        functools.partial(_attn_prompt_kernel, bq=bq, bk=bk),
        grid=(b, DA_HEADS, t // bq),
        in_specs=[pl.BlockSpec((1, bq, DA_DV), lambda i, h, j: (i, j, h)),
                  pl.BlockSpec((1, t, DA_DV), lambda i, h, j: (i, 0, h)),
                  pl.BlockSpec((1, t, DA_DV), lambda i, h, j: (i, 0, h)),
                  pl.BlockSpec((LANES, DA_DV), lambda i, h, j: (0, h)),
                  pl.BlockSpec((LANES, DA_DV), lambda i, h, j: (0, h)),
                  pl.BlockSpec((4, DA_DQK), lambda i, h, j: (0, 0)),
                  pl.BlockSpec((1, DA_DV), lambda i, h, j: (0, 0))],
        out_specs=pl.BlockSpec((1, bq, DA_DV), lambda i, h, j: (i, j, h)),
        out_shape=jax.ShapeDtypeStruct((b, t, DA_V_W), F32),
        compiler_params=pltpu.CompilerParams(
            dimension_semantics=("parallel", "parallel", "arbitrary"),
            vmem_limit_bytes=VMEM_LIMIT),
        name="attn_prompt",
    )(q, k, v, k_meta, v_meta, lam_vecs, wsub)


def _attn_sample_kernel(pt_ref, q_ref, kn_ref, vn_ref, lam_ref, wsub_ref, *rest, pps):
    k_refs, v_refs = rest[:pps], rest[pps:2 * pps]
    o_ref, m_scr, l_scr, acc_scr = rest[2 * pps:]
    step = pl.program_id(1)
    nrow = 2 * DA_HEADS
    twice = lambda x: jnp.concatenate([x, x], axis=0)
    rowq = lax.broadcasted_iota(jnp.int32, (nrow, DA_DV), 0)
    laneq = lax.broadcasted_iota(jnp.int32, (nrow, DA_DV), 1)
    own_comp = (laneq // DA_DQK) == (rowq // DA_HEADS)
    qm = jnp.where(own_comp, twice(q_ref[0]) * (DA_DQK ** -0.5), 0.0)
    page_rows = k_refs[0].shape[1]
    rowp = lax.broadcasted_iota(jnp.int32, (nrow, page_rows * pps), 0)
    colp = lax.broadcasted_iota(jnp.int32, (nrow, page_rows * pps), 1)
    own_head = (colp % DA_HEADS) == (rowp % DA_HEADS)

    def rows3(x):
        hi = x.astype(BF16).astype(F32)
        mid = (x - hi).astype(BF16).astype(F32)
        return jnp.concatenate([hi, mid, x - hi - mid], axis=0).astype(BF16)

    def unrows3(r):
        return (r[0:nrow] + r[nrow:2 * nrow]) + r[2 * nrow:3 * nrow]

    def hi_lo(x):
        hi = x.astype(BF16)
        return hi, (x - hi.astype(F32)).astype(BF16)

    @pl.when(step == 0)
    def _():
        s_new = jnp.sum(qm * twice(kn_ref[0]), axis=-1, keepdims=True)
        m_scr[...] = s_new
        l_scr[...] = jnp.ones_like(s_new)
        acc_scr[...] = twice(vn_ref[0])

    q3 = rows3(qm)
    scores = []
    for r in k_refs:
        k_hi, k_lo = hi_lo(r[0])
        scores.append(unrows3(_dot_nt(q3, k_hi) + _dot_nt(q3, k_lo)))
    s = jnp.where(own_head, jnp.concatenate(scores, axis=1), -jnp.inf)
    m_old = m_scr[...]
    m_new = jnp.maximum(m_old, jnp.max(s, axis=-1, keepdims=True))
    p = jnp.exp(s - m_new)
    alpha = jnp.exp(m_old - m_new)
    p3 = rows3(p)
    pv = jnp.zeros((3 * nrow, DA_DV), F32)
    for j, r in enumerate(v_refs):
        v_hi, v_lo = hi_lo(r[0])
        pj = p3[:, j * page_rows:(j + 1) * page_rows]
        pv = pv + (jnp.dot(pj, v_hi, preferred_element_type=F32)
                   + jnp.dot(pj, v_lo, preferred_element_type=F32))
    m_scr[...] = m_new
    l_scr[...] = alpha * l_scr[...] + jnp.sum(p, axis=-1, keepdims=True)
    acc_scr[...] = alpha * acc_scr[...] + unrows3(pv)

    @pl.when(step == pl.num_programs(1) - 1)
    def _():
        w = acc_scr[...] / l_scr[...]
        o = w[0:DA_HEADS, :] - _lambda(lam_ref) * w[DA_HEADS:nrow, :]
        o_ref[0] = _rms(o, wsub_ref[...]) * (1.0 - LAM_INIT)


def _attn_sample(q, k_new, v_new, cache_k, cache_v, page_table, lam_vecs, wsub, pps):
    bs = q.shape[0]
    n_pages = page_table.shape[1]
    page_rows = cache_k.shape[1]
    assert n_pages % pps == 0
    tok = lambda i, s, pt: (i, 0, 0)
    fixed = lambda i, s, pt: (0, 0)

    def page_spec(j):
        return pl.BlockSpec((1, page_rows, DA_DV), lambda i, s, pt: (pt[i, s * pps + j], 0, 0))

    grid_spec = pltpu.PrefetchScalarGridSpec(
        num_scalar_prefetch=1,
        grid=(bs, n_pages // pps),
        in_specs=[pl.BlockSpec((1, DA_HEADS, DA_DV), tok),
                  pl.BlockSpec((1, DA_HEADS, DA_DV), tok),
                  pl.BlockSpec((1, DA_HEADS, DA_DV), tok),
                  pl.BlockSpec((4, DA_DQK), fixed),
                  pl.BlockSpec((1, DA_DV), fixed)]
                 + [page_spec(j) for j in range(pps)] * 2,
        out_specs=pl.BlockSpec((1, DA_HEADS, DA_DV), tok),
        scratch_shapes=[pltpu.VMEM((2 * DA_HEADS, 1), F32),
                        pltpu.VMEM((2 * DA_HEADS, 1), F32),
                        pltpu.VMEM((2 * DA_HEADS, DA_DV), F32)])
    return pl.pallas_call(
        functools.partial(_attn_sample_kernel, pps=pps),
        grid_spec=grid_spec,
        out_shape=jax.ShapeDtypeStruct((bs, DA_HEADS, DA_DV), F32),
        compiler_params=pltpu.CompilerParams(dimension_semantics=("parallel", "arbitrary"),
                                             vmem_limit_bytes=VMEM_LIMIT),
        name="attn_sample",
    )(page_table, q, k_new, v_new, lam_vecs, wsub, *([cache_k] * pps), *([cache_v] * pps))


def _route(logits):
    lane = lax.broadcasted_iota(jnp.int32, logits.shape, 1).astype(F32)
    big = float(LANES)
    ninf = -jnp.inf

    def top(mask):
        val = jnp.where(mask, logits, ninf)
        mx = jnp.max(val, axis=-1, keepdims=True)
        idx = jnp.min(jnp.where(mask & (val == mx), lane, big), axis=-1, keepdims=True)
        return mx, idx

    gmask = (lane >= N_EXPERTS) & (lane < N_EXPERTS + N_GROUPS)
    gmax, gidx = top(gmask)
    g_w = 1.0 / jnp.sum(jnp.where(gmask, jnp.exp(logits - gmax), 0.0), axis=-1, keepdims=True)
    g_sel = gidx - N_EXPERTS
    emask = (lane >= g_sel * EXPERTS_PER_GROUP) & (lane < (g_sel + 1) * EXPERTS_PER_GROUP)
    v1, i1 = top(emask)
    v2, i2 = top(emask & (lane != i1))
    e21 = jnp.exp(v2 - v1)
    p1 = 1.0 / (1.0 + e21)
    return jnp.where(lane == i1, p1 * g_w, 0.0) + jnp.where(lane == i2, e21 * p1 * g_w, 0.0), g_sel


def _mix_route(g_ref, a_ref, x_ref, wo_ref, wnf_ref, wr_ref, precise):
    mix = (_mm(g_ref[...], wo_ref[0:GDN_V_W, :], _NN, precise)
           + _mm(a_ref[...], wo_ref[GDN_V_W:, :], _NN, precise))
    x1 = x_ref[...] + mix
    xn = _rms(x1, wnf_ref[...])
    wr = wr_ref[...]
    if precise:
        logits = _dot_f32(xn, wr)
    else:
        x_hi, w_hi = xn.astype(BF16), wr.astype(BF16)
        x_lo, w_lo = (xn - x_hi.astype(F32)).astype(BF16), (wr - w_hi.astype(F32)).astype(BF16)
        logits = (jnp.dot(x_hi, w_hi, preferred_element_type=F32)
                  + (jnp.dot(x_hi, w_lo, preferred_element_type=F32)
                     + jnp.dot(x_lo, w_hi, preferred_element_type=F32)))
    comb, g_sel = _route(logits)
    return x1, xn, comb, g_sel


def _expert(xn, comb, e_lane, wg, wu, wd):
    hg = jnp.dot(xn, wg, preferred_element_type=F32)
    hu = jnp.dot(xn, wu, preferred_element_type=F32)
    lane = lax.broadcasted_iota(jnp.int32, comb.shape, 1)
    cw = jnp.sum(jnp.where(lane == e_lane, comb, 0.0), axis=-1, keepdims=True)
    hh = (hg * _sigmoid(hg)) * hu * cw
    return jnp.dot(hh.astype(BF16), wd, preferred_element_type=F32)


def _ffn_kernel(g_ref, a_ref, x_ref, wo_ref, wnf_ref, wr_ref, wg_ref, wu_ref, wd_ref, wfin_ref,
                y_ref, x1_scr, xn_scr, comb_scr, acc_scr, *, precise):
    e = pl.program_id(1)

    @pl.when(e == 0)
    def _():
        x1, xn, comb, _ = _mix_route(g_ref, a_ref, x_ref, wo_ref, wnf_ref, wr_ref, precise)
        x1_scr[...] = x1
        xn_scr[...] = xn.astype(BF16)
        comb_scr[...] = comb
        acc_scr[...] = jnp.zeros_like(acc_scr)

    acc_scr[...] += _expert(xn_scr[...], comb_scr[...], e, wg_ref[0], wu_ref[0], wd_ref[0])

    @pl.when(e == pl.num_programs(1) - 1)
    def _():
        y_ref[...] = _rms(x1_scr[...] + acc_scr[...], wfin_ref[...])


def _ffn(g_out, a_out, x, wo, wnf, wr, wg, wu, wd, wfin, tm):
    n = x.shape[0]
    assert n % tm == 0
    row = lambda i, e: (i, 0)
    fixed = lambda i, e: (0, 0)
    exp = lambda i, e: (e, 0, 0)
    return pl.pallas_call(
        functools.partial(_ffn_kernel, precise=wo.dtype == F32),
        grid=(n // tm, N_EXPERTS),
        in_specs=[pl.BlockSpec((tm, GDN_V_W), row),
                  pl.BlockSpec((tm, DA_V_W), row),
                  pl.BlockSpec((tm, D_MODEL), row),
                  pl.BlockSpec((GDN_V_W + DA_V_W, D_MODEL), fixed),
                  pl.BlockSpec((1, D_MODEL), fixed),
                  pl.BlockSpec((D_MODEL, LANES), fixed),
                  pl.BlockSpec((1, D_MODEL, D_EXPERT), exp),
                  pl.BlockSpec((1, D_MODEL, D_EXPERT), exp),
                  pl.BlockSpec((1, D_EXPERT, D_MODEL), exp),
                  pl.BlockSpec((1, D_MODEL), fixed)],
        out_specs=pl.BlockSpec((tm, D_MODEL), row),
        out_shape=jax.ShapeDtypeStruct((n, D_MODEL), F32),
        scratch_shapes=[pltpu.VMEM((tm, D_MODEL), F32),
                        pltpu.VMEM((tm, D_MODEL), BF16),
                        pltpu.VMEM((tm, LANES), F32),
                        pltpu.VMEM((tm, D_MODEL), F32)],
        compiler_params=pltpu.CompilerParams(dimension_semantics=("parallel", "arbitrary"),
                                             vmem_limit_bytes=VMEM_LIMIT),
        name="ffn",
    )(g_out, a_out, x, wo, wnf, wr, wg, wu, wd, wfin)


ROW_W = D_MODEL + LANES
MOE_TILE = 512
DMA_CHUNK = 2048


def _ffn_pre_kernel(g_ref, a_ref, x_ref, wo_ref, wnf_ref, wr_ref, rows_ref, info_ref, cnt_ref,
                    cnt_scr):
    @pl.when(pl.program_id(0) == 0)
    def _():
        cnt_scr[...] = jnp.zeros_like(cnt_scr)

    x1, _, comb, g_sel = _mix_route(g_ref, a_ref, x_ref, wo_ref, wnf_ref, wr_ref, False)
    tm = x1.shape[0]
    lane = lax.broadcasted_iota(jnp.int32, (tm, LANES), 1).astype(F32)
    onehot = lane == g_sel
    ri = lax.broadcasted_iota(jnp.int32, (tm, tm), 0)
    ci = lax.broadcasted_iota(jnp.int32, (tm, tm), 1)
    earlier = jnp.dot((ci < ri).astype(BF16), onehot.astype(BF16), preferred_element_type=F32)
    rank = jnp.sum(jnp.where(onehot, earlier + cnt_scr[...], 0.0), axis=-1, keepdims=True)
    cnt_scr[...] += jnp.sum(onehot.astype(F32), axis=0, keepdims=True)
    cnt_ref[...] = cnt_scr[...]
    rows_ref[:, 0:D_MODEL] = x1
    rows_ref[:, D_MODEL:] = comb
    packed = jnp.where(lane == 0.0, g_sel, jnp.where(lane == 1.0, rank, 0.0))
    r8 = lax.broadcasted_iota(jnp.int32, (8, LANES), 0)
    l8 = lax.broadcasted_iota(jnp.int32, (8, LANES), 1)
    info_ref[...] = _dot_nt_f32((r8 == l8).astype(F32), packed)


def _ffn_pre(g_out, a_out, x, wo, wnf, wr, tm):
    n = x.shape[0]
    assert n % tm == 0
    row = lambda i: (i, 0)
    fixed = lambda i: (0, 0)
    return pl.pallas_call(
        _ffn_pre_kernel,
        grid=(n // tm,),
        in_specs=[pl.BlockSpec((tm, GDN_V_W), row),
                  pl.BlockSpec((tm, DA_V_W), row),
                  pl.BlockSpec((tm, D_MODEL), row),
                  pl.BlockSpec((GDN_V_W + DA_V_W, D_MODEL), fixed),
                  pl.BlockSpec((1, D_MODEL), fixed),
                  pl.BlockSpec((D_MODEL, LANES), fixed)],
        out_specs=[pl.BlockSpec((tm, ROW_W), row),
                   pl.BlockSpec((8, tm), lambda i: (0, i)),
                   pl.BlockSpec((1, LANES), fixed)],
        out_shape=[jax.ShapeDtypeStruct((n, ROW_W), F32),
                   jax.ShapeDtypeStruct((8, n), F32),
                   jax.ShapeDtypeStruct((1, LANES), F32)],
        scratch_shapes=[pltpu.VMEM((1, LANES), F32)],
        compiler_params=pltpu.CompilerParams(dimension_semantics=("arbitrary",),
                                             vmem_limit_bytes=VMEM_LIMIT),
        name="ffn_pre",
    )(g_out, a_out, x, wo, wnf, wr)


def _copy_rows(idx_ref, src_ref, dst_ref, sem, n, scatter):
    def issue(c):
        def body(t, carry):
            r = c * DMA_CHUNK + t
            j = idx_ref[r]
            src, dst = (r, j) if scatter else (j, r)
            pltpu.make_async_copy(src_ref.at[pl.ds(src, 1)], dst_ref.at[pl.ds(dst, 1)],
                                  sem.at[c % 2]).start()
            return carry
        lax.fori_loop(0, DMA_CHUNK, body, 0, unroll=8)

    def drain(c):
        pltpu.make_async_copy(src_ref.at[pl.ds(0, DMA_CHUNK)], dst_ref.at[pl.ds(0, DMA_CHUNK)],
                              sem.at[c % 2]).wait()

    def step(c, carry):
        issue(c)
        drain(c - 1)
        return carry

    issue(0)
    lax.fori_loop(1, n // DMA_CHUNK, step, 0)
    drain(n // DMA_CHUNK - 1)


def _scatter_kernel(idx_ref, src_ref, init_ref, dst_ref, sem, *, n):
    del init_ref
    _copy_rows(idx_ref, src_ref, dst_ref, sem, n, True)


def _gather_kernel(idx_ref, src_ref, dst_ref, sem, *, n):
    _copy_rows(idx_ref, src_ref, dst_ref, sem, n, False)


def _move_rows(idx, src, n_out, scatter):
    n, w = idx.shape[0], src.shape[1]
    assert n % DMA_CHUNK == 0
    any_spec = pl.BlockSpec(memory_space=pl.ANY)
    operands = (idx, src) + ((jnp.zeros((n_out, w), src.dtype),) if scatter else ())
    return pl.pallas_call(
        functools.partial(_scatter_kernel if scatter else _gather_kernel, n=n),
        grid_spec=pltpu.PrefetchScalarGridSpec(
            num_scalar_prefetch=1, grid=(1,),
            in_specs=[any_spec] * (len(operands) - 1),
            out_specs=any_spec,
            scratch_shapes=[pltpu.SemaphoreType.DMA((2,))]),
        out_shape=jax.ShapeDtypeStruct((n_out, w), src.dtype),
        input_output_aliases={2: 0} if scatter else {},
        compiler_params=pltpu.CompilerParams(dimension_semantics=("arbitrary",)),
        name="scatter_rows" if scatter else "gather_rows",
    )(*operands)


def _moe_groups_kernel(tg_ref, nt_ref, rows_ref, wnf_ref, wg_ref, wu_ref, wd_ref, wfin_ref, y_ref):
    j = pl.program_id(0)

    @pl.when(j < nt_ref[0])
    def _():
        x1 = rows_ref[:, 0:D_MODEL]
        comb = rows_ref[:, D_MODEL:]
        xn = _rms(x1, wnf_ref[...]).astype(BF16)
        first = tg_ref[j] * EXPERTS_PER_GROUP
        acc = jnp.zeros_like(x1)
        for e in range(EXPERTS_PER_GROUP):
            acc = acc + _expert(xn, comb, first + e, wg_ref[e], wu_ref[e], wd_ref[e])
        y_ref[...] = _rms(x1 + acc, wfin_ref[...])

    @pl.when(j >= nt_ref[0])
    def _():
        y_ref[...] = jnp.zeros_like(y_ref)


def _moe_groups(tile_group, n_tiles_used, rows, wnf, wg, wu, wd, wfin):
    n_tiles = tile_group.shape[0]
    fixed = lambda j, tg, nt: (0, 0)
    grp = lambda j, tg, nt: (tg[j], 0, 0)
    return pl.pallas_call(
        _moe_groups_kernel,
        grid_spec=pltpu.PrefetchScalarGridSpec(
            num_scalar_prefetch=2, grid=(n_tiles,),
            in_specs=[pl.BlockSpec((MOE_TILE, ROW_W), lambda j, tg, nt: (j, 0)),
                      pl.BlockSpec((1, D_MODEL), fixed),
                      pl.BlockSpec((EXPERTS_PER_GROUP, D_MODEL, D_EXPERT), grp),
                      pl.BlockSpec((EXPERTS_PER_GROUP, D_MODEL, D_EXPERT), grp),
                      pl.BlockSpec((EXPERTS_PER_GROUP, D_EXPERT, D_MODEL), grp),
                      pl.BlockSpec((1, D_MODEL), fixed)],
            out_specs=pl.BlockSpec((MOE_TILE, D_MODEL), lambda j, tg, nt: (j, 0))),
        out_shape=jax.ShapeDtypeStruct((n_tiles * MOE_TILE, D_MODEL), F32),
        compiler_params=pltpu.CompilerParams(dimension_semantics=("arbitrary",),
                                             vmem_limit_bytes=VMEM_LIMIT),
        name="moe_groups",
    )(tile_group, n_tiles_used, rows, wnf, wg, wu, wd, wfin)


def _ffn_sorted(g_out, a_out, x, wo, wnf, wr, wg, wu, wd, wfin):
    n = x.shape[0]
    rows, info, cnt = _ffn_pre(g_out, a_out, x, wo, wnf, wr, 512)
    grp, rank = info[0].astype(jnp.int32), info[1].astype(jnp.int32)
    tiles = (cnt[0, :N_GROUPS].astype(jnp.int32) + MOE_TILE - 1) // MOE_TILE
    ends = jnp.cumsum(tiles)
    dest = (ends - tiles)[grp] * MOE_TILE + rank
    n_tiles = n // MOE_TILE + N_GROUPS
    tile_group = jnp.minimum(
        jnp.sum(jnp.arange(n_tiles, dtype=jnp.int32)[:, None] >= ends[None, :], axis=1),
        N_GROUPS - 1).astype(jnp.int32)
    rows_sorted = _move_rows(dest, rows, n_tiles * MOE_TILE, True)
    y_sorted = _moe_groups(tile_group, ends[N_GROUPS - 1:], rows_sorted, wnf, wg, wu, wd, wfin)
    return _move_rows(dest, y_sorted, n, False)


def kernel(x_prompt, x_sample, cache_k, cache_v, state_conv, state_gdn, page_table, meta_tokens,
           w_norm_mix, w_in, w_conv, a_log, dt_bias, w_gnorm, lambda_q1, lambda_k1, lambda_q2,
           lambda_k2, w_subln, w_out, w_norm_ffn, w_group, w_expert, w_gate, w_up, w_down, w_final):
    l = 0
    bp, seq = x_prompt.shape[0], x_prompt.shape[1]
    bs = x_sample.shape[0]
    n_pool, page = cache_k.shape[1], cache_k.shape[2]
    past = page_table.shape[1] * page
    C = GDN_CHUNK

    w_in_p = _prep_w_in(w_in[l])
    wn_mix = w_norm_mix[l][None, :]
    pad_lanes = lambda v: jnp.pad(v, (0, LANES - v.shape[0]))[None, :]
    alog, dtb = pad_lanes(a_log[l]), pad_lanes(dt_bias[l])
    wgn = w_gnorm[l][None, :]
    lam_vecs = jnp.stack([lambda_q1[l], lambda_k1[l], lambda_q2[l], lambda_k2[l]])
    wsub = w_subln[l][None, :]
    w_router = jnp.pad(jnp.concatenate([w_expert[l], w_group[l]], axis=1),
                       ((0, 0), (0, LANES - N_EXPERTS - N_GROUPS)))
    ffn_w = (w_norm_ffn[l][None, :], w_router, w_gate[l].astype(BF16),
             w_up[l].astype(BF16), w_down[l].astype(BF16), w_final[None, :])

    pos_small = jnp.concatenate([jnp.arange(N_META), jnp.full((bs,), past)])
    rows_small = jnp.concatenate([meta_tokens, x_sample[:, 0, :]], axis=0)
    c_s, z_s, ab_s, q_s, k_s, v_s = _proj(rows_small, wn_mix, w_in_p, _rope_tables(pos_small),
                                           N_META + bs)
    xp = x_prompt.reshape(bp * seq, D_MODEL)
    c_p, z_p, ab_p, q_p, k_p, v_p = _proj(xp, wn_mix, w_in_p.astype(BF16),
                                           _rope_tables(N_META + jnp.arange(seq)), 512)

    front = lambda a: jnp.pad(a[:N_META], ((C - N_META, 0), (0, 0)))[None]
    zero_tail = jnp.zeros((1, 8, CONV_CH), F32)
    zero_state = jnp.zeros((1, GDN_HEADS, GDN_DK, GDN_DV), F32)
    _, s_meta = _gdn(front(c_s), front(z_s), front(ab_s), zero_tail, zero_state,
                     w_conv[l], alog, dtb, wgn, C)
    tail_meta = jnp.pad(c_s[N_META - (CONV_W - 1):N_META], ((8 - (CONV_W - 1), 0), (0, 0)))[None]
    c_p3 = c_p.reshape(bp, seq, CONV_CH)
    g_out, p_gdn = _gdn(c_p3, z_p.reshape(bp, seq, GDN_V_W), ab_p.reshape(bp, seq, AB_W),
                        tail_meta, s_meta, w_conv[l], alog, dtb, wgn, 256)
    p_conv = c_p3[:, seq - (CONV_W - 1):, :]

    with_meta = lambda m, p: jnp.concatenate(
        [jnp.broadcast_to(m[None, :N_META], (bp, N_META, m.shape[1])), p.reshape(bp, seq, -1)], axis=1)
    p_k = with_meta(k_s, k_p)
    p_v = with_meta(v_s, v_p)
    meta_blk = lambda m: jnp.pad(m[:N_META], ((0, LANES - N_META), (0, 0)))
    a_out = _attn_prompt(q_p.reshape(bp, seq, DA_QK_W), k_p.reshape(bp, seq, DA_QK_W),
                         v_p.reshape(bp, seq, DA_V_W), meta_blk(k_s), meta_blk(v_s),
                         lam_vecs, wsub, 256, 512)

    y_prompt = _ffn_sorted(g_out.reshape(bp * seq, GDN_V_W), a_out.reshape(bp * seq, DA_V_W), xp,
                           w_out[l].astype(BF16), *ffn_w).reshape(bp, seq, D_MODEL)

    sm = lambda a: a[N_META:]
    g_s, s_gdn = _gdn_step(sm(c_s), sm(z_s), sm(ab_s), jnp.swapaxes(state_conv[l], 0, 1),
                           state_gdn[l], w_conv[l], alog, dtb, wgn)
    s_conv = jnp.concatenate([state_conv[l][:, 1:, :], sm(c_s)[:, None, :]], axis=1)
    heads = lambda a: sm(a).reshape(bs, DA_HEADS, DA_DV)
    a_s = _attn_sample(heads(q_s), heads(k_s), heads(v_s),
                       cache_k[l].reshape(n_pool, page * DA_HEADS, 2 * DA_DQK),
                       cache_v[l].reshape(n_pool, page * DA_HEADS, DA_DV),
                       page_table, lam_vecs, wsub, 32)
    y_sample = _ffn(g_s, a_s.reshape(bs, DA_V_W), x_sample[:, 0, :], w_out[l], *ffn_w,
                    bs).reshape(bs, 1, D_MODEL)

    return (y_prompt, y_sample, p_conv[None], p_gdn[None],
            p_k.reshape(1, bp, seq + N_META, DA_HEADS, 2 * DA_DQK),
            p_v.reshape(1, bp, seq + N_META, DA_HEADS, DA_DV),
            s_conv[None], s_gdn[None],
            sm(k_s).reshape(1, bs, 1, DA_HEADS, 2 * DA_DQK),
            sm(v_s).reshape(1, bs, 1, DA_HEADS, DA_DV))
```

```python
import functools
import math

import jax
import jax.numpy as jnp
import numpy as np
from jax import lax
from jax.experimental import pallas as pl
from jax.experimental.pallas import tpu as pltpu

F32 = jnp.float32
BF16 = jnp.bfloat16

D_MODEL = 1024
N_META = 16
EPS = 1e-6
GDN_HEADS = 4
GDN_DK = 128
GDN_DV = 128
CONV_W = 4
GDN_QK_W = GDN_HEADS * GDN_DK
GDN_V_W = GDN_HEADS * GDN_DV
CONV_CH = 2 * GDN_QK_W + GDN_V_W
DA_HEADS = 4
DA_DQK = 64
DA_DV = 2 * DA_DQK
DA_QK_W = DA_HEADS * 2 * DA_DQK
DA_V_W = DA_HEADS * DA_DV
ROPE_DIM = DA_DQK // 4
ROPE_HALF = ROPE_DIM // 2
ROPE_THETA = 500000.0
N_GROUPS = 4
EXPERTS_PER_GROUP = 8
N_EXPERTS = N_GROUPS * EXPERTS_PER_GROUP
D_EXPERT = 256
LAM_INIT = 0.8 - 0.6 * math.exp(-0.3 * 0)

LANES = 128
GDN_CHUNK = 64
AB_W = LANES
PROJ_PAD_W = CONV_CH + GDN_V_W + 2 * DA_QK_W + DA_V_W + AB_W
VMEM_LIMIT = 56 * 1024 * 1024


_NN = (((1,), (0,)), ((), ()))
_NT = (((1,), (1,)), ((), ()))
_TN = (((0,), (0,)), ((), ()))


def _mm(a, b, dims=_NN, precise=False):
    if precise:
        return lax.dot_general(a.astype(F32), b.astype(F32), dims, preferred_element_type=F32,
                               precision=lax.Precision.HIGHEST)
    return lax.dot_general(a.astype(BF16), b.astype(BF16), dims, preferred_element_type=F32)


def _dot_nt(a, b):
    return _mm(a, b, _NT)


def _dot_f32(a, b):
    return _mm(a, b, _NN, True)


def _dot_nt_f32(a, b):
    return _mm(a, b, _NT, True)


def _sigmoid(x):
    return 1.0 / (1.0 + jnp.exp(-x))


def _softplus(x):
    return jnp.maximum(x, 0.0) + jnp.log(1.0 + jnp.exp(-jnp.abs(x)))


def _rms(x, w):
    return x * lax.rsqrt(jnp.mean(x * x, axis=-1, keepdims=True) + EPS) * w


def _rope(x, cs, sn_lo, sn_hi):
    w = x.shape[1]
    cs = jnp.concatenate([cs] * DA_HEADS, axis=1)
    sn_lo = jnp.concatenate([sn_lo] * DA_HEADS, axis=1)
    sn_hi = jnp.concatenate([sn_hi] * DA_HEADS, axis=1)
    up = pltpu.roll(x, w - ROPE_HALF, 1)
    dn = pltpu.roll(x, ROPE_HALF, 1)
    return x * cs + up * sn_lo + dn * sn_hi


def _proj_kernel(x_ref, wn_ref, w_ref, cs_ref, snl_ref, snh_ref,
                 c_ref, z_ref, ab_ref, q_ref, k_ref, v_ref, *, precise):
    h = _rms(x_ref[...], wn_ref[...])
    h = h if precise else h.astype(BF16)
    o = 0
    c_ref[...] = _mm(h, w_ref[:, o:o + CONV_CH], _NN, precise)
    o += CONV_CH
    z_ref[...] = _mm(h, w_ref[:, o:o + GDN_V_W], _NN, precise)
    o += GDN_V_W
    q = _mm(h, w_ref[:, o:o + DA_QK_W], _NN, precise)
    o += DA_QK_W
    k = _mm(h, w_ref[:, o:o + DA_QK_W], _NN, precise)
    o += DA_QK_W
    v_ref[...] = _mm(h, w_ref[:, o:o + DA_V_W], _NN, precise)
    o += DA_V_W
    ab_ref[...] = _mm(h, w_ref[:, o:o + AB_W], _NN, precise)
    cs, snl, snh = cs_ref[...], snl_ref[...], snh_ref[...]
    q_ref[...] = _rope(q, cs, snl, snh)
    k_ref[...] = _rope(k, cs, snl, snh)


def _proj(x, wn, w, tabs, tm):
    n = x.shape[0]
    assert n % tm == 0
    nt = tabs[0].shape[0] // tm
    row = lambda i: (i, 0)
    fixed = lambda i: (0, 0)
    tab = lambda i: (i % nt, 0)
    widths = (CONV_CH, GDN_V_W, AB_W, DA_QK_W, DA_QK_W, DA_V_W)
    return pl.pallas_call(
        functools.partial(_proj_kernel, precise=w.dtype == F32),
        grid=(n // tm,),
        in_specs=[pl.BlockSpec((tm, D_MODEL), row),
                  pl.BlockSpec((1, D_MODEL), fixed),
                  pl.BlockSpec((D_MODEL, PROJ_PAD_W), fixed),
                  pl.BlockSpec((tm, LANES), tab),
                  pl.BlockSpec((tm, LANES), tab),
                  pl.BlockSpec((tm, LANES), tab)],
        out_specs=[pl.BlockSpec((tm, wd), row) for wd in widths],
        out_shape=[jax.ShapeDtypeStruct((n, wd), F32) for wd in widths],
        compiler_params=pltpu.CompilerParams(dimension_semantics=("parallel",),
                                             vmem_limit_bytes=VMEM_LIMIT),
        name="proj",
    )(x, wn, w, *tabs)


def _rope_tables(pos):
    inv_freq = ROPE_THETA ** (-jnp.arange(ROPE_HALF, dtype=F32) * 2.0 / ROPE_DIM)
    ang = pos.astype(F32)[:, None] * inv_freq[None, :]
    cos, sin = jnp.cos(ang), jnp.sin(ang)
    t = pos.shape[0]
    one = jnp.ones((t, DA_DQK - ROPE_DIM), F32)
    zero = jnp.zeros((t, DA_DQK - ROPE_DIM), F32)
    zh = jnp.zeros((t, ROPE_HALF), F32)
    cs = jnp.concatenate([cos, cos, one] * 2, axis=1)
    snl = jnp.concatenate([-sin, zh, zero] * 2, axis=1)
    snh = jnp.concatenate([zh, sin, zero] * 2, axis=1)
    return cs, snl, snh


def _prep_w_in(w_in):
    c, z, a, b, dq, dk, dv = jnp.split(
        w_in, np.cumsum([CONV_CH, GDN_V_W, GDN_HEADS, GDN_HEADS, DA_QK_W, DA_QK_W]).tolist(), axis=-1)
    pad = jnp.zeros((D_MODEL, AB_W - 2 * GDN_HEADS), w_in.dtype)
    return jnp.concatenate([c, z, dq, dk, dv, a, b, pad], axis=-1)


def _gdn_kernel(c_ref, z_ref, ab_ref, tail_ref, s0_ref, wconv_ref, alog_ref, dtb_ref, wg_ref,
                o_ref, sfin_ref, cbuf, ybuf, s_scr, u0_scr, wq_scr, kw_scr, qkd_scr, el_scr, *, tt):
    C, H = GDN_CHUNK, GDN_HEADS
    R = H * C
    shift = C.bit_length() - 1
    t = pl.program_id(1)

    @pl.when(t == 0)
    def _():
        cbuf[0:8, :] = tail_ref[0]
        s_scr[...] = s0_ref[0]

    cbuf[8:8 + tt, :] = c_ref[0]
    acc = cbuf[8:8 + tt, :] * wconv_ref[CONV_W - 1:CONV_W, :]
    for i in range(CONV_W - 1):
        off = 8 - (CONV_W - 1) + i
        acc = acc + cbuf[off:off + tt, :] * wconv_ref[i:i + 1, :]
    ybuf[...] = acc * _sigmoid(acc)
    cbuf[0:8, :] = cbuf[tt:tt + 8, :]

    ii = lax.broadcasted_iota(jnp.int32, (R, R), 0)
    jj = lax.broadcasted_iota(jnp.int32, (R, R), 1)
    same_head = jnp.right_shift(ii, shift) == jnp.right_shift(jj, shift)
    strict = same_head & (ii > jj)
    causal = same_head & (ii >= jj)
    eye = (ii == jj).astype(F32)
    merge_masks = []
    s = 1
    while s < C:
        sh = s.bit_length() - 1
        merge_masks.append((jnp.right_shift(ii, sh + 1) == jnp.right_shift(jj, sh + 1))
                           & (jnp.right_shift(ii, sh) != jnp.right_shift(jj, sh)) & (ii > jj))
        s *= 2
    ci_ = lax.broadcasted_iota(jnp.int32, (C, C), 0)
    cj_ = lax.broadcasted_iota(jnp.int32, (C, C), 1)
    tri = (ci_ >= cj_).astype(F32)
    lane = lax.broadcasted_iota(jnp.int32, (R, LANES), 1)
    rhead = jnp.right_shift(lax.broadcasted_iota(jnp.int32, (R, LANES), 0), shift)
    ones8 = jnp.ones((8, LANES), F32)
    rep = lambda a: jnp.concatenate([a] * H, axis=0)
    pick = lambda a, off: jnp.sum(jnp.where(lane == rhead + off, a, 0.0), axis=-1, keepdims=True)

    def stacked(rows, base, scale):
        parts = []
        for h in range(H):
            x = ybuf[rows, base + h * GDN_DK:base + (h + 1) * GDN_DK]
            if scale is not None:
                x = x * lax.rsqrt(jnp.sum(x * x, axis=-1, keepdims=True) + EPS) * scale
            parts.append(x)
        return jnp.concatenate(parts, axis=0)

    chunks = range(tt // C)
    low, rhs, qes = [], [], []
    for ci in chunks:
        rows = slice(ci * C, (ci + 1) * C)
        ab = ab_ref[0, rows, :]
        gfull = -jnp.exp(alog_ref[...]) * _softplus(ab + dtb_ref[...])
        bfull = _sigmoid(ab)
        gcum = _dot_f32(tri, gfull)
        glast = gcum[C - 1:C, :]
        g_m = jnp.where(lane == rhead, rep(gcum), 0.0)
        g_col = jnp.sum(g_m, axis=-1, keepdims=True)
        g_row = _dot_nt_f32(ones8, g_m)[0:1, :]
        beta = pick(rep(bfull), H)
        gl_col = pick(jnp.broadcast_to(glast, (R, LANES)), 0)
        decay = jnp.exp(jnp.where(causal, g_col - g_row, -jnp.inf))
        q = stacked(rows, 0, GDN_DK ** -0.5)
        k = stacked(rows, GDN_QK_W, 1.0)
        v = stacked(rows, 2 * GDN_QK_W, None)
        kq = _mm(jnp.concatenate([k, q], axis=0), k, _NT)
        low.append(jnp.where(strict, kq[0:R] * decay * beta, 0.0))
        e_g = jnp.exp(g_col)
        rhs.append(jnp.concatenate([beta * v, (beta * e_g) * k], axis=1).astype(BF16))
        qes.append(q * e_g)
        kw_scr[ci] = (k * jnp.exp(gl_col - g_col)).astype(BF16)
        qkd_scr[ci] = jnp.where(causal, kq[R:2 * R] * decay, 0.0).astype(BF16)
        el_scr[ci] = jnp.broadcast_to(jnp.exp(glast), (8, LANES))
    tinv = [eye - jnp.where(merge_masks[0], low[ci], 0.0) for ci in chunks]
    for m in merge_masks[1:]:
        half = [_mm(tinv[ci], jnp.where(m, low[ci], 0.0)) for ci in chunks]
        tinv = [tinv[ci] - _mm(half[ci], tinv[ci]) for ci in chunks]
    for ci in chunks:
        w2 = _mm(tinv[ci], rhs[ci])
        u0_scr[ci] = w2[:, 0:GDN_DV]
        wk = w2[:, GDN_DV:]
        wq_scr[ci] = jnp.concatenate(
            [x[h * C:(h + 1) * C] for h in range(H) for x in (wk, qes[ci])], axis=0).astype(BF16)

    st = [s_scr[h] for h in range(H)]
    for ci in range(tt // C):
        rows = slice(ci * C, (ci + 1) * C)
        us, os_ = [], []
        for h in range(H):
            r = jnp.dot(wq_scr[ci, 2 * h * C:2 * (h + 1) * C, :], st[h].astype(BF16),
                        preferred_element_type=F32)
            us.append(u0_scr[ci, h * C:(h + 1) * C, :] - r[0:C])
            os_.append(r[C:2 * C])
        intra = jnp.dot(qkd_scr[ci], jnp.concatenate(us, axis=0).astype(BF16),
                        preferred_element_type=F32)
        el = el_scr[ci]
        for h in range(H):
            st[h] = st[h] * el[0:1, h:h + 1] + _mm(kw_scr[ci, h * C:(h + 1) * C, :], us[h], _TN)
            o = os_[h] + intra[h * C:(h + 1) * C]
            zz = z_ref[0, rows, h * GDN_DV:(h + 1) * GDN_DV]
            o_ref[0, rows, h * GDN_DV:(h + 1) * GDN_DV] = _rms(o, wg_ref[...]) * (zz * _sigmoid(zz))
    for h in range(H):
        s_scr[h] = st[h]

    @pl.when(t == pl.num_programs(1) - 1)
    def _():
        sfin_ref[0] = s_scr[...]


def _gdn_step_kernel(c_ref, z_ref, ab_ref, cs_ref, s_ref, wconv_ref, alog_ref, dtb_ref, wg_ref,
                     o_ref, snew_ref, *, nb):
    H = GDN_HEADS
    acc = c_ref[...] * wconv_ref[CONV_W - 1:CONV_W, :]
    for i in range(CONV_W - 1):
        acc = acc + cs_ref[i] * wconv_ref[i:i + 1, :]
    y = acc * _sigmoid(acc)
    ab = ab_ref[...]
    e_g = jnp.exp(-jnp.exp(alog_ref[...]) * _softplus(ab + dtb_ref[...]))
    bfull = _sigmoid(ab)
    unit = lambda x: x * lax.rsqrt(jnp.sum(x * x, axis=-1, keepdims=True) + EPS)
    qs = [unit(y[:, h * GDN_DK:(h + 1) * GDN_DK]) * (GDN_DK ** -0.5) for h in range(H)]
    ks = [unit(y[:, GDN_QK_W + h * GDN_DK:GDN_QK_W + (h + 1) * GDN_DK]) for h in range(H)]
    vs = [y[:, 2 * GDN_QK_W + h * GDN_DV:2 * GDN_QK_W + (h + 1) * GDN_DV] for h in range(H)]
    cols = jnp.concatenate(ks + qs, axis=0).T
    for h in range(H):
        ks_rows, qs_rows = [], []
        for s in range(nb):
            state = s_ref[s, h]
            kcol = cols[:, h * nb + s:h * nb + s + 1]
            qcol = cols[:, (H + h) * nb + s:(H + h) * nb + s + 1]
            ks_rows.append(jnp.sum(kcol * state, axis=0, keepdims=True))
            qs_rows.append(jnp.sum(qcol * state, axis=0, keepdims=True))
        eg_h = e_g[:, h:h + 1]
        u = bfull[:, H + h:H + h + 1] * (vs[h] - eg_h * jnp.concatenate(ks_rows, axis=0))
        o = (eg_h * jnp.concatenate(qs_rows, axis=0)
             + jnp.sum(qs[h] * ks[h], axis=-1, keepdims=True) * u)
        eg_b = jnp.broadcast_to(eg_h, (nb, GDN_DV))
        for s in range(nb):
            kcol = cols[:, h * nb + s:h * nb + s + 1]
            snew_ref[s, h] = eg_b[s:s + 1, :] * s_ref[s, h] + kcol * u[s:s + 1, :]
        zz = z_ref[:, h * GDN_DV:(h + 1) * GDN_DV]
        o_ref[:, h * GDN_DV:(h + 1) * GDN_DV] = _rms(o, wg_ref[...]) * (zz * _sigmoid(zz))


def _gdn_step(c, z, ab, conv_state, s, wconv, alog, dtb, wg):
    bs = c.shape[0]
    nb = LANES // (2 * GDN_HEADS)
    assert bs % nb == 0
    row = lambda i: (i, 0)
    fixed = lambda i: (0, 0)
    state = pl.BlockSpec((nb, GDN_HEADS, GDN_DK, GDN_DV), lambda i: (i, 0, 0, 0))
    return pl.pallas_call(
        functools.partial(_gdn_step_kernel, nb=nb),
        grid=(bs // nb,),
        in_specs=[pl.BlockSpec((nb, CONV_CH), row),
                  pl.BlockSpec((nb, GDN_V_W), row),
                  pl.BlockSpec((nb, AB_W), row),
                  pl.BlockSpec((CONV_W - 1, nb, CONV_CH), lambda i: (0, i, 0)),
                  state,
                  pl.BlockSpec((CONV_W, CONV_CH), fixed),
                  pl.BlockSpec((1, LANES), fixed),
                  pl.BlockSpec((1, LANES), fixed),
                  pl.BlockSpec((1, GDN_DV), fixed)],
        out_specs=[pl.BlockSpec((nb, GDN_V_W), row), state],
        out_shape=[jax.ShapeDtypeStruct((bs, GDN_V_W), F32),
                   jax.ShapeDtypeStruct(s.shape, F32)],
        compiler_params=pltpu.CompilerParams(dimension_semantics=("parallel",),
                                             vmem_limit_bytes=VMEM_LIMIT),
        name="gdn_step",
    )(c, z, ab, conv_state, s, wconv, alog, dtb, wg)


def _gdn(c, z, ab, tail, s0, wconv, alog, dtb, wg, tt):
    b, t, _ = c.shape
    assert t % tt == 0 and tt % GDN_CHUNK == 0
    nc, rs = tt // GDN_CHUNK, GDN_HEADS * GDN_CHUNK
    seq = lambda i, j: (i, j, 0)
    fixed2 = lambda i, j: (0, 0)
    tail_map = (lambda i, j: (i, 0, 0)) if tail.shape[0] == b else (lambda i, j: (0, 0, 0))
    s0_map = (lambda i, j: (i, 0, 0, 0)) if s0.shape[0] == b else (lambda i, j: (0, 0, 0, 0))
    return pl.pallas_call(
        functools.partial(_gdn_kernel, tt=tt),
        grid=(b, t // tt),
        in_specs=[pl.BlockSpec((1, tt, CONV_CH), seq),
                  pl.BlockSpec((1, tt, GDN_V_W), seq),
                  pl.BlockSpec((1, tt, AB_W), seq),
                  pl.BlockSpec((1, 8, CONV_CH), tail_map),
                  pl.BlockSpec((1, GDN_HEADS, GDN_DK, GDN_DV), s0_map),
                  pl.BlockSpec((CONV_W, CONV_CH), fixed2),
                  pl.BlockSpec((1, LANES), fixed2),
                  pl.BlockSpec((1, LANES), fixed2),
                  pl.BlockSpec((1, GDN_DV), fixed2)],
        out_specs=[pl.BlockSpec((1, tt, GDN_V_W), seq),
                   pl.BlockSpec((1, GDN_HEADS, GDN_DK, GDN_DV), lambda i, j: (i, 0, 0, 0))],
        out_shape=[jax.ShapeDtypeStruct((b, t, GDN_V_W), F32),
                   jax.ShapeDtypeStruct((b, GDN_HEADS, GDN_DK, GDN_DV), F32)],
        scratch_shapes=[pltpu.VMEM((tt + 8, CONV_CH), F32),
                        pltpu.VMEM((tt, CONV_CH), F32),
                        pltpu.VMEM((GDN_HEADS, GDN_DK, GDN_DV), F32),
                        pltpu.VMEM((nc, rs, GDN_DV), F32),
                        pltpu.VMEM((nc, 2 * rs, GDN_DK), BF16),
                        pltpu.VMEM((nc, rs, GDN_DK), BF16),
                        pltpu.VMEM((nc, rs, rs), BF16),
                        pltpu.VMEM((nc, 8, LANES), F32)],
        compiler_params=pltpu.CompilerParams(dimension_semantics=("parallel", "arbitrary"),
                                             vmem_limit_bytes=VMEM_LIMIT),
        name="gdn",
    )(c, z, ab, tail, s0, wconv, alog, dtb, wg)


def _lambda(lam_ref):
    lv = lam_ref[...]
    s1 = jnp.sum(lv[0:1, :] * lv[1:2, :], axis=-1, keepdims=True)
    s2 = jnp.sum(lv[2:3, :] * lv[3:4, :], axis=-1, keepdims=True)
    return jnp.exp(s1) - jnp.exp(s2) + LAM_INIT


def _split_components(q):
    lane = lax.broadcasted_iota(jnp.int32, q.shape, 1)
    first = (lane % DA_DV) < DA_DQK
    return jnp.where(first, q, 0.0).astype(BF16), jnp.where(first, 0.0, q).astype(BF16)


def _online_update(state, s, vb):
    m, l, acc = state
    m_new = jnp.maximum(m, jnp.max(s, axis=-1, keepdims=True))
    p = jnp.exp(s - m_new)
    alpha = jnp.exp(m - m_new)
    return (m_new, alpha * l + jnp.sum(p, axis=-1, keepdims=True),
            alpha * acc + jnp.dot(p.astype(BF16), vb, preferred_element_type=F32))


def _attn_prompt_kernel(q_ref, k_ref, v_ref, km_ref, vm_ref, lam_ref, wsub_ref, o_ref, *, bq, bk):
    qi = pl.program_id(2)
    q1, q2 = _split_components(q_ref[0] * (DA_DQK ** -0.5))
    neg = jnp.full((bq, 1), -jnp.inf, F32)
    zero = jnp.zeros((bq, 1), F32)
    zacc = jnp.zeros((bq, DA_DV), F32)
    init = ((neg, zero, zacc), (neg, zero, zacc))

    def block(kblk, vblk, mask, st):
        kb, vb = kblk.astype(BF16), vblk.astype(BF16)
        s1, s2 = _dot_nt(q1, kb), _dot_nt(q2, kb)
        if mask is not None:
            s1 = jnp.where(mask, s1, -jnp.inf)
            s2 = jnp.where(mask, s2, -jnp.inf)
        return (_online_update(st[0], s1, vb), _online_update(st[1], s2, vb))

    n_full = lax.shift_right_logical(qi * bq, bk.bit_length() - 1)

    def body(kb, st):
        r0 = pl.multiple_of(kb * bk, bk)
        return block(k_ref[0, pl.ds(r0, bk), :], v_ref[0, pl.ds(r0, bk), :], None, st)

    st = lax.fori_loop(0, n_full, body, init)
    r0 = pl.multiple_of(n_full * bk, bk)
    kcat = jnp.concatenate([km_ref[...], k_ref[0, pl.ds(r0, bk), :]], axis=0)
    vcat = jnp.concatenate([vm_ref[...], v_ref[0, pl.ds(r0, bk), :]], axis=0)
    ri = lax.broadcasted_iota(jnp.int32, (bq, LANES + bk), 0)
    ci = lax.broadcasted_iota(jnp.int32, (bq, LANES + bk), 1)
    seen = (ci < N_META) | ((ci >= LANES) & (ci - LANES <= ri + (qi * bq - n_full * bk)))
    st = block(kcat, vcat, seen, st)
    (m1, l1, a1), (m2, l2, a2) = st
    o = a1 / l1 - _lambda(lam_ref) * (a2 / l2)
    o_ref[0] = _rms(o, wsub_ref[...]) * (1.0 - LAM_INIT)


def _attn_prompt(q, k, v, k_meta, v_meta, lam_vecs, wsub, bq, bk):
    b, t, _ = q.shape
    assert t % bk == 0 and bk % bq == 0 and k_meta.shape[0] == LANES
    return pl.pallas_call(
        functools.partial(_attn_prompt_kernel, bq=bq, bk=bk),
        grid=(b, DA_HEADS, t // bq),
        in_specs=[pl.BlockSpec((1, bq, DA_DV), lambda i, h, j: (i, j, h)),
                  pl.BlockSpec((1, t, DA_DV), lambda i, h, j: (i, 0, h)),
                  pl.BlockSpec((1, t, DA_DV), lambda i, h, j: (i, 0, h)),
                  pl.BlockSpec((LANES, DA_DV), lambda i, h, j: (0, h)),
                  pl.BlockSpec((LANES, DA_DV), lambda i, h, j: (0, h)),
                  pl.BlockSpec((4, DA_DQK), lambda i, h, j: (0, 0)),
                  pl.BlockSpec((1, DA_DV), lambda i, h, j: (0, 0))],
        out_specs=pl.BlockSpec((1, bq, DA_DV), lambda i, h, j: (i, j, h)),
        out_shape=jax.ShapeDtypeStruct((b, t, DA_V_W), F32),
        compiler_params=pltpu.CompilerParams(
            dimension_semantics=("parallel", "parallel", "arbitrary"),
            vmem_limit_bytes=VMEM_LIMIT),
        name="attn_prompt",
    )(q, k, v, k_meta, v_meta, lam_vecs, wsub)


def _attn_sample_kernel(pt_ref, q_ref, kn_ref, vn_ref, lam_ref, wsub_ref, *rest, pps):
    k_refs, v_refs = rest[:pps], rest[pps:2 * pps]
    o_ref, m_scr, l_scr, acc_scr = rest[2 * pps:]
    step = pl.program_id(1)
    nrow = 2 * DA_HEADS
    twice = lambda x: jnp.concatenate([x, x], axis=0)
    rowq = lax.broadcasted_iota(jnp.int32, (nrow, DA_DV), 0)
    laneq = lax.broadcasted_iota(jnp.int32, (nrow, DA_DV), 1)
    own_comp = (laneq // DA_DQK) == (rowq // DA_HEADS)
    qm = jnp.where(own_comp, twice(q_ref[0]) * (DA_DQK ** -0.5), 0.0)
    page_rows = k_refs[0].shape[1]
    rowp = lax.broadcasted_iota(jnp.int32, (nrow, page_rows * pps), 0)
    colp = lax.broadcasted_iota(jnp.int32, (nrow, page_rows * pps), 1)
    own_head = (colp % DA_HEADS) == (rowp % DA_HEADS)

    def rows3(x):
        hi = x.astype(BF16).astype(F32)
        mid = (x - hi).astype(BF16).astype(F32)
        return jnp.concatenate([hi, mid, x - hi - mid], axis=0).astype(BF16)

    def unrows3(r):
        return (r[0:nrow] + r[nrow:2 * nrow]) + r[2 * nrow:3 * nrow]

    def hi_lo(x):
        hi = x.astype(BF16)
        return hi, (x - hi.astype(F32)).astype(BF16)

    @pl.when(step == 0)
    def _():
        s_new = jnp.sum(qm * twice(kn_ref[0]), axis=-1, keepdims=True)
        m_scr[...] = s_new
        l_scr[...] = jnp.ones_like(s_new)
        acc_scr[...] = twice(vn_ref[0])

    q3 = rows3(qm)
    scores = []
    for r in k_refs:
        k_hi, k_lo = hi_lo(r[0])
        scores.append(unrows3(_dot_nt(q3, k_hi) + _dot_nt(q3, k_lo)))
    s = jnp.where(own_head, jnp.concatenate(scores, axis=1), -jnp.inf)
    m_old = m_scr[...]
    m_new = jnp.maximum(m_old, jnp.max(s, axis=-1, keepdims=True))
    p = jnp.exp(s - m_new)
    alpha = jnp.exp(m_old - m_new)
    p3 = rows3(p)
    pv = jnp.zeros((3 * nrow, DA_DV), F32)
    for j, r in enumerate(v_refs):
        v_hi, v_lo = hi_lo(r[0])
        pj = p3[:, j * page_rows:(j + 1) * page_rows]
        pv = pv + (jnp.dot(pj, v_hi, preferred_element_type=F32)
                   + jnp.dot(pj, v_lo, preferred_element_type=F32))
    m_scr[...] = m_new
    l_scr[...] = alpha * l_scr[...] + jnp.sum(p, axis=-1, keepdims=True)
    acc_scr[...] = alpha * acc_scr[...] + unrows3(pv)

    @pl.when(step == pl.num_programs(1) - 1)
    def _():
        w = acc_scr[...] / l_scr[...]
        o = w[0:DA_HEADS, :] - _lambda(lam_ref) * w[DA_HEADS:nrow, :]
        o_ref[0] = _rms(o, wsub_ref[...]) * (1.0 - LAM_INIT)


def _attn_sample(q, k_new, v_new, cache_k, cache_v, page_table, lam_vecs, wsub, pps):
    bs = q.shape[0]
    n_pages = page_table.shape[1]
    page_rows = cache_k.shape[1]
    assert n_pages % pps == 0
    tok = lambda i, s, pt: (i, 0, 0)
    fixed = lambda i, s, pt: (0, 0)

    def page_spec(j):
        return pl.BlockSpec((1, page_rows, DA_DV), lambda i, s, pt: (pt[i, s * pps + j], 0, 0))

    grid_spec = pltpu.PrefetchScalarGridSpec(
        num_scalar_prefetch=1,
        grid=(bs, n_pages // pps),
        in_specs=[pl.BlockSpec((1, DA_HEADS, DA_DV), tok),
                  pl.BlockSpec((1, DA_HEADS, DA_DV), tok),
                  pl.BlockSpec((1, DA_HEADS, DA_DV), tok),
                  pl.BlockSpec((4, DA_DQK), fixed),
                  pl.BlockSpec((1, DA_DV), fixed)]
                 + [page_spec(j) for j in range(pps)] * 2,
        out_specs=pl.BlockSpec((1, DA_HEADS, DA_DV), tok),
        scratch_shapes=[pltpu.VMEM((2 * DA_HEADS, 1), F32),
                        pltpu.VMEM((2 * DA_HEADS, 1), F32),
                        pltpu.VMEM((2 * DA_HEADS, DA_DV), F32)])
    return pl.pallas_call(
        functools.partial(_attn_sample_kernel, pps=pps),
        grid_spec=grid_spec,
        out_shape=jax.ShapeDtypeStruct((bs, DA_HEADS, DA_DV), F32),
        compiler_params=pltpu.CompilerParams(dimension_semantics=("parallel", "arbitrary"),
                                             vmem_limit_bytes=VMEM_LIMIT),
        name="attn_sample",
    )(page_table, q, k_new, v_new, lam_vecs, wsub, *([cache_k] * pps), *([cache_v] * pps))


def _route(logits):
    lane = lax.broadcasted_iota(jnp.int32, logits.shape, 1).astype(F32)
    big = float(LANES)
    ninf = -jnp.inf

    def top(mask):
        val = jnp.where(mask, logits, ninf)
        mx = jnp.max(val, axis=-1, keepdims=True)
        idx = jnp.min(jnp.where(mask & (val == mx), lane, big), axis=-1, keepdims=True)
        return mx, idx

    gmask = (lane >= N_EXPERTS) & (lane < N_EXPERTS + N_GROUPS)
    gmax, gidx = top(gmask)
    g_w = 1.0 / jnp.sum(jnp.where(gmask, jnp.exp(logits - gmax), 0.0), axis=-1, keepdims=True)
    g_sel = gidx - N_EXPERTS
    emask = (lane >= g_sel * EXPERTS_PER_GROUP) & (lane < (g_sel + 1) * EXPERTS_PER_GROUP)
    v1, i1 = top(emask)
    v2, i2 = top(emask & (lane != i1))
    e21 = jnp.exp(v2 - v1)
    p1 = 1.0 / (1.0 + e21)
    return jnp.where(lane == i1, p1 * g_w, 0.0) + jnp.where(lane == i2, e21 * p1 * g_w, 0.0), g_sel


def _mix_route(g_ref, a_ref, x_ref, wo_ref, wnf_ref, wr_ref, precise):
    mix = (_mm(g_ref[...], wo_ref[0:GDN_V_W, :], _NN, precise)
           + _mm(a_ref[...], wo_ref[GDN_V_W:, :], _NN, precise))
    x1 = x_ref[...] + mix
    xn = _rms(x1, wnf_ref[...])
    wr = wr_ref[...]
    if precise:
        logits = _dot_f32(xn, wr)
    else:
        x_hi, w_hi = xn.astype(BF16), wr.astype(BF16)
        x_lo, w_lo = (xn - x_hi.astype(F32)).astype(BF16), (wr - w_hi.astype(F32)).astype(BF16)
        logits = (jnp.dot(x_hi, w_hi, preferred_element_type=F32)
                  + (jnp.dot(x_hi, w_lo, preferred_element_type=F32)
                     + jnp.dot(x_lo, w_hi, preferred_element_type=F32)))
    comb, g_sel = _route(logits)
    return x1, xn, comb, g_sel


def _expert(xn, comb, e_lane, wg, wu, wd):
    hg = jnp.dot(xn, wg, preferred_element_type=F32)
    hu = jnp.dot(xn, wu, preferred_element_type=F32)
    lane = lax.broadcasted_iota(jnp.int32, comb.shape, 1)
    cw = jnp.sum(jnp.where(lane == e_lane, comb, 0.0), axis=-1, keepdims=True)
    hh = (hg * _sigmoid(hg)) * hu * cw
    return jnp.dot(hh.astype(BF16), wd, preferred_element_type=F32)


def _ffn_kernel(g_ref, a_ref, x_ref, wo_ref, wnf_ref, wr_ref, wg_ref, wu_ref, wd_ref, wfin_ref,
                y_ref, x1_scr, xn_scr, comb_scr, acc_scr, *, precise):
    e = pl.program_id(1)

    @pl.when(e == 0)
    def _():
        x1, xn, comb, _ = _mix_route(g_ref, a_ref, x_ref, wo_ref, wnf_ref, wr_ref, precise)
        x1_scr[...] = x1
        xn_scr[...] = xn.astype(BF16)
        comb_scr[...] = comb
        acc_scr[...] = jnp.zeros_like(acc_scr)

    acc_scr[...] += _expert(xn_scr[...], comb_scr[...], e, wg_ref[0], wu_ref[0], wd_ref[0])

    @pl.when(e == pl.num_programs(1) - 1)
    def _():
        y_ref[...] = _rms(x1_scr[...] + acc_scr[...], wfin_ref[...])


def _ffn(g_out, a_out, x, wo, wnf, wr, wg, wu, wd, wfin, tm):
    n = x.shape[0]
    assert n % tm == 0
    row = lambda i, e: (i, 0)
    fixed = lambda i, e: (0, 0)
    exp = lambda i, e: (e, 0, 0)
    return pl.pallas_call(
        functools.partial(_ffn_kernel, precise=wo.dtype == F32),
        grid=(n // tm, N_EXPERTS),
        in_specs=[pl.BlockSpec((tm, GDN_V_W), row),
                  pl.BlockSpec((tm, DA_V_W), row),
                  pl.BlockSpec((tm, D_MODEL), row),
                  pl.BlockSpec((GDN_V_W + DA_V_W, D_MODEL), fixed),
                  pl.BlockSpec((1, D_MODEL), fixed),
                  pl.BlockSpec((D_MODEL, LANES), fixed),
                  pl.BlockSpec((1, D_MODEL, D_EXPERT), exp),
                  pl.BlockSpec((1, D_MODEL, D_EXPERT), exp),
                  pl.BlockSpec((1, D_EXPERT, D_MODEL), exp),
                  pl.BlockSpec((1, D_MODEL), fixed)],
        out_specs=pl.BlockSpec((tm, D_MODEL), row),
        out_shape=jax.ShapeDtypeStruct((n, D_MODEL), F32),
        scratch_shapes=[pltpu.VMEM((tm, D_MODEL), F32),
                        pltpu.VMEM((tm, D_MODEL), BF16),
                        pltpu.VMEM((tm, LANES), F32),
                        pltpu.VMEM((tm, D_MODEL), F32)],
        compiler_params=pltpu.CompilerParams(dimension_semantics=("parallel", "arbitrary"),
                                             vmem_limit_bytes=VMEM_LIMIT),
        name="ffn",
    )(g_out, a_out, x, wo, wnf, wr, wg, wu, wd, wfin)


ROW_W = D_MODEL + LANES
MOE_TILE = 512
ROW_TILE = 1024


def _ffn_pre_kernel(g_ref, a_ref, x_ref, wo_ref, wnf_ref, wr_ref, rows_ref, info_ref, cnt_ref,
                    cnt_scr):
    @pl.when(pl.program_id(0) == 0)
    def _():
        cnt_scr[...] = jnp.zeros_like(cnt_scr)

    x1, _, comb, g_sel = _mix_route(g_ref, a_ref, x_ref, wo_ref, wnf_ref, wr_ref, False)
    tm = x1.shape[0]
    lane = lax.broadcasted_iota(jnp.int32, (tm, LANES), 1).astype(F32)
    onehot = lane == g_sel
    ri = lax.broadcasted_iota(jnp.int32, (tm, tm), 0)
    ci = lax.broadcasted_iota(jnp.int32, (tm, tm), 1)
    earlier = jnp.dot((ci < ri).astype(BF16), onehot.astype(BF16), preferred_element_type=F32)
    rank = jnp.sum(jnp.where(onehot, earlier + cnt_scr[...], 0.0), axis=-1, keepdims=True)
    cnt_scr[...] += jnp.sum(onehot.astype(F32), axis=0, keepdims=True)
    cnt_ref[...] = cnt_scr[...]
    rows_ref[:, 0:D_MODEL] = x1
    rows_ref[:, D_MODEL:] = comb
    packed = jnp.where(lane == 0.0, g_sel, jnp.where(lane == 1.0, rank, 0.0))
    r8 = lax.broadcasted_iota(jnp.int32, (8, LANES), 0)
    l8 = lax.broadcasted_iota(jnp.int32, (8, LANES), 1)
    info_ref[...] = _dot_nt_f32((r8 == l8).astype(F32), packed)


def _ffn_pre(g_out, a_out, x, wo, wnf, wr, tm):
    n = x.shape[0]
    assert n % tm == 0
    row = lambda i: (i, 0)
    fixed = lambda i: (0, 0)
    return pl.pallas_call(
        _ffn_pre_kernel,
        grid=(n // tm,),
        in_specs=[pl.BlockSpec((tm, GDN_V_W), row),
                  pl.BlockSpec((tm, DA_V_W), row),
                  pl.BlockSpec((tm, D_MODEL), row),
                  pl.BlockSpec((GDN_V_W + DA_V_W, D_MODEL), fixed),
                  pl.BlockSpec((1, D_MODEL), fixed),
                  pl.BlockSpec((D_MODEL, LANES), fixed)],
        out_specs=[pl.BlockSpec((tm, ROW_W), row),
                   pl.BlockSpec((8, tm), lambda i: (0, i)),
                   pl.BlockSpec((1, LANES), fixed)],
        out_shape=[jax.ShapeDtypeStruct((n, ROW_W), F32),
                   jax.ShapeDtypeStruct((8, n), F32),
                   jax.ShapeDtypeStruct((1, LANES), F32)],
        scratch_shapes=[pltpu.VMEM((1, LANES), F32)],
        compiler_params=pltpu.CompilerParams(dimension_semantics=("arbitrary",),
                                             vmem_limit_bytes=VMEM_LIMIT),
        name="ffn_pre",
    )(g_out, a_out, x, wo, wnf, wr)


def _copy_rows(idx_ref, tile_ref, hbm_ref, sem, scatter):
    tm = tile_ref.shape[0]
    base = pl.program_id(0) * tm

    def body(r, carry):
        there, here = hbm_ref.at[pl.ds(idx_ref[base + r], 1)], tile_ref.at[pl.ds(r, 1)]
        src, dst = (here, there) if scatter else (there, here)
        pltpu.make_async_copy(src, dst, sem).start()
        return carry

    lax.fori_loop(0, tm, body, 0, unroll=8)
    there, here = hbm_ref.at[pl.ds(0, tm)], tile_ref
    src, dst = (here, there) if scatter else (there, here)
    pltpu.make_async_copy(src, dst, sem).wait()


def _scatter_kernel(idx_ref, tile_ref, init_ref, dst_ref, sem):
    del init_ref
    _copy_rows(idx_ref, tile_ref, dst_ref, sem, True)


def _gather_kernel(idx_ref, src_ref, tile_ref, sem):
    _copy_rows(idx_ref, tile_ref, src_ref, sem, False)


def _move_rows(idx, src, n_out, scatter):
    n, w = idx.shape[0], src.shape[1]
    assert n % ROW_TILE == 0
    any_spec = pl.BlockSpec(memory_space=pl.ANY)
    tile_spec = pl.BlockSpec((ROW_TILE, w), lambda i, idx: (i, 0))
    if scatter:
        operands = (idx, src, jnp.zeros((n_out, w), src.dtype))
        in_specs, out_spec = [tile_spec, any_spec], any_spec
    else:
        operands = (idx, src)
        in_specs, out_spec = [any_spec], tile_spec
    return pl.pallas_call(
        _scatter_kernel if scatter else _gather_kernel,
        grid_spec=pltpu.PrefetchScalarGridSpec(
            num_scalar_prefetch=1, grid=(n // ROW_TILE,),
            in_specs=in_specs, out_specs=out_spec,
            scratch_shapes=[pltpu.SemaphoreType.DMA(())]),
        out_shape=jax.ShapeDtypeStruct((n_out, w), src.dtype),
        input_output_aliases={2: 0} if scatter else {},
        compiler_params=pltpu.CompilerParams(dimension_semantics=("arbitrary",)),
        name="scatter_rows" if scatter else "gather_rows",
    )(*operands)


def _moe_groups_kernel(tg_ref, nt_ref, rows_ref, wnf_ref, wg_ref, wu_ref, wd_ref, wfin_ref, y_ref):
    j = pl.program_id(0)

    @pl.when(j < nt_ref[0])
    def _():
        x1 = rows_ref[:, 0:D_MODEL]
        comb = rows_ref[:, D_MODEL:]
        xn = _rms(x1, wnf_ref[...]).astype(BF16)
        first = tg_ref[j] * EXPERTS_PER_GROUP
        acc = jnp.zeros_like(x1)
        for e in range(EXPERTS_PER_GROUP):
            acc = acc + _expert(xn, comb, first + e, wg_ref[e], wu_ref[e], wd_ref[e])
        y_ref[...] = _rms(x1 + acc, wfin_ref[...])

    @pl.when(j >= nt_ref[0])
    def _():
        y_ref[...] = jnp.zeros_like(y_ref)


def _moe_groups(tile_group, n_tiles_used, rows, wnf, wg, wu, wd, wfin):
    n_tiles = tile_group.shape[0]
    fixed = lambda j, tg, nt: (0, 0)
    grp = lambda j, tg, nt: (tg[j], 0, 0)
    return pl.pallas_call(
        _moe_groups_kernel,
        grid_spec=pltpu.PrefetchScalarGridSpec(
            num_scalar_prefetch=2, grid=(n_tiles,),
            in_specs=[pl.BlockSpec((MOE_TILE, ROW_W), lambda j, tg, nt: (j, 0)),
                      pl.BlockSpec((1, D_MODEL), fixed),
                      pl.BlockSpec((EXPERTS_PER_GROUP, D_MODEL, D_EXPERT), grp),
                      pl.BlockSpec((EXPERTS_PER_GROUP, D_MODEL, D_EXPERT), grp),
                      pl.BlockSpec((EXPERTS_PER_GROUP, D_EXPERT, D_MODEL), grp),
                      pl.BlockSpec((1, D_MODEL), fixed)],
            out_specs=pl.BlockSpec((MOE_TILE, D_MODEL), lambda j, tg, nt: (j, 0))),
        out_shape=jax.ShapeDtypeStruct((n_tiles * MOE_TILE, D_MODEL), F32),
        compiler_params=pltpu.CompilerParams(dimension_semantics=("arbitrary",),
                                             vmem_limit_bytes=VMEM_LIMIT),
        name="moe_groups",
    )(tile_group, n_tiles_used, rows, wnf, wg, wu, wd, wfin)


def _ffn_sorted(g_out, a_out, x, wo, wnf, wr, wg, wu, wd, wfin):
    n = x.shape[0]
    rows, info, cnt = _ffn_pre(g_out, a_out, x, wo, wnf, wr, 512)
    grp, rank = info[0].astype(jnp.int32), info[1].astype(jnp.int32)
    tiles = (cnt[0, :N_GROUPS].astype(jnp.int32) + MOE_TILE - 1) // MOE_TILE
    ends = jnp.cumsum(tiles)
    dest = (ends - tiles)[grp] * MOE_TILE + rank
    n_tiles = n // MOE_TILE + N_GROUPS
    tile_group = jnp.minimum(
        jnp.sum(jnp.arange(n_tiles, dtype=jnp.int32)[:, None] >= ends[None, :], axis=1),
        N_GROUPS - 1).astype(jnp.int32)
    rows_sorted = _move_rows(dest, rows, n_tiles * MOE_TILE, True)
    y_sorted = _moe_groups(tile_group, ends[N_GROUPS - 1:], rows_sorted, wnf, wg, wu, wd, wfin)
    return _move_rows(dest, y_sorted, n, False)


def kernel(x_prompt, x_sample, cache_k, cache_v, state_conv, state_gdn, page_table, meta_tokens,
           w_norm_mix, w_in, w_conv, a_log, dt_bias, w_gnorm, lambda_q1, lambda_k1, lambda_q2,
           lambda_k2, w_subln, w_out, w_norm_ffn, w_group, w_expert, w_gate, w_up, w_down, w_final):
    l = 0
    bp, seq = x_prompt.shape[0], x_prompt.shape[1]
    bs = x_sample.shape[0]
    n_pool, page = cache_k.shape[1], cache_k.shape[2]
    past = page_table.shape[1] * page
    C = GDN_CHUNK

    w_in_p = _prep_w_in(w_in[l])
    wn_mix = w_norm_mix[l][None, :]
    pad_lanes = lambda v: jnp.pad(v, (0, LANES - v.shape[0]))[None, :]
    alog, dtb = pad_lanes(a_log[l]), pad_lanes(dt_bias[l])
    wgn = w_gnorm[l][None, :]
    lam_vecs = jnp.stack([lambda_q1[l], lambda_k1[l], lambda_q2[l], lambda_k2[l]])
    wsub = w_subln[l][None, :]
    w_router = jnp.pad(jnp.concatenate([w_expert[l], w_group[l]], axis=1),
                       ((0, 0), (0, LANES - N_EXPERTS - N_GROUPS)))
    ffn_w = (w_norm_ffn[l][None, :], w_router, w_gate[l].astype(BF16),
             w_up[l].astype(BF16), w_down[l].astype(BF16), w_final[None, :])

    pos_small = jnp.concatenate([jnp.arange(N_META), jnp.full((bs,), past)])
    rows_small = jnp.concatenate([meta_tokens, x_sample[:, 0, :]], axis=0)
    c_s, z_s, ab_s, q_s, k_s, v_s = _proj(rows_small, wn_mix, w_in_p, _rope_tables(pos_small),
                                           N_META + bs)
    xp = x_prompt.reshape(bp * seq, D_MODEL)
    c_p, z_p, ab_p, q_p, k_p, v_p = _proj(xp, wn_mix, w_in_p.astype(BF16),
                                           _rope_tables(N_META + jnp.arange(seq)), 512)

    front = lambda a: jnp.pad(a[:N_META], ((C - N_META, 0), (0, 0)))[None]
    zero_tail = jnp.zeros((1, 8, CONV_CH), F32)
    zero_state = jnp.zeros((1, GDN_HEADS, GDN_DK, GDN_DV), F32)
    _, s_meta = _gdn(front(c_s), front(z_s), front(ab_s), zero_tail, zero_state,
                     w_conv[l], alog, dtb, wgn, C)
    tail_meta = jnp.pad(c_s[N_META - (CONV_W - 1):N_META], ((8 - (CONV_W - 1), 0), (0, 0)))[None]
    c_p3 = c_p.reshape(bp, seq, CONV_CH)
    g_out, p_gdn = _gdn(c_p3, z_p.reshape(bp, seq, GDN_V_W), ab_p.reshape(bp, seq, AB_W),
                        tail_meta, s_meta, w_conv[l], alog, dtb, wgn, 256)
    p_conv = c_p3[:, seq - (CONV_W - 1):, :]

    with_meta = lambda m, p: jnp.concatenate(
        [jnp.broadcast_to(m[None, :N_META], (bp, N_META, m.shape[1])), p.reshape(bp, seq, -1)], axis=1)
    p_k = with_meta(k_s, k_p)
    p_v = with_meta(v_s, v_p)
    meta_blk = lambda m: jnp.pad(m[:N_META], ((0, LANES - N_META), (0, 0)))
    a_out = _attn_prompt(q_p.reshape(bp, seq, DA_QK_W), k_p.reshape(bp, seq, DA_QK_W),
                         v_p.reshape(bp, seq, DA_V_W), meta_blk(k_s), meta_blk(v_s),
                         lam_vecs, wsub, 256, 512)

    y_prompt = _ffn_sorted(g_out.reshape(bp * seq, GDN_V_W), a_out.reshape(bp * seq, DA_V_W), xp,
                           w_out[l].astype(BF16), *ffn_w).reshape(bp, seq, D_MODEL)

    sm = lambda a: a[N_META:]
    g_s, s_gdn = _gdn_step(sm(c_s), sm(z_s), sm(ab_s), jnp.swapaxes(state_conv[l], 0, 1),
                           state_gdn[l], w_conv[l], alog, dtb, wgn)
    s_conv = jnp.concatenate([state_conv[l][:, 1:, :], sm(c_s)[:, None, :]], axis=1)
    heads = lambda a: sm(a).reshape(bs, DA_HEADS, DA_DV)
    a_s = _attn_sample(heads(q_s), heads(k_s), heads(v_s),
                       cache_k[l].reshape(n_pool, page * DA_HEADS, 2 * DA_DQK),
                       cache_v[l].reshape(n_pool, page * DA_HEADS, DA_DV),
                       page_table, lam_vecs, wsub, 32)
    y_sample = _ffn(g_s, a_s.reshape(bs, DA_V_W), x_sample[:, 0, :], w_out[l], *ffn_w,
                    bs).reshape(bs, 1, D_MODEL)

    return (y_prompt, y_sample, p_conv[None], p_gdn[None],
            p_k.reshape(1, bp, seq + N_META, DA_HEADS, 2 * DA_DQK),
            p_v.reshape(1, bp, seq + N_META, DA_HEADS, DA_DV),
            s_conv[None], s_gdn[None],
            sm(k_s).reshape(1, bs, 1, DA_HEADS, 2 * DA_DQK),
            sm(v_s).reshape(1, bs, 1, DA_HEADS, DA_DV))
```

```python
import functools
import math

import jax
import jax.numpy as jnp
import numpy as np
from jax import lax
from jax.experimental import pallas as pl
from jax.experimental.pallas import tpu as pltpu

F32 = jnp.float32
BF16 = jnp.bfloat16

D_MODEL = 1024
N_META = 16
EPS = 1e-6
GDN_HEADS = 4
GDN_DK = 128
GDN_DV = 128
CONV_W = 4
GDN_QK_W = GDN_HEADS * GDN_DK
GDN_V_W = GDN_HEADS * GDN_DV
CONV_CH = 2 * GDN_QK_W + GDN_V_W
DA_HEADS = 4
DA_DQK = 64
DA_DV = 2 * DA_DQK
DA_QK_W = DA_HEADS * 2 * DA_DQK
DA_V_W = DA_HEADS * DA_DV
ROPE_DIM = DA_DQK // 4
ROPE_HALF = ROPE_DIM // 2
ROPE_THETA = 500000.0
N_GROUPS = 4
EXPERTS_PER_GROUP = 8
N_EXPERTS = N_GROUPS * EXPERTS_PER_GROUP
D_EXPERT = 256
LAM_INIT = 0.8 - 0.6 * math.exp(-0.3 * 0)

LANES = 128
GDN_CHUNK = 64
AB_W = LANES
PROJ_PAD_W = CONV_CH + GDN_V_W + 2 * DA_QK_W + DA_V_W + AB_W
VMEM_LIMIT = 56 * 1024 * 1024


_NN = (((1,), (0,)), ((), ()))
_NT = (((1,), (1,)), ((), ()))
_TN = (((0,), (0,)), ((), ()))


def _mm(a, b, dims=_NN, precise=False):
    if precise:
        return lax.dot_general(a.astype(F32), b.astype(F32), dims, preferred_element_type=F32,
                               precision=lax.Precision.HIGHEST)
    return lax.dot_general(a.astype(BF16), b.astype(BF16), dims, preferred_element_type=F32)


def _dot_nt(a, b):
    return _mm(a, b, _NT)


def _dot_f32(a, b):
    return _mm(a, b, _NN, True)


def _dot_nt_f32(a, b):
    return _mm(a, b, _NT, True)


def _sigmoid(x):
    return 1.0 / (1.0 + jnp.exp(-x))


def _softplus(x):
    return jnp.maximum(x, 0.0) + jnp.log(1.0 + jnp.exp(-jnp.abs(x)))


def _rms(x, w):
    return x * lax.rsqrt(jnp.mean(x * x, axis=-1, keepdims=True) + EPS) * w


def _rope(x, cs, sn_lo, sn_hi):
    w = x.shape[1]
    cs = jnp.concatenate([cs] * DA_HEADS, axis=1)
    sn_lo = jnp.concatenate([sn_lo] * DA_HEADS, axis=1)
    sn_hi = jnp.concatenate([sn_hi] * DA_HEADS, axis=1)
    up = pltpu.roll(x, w - ROPE_HALF, 1)
    dn = pltpu.roll(x, ROPE_HALF, 1)
    return x * cs + up * sn_lo + dn * sn_hi


def _proj_kernel(x_ref, wn_ref, w_ref, cs_ref, snl_ref, snh_ref,
                 c_ref, z_ref, ab_ref, q_ref, k_ref, v_ref, *, precise):
    h = _rms(x_ref[...], wn_ref[...])
    h = h if precise else h.astype(BF16)
    o = 0
    c_ref[...] = _mm(h, w_ref[:, o:o + CONV_CH], _NN, precise)
    o += CONV_CH
    z_ref[...] = _mm(h, w_ref[:, o:o + GDN_V_W], _NN, precise)
    o += GDN_V_W
    q = _mm(h, w_ref[:, o:o + DA_QK_W], _NN, precise)
    o += DA_QK_W
    k = _mm(h, w_ref[:, o:o + DA_QK_W], _NN, precise)
    o += DA_QK_W
    v_ref[...] = _mm(h, w_ref[:, o:o + DA_V_W], _NN, precise)
    o += DA_V_W
    ab_ref[...] = _mm(h, w_ref[:, o:o + AB_W], _NN, precise)
    cs, snl, snh = cs_ref[...], snl_ref[...], snh_ref[...]
    q_ref[...] = _rope(q, cs, snl, snh)
    k_ref[...] = _rope(k, cs, snl, snh)


def _proj(x, wn, w, tabs, tm):
    n = x.shape[0]
    assert n % tm == 0
    nt = tabs[0].shape[0] // tm
    row = lambda i: (i, 0)
    fixed = lambda i: (0, 0)
    tab = lambda i: (i % nt, 0)
    widths = (CONV_CH, GDN_V_W, AB_W, DA_QK_W, DA_QK_W, DA_V_W)
    return pl.pallas_call(
        functools.partial(_proj_kernel, precise=w.dtype == F32),
        grid=(n // tm,),
        in_specs=[pl.BlockSpec((tm, D_MODEL), row),
                  pl.BlockSpec((1, D_MODEL), fixed),
                  pl.BlockSpec((D_MODEL, PROJ_PAD_W), fixed),
                  pl.BlockSpec((tm, LANES), tab),
                  pl.BlockSpec((tm, LANES), tab),
                  pl.BlockSpec((tm, LANES), tab)],
        out_specs=[pl.BlockSpec((tm, wd), row) for wd in widths],
        out_shape=[jax.ShapeDtypeStruct((n, wd), F32) for wd in widths],
        compiler_params=pltpu.CompilerParams(dimension_semantics=("parallel",),
                                             vmem_limit_bytes=VMEM_LIMIT),
        name="proj",
    )(x, wn, w, *tabs)


def _rope_tables(pos):
    inv_freq = ROPE_THETA ** (-jnp.arange(ROPE_HALF, dtype=F32) * 2.0 / ROPE_DIM)
    ang = pos.astype(F32)[:, None] * inv_freq[None, :]
    cos, sin = jnp.cos(ang), jnp.sin(ang)
    t = pos.shape[0]
    one = jnp.ones((t, DA_DQK - ROPE_DIM), F32)
    zero = jnp.zeros((t, DA_DQK - ROPE_DIM), F32)
    zh = jnp.zeros((t, ROPE_HALF), F32)
    cs = jnp.concatenate([cos, cos, one] * 2, axis=1)
    snl = jnp.concatenate([-sin, zh, zero] * 2, axis=1)
    snh = jnp.concatenate([zh, sin, zero] * 2, axis=1)
    return cs, snl, snh


def _prep_w_in(w_in):
    c, z, a, b, dq, dk, dv = jnp.split(
        w_in, np.cumsum([CONV_CH, GDN_V_W, GDN_HEADS, GDN_HEADS, DA_QK_W, DA_QK_W]).tolist(), axis=-1)
    pad = jnp.zeros((D_MODEL, AB_W - 2 * GDN_HEADS), w_in.dtype)
    return jnp.concatenate([c, z, dq, dk, dv, a, b, pad], axis=-1)


def _gdn_kernel(c_ref, z_ref, ab_ref, tail_ref, s0_ref, wconv_ref, alog_ref, dtb_ref, wg_ref,
                o_ref, sfin_ref, cbuf, ybuf, s_scr, u0_scr, wq_scr, kw_scr, qkd_scr, el_scr, *, tt):
    C, H = GDN_CHUNK, GDN_HEADS
    R = H * C
    shift = C.bit_length() - 1
    t = pl.program_id(1)

    @pl.when(t == 0)
    def _():
        cbuf[0:8, :] = tail_ref[0]
        s_scr[...] = s0_ref[0]

    cbuf[8:8 + tt, :] = c_ref[0]
    acc = cbuf[8:8 + tt, :] * wconv_ref[CONV_W - 1:CONV_W, :]
    for i in range(CONV_W - 1):
        off = 8 - (CONV_W - 1) + i
        acc = acc + cbuf[off:off + tt, :] * wconv_ref[i:i + 1, :]
    ybuf[...] = acc * _sigmoid(acc)
    cbuf[0:8, :] = cbuf[tt:tt + 8, :]

    ii = lax.broadcasted_iota(jnp.int32, (R, R), 0)
    jj = lax.broadcasted_iota(jnp.int32, (R, R), 1)
    same_head = jnp.right_shift(ii, shift) == jnp.right_shift(jj, shift)
    strict = same_head & (ii > jj)
    causal = same_head & (ii >= jj)
    eye = (ii == jj).astype(F32)
    merge_masks = []
    s = 1
    while s < C:
        sh = s.bit_length() - 1
        merge_masks.append((jnp.right_shift(ii, sh + 1) == jnp.right_shift(jj, sh + 1))
                           & (jnp.right_shift(ii, sh) != jnp.right_shift(jj, sh)) & (ii > jj))
        s *= 2
    ci_ = lax.broadcasted_iota(jnp.int32, (C, C), 0)
    cj_ = lax.broadcasted_iota(jnp.int32, (C, C), 1)
    tri = (ci_ >= cj_).astype(F32)
    lane = lax.broadcasted_iota(jnp.int32, (R, LANES), 1)
    rhead = jnp.right_shift(lax.broadcasted_iota(jnp.int32, (R, LANES), 0), shift)
    ones8 = jnp.ones((8, LANES), F32)
    rep = lambda a: jnp.concatenate([a] * H, axis=0)
    pick = lambda a, off: jnp.sum(jnp.where(lane == rhead + off, a, 0.0), axis=-1, keepdims=True)

    def stacked(rows, base, scale):
        parts = []
        for h in range(H):
            x = ybuf[rows, base + h * GDN_DK:base + (h + 1) * GDN_DK]
            if scale is not None:
                x = x * lax.rsqrt(jnp.sum(x * x, axis=-1, keepdims=True) + EPS) * scale
            parts.append(x)
        return jnp.concatenate(parts, axis=0)

    chunks = range(tt // C)
    low, rhs, qes = [], [], []
    for ci in chunks:
        rows = slice(ci * C, (ci + 1) * C)
        ab = ab_ref[0, rows, :]
        gfull = -jnp.exp(alog_ref[...]) * _softplus(ab + dtb_ref[...])
        bfull = _sigmoid(ab)
        gcum = _dot_f32(tri, gfull)
        glast = gcum[C - 1:C, :]
        g_m = jnp.where(lane == rhead, rep(gcum), 0.0)
        g_col = jnp.sum(g_m, axis=-1, keepdims=True)
        g_row = _dot_nt_f32(ones8, g_m)[0:1, :]
        beta = pick(rep(bfull), H)
        gl_col = pick(jnp.broadcast_to(glast, (R, LANES)), 0)
        decay = jnp.exp(jnp.where(causal, g_col - g_row, -jnp.inf))
        q = stacked(rows, 0, GDN_DK ** -0.5)
        k = stacked(rows, GDN_QK_W, 1.0)
        v = stacked(rows, 2 * GDN_QK_W, None)
        kq = _mm(jnp.concatenate([k, q], axis=0), k, _NT)
        low.append(jnp.where(strict, kq[0:R] * decay * beta, 0.0))
        e_g = jnp.exp(g_col)
        rhs.append(jnp.concatenate([beta * v, (beta * e_g) * k], axis=1).astype(BF16))
        qes.append(q * e_g)
        kw_scr[ci] = (k * jnp.exp(gl_col - g_col)).astype(BF16)
        qkd_scr[ci] = jnp.where(causal, kq[R:2 * R] * decay, 0.0).astype(BF16)
        el_scr[ci] = jnp.broadcast_to(jnp.exp(glast), (8, LANES))
    tinv = [eye - jnp.where(merge_masks[0], low[ci], 0.0) for ci in chunks]
    for m in merge_masks[1:]:
        half = [_mm(tinv[ci], jnp.where(m, low[ci], 0.0)) for ci in chunks]
        tinv = [tinv[ci] - _mm(half[ci], tinv[ci]) for ci in chunks]
    for ci in chunks:
        w2 = _mm(tinv[ci], rhs[ci])
        u0_scr[ci] = w2[:, 0:GDN_DV]
        wk = w2[:, GDN_DV:]
        wq_scr[ci] = jnp.concatenate(
            [x[h * C:(h + 1) * C] for h in range(H) for x in (wk, qes[ci])], axis=0).astype(BF16)

    st = [s_scr[h] for h in range(H)]
    for ci in range(tt // C):
        rows = slice(ci * C, (ci + 1) * C)
        us, os_ = [], []
        for h in range(H):
            r = jnp.dot(wq_scr[ci, 2 * h * C:2 * (h + 1) * C, :], st[h].astype(BF16),
                        preferred_element_type=F32)
            us.append(u0_scr[ci, h * C:(h + 1) * C, :] - r[0:C])
            os_.append(r[C:2 * C])
        intra = jnp.dot(qkd_scr[ci], jnp.concatenate(us, axis=0).astype(BF16),
                        preferred_element_type=F32)
        el = el_scr[ci]
        for h in range(H):
            st[h] = st[h] * el[0:1, h:h + 1] + _mm(kw_scr[ci, h * C:(h + 1) * C, :], us[h], _TN)
            o = os_[h] + intra[h * C:(h + 1) * C]
            zz = z_ref[0, rows, h * GDN_DV:(h + 1) * GDN_DV]
            o_ref[0, rows, h * GDN_DV:(h + 1) * GDN_DV] = _rms(o, wg_ref[...]) * (zz * _sigmoid(zz))
    for h in range(H):
        s_scr[h] = st[h]

    @pl.when(t == pl.num_programs(1) - 1)
    def _():
        sfin_ref[0] = s_scr[...]


def _gdn_step_kernel(c_ref, z_ref, ab_ref, cs_ref, s_ref, wconv_ref, alog_ref, dtb_ref, wg_ref,
                     o_ref, snew_ref, *, nb):
    H = GDN_HEADS
    acc = c_ref[...] * wconv_ref[CONV_W - 1:CONV_W, :]
    for i in range(CONV_W - 1):
        acc = acc + cs_ref[i] * wconv_ref[i:i + 1, :]
    y = acc * _sigmoid(acc)
    ab = ab_ref[...]
    e_g = jnp.exp(-jnp.exp(alog_ref[...]) * _softplus(ab + dtb_ref[...]))
    bfull = _sigmoid(ab)
    unit = lambda x: x * lax.rsqrt(jnp.sum(x * x, axis=-1, keepdims=True) + EPS)
    qs = [unit(y[:, h * GDN_DK:(h + 1) * GDN_DK]) * (GDN_DK ** -0.5) for h in range(H)]
    ks = [unit(y[:, GDN_QK_W + h * GDN_DK:GDN_QK_W + (h + 1) * GDN_DK]) for h in range(H)]
    vs = [y[:, 2 * GDN_QK_W + h * GDN_DV:2 * GDN_QK_W + (h + 1) * GDN_DV] for h in range(H)]
    cols = jnp.concatenate(ks + qs, axis=0).T
    for h in range(H):
        ks_rows, qs_rows = [], []
        for s in range(nb):
            state = s_ref[s, h]
            kcol = cols[:, h * nb + s:h * nb + s + 1]
            qcol = cols[:, (H + h) * nb + s:(H + h) * nb + s + 1]
            ks_rows.append(jnp.sum(kcol * state, axis=0, keepdims=True))
            qs_rows.append(jnp.sum(qcol * state, axis=0, keepdims=True))
        eg_h = e_g[:, h:h + 1]
        u = bfull[:, H + h:H + h + 1] * (vs[h] - eg_h * jnp.concatenate(ks_rows, axis=0))
        o = (eg_h * jnp.concatenate(qs_rows, axis=0)
             + jnp.sum(qs[h] * ks[h], axis=-1, keepdims=True) * u)
        eg_b = jnp.broadcast_to(eg_h, (nb, GDN_DV))
        for s in range(nb):
            kcol = cols[:, h * nb + s:h * nb + s + 1]
            snew_ref[s, h] = eg_b[s:s + 1, :] * s_ref[s, h] + kcol * u[s:s + 1, :]
        zz = z_ref[:, h * GDN_DV:(h + 1) * GDN_DV]
        o_ref[:, h * GDN_DV:(h + 1) * GDN_DV] = _rms(o, wg_ref[...]) * (zz * _sigmoid(zz))


def _gdn_step(c, z, ab, conv_state, s, wconv, alog, dtb, wg):
    bs = c.shape[0]
    nb = LANES // (2 * GDN_HEADS)
    assert bs % nb == 0
    row = lambda i: (i, 0)
    fixed = lambda i: (0, 0)
    state = pl.BlockSpec((nb, GDN_HEADS, GDN_DK, GDN_DV), lambda i: (i, 0, 0, 0))
    return pl.pallas_call(
        functools.partial(_gdn_step_kernel, nb=nb),
        grid=(bs // nb,),
        in_specs=[pl.BlockSpec((nb, CONV_CH), row),
                  pl.BlockSpec((nb, GDN_V_W), row),
                  pl.BlockSpec((nb, AB_W), row),
                  pl.BlockSpec((CONV_W - 1, nb, CONV_CH), lambda i: (0, i, 0)),
                  state,
                  pl.BlockSpec((CONV_W, CONV_CH), fixed),
                  pl.BlockSpec((1, LANES), fixed),
                  pl.BlockSpec((1, LANES), fixed),
                  pl.BlockSpec((1, GDN_DV), fixed)],
        out_specs=[pl.BlockSpec((nb, GDN_V_W), row), state],
        out_shape=[jax.ShapeDtypeStruct((bs, GDN_V_W), F32),
                   jax.ShapeDtypeStruct(s.shape, F32)],
        compiler_params=pltpu.CompilerParams(dimension_semantics=("parallel",),
                                             vmem_limit_bytes=VMEM_LIMIT),
        name="gdn_step",
    )(c, z, ab, conv_state, s, wconv, alog, dtb, wg)


def _gdn(c, z, ab, tail, s0, wconv, alog, dtb, wg, tt):
    b, t, _ = c.shape
    assert t % tt == 0 and tt % GDN_CHUNK == 0
    nc, rs = tt // GDN_CHUNK, GDN_HEADS * GDN_CHUNK
    seq = lambda i, j: (i, j, 0)
    fixed2 = lambda i, j: (0, 0)
    tail_map = (lambda i, j: (i, 0, 0)) if tail.shape[0] == b else (lambda i, j: (0, 0, 0))
    s0_map = (lambda i, j: (i, 0, 0, 0)) if s0.shape[0] == b else (lambda i, j: (0, 0, 0, 0))
    return pl.pallas_call(
        functools.partial(_gdn_kernel, tt=tt),
        grid=(b, t // tt),
        in_specs=[pl.BlockSpec((1, tt, CONV_CH), seq),
                  pl.BlockSpec((1, tt, GDN_V_W), seq),
                  pl.BlockSpec((1, tt, AB_W), seq),
                  pl.BlockSpec((1, 8, CONV_CH), tail_map),
                  pl.BlockSpec((1, GDN_HEADS, GDN_DK, GDN_DV), s0_map),
                  pl.BlockSpec((CONV_W, CONV_CH), fixed2),
                  pl.BlockSpec((1, LANES), fixed2),
                  pl.BlockSpec((1, LANES), fixed2),
                  pl.BlockSpec((1, GDN_DV), fixed2)],
        out_specs=[pl.BlockSpec((1, tt, GDN_V_W), seq),
                   pl.BlockSpec((1, GDN_HEADS, GDN_DK, GDN_DV), lambda i, j: (i, 0, 0, 0))],
        out_shape=[jax.ShapeDtypeStruct((b, t, GDN_V_W), F32),
                   jax.ShapeDtypeStruct((b, GDN_HEADS, GDN_DK, GDN_DV), F32)],
        scratch_shapes=[pltpu.VMEM((tt + 8, CONV_CH), F32),
                        pltpu.VMEM((tt, CONV_CH), F32),
                        pltpu.VMEM((GDN_HEADS, GDN_DK, GDN_DV), F32),
                        pltpu.VMEM((nc, rs, GDN_DV), F32),
                        pltpu.VMEM((nc, 2 * rs, GDN_DK), BF16),
                        pltpu.VMEM((nc, rs, GDN_DK), BF16),
                        pltpu.VMEM((nc, rs, rs), BF16),
                        pltpu.VMEM((nc, 8, LANES), F32)],
        compiler_params=pltpu.CompilerParams(dimension_semantics=("parallel", "arbitrary"),
                                             vmem_limit_bytes=VMEM_LIMIT),
        name="gdn",
    )(c, z, ab, tail, s0, wconv, alog, dtb, wg)


def _kv_out_kernel(k_ref, v_ref, km_ref, vm_ref, ko_ref, vo_ref):
    t = k_ref.shape[1]
    for src, meta, dst in ((k_ref, km_ref, ko_ref), (v_ref, vm_ref, vo_ref)):
        for h in range(DA_HEADS):
            cols = slice(h * DA_DV, (h + 1) * DA_DV)
            dst[0, pl.ds(h, N_META, stride=DA_HEADS), :] = meta[:, cols]
            dst[0, pl.ds(N_META * DA_HEADS + h, t, stride=DA_HEADS), :] = src[0, :, cols]


def _kv_out(k, v, k_meta, v_meta):
    b, t, w = k.shape
    rows = (t + N_META) * DA_HEADS
    seq = pl.BlockSpec((1, t, w), lambda i: (i, 0, 0))
    meta = pl.BlockSpec((N_META, w), lambda i: (0, 0))
    out = pl.BlockSpec((1, rows, DA_DV), lambda i: (i, 0, 0))
    return pl.pallas_call(
        _kv_out_kernel,
        grid=(b,),
        in_specs=[seq, seq, meta, meta],
        out_specs=[out, out],
        out_shape=[jax.ShapeDtypeStruct((b, rows, DA_DV), F32)] * 2,
        compiler_params=pltpu.CompilerParams(dimension_semantics=("parallel",),
                                             vmem_limit_bytes=VMEM_LIMIT),
        name="kv_out",
    )(k, v, k_meta, v_meta)


def _lambda(lam_ref):
    lv = lam_ref[...]
    s1 = jnp.sum(lv[0:1, :] * lv[1:2, :], axis=-1, keepdims=True)
    s2 = jnp.sum(lv[2:3, :] * lv[3:4, :], axis=-1, keepdims=True)
    return jnp.exp(s1) - jnp.exp(s2) + LAM_INIT


def _split_components(q):
    lane = lax.broadcasted_iota(jnp.int32, q.shape, 1)
    first = (lane % DA_DV) < DA_DQK
    return jnp.where(first, q, 0.0).astype(BF16), jnp.where(first, 0.0, q).astype(BF16)


def _online_update(state, s, vb):
    m, l, acc = state
    m_new = jnp.maximum(m, jnp.max(s, axis=-1, keepdims=True))
    p = jnp.exp(s - m_new)
    alpha = jnp.exp(m - m_new)
    return (m_new, alpha * l + jnp.sum(p, axis=-1, keepdims=True),
            alpha * acc + jnp.dot(p.astype(BF16), vb, preferred_element_type=F32))


def _attn_prompt_kernel(q_ref, k_ref, v_ref, km_ref, vm_ref, lam_ref, wsub_ref, o_ref, *, bq, bk):
    qi = pl.program_id(2)
    q1, q2 = _split_components(q_ref[0] * (DA_DQK ** -0.5))
    neg = jnp.full((bq, 1), -jnp.inf, F32)
    zero = jnp.zeros((bq, 1), F32)
    zacc = jnp.zeros((bq, DA_DV), F32)
    init = ((neg, zero, zacc), (neg, zero, zacc))

    def block(kblk, vblk, mask, st):
        kb, vb = kblk.astype(BF16), vblk.astype(BF16)
        s1, s2 = _dot_nt(q1, kb), _dot_nt(q2, kb)
        if mask is not None:
            s1 = jnp.where(mask, s1, -jnp.inf)
            s2 = jnp.where(mask, s2, -jnp.inf)
        return (_online_update(st[0], s1, vb), _online_update(st[1], s2, vb))

    n_full = lax.shift_right_logical(qi * bq, bk.bit_length() - 1)

    def body(kb, st):
        r0 = pl.multiple_of(kb * bk, bk)
        return block(k_ref[0, pl.ds(r0, bk), :], v_ref[0, pl.ds(r0, bk), :], None, st)

    st = lax.fori_loop(0, n_full, body, init)
    r0 = pl.multiple_of(n_full * bk, bk)
    kcat = jnp.concatenate([km_ref[...], k_ref[0, pl.ds(r0, bk), :]], axis=0)
    vcat = jnp.concatenate([vm_ref[...], v_ref[0, pl.ds(r0, bk), :]], axis=0)
    ri = lax.broadcasted_iota(jnp.int32, (bq, LANES + bk), 0)
    ci = lax.broadcasted_iota(jnp.int32, (bq, LANES + bk), 1)
    seen = (ci < N_META) | ((ci >= LANES) & (ci - LANES <= ri + (qi * bq - n_full * bk)))
    st = block(kcat, vcat, seen, st)
    (m1, l1, a1), (m2, l2, a2) = st
    o = a1 / l1 - _lambda(lam_ref) * (a2 / l2)
    o_ref[0] = _rms(o, wsub_ref[...]) * (1.0 - LAM_INIT)


def _attn_prompt(q, k, v, k_meta, v_meta, lam_vecs, wsub, bq, bk):
    b, t, _ = q.shape
    assert t % bk == 0 and bk % bq == 0 and k_meta.shape[0] == LANES
    return pl.pallas_call(
        functools.partial(_attn_prompt_kernel, bq=bq, bk=bk),
        grid=(b, DA_HEADS, t // bq),
        in_specs=[pl.BlockSpec((1, bq, DA_DV), lambda i, h, j: (i, j, h)),
                  pl.BlockSpec((1, t, DA_DV), lambda i, h, j: (i, 0, h)),
                  pl.BlockSpec((1, t, DA_DV), lambda i, h, j: (i, 0, h)),
                  pl.BlockSpec((LANES, DA_DV), lambda i, h, j: (0, h)),
                  pl.BlockSpec((LANES, DA_DV), lambda i, h, j: (0, h)),
                  pl.BlockSpec((4, DA_DQK), lambda i, h, j: (0, 0)),
                  pl.BlockSpec((1, DA_DV), lambda i, h, j: (0, 0))],
        out_specs=pl.BlockSpec((1, bq, DA_DV), lambda i, h, j: (i, j, h)),
        out_shape=jax.ShapeDtypeStruct((b, t, DA_V_W), F32),
        compiler_params=pltpu.CompilerParams(
            dimension_semantics=("parallel", "parallel", "arbitrary"),
            vmem_limit_bytes=VMEM_LIMIT),
        name="attn_prompt",
    )(q, k, v, k_meta, v_meta, lam_vecs, wsub)


def _attn_sample_kernel(pt_ref, q_ref, kn_ref, vn_ref, lam_ref, wsub_ref, *rest, pps):
    k_refs, v_refs = rest[:pps], rest[pps:2 * pps]
    o_ref, m_scr, l_scr, acc_scr = rest[2 * pps:]
    step = pl.program_id(1)
    nrow = 2 * DA_HEADS
    twice = lambda x: jnp.concatenate([x, x], axis=0)
    rowq = lax.broadcasted_iota(jnp.int32, (nrow, DA_DV), 0)
    laneq = lax.broadcasted_iota(jnp.int32, (nrow, DA_DV), 1)
    own_comp = (laneq // DA_DQK) == (rowq // DA_HEADS)
    qm = jnp.where(own_comp, twice(q_ref[0]) * (DA_DQK ** -0.5), 0.0)
    page_rows = k_refs[0].shape[1]
    rowp = lax.broadcasted_iota(jnp.int32, (nrow, page_rows * pps), 0)
    colp = lax.broadcasted_iota(jnp.int32, (nrow, page_rows * pps), 1)
    own_head = (colp % DA_HEADS) == (rowp % DA_HEADS)

    def rows3(x):
        hi = x.astype(BF16).astype(F32)
        mid = (x - hi).astype(BF16).astype(F32)
        return jnp.concatenate([hi, mid, x - hi - mid], axis=0).astype(BF16)

    def unrows3(r):
        return (r[0:nrow] + r[nrow:2 * nrow]) + r[2 * nrow:3 * nrow]

    def hi_lo(x):
        hi = x.astype(BF16)
        return hi, (x - hi.astype(F32)).astype(BF16)

    @pl.when(step == 0)
    def _():
        s_new = jnp.sum(qm * twice(kn_ref[0]), axis=-1, keepdims=True)
        m_scr[...] = s_new
        l_scr[...] = jnp.ones_like(s_new)
        acc_scr[...] = twice(vn_ref[0])

    q3 = rows3(qm)
    scores = []
    for r in k_refs:
        k_hi, k_lo = hi_lo(r[0])
        scores.append(unrows3(_dot_nt(q3, k_hi) + _dot_nt(q3, k_lo)))
    s = jnp.where(own_head, jnp.concatenate(scores, axis=1), -jnp.inf)
    m_old = m_scr[...]
    m_new = jnp.maximum(m_old, jnp.max(s, axis=-1, keepdims=True))
    p = jnp.exp(s - m_new)
    alpha = jnp.exp(m_old - m_new)
    p3 = rows3(p)
    pv = jnp.zeros((3 * nrow, DA_DV), F32)
    for j, r in enumerate(v_refs):
        v_hi, v_lo = hi_lo(r[0])
        pj = p3[:, j * page_rows:(j + 1) * page_rows]
        pv = pv + (jnp.dot(pj, v_hi, preferred_element_type=F32)
                   + jnp.dot(pj, v_lo, preferred_element_type=F32))
    m_scr[...] = m_new
    l_scr[...] = alpha * l_scr[...] + jnp.sum(p, axis=-1, keepdims=True)
    acc_scr[...] = alpha * acc_scr[...] + unrows3(pv)

    @pl.when(step == pl.num_programs(1) - 1)
    def _():
        w = acc_scr[...] / l_scr[...]
        o = w[0:DA_HEADS, :] - _lambda(lam_ref) * w[DA_HEADS:nrow, :]
        o_ref[0] = _rms(o, wsub_ref[...]) * (1.0 - LAM_INIT)


def _attn_sample(q, k_new, v_new, cache_k, cache_v, page_table, lam_vecs, wsub, pps):
    bs = q.shape[0]
    n_pages = page_table.shape[1]
    page_rows = cache_k.shape[1]
    assert n_pages % pps == 0
    tok = lambda i, s, pt: (i, 0, 0)
    fixed = lambda i, s, pt: (0, 0)

    def page_spec(j):
        return pl.BlockSpec((1, page_rows, DA_DV), lambda i, s, pt: (pt[i, s * pps + j], 0, 0))

    grid_spec = pltpu.PrefetchScalarGridSpec(
        num_scalar_prefetch=1,
        grid=(bs, n_pages // pps),
        in_specs=[pl.BlockSpec((1, DA_HEADS, DA_DV), tok),
                  pl.BlockSpec((1, DA_HEADS, DA_DV), tok),
                  pl.BlockSpec((1, DA_HEADS, DA_DV), tok),
                  pl.BlockSpec((4, DA_DQK), fixed),
                  pl.BlockSpec((1, DA_DV), fixed)]
                 + [page_spec(j) for j in range(pps)] * 2,
        out_specs=pl.BlockSpec((1, DA_HEADS, DA_DV), tok),
        scratch_shapes=[pltpu.VMEM((2 * DA_HEADS, 1), F32),
                        pltpu.VMEM((2 * DA_HEADS, 1), F32),
                        pltpu.VMEM((2 * DA_HEADS, DA_DV), F32)])
    return pl.pallas_call(
        functools.partial(_attn_sample_kernel, pps=pps),
        grid_spec=grid_spec,
        out_shape=jax.ShapeDtypeStruct((bs, DA_HEADS, DA_DV), F32),
        compiler_params=pltpu.CompilerParams(dimension_semantics=("parallel", "arbitrary"),
                                             vmem_limit_bytes=VMEM_LIMIT),
        name="attn_sample",
    )(page_table, q, k_new, v_new, lam_vecs, wsub, *([cache_k] * pps), *([cache_v] * pps))


def _route(logits):
    lane = lax.broadcasted_iota(jnp.int32, logits.shape, 1).astype(F32)
    big = float(LANES)
    ninf = -jnp.inf

    def top(mask):
        val = jnp.where(mask, logits, ninf)
        mx = jnp.max(val, axis=-1, keepdims=True)
        idx = jnp.min(jnp.where(mask & (val == mx), lane, big), axis=-1, keepdims=True)
        return mx, idx

    gmask = (lane >= N_EXPERTS) & (lane < N_EXPERTS + N_GROUPS)
    gmax, gidx = top(gmask)
    g_w = 1.0 / jnp.sum(jnp.where(gmask, jnp.exp(logits - gmax), 0.0), axis=-1, keepdims=True)
    g_sel = gidx - N_EXPERTS
    emask = (lane >= g_sel * EXPERTS_PER_GROUP) & (lane < (g_sel + 1) * EXPERTS_PER_GROUP)
    v1, i1 = top(emask)
    v2, i2 = top(emask & (lane != i1))
    e21 = jnp.exp(v2 - v1)
    p1 = 1.0 / (1.0 + e21)
    return jnp.where(lane == i1, p1 * g_w, 0.0) + jnp.where(lane == i2, e21 * p1 * g_w, 0.0), g_sel


def _mix_route(g_ref, a_ref, x_ref, wo_ref, wnf_ref, wr_ref, precise):
    mix = (_mm(g_ref[...], wo_ref[0:GDN_V_W, :], _NN, precise)
           + _mm(a_ref[...], wo_ref[GDN_V_W:, :], _NN, precise))
    x1 = x_ref[...] + mix
    xn = _rms(x1, wnf_ref[...])
    wr = wr_ref[...]
    if precise:
        logits = _dot_f32(xn, wr)
    else:
        x_hi, w_hi = xn.astype(BF16), wr.astype(BF16)
        x_lo, w_lo = (xn - x_hi.astype(F32)).astype(BF16), (wr - w_hi.astype(F32)).astype(BF16)
        logits = (jnp.dot(x_hi, w_hi, preferred_element_type=F32)
                  + (jnp.dot(x_hi, w_lo, preferred_element_type=F32)
                     + jnp.dot(x_lo, w_hi, preferred_element_type=F32)))
    comb, g_sel = _route(logits)
    return x1, xn, comb, g_sel


def _expert(xn, comb, e_lane, wg, wu, wd):
    hg = jnp.dot(xn, wg.astype(BF16), preferred_element_type=F32)
    hu = jnp.dot(xn, wu.astype(BF16), preferred_element_type=F32)
    lane = lax.broadcasted_iota(jnp.int32, comb.shape, 1)
    cw = jnp.sum(jnp.where(lane == e_lane, comb, 0.0), axis=-1, keepdims=True)
    hh = (hg * _sigmoid(hg)) * hu * cw
    return jnp.dot(hh.astype(BF16), wd.astype(BF16), preferred_element_type=F32)


def _ffn_kernel(g_ref, a_ref, x_ref, wo_ref, wnf_ref, wr_ref, wg_ref, wu_ref, wd_ref, wfin_ref,
                y_ref, x1_scr, xn_scr, comb_scr, acc_scr, *, precise):
    e = pl.program_id(1)

    @pl.when(e == 0)
    def _():
        x1, xn, comb, _ = _mix_route(g_ref, a_ref, x_ref, wo_ref, wnf_ref, wr_ref, precise)
        x1_scr[...] = x1
        xn_scr[...] = xn.astype(BF16)
        comb_scr[...] = comb
        acc_scr[...] = jnp.zeros_like(acc_scr)

    acc_scr[...] += _expert(xn_scr[...], comb_scr[...], e, wg_ref[0], wu_ref[0], wd_ref[0])

    @pl.when(e == pl.num_programs(1) - 1)
    def _():
        y_ref[...] = _rms(x1_scr[...] + acc_scr[...], wfin_ref[...])


def _ffn(g_out, a_out, x, wo, wnf, wr, wg, wu, wd, wfin, tm):
    n = x.shape[0]
    assert n % tm == 0
    row = lambda i, e: (i, 0)
    fixed = lambda i, e: (0, 0)
    exp = lambda i, e: (e, 0, 0)
    return pl.pallas_call(
        functools.partial(_ffn_kernel, precise=wo.dtype == F32),
        grid=(n // tm, N_EXPERTS),
        in_specs=[pl.BlockSpec((tm, GDN_V_W), row),
                  pl.BlockSpec((tm, DA_V_W), row),
                  pl.BlockSpec((tm, D_MODEL), row),
                  pl.BlockSpec((GDN_V_W + DA_V_W, D_MODEL), fixed),
                  pl.BlockSpec((1, D_MODEL), fixed),
                  pl.BlockSpec((D_MODEL, LANES), fixed),
                  pl.BlockSpec((1, D_MODEL, D_EXPERT), exp),
                  pl.BlockSpec((1, D_MODEL, D_EXPERT), exp),
                  pl.BlockSpec((1, D_EXPERT, D_MODEL), exp),
                  pl.BlockSpec((1, D_MODEL), fixed)],
        out_specs=pl.BlockSpec((tm, D_MODEL), row),
        out_shape=jax.ShapeDtypeStruct((n, D_MODEL), F32),
        scratch_shapes=[pltpu.VMEM((tm, D_MODEL), F32),
                        pltpu.VMEM((tm, D_MODEL), BF16),
                        pltpu.VMEM((tm, LANES), F32),
                        pltpu.VMEM((tm, D_MODEL), F32)],
        compiler_params=pltpu.CompilerParams(dimension_semantics=("parallel", "arbitrary"),
                                             vmem_limit_bytes=VMEM_LIMIT),
        name="ffn",
    )(g_out, a_out, x, wo, wnf, wr, wg, wu, wd, wfin)


ROW_W = D_MODEL + LANES
MOE_TILE = 512
ROW_TILE = 1024


def _ffn_pre_kernel(g_ref, a_ref, x_ref, wo_ref, wnf_ref, wr_ref, rows_ref, info_ref, cnt_ref,
                    cnt_scr):
    @pl.when(pl.program_id(0) == 0)
    def _():
        cnt_scr[...] = jnp.zeros_like(cnt_scr)

    x1, _, comb, g_sel = _mix_route(g_ref, a_ref, x_ref, wo_ref, wnf_ref, wr_ref, False)
    tm = x1.shape[0]
    lane = lax.broadcasted_iota(jnp.int32, (tm, LANES), 1).astype(F32)
    onehot = lane == g_sel
    ri = lax.broadcasted_iota(jnp.int32, (tm, tm), 0)
    ci = lax.broadcasted_iota(jnp.int32, (tm, tm), 1)
    earlier = jnp.dot((ci < ri).astype(BF16), onehot.astype(BF16), preferred_element_type=F32)
    rank = jnp.sum(jnp.where(onehot, earlier + cnt_scr[...], 0.0), axis=-1, keepdims=True)
    cnt_scr[...] += jnp.sum(onehot.astype(F32), axis=0, keepdims=True)
    cnt_ref[...] = cnt_scr[...]
    rows_ref[:, 0:D_MODEL] = x1
    rows_ref[:, D_MODEL:] = comb
    packed = jnp.where(lane == 0.0, g_sel, jnp.where(lane == 1.0, rank, 0.0))
    r8 = lax.broadcasted_iota(jnp.int32, (8, LANES), 0)
    l8 = lax.broadcasted_iota(jnp.int32, (8, LANES), 1)
    info_ref[...] = _dot_nt_f32((r8 == l8).astype(F32), packed)


def _ffn_pre(g_out, a_out, x, wo, wnf, wr, tm):
    n = x.shape[0]
    assert n % tm == 0
    row = lambda i: (i, 0)
    fixed = lambda i: (0, 0)
    return pl.pallas_call(
        _ffn_pre_kernel,
        grid=(n // tm,),
        in_specs=[pl.BlockSpec((tm, GDN_V_W), row),
                  pl.BlockSpec((tm, DA_V_W), row),
                  pl.BlockSpec((tm, D_MODEL), row),
                  pl.BlockSpec((GDN_V_W + DA_V_W, D_MODEL), fixed),
                  pl.BlockSpec((1, D_MODEL), fixed),
                  pl.BlockSpec((D_MODEL, LANES), fixed)],
        out_specs=[pl.BlockSpec((tm, ROW_W), row),
                   pl.BlockSpec((8, tm), lambda i: (0, i)),
                   pl.BlockSpec((1, LANES), fixed)],
        out_shape=[jax.ShapeDtypeStruct((n, ROW_W), F32),
                   jax.ShapeDtypeStruct((8, n), F32),
                   jax.ShapeDtypeStruct((1, LANES), F32)],
        scratch_shapes=[pltpu.VMEM((1, LANES), F32)],
        compiler_params=pltpu.CompilerParams(dimension_semantics=("arbitrary",),
                                             vmem_limit_bytes=VMEM_LIMIT),
        name="ffn_pre",
    )(g_out, a_out, x, wo, wnf, wr)


def _copy_rows(idx_ref, tile_ref, hbm_ref, sem, scatter):
    tm = tile_ref.shape[0]
    base = pl.program_id(0) * tm

    def body(r, carry):
        there, here = hbm_ref.at[pl.ds(idx_ref[base + r], 1)], tile_ref.at[pl.ds(r, 1)]
        src, dst = (here, there) if scatter else (there, here)
        pltpu.make_async_copy(src, dst, sem).start()
        return carry

    lax.fori_loop(0, tm, body, 0, unroll=8)
    there, here = hbm_ref.at[pl.ds(0, tm)], tile_ref
    src, dst = (here, there) if scatter else (there, here)
    pltpu.make_async_copy(src, dst, sem).wait()


def _scatter_kernel(idx_ref, tile_ref, init_ref, dst_ref, sem):
    del init_ref
    _copy_rows(idx_ref, tile_ref, dst_ref, sem, True)


def _gather_kernel(idx_ref, src_ref, tile_ref, sem):
    _copy_rows(idx_ref, tile_ref, src_ref, sem, False)


def _move_rows(idx, src, n_out, scatter):
    n, w = idx.shape[0], src.shape[1]
    assert n % ROW_TILE == 0
    any_spec = pl.BlockSpec(memory_space=pl.ANY)
    tile_spec = pl.BlockSpec((ROW_TILE, w), lambda i, idx: (i, 0))
    if scatter:
        operands = (idx, src, jnp.zeros((n_out, w), src.dtype))
        in_specs, out_spec = [tile_spec, any_spec], any_spec
    else:
        operands = (idx, src)
        in_specs, out_spec = [any_spec], tile_spec
    return pl.pallas_call(
        _scatter_kernel if scatter else _gather_kernel,
        grid_spec=pltpu.PrefetchScalarGridSpec(
            num_scalar_prefetch=1, grid=(n // ROW_TILE,),
            in_specs=in_specs, out_specs=out_spec,
            scratch_shapes=[pltpu.SemaphoreType.DMA(())]),
        out_shape=jax.ShapeDtypeStruct((n_out, w), src.dtype),
        input_output_aliases={2: 0} if scatter else {},
        compiler_params=pltpu.CompilerParams(dimension_semantics=("arbitrary",)),
        name="scatter_rows" if scatter else "gather_rows",
    )(*operands)


def _moe_groups_kernel(tg_ref, nt_ref, rows_ref, wnf_ref, wg_ref, wu_ref, wd_ref, wfin_ref, y_ref,
                       wg_bf, wu_bf, wd_bf):
    j = pl.program_id(0)

    @pl.when((j == 0) | (tg_ref[j] != tg_ref[jnp.maximum(j - 1, 0)]))
    def _():
        for e in range(EXPERTS_PER_GROUP):
            wg_bf[e] = wg_ref[e].astype(BF16)
            wu_bf[e] = wu_ref[e].astype(BF16)
            wd_bf[e] = wd_ref[e].astype(BF16)

    @pl.when(j < nt_ref[0])
    def _():
        x1 = rows_ref[:, 0:D_MODEL]
        comb = rows_ref[:, D_MODEL:]
        xn = _rms(x1, wnf_ref[...]).astype(BF16)
        first = tg_ref[j] * EXPERTS_PER_GROUP
        acc = jnp.zeros_like(x1)
        for e in range(EXPERTS_PER_GROUP):
            acc = acc + _expert(xn, comb, first + e, wg_bf[e], wu_bf[e], wd_bf[e])
        y_ref[...] = _rms(x1 + acc, wfin_ref[...])

    @pl.when(j >= nt_ref[0])
    def _():
        y_ref[...] = jnp.zeros_like(y_ref)


def _moe_groups(tile_group, n_tiles_used, rows, wnf, wg, wu, wd, wfin):
    n_tiles = tile_group.shape[0]
    fixed = lambda j, tg, nt: (0, 0)
    grp = lambda j, tg, nt: (tg[j], 0, 0)
    in_shape, out_shape = (EXPERTS_PER_GROUP, D_MODEL, D_EXPERT), (EXPERTS_PER_GROUP, D_EXPERT, D_MODEL)
    once = pl.Buffered(1)
    return pl.pallas_call(
        _moe_groups_kernel,
        grid_spec=pltpu.PrefetchScalarGridSpec(
            num_scalar_prefetch=2, grid=(n_tiles,),
            in_specs=[pl.BlockSpec((MOE_TILE, ROW_W), lambda j, tg, nt: (j, 0)),
                      pl.BlockSpec((1, D_MODEL), fixed),
                      pl.BlockSpec(in_shape, grp, pipeline_mode=once),
                      pl.BlockSpec(in_shape, grp, pipeline_mode=once),
                      pl.BlockSpec(out_shape, grp, pipeline_mode=once),
                      pl.BlockSpec((1, D_MODEL), fixed)],
            out_specs=pl.BlockSpec((MOE_TILE, D_MODEL), lambda j, tg, nt: (j, 0)),
            scratch_shapes=[pltpu.VMEM(in_shape, BF16), pltpu.VMEM(in_shape, BF16),
                            pltpu.VMEM(out_shape, BF16)]),
        out_shape=jax.ShapeDtypeStruct((n_tiles * MOE_TILE, D_MODEL), F32),
        compiler_params=pltpu.CompilerParams(dimension_semantics=("arbitrary",),
                                             vmem_limit_bytes=VMEM_LIMIT),
        name="moe_groups",
    )(tile_group, n_tiles_used, rows, wnf, wg, wu, wd, wfin)


def _ffn_sorted(g_out, a_out, x, wo, wnf, wr, wg, wu, wd, wfin):
    n = x.shape[0]
    rows, info, cnt = _ffn_pre(g_out, a_out, x, wo, wnf, wr, 512)
    grp, rank = info[0].astype(jnp.int32), info[1].astype(jnp.int32)
    tiles = (cnt[0, :N_GROUPS].astype(jnp.int32) + MOE_TILE - 1) // MOE_TILE
    ends = jnp.cumsum(tiles)
    dest = (ends - tiles)[grp] * MOE_TILE + rank
    n_tiles = n // MOE_TILE + N_GROUPS
    tile_group = jnp.minimum(
        jnp.sum(jnp.arange(n_tiles, dtype=jnp.int32)[:, None] >= ends[None, :], axis=1),
        N_GROUPS - 1).astype(jnp.int32)
    rows_sorted = _move_rows(dest, rows, n_tiles * MOE_TILE, True)
    y_sorted = _moe_groups(tile_group, ends[N_GROUPS - 1:], rows_sorted, wnf, wg, wu, wd, wfin)
    return _move_rows(dest, y_sorted, n, False)


def kernel(x_prompt, x_sample, cache_k, cache_v, state_conv, state_gdn, page_table, meta_tokens,
           w_norm_mix, w_in, w_conv, a_log, dt_bias, w_gnorm, lambda_q1, lambda_k1, lambda_q2,
           lambda_k2, w_subln, w_out, w_norm_ffn, w_group, w_expert, w_gate, w_up, w_down, w_final):
    l = 0
    bp, seq = x_prompt.shape[0], x_prompt.shape[1]
    bs = x_sample.shape[0]
    n_pool, page = cache_k.shape[1], cache_k.shape[2]
    past = page_table.shape[1] * page
    C = GDN_CHUNK

    w_in_p = _prep_w_in(w_in[l])
    wn_mix = w_norm_mix[l][None, :]
    pad_lanes = lambda v: jnp.pad(v, (0, LANES - v.shape[0]))[None, :]
    alog, dtb = pad_lanes(a_log[l]), pad_lanes(dt_bias[l])
    wgn = w_gnorm[l][None, :]
    lam_vecs = jnp.stack([lambda_q1[l], lambda_k1[l], lambda_q2[l], lambda_k2[l]])
    wsub = w_subln[l][None, :]
    w_router = jnp.pad(jnp.concatenate([w_expert[l], w_group[l]], axis=1),
                       ((0, 0), (0, LANES - N_EXPERTS - N_GROUPS)))
    ffn_w = (w_norm_ffn[l][None, :], w_router, w_gate[l], w_up[l], w_down[l], w_final[None, :])

    pos_small = jnp.concatenate([jnp.arange(N_META), jnp.full((bs,), past)])
    rows_small = jnp.concatenate([meta_tokens, x_sample[:, 0, :]], axis=0)
    c_s, z_s, ab_s, q_s, k_s, v_s = _proj(rows_small, wn_mix, w_in_p, _rope_tables(pos_small),
                                           N_META + bs)
    xp = x_prompt.reshape(bp * seq, D_MODEL)
    c_p, z_p, ab_p, q_p, k_p, v_p = _proj(xp, wn_mix, w_in_p.astype(BF16),
                                           _rope_tables(N_META + jnp.arange(seq)), 512)

    front = lambda a: jnp.pad(a[:N_META], ((C - N_META, 0), (0, 0)))[None]
    zero_tail = jnp.zeros((1, 8, CONV_CH), F32)
    zero_state = jnp.zeros((1, GDN_HEADS, GDN_DK, GDN_DV), F32)
    _, s_meta = _gdn(front(c_s), front(z_s), front(ab_s), zero_tail, zero_state,
                     w_conv[l], alog, dtb, wgn, C)
    tail_meta = jnp.pad(c_s[N_META - (CONV_W - 1):N_META], ((8 - (CONV_W - 1), 0), (0, 0)))[None]
    c_p3 = c_p.reshape(bp, seq, CONV_CH)
    g_out, p_gdn = _gdn(c_p3, z_p.reshape(bp, seq, GDN_V_W), ab_p.reshape(bp, seq, AB_W),
                        tail_meta, s_meta, w_conv[l], alog, dtb, wgn, 256)
    p_conv = c_p3[:, seq - (CONV_W - 1):, :]

    p_k, p_v = _kv_out(k_p.reshape(bp, seq, DA_QK_W), v_p.reshape(bp, seq, DA_V_W),
                       k_s[:N_META], v_s[:N_META])
    meta_blk = lambda m: jnp.pad(m[:N_META], ((0, LANES - N_META), (0, 0)))
    a_out = _attn_prompt(q_p.reshape(bp, seq, DA_QK_W), k_p.reshape(bp, seq, DA_QK_W),
                         v_p.reshape(bp, seq, DA_V_W), meta_blk(k_s), meta_blk(v_s),
                         lam_vecs, wsub, 256, 512)

    y_prompt = _ffn_sorted(g_out.reshape(bp * seq, GDN_V_W), a_out.reshape(bp * seq, DA_V_W), xp,
                           w_out[l].astype(BF16), *ffn_w).reshape(bp, seq, D_MODEL)

    sm = lambda a: a[N_META:]
    g_s, s_gdn = _gdn_step(sm(c_s), sm(z_s), sm(ab_s), jnp.swapaxes(state_conv[l], 0, 1),
                           state_gdn[l], w_conv[l], alog, dtb, wgn)
    s_conv = jnp.concatenate([state_conv[l][:, 1:, :], sm(c_s)[:, None, :]], axis=1)
    heads = lambda a: sm(a).reshape(bs, DA_HEADS, DA_DV)
    a_s = _attn_sample(heads(q_s), heads(k_s), heads(v_s),
                       cache_k[l].reshape(n_pool, page * DA_HEADS, 2 * DA_DQK),
                       cache_v[l].reshape(n_pool, page * DA_HEADS, DA_DV),
                       page_table, lam_vecs, wsub, 32)
    y_sample = _ffn(g_s, a_s.reshape(bs, DA_V_W), x_sample[:, 0, :], w_out[l], *ffn_w,
                    bs).reshape(bs, 1, D_MODEL)

    return (y_prompt, y_sample, p_conv[None], p_gdn[None],
            p_k.reshape(1, bp, seq + N_META, DA_HEADS, 2 * DA_DQK),
            p_v.reshape(1, bp, seq + N_META, DA_HEADS, DA_DV),
            s_conv[None], s_gdn[None],
            sm(k_s).reshape(1, bs, 1, DA_HEADS, 2 * DA_DQK),
            sm(v_s).reshape(1, bs, 1, DA_HEADS, DA_DV))
```

```python
import functools
import math

import jax
import jax.numpy as jnp
import numpy as np
from jax import lax
from jax.experimental import pallas as pl
from jax.experimental.pallas import tpu as pltpu

F32 = jnp.float32
BF16 = jnp.bfloat16

D_MODEL = 1024
N_META = 16
EPS = 1e-6
GDN_HEADS = 4
GDN_DK = 128
GDN_DV = 128
CONV_W = 4
GDN_QK_W = GDN_HEADS * GDN_DK
GDN_V_W = GDN_HEADS * GDN_DV
CONV_CH = 2 * GDN_QK_W + GDN_V_W
DA_HEADS = 4
DA_DQK = 64
DA_DV = 2 * DA_DQK
DA_QK_W = DA_HEADS * 2 * DA_DQK
DA_V_W = DA_HEADS * DA_DV
ROPE_DIM = DA_DQK // 4
ROPE_HALF = ROPE_DIM // 2
ROPE_THETA = 500000.0
N_GROUPS = 4
EXPERTS_PER_GROUP = 8
N_EXPERTS = N_GROUPS * EXPERTS_PER_GROUP
D_EXPERT = 256
LAM_INIT = 0.8 - 0.6 * math.exp(-0.3 * 0)

LANES = 128
GDN_CHUNK = 64
GDN_STACK = 2
AB_W = LANES
PROJ_PAD_W = CONV_CH + GDN_V_W + 2 * DA_QK_W + DA_V_W + AB_W
VMEM_LIMIT = 56 * 1024 * 1024


_NN = (((1,), (0,)), ((), ()))
_NT = (((1,), (1,)), ((), ()))
_TN = (((0,), (0,)), ((), ()))


def _mm(a, b, dims=_NN, precise=False):
    if precise:
        return lax.dot_general(a.astype(F32), b.astype(F32), dims, preferred_element_type=F32,
                               precision=lax.Precision.HIGHEST)
    return lax.dot_general(a.astype(BF16), b.astype(BF16), dims, preferred_element_type=F32)


def _dot_nt(a, b):
    return _mm(a, b, _NT)


def _dot_f32(a, b):
    return _mm(a, b, _NN, True)


def _dot_nt_f32(a, b):
    return _mm(a, b, _NT, True)


def _sigmoid(x):
    return 1.0 / (1.0 + jnp.exp(-x))


def _softplus(x):
    return jnp.maximum(x, 0.0) + jnp.log(1.0 + jnp.exp(-jnp.abs(x)))


def _rms(x, w):
    return x * lax.rsqrt(jnp.mean(x * x, axis=-1, keepdims=True) + EPS) * w


def _rope(x, cs, sn_lo, sn_hi):
    w = x.shape[1]
    cs = jnp.concatenate([cs] * DA_HEADS, axis=1)
    sn_lo = jnp.concatenate([sn_lo] * DA_HEADS, axis=1)
    sn_hi = jnp.concatenate([sn_hi] * DA_HEADS, axis=1)
    up = pltpu.roll(x, w - ROPE_HALF, 1)
    dn = pltpu.roll(x, ROPE_HALF, 1)
    return x * cs + up * sn_lo + dn * sn_hi


def _proj_kernel(x_ref, wn_ref, w_ref, cs_ref, snl_ref, snh_ref,
                 c_ref, z_ref, ab_ref, q_ref, k_ref, v_ref, *, precise):
    h = _rms(x_ref[...], wn_ref[...])
    h = h if precise else h.astype(BF16)
    o = 0
    c_ref[...] = _mm(h, w_ref[:, o:o + CONV_CH], _NN, precise)
    o += CONV_CH
    z_ref[...] = _mm(h, w_ref[:, o:o + GDN_V_W], _NN, precise)
    o += GDN_V_W
    q = _mm(h, w_ref[:, o:o + DA_QK_W], _NN, precise)
    o += DA_QK_W
    k = _mm(h, w_ref[:, o:o + DA_QK_W], _NN, precise)
    o += DA_QK_W
    v_ref[...] = _mm(h, w_ref[:, o:o + DA_V_W], _NN, precise)
    o += DA_V_W
    ab_ref[...] = _mm(h, w_ref[:, o:o + AB_W], _NN, precise)
    cs, snl, snh = cs_ref[...], snl_ref[...], snh_ref[...]
    q_ref[...] = _rope(q, cs, snl, snh)
    k_ref[...] = _rope(k, cs, snl, snh)


def _proj(x, wn, w, tabs, tm):
    n = x.shape[0]
    assert n % tm == 0
    nt = tabs[0].shape[0] // tm
    row = lambda i: (i, 0)
    fixed = lambda i: (0, 0)
    tab = lambda i: (i % nt, 0)
    widths = (CONV_CH, GDN_V_W, AB_W, DA_QK_W, DA_QK_W, DA_V_W)
    return pl.pallas_call(
        functools.partial(_proj_kernel, precise=w.dtype == F32),
        grid=(n // tm,),
        in_specs=[pl.BlockSpec((tm, D_MODEL), row),
                  pl.BlockSpec((1, D_MODEL), fixed),
                  pl.BlockSpec((D_MODEL, PROJ_PAD_W), fixed),
                  pl.BlockSpec((tm, LANES), tab),
                  pl.BlockSpec((tm, LANES), tab),
                  pl.BlockSpec((tm, LANES), tab)],
        out_specs=[pl.BlockSpec((tm, wd), row) for wd in widths],
        out_shape=[jax.ShapeDtypeStruct((n, wd), F32) for wd in widths],
        compiler_params=pltpu.CompilerParams(dimension_semantics=("parallel",),
                                             vmem_limit_bytes=VMEM_LIMIT),
        name="proj",
    )(x, wn, w, *tabs)


def _rope_tables(pos):
    inv_freq = ROPE_THETA ** (-jnp.arange(ROPE_HALF, dtype=F32) * 2.0 / ROPE_DIM)
    ang = pos.astype(F32)[:, None] * inv_freq[None, :]
    cos, sin = jnp.cos(ang), jnp.sin(ang)
    t = pos.shape[0]
    one = jnp.ones((t, DA_DQK - ROPE_DIM), F32)
    zero = jnp.zeros((t, DA_DQK - ROPE_DIM), F32)
    zh = jnp.zeros((t, ROPE_HALF), F32)
    cs = jnp.concatenate([cos, cos, one] * 2, axis=1)
    snl = jnp.concatenate([-sin, zh, zero] * 2, axis=1)
    snh = jnp.concatenate([zh, sin, zero] * 2, axis=1)
    return cs, snl, snh


def _prep_w_in(w_in):
    c, z, a, b, dq, dk, dv = jnp.split(
        w_in, np.cumsum([CONV_CH, GDN_V_W, GDN_HEADS, GDN_HEADS, DA_QK_W, DA_QK_W]).tolist(), axis=-1)
    pad = jnp.zeros((D_MODEL, AB_W - 2 * GDN_HEADS), w_in.dtype)
    return jnp.concatenate([c, z, dq, dk, dv, a, b, pad], axis=-1)


def _gdn_kernel(c_ref, z_ref, ab_ref, tail_ref, s0_ref, wconv_ref, alog_ref, dtb_ref, wg_ref,
                o_ref, sfin_ref, cbuf, ybuf, s_scr, u0_scr, wq_scr, kw_scr, qkd_scr, el_scr, *, tt):
    C, H = GDN_CHUNK, GDN_HEADS
    HS = GDN_STACK
    R = HS * C
    shift = C.bit_length() - 1
    t = pl.program_id(1)

    @pl.when(t == 0)
    def _():
        cbuf[0:8, :] = tail_ref[0]
        s_scr[...] = s0_ref[0]

    cbuf[8:8 + tt, :] = c_ref[0]
    acc = cbuf[8:8 + tt, :] * wconv_ref[CONV_W - 1:CONV_W, :]
    for i in range(CONV_W - 1):
        off = 8 - (CONV_W - 1) + i
        acc = acc + cbuf[off:off + tt, :] * wconv_ref[i:i + 1, :]
    ybuf[...] = acc * _sigmoid(acc)
    cbuf[0:8, :] = cbuf[tt:tt + 8, :]

    ii = lax.broadcasted_iota(jnp.int32, (R, R), 0)
    jj = lax.broadcasted_iota(jnp.int32, (R, R), 1)
    same_head = jnp.right_shift(ii, shift) == jnp.right_shift(jj, shift)
    strict = same_head & (ii > jj)
    causal = same_head & (ii >= jj)
    eye = (ii == jj).astype(F32)
    merge_masks = []
    s = 1
    while s < C:
        sh = s.bit_length() - 1
        merge_masks.append((jnp.right_shift(ii, sh + 1) == jnp.right_shift(jj, sh + 1))
                           & (jnp.right_shift(ii, sh) != jnp.right_shift(jj, sh)) & (ii > jj))
        s *= 2
    ci_ = lax.broadcasted_iota(jnp.int32, (C, C), 0)
    cj_ = lax.broadcasted_iota(jnp.int32, (C, C), 1)
    tri = (ci_ >= cj_).astype(F32)
    lane = lax.broadcasted_iota(jnp.int32, (R, LANES), 1)
    rhead = jnp.right_shift(lax.broadcasted_iota(jnp.int32, (R, LANES), 0), shift)
    ones8 = jnp.ones((8, LANES), F32)
    rep = lambda a: jnp.concatenate([a] * HS, axis=0)
    pick = lambda a, off: jnp.sum(jnp.where(lane == rhead + off, a, 0.0), axis=-1, keepdims=True)

    def stacked(rows, base, scale, h0):
        parts = []
        for h in range(h0, h0 + HS):
            x = ybuf[rows, base + h * GDN_DK:base + (h + 1) * GDN_DK]
            if scale is not None:
                x = x * lax.rsqrt(jnp.sum(x * x, axis=-1, keepdims=True) + EPS) * scale
            parts.append(x)
        return jnp.concatenate(parts, axis=0)

    n_stack = H // HS
    units = range((tt // C) * n_stack)
    low, rhs, qes = [], [], []
    for ci in range(tt // C):
        rows = slice(ci * C, (ci + 1) * C)
        ab = ab_ref[0, rows, :]
        gfull = -jnp.exp(alog_ref[...]) * _softplus(ab + dtb_ref[...])
        bfull = _sigmoid(ab)
        gcum = _dot_f32(tri, gfull)
        glast = gcum[C - 1:C, :]
        el_scr[ci] = jnp.broadcast_to(jnp.exp(glast), (8, LANES))
        for p in range(n_stack):
            un, h0 = ci * n_stack + p, p * HS
            g_m = jnp.where(lane == rhead + h0, rep(gcum), 0.0)
            g_col = jnp.sum(g_m, axis=-1, keepdims=True)
            g_row = _dot_nt_f32(ones8, g_m)[0:1, :]
            beta = pick(rep(bfull), H + h0)
            gl_col = pick(jnp.broadcast_to(glast, (R, LANES)), h0)
            decay = jnp.exp(jnp.where(causal, g_col - g_row, -jnp.inf))
            q = stacked(rows, 0, GDN_DK ** -0.5, h0)
            k = stacked(rows, GDN_QK_W, 1.0, h0)
            v = stacked(rows, 2 * GDN_QK_W, None, h0)
            kq = _mm(jnp.concatenate([k, q], axis=0), k, _NT)
            low.append(jnp.where(strict, kq[0:R] * decay * beta, 0.0))
            e_g = jnp.exp(g_col)
            rhs.append(jnp.concatenate([beta * v, (beta * e_g) * k], axis=1).astype(BF16))
            qes.append(q * e_g)
            kw_scr[un] = (k * jnp.exp(gl_col - g_col)).astype(BF16)
            qkd_scr[un] = jnp.where(causal, kq[R:2 * R] * decay, 0.0).astype(BF16)
    tinv = [eye - jnp.where(merge_masks[0], low[un], 0.0) for un in units]
    for m in merge_masks[1:]:
        half = [_mm(tinv[un], jnp.where(m, low[un], 0.0)) for un in units]
        tinv = [tinv[un] - _mm(half[un], tinv[un]) for un in units]
    for un in units:
        w2 = _mm(tinv[un], rhs[un])
        u0_scr[un] = w2[:, 0:GDN_DV]
        wk = w2[:, GDN_DV:]
        wq_scr[un] = jnp.concatenate(
            [x[h * C:(h + 1) * C] for h in range(HS) for x in (wk, qes[un])], axis=0).astype(BF16)

    st = [s_scr[h] for h in range(H)]
    for ci in range(tt // C):
        rows = slice(ci * C, (ci + 1) * C)
        el = el_scr[ci]
        for p in range(n_stack):
            un = ci * n_stack + p
            us, os_ = [], []
            for hh in range(HS):
                r = jnp.dot(wq_scr[un, 2 * hh * C:2 * (hh + 1) * C, :], st[p * HS + hh].astype(BF16),
                            preferred_element_type=F32)
                us.append(u0_scr[un, hh * C:(hh + 1) * C, :] - r[0:C])
                os_.append(r[C:2 * C])
            intra = jnp.dot(qkd_scr[un], jnp.concatenate(us, axis=0).astype(BF16),
                            preferred_element_type=F32)
            for hh in range(HS):
                h = p * HS + hh
                st[h] = (st[h] * el[0:1, h:h + 1]
                         + _mm(kw_scr[un, hh * C:(hh + 1) * C, :], us[hh], _TN))
                o = os_[hh] + intra[hh * C:(hh + 1) * C]
                zz = z_ref[0, rows, h * GDN_DV:(h + 1) * GDN_DV]
                o_ref[0, rows, h * GDN_DV:(h + 1) * GDN_DV] = (_rms(o, wg_ref[...])
                                                              * (zz * _sigmoid(zz)))
    for h in range(H):
        s_scr[h] = st[h]

    @pl.when(t == pl.num_programs(1) - 1)
    def _():
        sfin_ref[0] = s_scr[...]


def _gdn_step_kernel(c_ref, z_ref, ab_ref, cs_ref, s_ref, wconv_ref, alog_ref, dtb_ref, wg_ref,
                     o_ref, snew_ref, *, nb):
    H = GDN_HEADS
    acc = c_ref[...] * wconv_ref[CONV_W - 1:CONV_W, :]
    for i in range(CONV_W - 1):
        acc = acc + cs_ref[i] * wconv_ref[i:i + 1, :]
    y = acc * _sigmoid(acc)
    ab = ab_ref[...]
    e_g = jnp.exp(-jnp.exp(alog_ref[...]) * _softplus(ab + dtb_ref[...]))
    bfull = _sigmoid(ab)
    unit = lambda x: x * lax.rsqrt(jnp.sum(x * x, axis=-1, keepdims=True) + EPS)
    qs = [unit(y[:, h * GDN_DK:(h + 1) * GDN_DK]) * (GDN_DK ** -0.5) for h in range(H)]
    ks = [unit(y[:, GDN_QK_W + h * GDN_DK:GDN_QK_W + (h + 1) * GDN_DK]) for h in range(H)]
    vs = [y[:, 2 * GDN_QK_W + h * GDN_DV:2 * GDN_QK_W + (h + 1) * GDN_DV] for h in range(H)]
    cols = jnp.concatenate(ks + qs, axis=0).T
    for h in range(H):
        ks_rows, qs_rows = [], []
        for s in range(nb):
            state = s_ref[s, h]
            kcol = cols[:, h * nb + s:h * nb + s + 1]
            qcol = cols[:, (H + h) * nb + s:(H + h) * nb + s + 1]
            ks_rows.append(jnp.sum(kcol * state, axis=0, keepdims=True))
            qs_rows.append(jnp.sum(qcol * state, axis=0, keepdims=True))
        eg_h = e_g[:, h:h + 1]
        u = bfull[:, H + h:H + h + 1] * (vs[h] - eg_h * jnp.concatenate(ks_rows, axis=0))
        o = (eg_h * jnp.concatenate(qs_rows, axis=0)
             + jnp.sum(qs[h] * ks[h], axis=-1, keepdims=True) * u)
        eg_b = jnp.broadcast_to(eg_h, (nb, GDN_DV))
        for s in range(nb):
            kcol = cols[:, h * nb + s:h * nb + s + 1]
            snew_ref[s, h] = eg_b[s:s + 1, :] * s_ref[s, h] + kcol * u[s:s + 1, :]
        zz = z_ref[:, h * GDN_DV:(h + 1) * GDN_DV]
        o_ref[:, h * GDN_DV:(h + 1) * GDN_DV] = _rms(o, wg_ref[...]) * (zz * _sigmoid(zz))


def _gdn_step(c, z, ab, conv_state, s, wconv, alog, dtb, wg):
    bs = c.shape[0]
    nb = LANES // (2 * GDN_HEADS)
    assert bs % nb == 0
    row = lambda i: (i, 0)
    fixed = lambda i: (0, 0)
    state = pl.BlockSpec((nb, GDN_HEADS, GDN_DK, GDN_DV), lambda i: (i, 0, 0, 0))
    return pl.pallas_call(
        functools.partial(_gdn_step_kernel, nb=nb),
        grid=(bs // nb,),
        in_specs=[pl.BlockSpec((nb, CONV_CH), row),
                  pl.BlockSpec((nb, GDN_V_W), row),
                  pl.BlockSpec((nb, AB_W), row),
                  pl.BlockSpec((CONV_W - 1, nb, CONV_CH), lambda i: (0, i, 0)),
                  state,
                  pl.BlockSpec((CONV_W, CONV_CH), fixed),
                  pl.BlockSpec((1, LANES), fixed),
                  pl.BlockSpec((1, LANES), fixed),
                  pl.BlockSpec((1, GDN_DV), fixed)],
        out_specs=[pl.BlockSpec((nb, GDN_V_W), row), state],
        out_shape=[jax.ShapeDtypeStruct((bs, GDN_V_W), F32),
                   jax.ShapeDtypeStruct(s.shape, F32)],
        compiler_params=pltpu.CompilerParams(dimension_semantics=("parallel",),
                                             vmem_limit_bytes=VMEM_LIMIT),
        name="gdn_step",
    )(c, z, ab, conv_state, s, wconv, alog, dtb, wg)


def _gdn(c, z, ab, tail, s0, wconv, alog, dtb, wg, tt):
    b, t, _ = c.shape
    assert t % tt == 0 and tt % GDN_CHUNK == 0
    rs = GDN_STACK * GDN_CHUNK
    nc, nu = tt // GDN_CHUNK, (tt // GDN_CHUNK) * (GDN_HEADS // GDN_STACK)
    seq = lambda i, j: (i, j, 0)
    fixed2 = lambda i, j: (0, 0)
    tail_map = (lambda i, j: (i, 0, 0)) if tail.shape[0] == b else (lambda i, j: (0, 0, 0))
    s0_map = (lambda i, j: (i, 0, 0, 0)) if s0.shape[0] == b else (lambda i, j: (0, 0, 0, 0))
    return pl.pallas_call(
        functools.partial(_gdn_kernel, tt=tt),
        grid=(b, t // tt),
        in_specs=[pl.BlockSpec((1, tt, CONV_CH), seq),
                  pl.BlockSpec((1, tt, GDN_V_W), seq),
                  pl.BlockSpec((1, tt, AB_W), seq),
                  pl.BlockSpec((1, 8, CONV_CH), tail_map),
                  pl.BlockSpec((1, GDN_HEADS, GDN_DK, GDN_DV), s0_map),
                  pl.BlockSpec((CONV_W, CONV_CH), fixed2),
                  pl.BlockSpec((1, LANES), fixed2),
                  pl.BlockSpec((1, LANES), fixed2),
                  pl.BlockSpec((1, GDN_DV), fixed2)],
        out_specs=[pl.BlockSpec((1, tt, GDN_V_W), seq),
                   pl.BlockSpec((1, GDN_HEADS, GDN_DK, GDN_DV), lambda i, j: (i, 0, 0, 0))],
        out_shape=[jax.ShapeDtypeStruct((b, t, GDN_V_W), F32),
                   jax.ShapeDtypeStruct((b, GDN_HEADS, GDN_DK, GDN_DV), F32)],
        scratch_shapes=[pltpu.VMEM((tt + 8, CONV_CH), F32),
                        pltpu.VMEM((tt, CONV_CH), F32),
                        pltpu.VMEM((GDN_HEADS, GDN_DK, GDN_DV), F32),
                        pltpu.VMEM((nu, rs, GDN_DV), F32),
                        pltpu.VMEM((nu, 2 * rs, GDN_DK), BF16),
                        pltpu.VMEM((nu, rs, GDN_DK), BF16),
                        pltpu.VMEM((nu, rs, rs), BF16),
                        pltpu.VMEM((nc, 8, LANES), F32)],
        compiler_params=pltpu.CompilerParams(dimension_semantics=("parallel", "arbitrary"),
                                             vmem_limit_bytes=VMEM_LIMIT),
        name="gdn",
    )(c, z, ab, tail, s0, wconv, alog, dtb, wg)


def _kv_out_kernel(k_ref, v_ref, km_ref, vm_ref, ko_ref, vo_ref):
    t = k_ref.shape[1]
    for src, meta, dst in ((k_ref, km_ref, ko_ref), (v_ref, vm_ref, vo_ref)):
        for h in range(DA_HEADS):
            cols = slice(h * DA_DV, (h + 1) * DA_DV)
            dst[0, pl.ds(h, N_META, stride=DA_HEADS), :] = meta[:, cols]
            dst[0, pl.ds(N_META * DA_HEADS + h, t, stride=DA_HEADS), :] = src[0, :, cols]


def _kv_out(k, v, k_meta, v_meta):
    b, t, w = k.shape
    rows = (t + N_META) * DA_HEADS
    seq = pl.BlockSpec((1, t, w), lambda i: (i, 0, 0))
    meta = pl.BlockSpec((N_META, w), lambda i: (0, 0))
    out = pl.BlockSpec((1, rows, DA_DV), lambda i: (i, 0, 0))
    return pl.pallas_call(
        _kv_out_kernel,
        grid=(b,),
        in_specs=[seq, seq, meta, meta],
        out_specs=[out, out],
        out_shape=[jax.ShapeDtypeStruct((b, rows, DA_DV), F32)] * 2,
        compiler_params=pltpu.CompilerParams(dimension_semantics=("parallel",),
                                             vmem_limit_bytes=VMEM_LIMIT),
        name="kv_out",
    )(k, v, k_meta, v_meta)


def _lambda(lam_ref):
    lv = lam_ref[...]
    s1 = jnp.sum(lv[0:1, :] * lv[1:2, :], axis=-1, keepdims=True)
    s2 = jnp.sum(lv[2:3, :] * lv[3:4, :], axis=-1, keepdims=True)
    return jnp.exp(s1) - jnp.exp(s2) + LAM_INIT


def _split_components(q):
    lane = lax.broadcasted_iota(jnp.int32, q.shape, 1)
    first = (lane % DA_DV) < DA_DQK
    return jnp.where(first, q, 0.0).astype(BF16), jnp.where(first, 0.0, q).astype(BF16)


def _online_update(state, s, vb):
    m, l, acc = state
    m_new = jnp.maximum(m, jnp.max(s, axis=-1, keepdims=True))
    p = jnp.exp2(s - m_new)
    alpha = jnp.exp2(m - m_new)
    return (m_new, alpha * l + jnp.sum(p, axis=-1, keepdims=True),
            alpha * acc + jnp.dot(p.astype(BF16), vb, preferred_element_type=F32))


def _attn_prompt_kernel(q_ref, k_ref, v_ref, km_ref, vm_ref, lam_ref, wsub_ref, o_ref, *, bq, bk):
    qi = pl.program_id(2)
    q1, q2 = _split_components(q_ref[0] * (DA_DQK ** -0.5 * math.log2(math.e)))
    neg = jnp.full((bq, 1), -jnp.inf, F32)
    zero = jnp.zeros((bq, 1), F32)
    zacc = jnp.zeros((bq, DA_DV), F32)
    init = ((neg, zero, zacc), (neg, zero, zacc))

    def block(kblk, vblk, mask, st):
        kb, vb = kblk.astype(BF16), vblk.astype(BF16)
        s1, s2 = _dot_nt(q1, kb), _dot_nt(q2, kb)
        if mask is not None:
            s1 = jnp.where(mask, s1, -jnp.inf)
            s2 = jnp.where(mask, s2, -jnp.inf)
        return (_online_update(st[0], s1, vb), _online_update(st[1], s2, vb))

    n_full = lax.shift_right_logical(qi * bq, bk.bit_length() - 1)

    def body(kb, st):
        r0 = pl.multiple_of(kb * bk, bk)
        return block(k_ref[0, pl.ds(r0, bk), :], v_ref[0, pl.ds(r0, bk), :], None, st)

    st = lax.fori_loop(0, n_full, body, init)
    r0 = pl.multiple_of(n_full * bk, bk)
    kcat = jnp.concatenate([km_ref[...], k_ref[0, pl.ds(r0, bk), :]], axis=0)
    vcat = jnp.concatenate([vm_ref[...], v_ref[0, pl.ds(r0, bk), :]], axis=0)
    ri = lax.broadcasted_iota(jnp.int32, (bq, LANES + bk), 0)
    ci = lax.broadcasted_iota(jnp.int32, (bq, LANES + bk), 1)
    seen = (ci < N_META) | ((ci >= LANES) & (ci - LANES <= ri + (qi * bq - n_full * bk)))
    st = block(kcat, vcat, seen, st)
    (m1, l1, a1), (m2, l2, a2) = st
    o = a1 / l1 - _lambda(lam_ref) * (a2 / l2)
    o_ref[0] = _rms(o, wsub_ref[...]) * (1.0 - LAM_INIT)


def _attn_prompt(q, k, v, k_meta, v_meta, lam_vecs, wsub, bq, bk):
    b, t, _ = q.shape
    assert t % bk == 0 and bk % bq == 0 and k_meta.shape[0] == LANES
    return pl.pallas_call(
        functools.partial(_attn_prompt_kernel, bq=bq, bk=bk),
        grid=(b, DA_HEADS, t // bq),
        in_specs=[pl.BlockSpec((1, bq, DA_DV), lambda i, h, j: (i, j, h)),
                  pl.BlockSpec((1, t, DA_DV), lambda i, h, j: (i, 0, h)),
                  pl.BlockSpec((1, t, DA_DV), lambda i, h, j: (i, 0, h)),
                  pl.BlockSpec((LANES, DA_DV), lambda i, h, j: (0, h)),
                  pl.BlockSpec((LANES, DA_DV), lambda i, h, j: (0, h)),
                  pl.BlockSpec((4, DA_DQK), lambda i, h, j: (0, 0)),
                  pl.BlockSpec((1, DA_DV), lambda i, h, j: (0, 0))],
        out_specs=pl.BlockSpec((1, bq, DA_DV), lambda i, h, j: (i, j, h)),
        out_shape=jax.ShapeDtypeStruct((b, t, DA_V_W), F32),
        compiler_params=pltpu.CompilerParams(
            dimension_semantics=("parallel", "parallel", "arbitrary"),
            vmem_limit_bytes=VMEM_LIMIT),
        name="attn_prompt",
    )(q, k, v, k_meta, v_meta, lam_vecs, wsub)


def _attn_sample_kernel(pt_ref, q_ref, kn_ref, vn_ref, lam_ref, wsub_ref, *rest, pps):
    k_refs, v_refs = rest[:pps], rest[pps:2 * pps]
    o_ref, m_scr, l_scr, acc_scr = rest[2 * pps:]
    step = pl.program_id(1)
    nrow = 2 * DA_HEADS
    twice = lambda x: jnp.concatenate([x, x], axis=0)
    rowq = lax.broadcasted_iota(jnp.int32, (nrow, DA_DV), 0)
    laneq = lax.broadcasted_iota(jnp.int32, (nrow, DA_DV), 1)
    own_comp = (laneq // DA_DQK) == (rowq // DA_HEADS)
    qm = jnp.where(own_comp, twice(q_ref[0]) * (DA_DQK ** -0.5), 0.0)
    page_rows = k_refs[0].shape[1]
    rowp = lax.broadcasted_iota(jnp.int32, (nrow, page_rows * pps), 0)
    colp = lax.broadcasted_iota(jnp.int32, (nrow, page_rows * pps), 1)
    own_head = (colp % DA_HEADS) == (rowp % DA_HEADS)

    def rows3(x):
        hi = x.astype(BF16).astype(F32)
        mid = (x - hi).astype(BF16).astype(F32)
        return jnp.concatenate([hi, mid, x - hi - mid], axis=0).astype(BF16)

    def unrows3(r):
        return (r[0:nrow] + r[nrow:2 * nrow]) + r[2 * nrow:3 * nrow]

    def hi_lo(x):
        hi = x.astype(BF16)
        return hi, (x - hi.astype(F32)).astype(BF16)

    @pl.when(step == 0)
    def _():
        s_new = jnp.sum(qm * twice(kn_ref[0]), axis=-1, keepdims=True)
        m_scr[...] = s_new
        l_scr[...] = jnp.ones_like(s_new)
        acc_scr[...] = twice(vn_ref[0])

    q3 = rows3(qm)
    scores = []
    for r in k_refs:
        k_hi, k_lo = hi_lo(r[0])
        scores.append(unrows3(_dot_nt(q3, k_hi) + _dot_nt(q3, k_lo)))
    s = jnp.where(own_head, jnp.concatenate(scores, axis=1), -jnp.inf)
    m_old = m_scr[...]
    m_new = jnp.maximum(m_old, jnp.max(s, axis=-1, keepdims=True))
    p = jnp.exp(s - m_new)
    alpha = jnp.exp(m_old - m_new)
    p3 = rows3(p)
    pv = jnp.zeros((3 * nrow, DA_DV), F32)
    for j, r in enumerate(v_refs):
        v_hi, v_lo = hi_lo(r[0])
        pj = p3[:, j * page_rows:(j + 1) * page_rows]
        pv = pv + (jnp.dot(pj, v_hi, preferred_element_type=F32)
                   + jnp.dot(pj, v_lo, preferred_element_type=F32))
    m_scr[...] = m_new
    l_scr[...] = alpha * l_scr[...] + jnp.sum(p, axis=-1, keepdims=True)
    acc_scr[...] = alpha * acc_scr[...] + unrows3(pv)

    @pl.when(step == pl.num_programs(1) - 1)
    def _():
        w = acc_scr[...] / l_scr[...]
        o = w[0:DA_HEADS, :] - _lambda(lam_ref) * w[DA_HEADS:nrow, :]
        o_ref[0] = _rms(o, wsub_ref[...]) * (1.0 - LAM_INIT)


def _attn_sample(q, k_new, v_new, cache_k, cache_v, page_table, lam_vecs, wsub, pps):
    bs = q.shape[0]
    n_pages = page_table.shape[1]
    page_rows = cache_k.shape[1]
    assert n_pages % pps == 0
    tok = lambda i, s, pt: (i, 0, 0)
    fixed = lambda i, s, pt: (0, 0)

    def page_spec(j):
        return pl.BlockSpec((1, page_rows, DA_DV), lambda i, s, pt: (pt[i, s * pps + j], 0, 0))

    grid_spec = pltpu.PrefetchScalarGridSpec(
        num_scalar_prefetch=1,
        grid=(bs, n_pages // pps),
        in_specs=[pl.BlockSpec((1, DA_HEADS, DA_DV), tok),
                  pl.BlockSpec((1, DA_HEADS, DA_DV), tok),
                  pl.BlockSpec((1, DA_HEADS, DA_DV), tok),
                  pl.BlockSpec((4, DA_DQK), fixed),
                  pl.BlockSpec((1, DA_DV), fixed)]
                 + [page_spec(j) for j in range(pps)] * 2,
        out_specs=pl.BlockSpec((1, DA_HEADS, DA_DV), tok),
        scratch_shapes=[pltpu.VMEM((2 * DA_HEADS, 1), F32),
                        pltpu.VMEM((2 * DA_HEADS, 1), F32),
                        pltpu.VMEM((2 * DA_HEADS, DA_DV), F32)])
    return pl.pallas_call(
        functools.partial(_attn_sample_kernel, pps=pps),
        grid_spec=grid_spec,
        out_shape=jax.ShapeDtypeStruct((bs, DA_HEADS, DA_DV), F32),
        compiler_params=pltpu.CompilerParams(dimension_semantics=("parallel", "arbitrary"),
                                             vmem_limit_bytes=VMEM_LIMIT),
        name="attn_sample",
    )(page_table, q, k_new, v_new, lam_vecs, wsub, *([cache_k] * pps), *([cache_v] * pps))


def _route(logits):
    lane = lax.broadcasted_iota(jnp.int32, logits.shape, 1).astype(F32)
    big = float(LANES)
    ninf = -jnp.inf

    def top(mask):
        val = jnp.where(mask, logits, ninf)
        mx = jnp.max(val, axis=-1, keepdims=True)
        idx = jnp.min(jnp.where(mask & (val == mx), lane, big), axis=-1, keepdims=True)
        return mx, idx

    gmask = (lane >= N_EXPERTS) & (lane < N_EXPERTS + N_GROUPS)
    gmax, gidx = top(gmask)
    g_w = 1.0 / jnp.sum(jnp.where(gmask, jnp.exp(logits - gmax), 0.0), axis=-1, keepdims=True)
    g_sel = gidx - N_EXPERTS
    emask = (lane >= g_sel * EXPERTS_PER_GROUP) & (lane < (g_sel + 1) * EXPERTS_PER_GROUP)
    v1, i1 = top(emask)
    v2, i2 = top(emask & (lane != i1))
    e21 = jnp.exp(v2 - v1)
    p1 = 1.0 / (1.0 + e21)
    return jnp.where(lane == i1, p1 * g_w, 0.0) + jnp.where(lane == i2, e21 * p1 * g_w, 0.0), g_sel


def _mix_route(g_ref, a_ref, x_ref, wo_ref, wnf_ref, wr_ref, precise):
    mix = (_mm(g_ref[...], wo_ref[0:GDN_V_W, :], _NN, precise)
           + _mm(a_ref[...], wo_ref[GDN_V_W:, :], _NN, precise))
    x1 = x_ref[...] + mix
    xn = _rms(x1, wnf_ref[...])
    wr = wr_ref[...]
    if precise:
        logits = _dot_f32(xn, wr)
    else:
        x_hi, w_hi = xn.astype(BF16), wr.astype(BF16)
        x_lo, w_lo = (xn - x_hi.astype(F32)).astype(BF16), (wr - w_hi.astype(F32)).astype(BF16)
        logits = (jnp.dot(x_hi, w_hi, preferred_element_type=F32)
                  + (jnp.dot(x_hi, w_lo, preferred_element_type=F32)
                     + jnp.dot(x_lo, w_hi, preferred_element_type=F32)))
    comb, g_sel = _route(logits)
    return x1, xn, comb, g_sel


def _expert(xn, comb, e_lane, wg, wu, wd):
    hg = jnp.dot(xn, wg.astype(BF16), preferred_element_type=F32)
    hu = jnp.dot(xn, wu.astype(BF16), preferred_element_type=F32)
    lane = lax.broadcasted_iota(jnp.int32, comb.shape, 1)
    cw = jnp.sum(jnp.where(lane == e_lane, comb, 0.0), axis=-1, keepdims=True)
    hh = (hg * _sigmoid(hg)) * hu * cw
    return jnp.dot(hh.astype(BF16), wd.astype(BF16), preferred_element_type=F32)


def _ffn_kernel(g_ref, a_ref, x_ref, wo_ref, wnf_ref, wr_ref, wg_ref, wu_ref, wd_ref, wfin_ref,
                y_ref, x1_scr, xn_scr, comb_scr, acc_scr, *, precise):
    e = pl.program_id(1)

    @pl.when(e == 0)
    def _():
        x1, xn, comb, _ = _mix_route(g_ref, a_ref, x_ref, wo_ref, wnf_ref, wr_ref, precise)
        x1_scr[...] = x1
        xn_scr[...] = xn.astype(BF16)
        comb_scr[...] = comb
        acc_scr[...] = jnp.zeros_like(acc_scr)

    acc_scr[...] += _expert(xn_scr[...], comb_scr[...], e, wg_ref[0], wu_ref[0], wd_ref[0])

    @pl.when(e == pl.num_programs(1) - 1)
    def _():
        y_ref[...] = _rms(x1_scr[...] + acc_scr[...], wfin_ref[...])


def _ffn(g_out, a_out, x, wo, wnf, wr, wg, wu, wd, wfin, tm):
    n = x.shape[0]
    assert n % tm == 0
    row = lambda i, e: (i, 0)
    fixed = lambda i, e: (0, 0)
    exp = lambda i, e: (e, 0, 0)
    return pl.pallas_call(
        functools.partial(_ffn_kernel, precise=wo.dtype == F32),
        grid=(n // tm, N_EXPERTS),
        in_specs=[pl.BlockSpec((tm, GDN_V_W), row),
                  pl.BlockSpec((tm, DA_V_W), row),
                  pl.BlockSpec((tm, D_MODEL), row),
                  pl.BlockSpec((GDN_V_W + DA_V_W, D_MODEL), fixed),
                  pl.BlockSpec((1, D_MODEL), fixed),
                  pl.BlockSpec((D_MODEL, LANES), fixed),
                  pl.BlockSpec((1, D_MODEL, D_EXPERT), exp),
                  pl.BlockSpec((1, D_MODEL, D_EXPERT), exp),
                  pl.BlockSpec((1, D_EXPERT, D_MODEL), exp),
                  pl.BlockSpec((1, D_MODEL), fixed)],
        out_specs=pl.BlockSpec((tm, D_MODEL), row),
        out_shape=jax.ShapeDtypeStruct((n, D_MODEL), F32),
        scratch_shapes=[pltpu.VMEM((tm, D_MODEL), F32),
                        pltpu.VMEM((tm, D_MODEL), BF16),
                        pltpu.VMEM((tm, LANES), F32),
                        pltpu.VMEM((tm, D_MODEL), F32)],
        compiler_params=pltpu.CompilerParams(dimension_semantics=("parallel", "arbitrary"),
                                             vmem_limit_bytes=VMEM_LIMIT),
        name="ffn",
    )(g_out, a_out, x, wo, wnf, wr, wg, wu, wd, wfin)


ROW_W = D_MODEL + LANES
MOE_TILE = 512
ROW_TILE = 1024


def _ffn_pre_kernel(g_ref, a_ref, x_ref, wo_ref, wnf_ref, wr_ref, rows_ref, info_ref, cnt_ref,
                    cnt_scr):
    @pl.when(pl.program_id(0) == 0)
    def _():
        cnt_scr[...] = jnp.zeros_like(cnt_scr)

    x1, _, comb, g_sel = _mix_route(g_ref, a_ref, x_ref, wo_ref, wnf_ref, wr_ref, False)
    tm = x1.shape[0]
    lane = lax.broadcasted_iota(jnp.int32, (tm, LANES), 1).astype(F32)
    onehot = lane == g_sel
    ri = lax.broadcasted_iota(jnp.int32, (tm, tm), 0)
    ci = lax.broadcasted_iota(jnp.int32, (tm, tm), 1)
    earlier = jnp.dot((ci < ri).astype(BF16), onehot.astype(BF16), preferred_element_type=F32)
    rank = jnp.sum(jnp.where(onehot, earlier + cnt_scr[...], 0.0), axis=-1, keepdims=True)
    cnt_scr[...] += jnp.sum(onehot.astype(F32), axis=0, keepdims=True)
    cnt_ref[...] = cnt_scr[...]
    rows_ref[:, 0:D_MODEL] = x1
    rows_ref[:, D_MODEL:] = comb
    packed = jnp.where(lane == 0.0, g_sel, jnp.where(lane == 1.0, rank, 0.0))
    r8 = lax.broadcasted_iota(jnp.int32, (8, LANES), 0)
    l8 = lax.broadcasted_iota(jnp.int32, (8, LANES), 1)
    sel = (r8 == l8).astype(BF16)
    hi = packed.astype(BF16)
    info_ref[...] = _dot_nt(sel, hi) + _dot_nt(sel, packed - hi.astype(F32))


def _ffn_pre(g_out, a_out, x, wo, wnf, wr, tm):
    n = x.shape[0]
    assert n % tm == 0
    row = lambda i: (i, 0)
    fixed = lambda i: (0, 0)
    return pl.pallas_call(
        _ffn_pre_kernel,
        grid=(n // tm,),
        in_specs=[pl.BlockSpec((tm, GDN_V_W), row),
                  pl.BlockSpec((tm, DA_V_W), row),
                  pl.BlockSpec((tm, D_MODEL), row),
                  pl.BlockSpec((GDN_V_W + DA_V_W, D_MODEL), fixed),
                  pl.BlockSpec((1, D_MODEL), fixed),
                  pl.BlockSpec((D_MODEL, LANES), fixed)],
        out_specs=[pl.BlockSpec((tm, ROW_W), row),
                   pl.BlockSpec((8, tm), lambda i: (0, i)),
                   pl.BlockSpec((1, LANES), fixed)],
        out_shape=[jax.ShapeDtypeStruct((n, ROW_W), F32),
                   jax.ShapeDtypeStruct((8, n), F32),
                   jax.ShapeDtypeStruct((1, LANES), F32)],
        scratch_shapes=[pltpu.VMEM((1, LANES), F32)],
        compiler_params=pltpu.CompilerParams(dimension_semantics=("arbitrary",),
                                             vmem_limit_bytes=VMEM_LIMIT),
        name="ffn_pre",
    )(g_out, a_out, x, wo, wnf, wr)


def _copy_rows(idx_ref, tile_ref, hbm_ref, sem, scatter):
    tm = tile_ref.shape[0]
    base = pl.program_id(0) * tm

    def body(r, carry):
        there, here = hbm_ref.at[pl.ds(idx_ref[base + r], 1)], tile_ref.at[pl.ds(r, 1)]
        src, dst = (here, there) if scatter else (there, here)
        pltpu.make_async_copy(src, dst, sem).start()
        return carry

    lax.fori_loop(0, tm, body, 0, unroll=8)
    there, here = hbm_ref.at[pl.ds(0, tm)], tile_ref
    src, dst = (here, there) if scatter else (there, here)
    pltpu.make_async_copy(src, dst, sem).wait()


def _scatter_kernel(idx_ref, tile_ref, init_ref, dst_ref, sem):
    del init_ref
    _copy_rows(idx_ref, tile_ref, dst_ref, sem, True)


def _gather_kernel(idx_ref, src_ref, tile_ref, sem):
    _copy_rows(idx_ref, tile_ref, src_ref, sem, False)


def _move_rows(idx, src, n_out, scatter):
    n, w = idx.shape[0], src.shape[1]
    assert n % ROW_TILE == 0
    any_spec = pl.BlockSpec(memory_space=pl.ANY)
    tile_spec = pl.BlockSpec((ROW_TILE, w), lambda i, idx: (i, 0))
    if scatter:
        operands = (idx, src, jnp.zeros((n_out, w), src.dtype))
        in_specs, out_spec = [tile_spec, any_spec], any_spec
    else:
        operands = (idx, src)
        in_specs, out_spec = [any_spec], tile_spec
    return pl.pallas_call(
        _scatter_kernel if scatter else _gather_kernel,
        grid_spec=pltpu.PrefetchScalarGridSpec(
            num_scalar_prefetch=1, grid=(n // ROW_TILE,),
            in_specs=in_specs, out_specs=out_spec,
            scratch_shapes=[pltpu.SemaphoreType.DMA(())]),
        out_shape=jax.ShapeDtypeStruct((n_out, w), src.dtype),
        input_output_aliases={2: 0} if scatter else {},
        compiler_params=pltpu.CompilerParams(dimension_semantics=("arbitrary",)),
        name="scatter_rows" if scatter else "gather_rows",
    )(*operands)


def _moe_groups_kernel(tg_ref, nt_ref, rows_ref, wnf_ref, wg_ref, wu_ref, wd_ref, wfin_ref, y_ref,
                       wg_bf, wu_bf, wd_bf):
    j = pl.program_id(0)

    @pl.when((j == 0) | (tg_ref[j] != tg_ref[jnp.maximum(j - 1, 0)]))
    def _():
        for e in range(EXPERTS_PER_GROUP):
            wg_bf[e] = wg_ref[e].astype(BF16)
            wu_bf[e] = wu_ref[e].astype(BF16)
            wd_bf[e] = wd_ref[e].astype(BF16)

    @pl.when(j < nt_ref[0])
    def _():
        x1 = rows_ref[:, 0:D_MODEL]
        comb = rows_ref[:, D_MODEL:]
        xn = _rms(x1, wnf_ref[...]).astype(BF16)
        first = tg_ref[j] * EXPERTS_PER_GROUP
        acc = jnp.zeros_like(x1)
        for e in range(EXPERTS_PER_GROUP):
            acc = acc + _expert(xn, comb, first + e, wg_bf[e], wu_bf[e], wd_bf[e])
        y_ref[...] = _rms(x1 + acc, wfin_ref[...])

    @pl.when(j >= nt_ref[0])
    def _():
        y_ref[...] = jnp.zeros_like(y_ref)


def _moe_groups(tile_group, n_tiles_used, rows, wnf, wg, wu, wd, wfin):
    n_tiles = tile_group.shape[0]
    fixed = lambda j, tg, nt: (0, 0)
    grp = lambda j, tg, nt: (tg[j], 0, 0)
    in_shape, out_shape = (EXPERTS_PER_GROUP, D_MODEL, D_EXPERT), (EXPERTS_PER_GROUP, D_EXPERT, D_MODEL)
    once = pl.Buffered(1)
    return pl.pallas_call(
        _moe_groups_kernel,
        grid_spec=pltpu.PrefetchScalarGridSpec(
            num_scalar_prefetch=2, grid=(n_tiles,),
            in_specs=[pl.BlockSpec((MOE_TILE, ROW_W), lambda j, tg, nt: (j, 0)),
                      pl.BlockSpec((1, D_MODEL), fixed),
                      pl.BlockSpec(in_shape, grp, pipeline_mode=once),
                      pl.BlockSpec(in_shape, grp, pipeline_mode=once),
                      pl.BlockSpec(out_shape, grp, pipeline_mode=once),
                      pl.BlockSpec((1, D_MODEL), fixed)],
            out_specs=pl.BlockSpec((MOE_TILE, D_MODEL), lambda j, tg, nt: (j, 0)),
            scratch_shapes=[pltpu.VMEM(in_shape, BF16), pltpu.VMEM(in_shape, BF16),
                            pltpu.VMEM(out_shape, BF16)]),
        out_shape=jax.ShapeDtypeStruct((n_tiles * MOE_TILE, D_MODEL), F32),
        compiler_params=pltpu.CompilerParams(dimension_semantics=("arbitrary",),
                                             vmem_limit_bytes=VMEM_LIMIT),
        name="moe_groups",
    )(tile_group, n_tiles_used, rows, wnf, wg, wu, wd, wfin)


def _ffn_sorted(g_out, a_out, x, wo, wnf, wr, wg, wu, wd, wfin):
    n = x.shape[0]
    rows, info, cnt = _ffn_pre(g_out, a_out, x, wo, wnf, wr, 512)
    grp, rank = info[0].astype(jnp.int32), info[1].astype(jnp.int32)
    tiles = (cnt[0, :N_GROUPS].astype(jnp.int32) + MOE_TILE - 1) // MOE_TILE
    ends = jnp.cumsum(tiles)
    dest = (ends - tiles)[grp] * MOE_TILE + rank
    n_tiles = n // MOE_TILE + N_GROUPS
    tile_group = jnp.minimum(
        jnp.sum(jnp.arange(n_tiles, dtype=jnp.int32)[:, None] >= ends[None, :], axis=1),
        N_GROUPS - 1).astype(jnp.int32)
    rows_sorted = _move_rows(dest, rows, n_tiles * MOE_TILE, True)
    y_sorted = _moe_groups(tile_group, ends[N_GROUPS - 1:], rows_sorted, wnf, wg, wu, wd, wfin)
    return _move_rows(dest, y_sorted, n, False)


def kernel(x_prompt, x_sample, cache_k, cache_v, state_conv, state_gdn, page_table, meta_tokens,
           w_norm_mix, w_in, w_conv, a_log, dt_bias, w_gnorm, lambda_q1, lambda_k1, lambda_q2,
           lambda_k2, w_subln, w_out, w_norm_ffn, w_group, w_expert, w_gate, w_up, w_down, w_final):
    l = 0
    bp, seq = x_prompt.shape[0], x_prompt.shape[1]
    bs = x_sample.shape[0]
    n_pool, page = cache_k.shape[1], cache_k.shape[2]
    past = page_table.shape[1] * page
    C = GDN_CHUNK

    w_in_p = _prep_w_in(w_in[l])
    wn_mix = w_norm_mix[l][None, :]
    pad_lanes = lambda v: jnp.pad(v, (0, LANES - v.shape[0]))[None, :]
    alog, dtb = pad_lanes(a_log[l]), pad_lanes(dt_bias[l])
    wgn = w_gnorm[l][None, :]
    lam_vecs = jnp.stack([lambda_q1[l], lambda_k1[l], lambda_q2[l], lambda_k2[l]])
    wsub = w_subln[l][None, :]
    w_router = jnp.pad(jnp.concatenate([w_expert[l], w_group[l]], axis=1),
                       ((0, 0), (0, LANES - N_EXPERTS - N_GROUPS)))
    ffn_w = (w_norm_ffn[l][None, :], w_router, w_gate[l], w_up[l], w_down[l], w_final[None, :])

    pos_small = jnp.concatenate([jnp.arange(N_META), jnp.full((bs,), past)])
    rows_small = jnp.concatenate([meta_tokens, x_sample[:, 0, :]], axis=0)
    c_s, z_s, ab_s, q_s, k_s, v_s = _proj(rows_small, wn_mix, w_in_p, _rope_tables(pos_small),
                                           N_META + bs)
    xp = x_prompt.reshape(bp * seq, D_MODEL)
    c_p, z_p, ab_p, q_p, k_p, v_p = _proj(xp, wn_mix, w_in_p.astype(BF16),
                                           _rope_tables(N_META + jnp.arange(seq)), 512)

    front = lambda a: jnp.pad(a[:N_META], ((C - N_META, 0), (0, 0)))[None]
    zero_tail = jnp.zeros((1, 8, CONV_CH), F32)
    zero_state = jnp.zeros((1, GDN_HEADS, GDN_DK, GDN_DV), F32)
    _, s_meta = _gdn(front(c_s), front(z_s), front(ab_s), zero_tail, zero_state,
                     w_conv[l], alog, dtb, wgn, C)
    tail_meta = jnp.pad(c_s[N_META - (CONV_W - 1):N_META], ((8 - (CONV_W - 1), 0), (0, 0)))[None]
    c_p3 = c_p.reshape(bp, seq, CONV_CH)
    g_out, p_gdn = _gdn(c_p3, z_p.reshape(bp, seq, GDN_V_W), ab_p.reshape(bp, seq, AB_W),
                        tail_meta, s_meta, w_conv[l], alog, dtb, wgn, 256)
    p_conv = c_p3[:, seq - (CONV_W - 1):, :]

    p_k, p_v = _kv_out(k_p.reshape(bp, seq, DA_QK_W), v_p.reshape(bp, seq, DA_V_W),
                       k_s[:N_META], v_s[:N_META])
    meta_blk = lambda m: jnp.pad(m[:N_META], ((0, LANES - N_META), (0, 0)))
    a_out = _attn_prompt(q_p.reshape(bp, seq, DA_QK_W), k_p.reshape(bp, seq, DA_QK_W),
                         v_p.reshape(bp, seq, DA_V_W), meta_blk(k_s), meta_blk(v_s),
                         lam_vecs, wsub, 256, 512)

    y_prompt = _ffn_sorted(g_out.reshape(bp * seq, GDN_V_W), a_out.reshape(bp * seq, DA_V_W), xp,
                           w_out[l].astype(BF16), *ffn_w).reshape(bp, seq, D_MODEL)

    sm = lambda a: a[N_META:]
    g_s, s_gdn = _gdn_step(sm(c_s), sm(z_s), sm(ab_s), jnp.swapaxes(state_conv[l], 0, 1),
                           state_gdn[l], w_conv[l], alog, dtb, wgn)
    s_conv = jnp.concatenate([state_conv[l][:, 1:, :], sm(c_s)[:, None, :]], axis=1)
    heads = lambda a: sm(a).reshape(bs, DA_HEADS, DA_DV)
    a_s = _attn_sample(heads(q_s), heads(k_s), heads(v_s),
                       cache_k[l].reshape(n_pool, page * DA_HEADS, 2 * DA_DQK),
                       cache_v[l].reshape(n_pool, page * DA_HEADS, DA_DV),
                       page_table, lam_vecs, wsub, 32)
    y_sample = _ffn(g_s, a_s.reshape(bs, DA_V_W), x_sample[:, 0, :], w_out[l], *ffn_w,
                    bs).reshape(bs, 1, D_MODEL)

    return (y_prompt, y_sample, p_conv[None], p_gdn[None],
            p_k.reshape(1, bp, seq + N_META, DA_HEADS, 2 * DA_DQK),
            p_v.reshape(1, bp, seq + N_META, DA_HEADS, DA_DV),
            s_conv[None], s_gdn[None],
            sm(k_s).reshape(1, bs, 1, DA_HEADS, 2 * DA_DQK),
            sm(v_s).reshape(1, bs, 1, DA_HEADS, DA_DV))
```

```python
import functools
import math

import jax
import jax.numpy as jnp
import numpy as np
from jax import lax
from jax.experimental import pallas as pl
from jax.experimental.pallas import tpu as pltpu

F32 = jnp.float32
BF16 = jnp.bfloat16

D_MODEL = 1024
N_META = 16
EPS = 1e-6
GDN_HEADS = 4
GDN_DK = 128
GDN_DV = 128
CONV_W = 4
GDN_QK_W = GDN_HEADS * GDN_DK
GDN_V_W = GDN_HEADS * GDN_DV
CONV_CH = 2 * GDN_QK_W + GDN_V_W
DA_HEADS = 4
DA_DQK = 64
DA_DV = 2 * DA_DQK
DA_QK_W = DA_HEADS * 2 * DA_DQK
DA_V_W = DA_HEADS * DA_DV
ROPE_DIM = DA_DQK // 4
ROPE_HALF = ROPE_DIM // 2
ROPE_THETA = 500000.0
N_GROUPS = 4
EXPERTS_PER_GROUP = 8
N_EXPERTS = N_GROUPS * EXPERTS_PER_GROUP
D_EXPERT = 256
LAM_INIT = 0.8 - 0.6 * math.exp(-0.3 * 0)

LANES = 128
GDN_CHUNK = 64
GDN_STACK = 2
AB_W = LANES
PROJ_PAD_W = CONV_CH + GDN_V_W + 2 * DA_QK_W + DA_V_W + AB_W
VMEM_LIMIT = 56 * 1024 * 1024


_NN = (((1,), (0,)), ((), ()))
_NT = (((1,), (1,)), ((), ()))
_TN = (((0,), (0,)), ((), ()))


def _mm(a, b, dims=_NN, precise=False):
    if precise:
        return lax.dot_general(a.astype(F32), b.astype(F32), dims, preferred_element_type=F32,
                               precision=lax.Precision.HIGHEST)
    return lax.dot_general(a.astype(BF16), b.astype(BF16), dims, preferred_element_type=F32)


def _dot_nt(a, b):
    return _mm(a, b, _NT)


def _dot_f32(a, b):
    return _mm(a, b, _NN, True)


def _dot_nt_f32(a, b):
    return _mm(a, b, _NT, True)


def _sigmoid(x):
    return 1.0 / (1.0 + jnp.exp(-x))


def _softplus(x):
    return jnp.maximum(x, 0.0) + jnp.log(1.0 + jnp.exp(-jnp.abs(x)))


def _rms(x, w):
    return x * lax.rsqrt(jnp.mean(x * x, axis=-1, keepdims=True) + EPS) * w


def _rope(x, cs, sn_lo, sn_hi):
    w = x.shape[1]
    cs = jnp.concatenate([cs] * DA_HEADS, axis=1)
    sn_lo = jnp.concatenate([sn_lo] * DA_HEADS, axis=1)
    sn_hi = jnp.concatenate([sn_hi] * DA_HEADS, axis=1)
    up = pltpu.roll(x, w - ROPE_HALF, 1)
    dn = pltpu.roll(x, ROPE_HALF, 1)
    return x * cs + up * sn_lo + dn * sn_hi


def _proj_kernel(x_ref, wn_ref, w_ref, cs_ref, snl_ref, snh_ref,
                 c_ref, z_ref, ab_ref, q_ref, k_ref, v_ref, *, precise):
    h = _rms(x_ref[...], wn_ref[...])
    h = h if precise else h.astype(BF16)
    o = 0
    c_ref[...] = _mm(h, w_ref[:, o:o + CONV_CH], _NN, precise)
    o += CONV_CH
    z_ref[...] = _mm(h, w_ref[:, o:o + GDN_V_W], _NN, precise)
    o += GDN_V_W
    q = _mm(h, w_ref[:, o:o + DA_QK_W], _NN, precise)
    o += DA_QK_W
    k = _mm(h, w_ref[:, o:o + DA_QK_W], _NN, precise)
    o += DA_QK_W
    v_ref[...] = _mm(h, w_ref[:, o:o + DA_V_W], _NN, precise)
    o += DA_V_W
    ab_ref[...] = _mm(h, w_ref[:, o:o + AB_W], _NN, precise)
    cs, snl, snh = cs_ref[...], snl_ref[...], snh_ref[...]
    q_ref[...] = _rope(q, cs, snl, snh)
    k_ref[...] = _rope(k, cs, snl, snh)


def _proj(x, wn, w, tabs, tm):
    n = x.shape[0]
    assert n % tm == 0
    nt = tabs[0].shape[0] // tm
    row = lambda i: (i, 0)
    fixed = lambda i: (0, 0)
    tab = lambda i: (i % nt, 0)
    widths = (CONV_CH, GDN_V_W, AB_W, DA_QK_W, DA_QK_W, DA_V_W)
    return pl.pallas_call(
        functools.partial(_proj_kernel, precise=w.dtype == F32),
        grid=(n // tm,),
        in_specs=[pl.BlockSpec((tm, D_MODEL), row),
                  pl.BlockSpec((1, D_MODEL), fixed),
                  pl.BlockSpec((D_MODEL, PROJ_PAD_W), fixed),
                  pl.BlockSpec((tm, LANES), tab),
                  pl.BlockSpec((tm, LANES), tab),
                  pl.BlockSpec((tm, LANES), tab)],
        out_specs=[pl.BlockSpec((tm, wd), row) for wd in widths],
        out_shape=[jax.ShapeDtypeStruct((n, wd), F32) for wd in widths],
        compiler_params=pltpu.CompilerParams(dimension_semantics=("parallel",),
                                             vmem_limit_bytes=VMEM_LIMIT),
        name="proj",
    )(x, wn, w, *tabs)


def _proj_qkv_kernel(x_ref, wn_ref, w_ref, cs_ref, snl_ref, snh_ref, q_ref, k_ref, v_ref):
    h = _rms(x_ref[...], wn_ref[...]).astype(BF16)
    cs, snl, snh = cs_ref[...], snl_ref[...], snh_ref[...]
    q = jnp.dot(h, w_ref[:, 0:DA_QK_W], preferred_element_type=F32)
    k = jnp.dot(h, w_ref[:, DA_QK_W:2 * DA_QK_W], preferred_element_type=F32)
    v_ref[...] = jnp.dot(h, w_ref[:, 2 * DA_QK_W:], preferred_element_type=F32)
    q_ref[...] = _rope(q, cs, snl, snh)
    k_ref[...] = _rope(k, cs, snl, snh)


def _proj_qkv(x, wn, w, tabs, tm):
    n = x.shape[0]
    assert n % tm == 0
    nt = tabs[0].shape[0] // tm
    row = lambda i: (i, 0)
    fixed = lambda i: (0, 0)
    tab = lambda i: (i % nt, 0)
    widths = (DA_QK_W, DA_QK_W, DA_V_W)
    return pl.pallas_call(
        _proj_qkv_kernel,
        grid=(n // tm,),
        in_specs=[pl.BlockSpec((tm, D_MODEL), row),
                  pl.BlockSpec((1, D_MODEL), fixed),
                  pl.BlockSpec((D_MODEL, sum(widths)), fixed),
                  pl.BlockSpec((tm, LANES), tab),
                  pl.BlockSpec((tm, LANES), tab),
                  pl.BlockSpec((tm, LANES), tab)],
        out_specs=[pl.BlockSpec((tm, wd), row) for wd in widths],
        out_shape=[jax.ShapeDtypeStruct((n, wd), F32) for wd in widths],
        compiler_params=pltpu.CompilerParams(dimension_semantics=("parallel",),
                                             vmem_limit_bytes=VMEM_LIMIT),
        name="proj_qkv",
    )(x, wn, w, *tabs)


def _rope_tables(pos):
    inv_freq = ROPE_THETA ** (-jnp.arange(ROPE_HALF, dtype=F32) * 2.0 / ROPE_DIM)
    ang = pos.astype(F32)[:, None] * inv_freq[None, :]
    cos, sin = jnp.cos(ang), jnp.sin(ang)
    t = pos.shape[0]
    one = jnp.ones((t, DA_DQK - ROPE_DIM), F32)
    zero = jnp.zeros((t, DA_DQK - ROPE_DIM), F32)
    zh = jnp.zeros((t, ROPE_HALF), F32)
    cs = jnp.concatenate([cos, cos, one] * 2, axis=1)
    snl = jnp.concatenate([-sin, zh, zero] * 2, axis=1)
    snh = jnp.concatenate([zh, sin, zero] * 2, axis=1)
    return cs, snl, snh


def _prep_w_in(w_in):
    c, z, a, b, dq, dk, dv = jnp.split(
        w_in, np.cumsum([CONV_CH, GDN_V_W, GDN_HEADS, GDN_HEADS, DA_QK_W, DA_QK_W]).tolist(), axis=-1)
    pad = jnp.zeros((D_MODEL, AB_W - 2 * GDN_HEADS), w_in.dtype)
    return jnp.concatenate([c, z, dq, dk, dv, a, b, pad], axis=-1)


def _gdn_kernel(x_ref, wn_ref, wcz_ref, tail_ref, s0_ref, wconv_ref, alog_ref, dtb_ref, wg_ref,
                o_ref, sfin_ref, ctail_ref, cbuf, ybuf, zbuf, abbuf, s_scr, u0_scr, wq_scr, kw_scr,
                qkd_scr, el_scr, *, tt):
    C, H = GDN_CHUNK, GDN_HEADS
    HS = GDN_STACK
    R = HS * C
    shift = C.bit_length() - 1
    t = pl.program_id(1)

    @pl.when(t == 0)
    def _():
        cbuf[0:8, :] = tail_ref[0]
        s_scr[...] = s0_ref[0]

    hn = _rms(x_ref[0], wn_ref[...]).astype(BF16)
    cbuf[8:8 + tt, :] = jnp.dot(hn, wcz_ref[:, 0:CONV_CH], preferred_element_type=F32)
    zbuf[...] = jnp.dot(hn, wcz_ref[:, CONV_CH:CONV_CH + GDN_V_W], preferred_element_type=F32)
    abbuf[...] = jnp.dot(hn, wcz_ref[:, CONV_CH + GDN_V_W:], preferred_element_type=F32)

    acc = cbuf[8:8 + tt, :] * wconv_ref[CONV_W - 1:CONV_W, :]
    for i in range(CONV_W - 1):
        off = 8 - (CONV_W - 1) + i
        acc = acc + cbuf[off:off + tt, :] * wconv_ref[i:i + 1, :]
    ybuf[...] = acc * _sigmoid(acc)
    cbuf[0:8, :] = cbuf[tt:tt + 8, :]

    ii = lax.broadcasted_iota(jnp.int32, (R, R), 0)
    jj = lax.broadcasted_iota(jnp.int32, (R, R), 1)
    same_head = jnp.right_shift(ii, shift) == jnp.right_shift(jj, shift)
    strict = same_head & (ii > jj)
    causal = same_head & (ii >= jj)
    eye = (ii == jj).astype(F32)
    merge_masks = []
    s = 1
    while s < C:
        sh = s.bit_length() - 1
        merge_masks.append((jnp.right_shift(ii, sh + 1) == jnp.right_shift(jj, sh + 1))
                           & (jnp.right_shift(ii, sh) != jnp.right_shift(jj, sh)) & (ii > jj))
        s *= 2
    ci_ = lax.broadcasted_iota(jnp.int32, (C, C), 0)
    cj_ = lax.broadcasted_iota(jnp.int32, (C, C), 1)
    tri = (ci_ >= cj_).astype(F32)
    lane = lax.broadcasted_iota(jnp.int32, (R, LANES), 1)
    rhead = jnp.right_shift(lax.broadcasted_iota(jnp.int32, (R, LANES), 0), shift)
    ones8 = jnp.ones((8, LANES), F32)
    rep = lambda a: jnp.concatenate([a] * HS, axis=0)
    pick = lambda a, off: jnp.sum(jnp.where(lane == rhead + off, a, 0.0), axis=-1, keepdims=True)

    def stacked(rows, base, scale, h0):
        parts = []
        for h in range(h0, h0 + HS):
            x = ybuf[rows, base + h * GDN_DK:base + (h + 1) * GDN_DK]
            if scale is not None:
                x = x * lax.rsqrt(jnp.sum(x * x, axis=-1, keepdims=True) + EPS) * scale
            parts.append(x)
        return jnp.concatenate(parts, axis=0)

    n_stack = H // HS
    units = range((tt // C) * n_stack)
    low, rhs, qes = [], [], []
    for ci in range(tt // C):
        rows = slice(ci * C, (ci + 1) * C)
        ab = abbuf[rows, :]
        gfull = -jnp.exp(alog_ref[...]) * _softplus(ab + dtb_ref[...])
        bfull = _sigmoid(ab)
        gcum = _dot_f32(tri, gfull)
        glast = gcum[C - 1:C, :]
        el_scr[ci] = jnp.broadcast_to(jnp.exp(glast), (8, LANES))
        for p in range(n_stack):
            un, h0 = ci * n_stack + p, p * HS
            g_m = jnp.where(lane == rhead + h0, rep(gcum), 0.0)
            g_col = jnp.sum(g_m, axis=-1, keepdims=True)
            g_row = _dot_nt_f32(ones8, g_m)[0:1, :]
            beta = pick(rep(bfull), H + h0)
            gl_col = pick(jnp.broadcast_to(glast, (R, LANES)), h0)
            decay = jnp.exp(jnp.where(causal, g_col - g_row, -jnp.inf))
            q = stacked(rows, 0, GDN_DK ** -0.5, h0)
            k = stacked(rows, GDN_QK_W, 1.0, h0)
            v = stacked(rows, 2 * GDN_QK_W, None, h0)
            kq = _mm(jnp.concatenate([k, q], axis=0), k, _NT)
            low.append(jnp.where(strict, kq[0:R] * decay * beta, 0.0))
            e_g = jnp.exp(g_col)
            rhs.append(jnp.concatenate([beta * v, (beta * e_g) * k], axis=1).astype(BF16))
            qes.append(q * e_g)
            kw_scr[un] = (k * jnp.exp(gl_col - g_col)).astype(BF16)
            qkd_scr[un] = jnp.where(causal, kq[R:2 * R] * decay, 0.0).astype(BF16)
    tinv = [eye - jnp.where(merge_masks[0], low[un], 0.0) for un in units]
    for m in merge_masks[1:]:
        half = [_mm(tinv[un], jnp.where(m, low[un], 0.0)) for un in units]
        tinv = [tinv[un] - _mm(half[un], tinv[un]) for un in units]
    for un in units:
        w2 = _mm(tinv[un], rhs[un])
        u0_scr[un] = w2[:, 0:GDN_DV]
        wk = w2[:, GDN_DV:]
        wq_scr[un] = jnp.concatenate(
            [x[h * C:(h + 1) * C] for h in range(HS) for x in (wk, qes[un])], axis=0).astype(BF16)

    st = [s_scr[h] for h in range(H)]
    for ci in range(tt // C):
        rows = slice(ci * C, (ci + 1) * C)
        el = el_scr[ci]
        for p in range(n_stack):
            un = ci * n_stack + p
            us, os_ = [], []
            for hh in range(HS):
                r = jnp.dot(wq_scr[un, 2 * hh * C:2 * (hh + 1) * C, :], st[p * HS + hh].astype(BF16),
                            preferred_element_type=F32)
                us.append(u0_scr[un, hh * C:(hh + 1) * C, :] - r[0:C])
                os_.append(r[C:2 * C])
            intra = jnp.dot(qkd_scr[un], jnp.concatenate(us, axis=0).astype(BF16),
                            preferred_element_type=F32)
            for hh in range(HS):
                h = p * HS + hh
                st[h] = (st[h] * el[0:1, h:h + 1]
                         + _mm(kw_scr[un, hh * C:(hh + 1) * C, :], us[hh], _TN))
                o = os_[hh] + intra[hh * C:(hh + 1) * C]
                zz = zbuf[rows, h * GDN_DV:(h + 1) * GDN_DV]
                o_ref[0, rows, h * GDN_DV:(h + 1) * GDN_DV] = (_rms(o, wg_ref[...])
                                                              * (zz * _sigmoid(zz)))
    for h in range(H):
        s_scr[h] = st[h]

    @pl.when(t == pl.num_programs(1) - 1)
    def _():
        sfin_ref[0] = s_scr[...]
        ctail_ref[0] = cbuf[0:8, :]


def _gdn_step_kernel(c_ref, z_ref, ab_ref, cs_ref, s_ref, wconv_ref, alog_ref, dtb_ref, wg_ref,
                     o_ref, snew_ref, *, nb):
    H = GDN_HEADS
    acc = c_ref[...] * wconv_ref[CONV_W - 1:CONV_W, :]
    for i in range(CONV_W - 1):
        acc = acc + cs_ref[i] * wconv_ref[i:i + 1, :]
    y = acc * _sigmoid(acc)
    ab = ab_ref[...]
    e_g = jnp.exp(-jnp.exp(alog_ref[...]) * _softplus(ab + dtb_ref[...]))
    bfull = _sigmoid(ab)
    unit = lambda x: x * lax.rsqrt(jnp.sum(x * x, axis=-1, keepdims=True) + EPS)
    qs = [unit(y[:, h * GDN_DK:(h + 1) * GDN_DK]) * (GDN_DK ** -0.5) for h in range(H)]
    ks = [unit(y[:, GDN_QK_W + h * GDN_DK:GDN_QK_W + (h + 1) * GDN_DK]) for h in range(H)]
    vs = [y[:, 2 * GDN_QK_W + h * GDN_DV:2 * GDN_QK_W + (h + 1) * GDN_DV] for h in range(H)]
    cols = jnp.concatenate(ks + qs, axis=0).T
    for h in range(H):
        ks_rows, qs_rows = [], []
        for s in range(nb):
            state = s_ref[s, h]
            kcol = cols[:, h * nb + s:h * nb + s + 1]
            qcol = cols[:, (H + h) * nb + s:(H + h) * nb + s + 1]
            ks_rows.append(jnp.sum(kcol * state, axis=0, keepdims=True))
            qs_rows.append(jnp.sum(qcol * state, axis=0, keepdims=True))
        eg_h = e_g[:, h:h + 1]
        u = bfull[:, H + h:H + h + 1] * (vs[h] - eg_h * jnp.concatenate(ks_rows, axis=0))
        o = (eg_h * jnp.concatenate(qs_rows, axis=0)
             + jnp.sum(qs[h] * ks[h], axis=-1, keepdims=True) * u)
        eg_b = jnp.broadcast_to(eg_h, (nb, GDN_DV))
        for s in range(nb):
            kcol = cols[:, h * nb + s:h * nb + s + 1]
            snew_ref[s, h] = eg_b[s:s + 1, :] * s_ref[s, h] + kcol * u[s:s + 1, :]
        zz = z_ref[:, h * GDN_DV:(h + 1) * GDN_DV]
        o_ref[:, h * GDN_DV:(h + 1) * GDN_DV] = _rms(o, wg_ref[...]) * (zz * _sigmoid(zz))


def _gdn_step(c, z, ab, conv_state, s, wconv, alog, dtb, wg):
    bs = c.shape[0]
    nb = LANES // (2 * GDN_HEADS)
    assert bs % nb == 0
    row = lambda i: (i, 0)
    fixed = lambda i: (0, 0)
    state = pl.BlockSpec((nb, GDN_HEADS, GDN_DK, GDN_DV), lambda i: (i, 0, 0, 0))
    return pl.pallas_call(
        functools.partial(_gdn_step_kernel, nb=nb),
        grid=(bs // nb,),
        in_specs=[pl.BlockSpec((nb, CONV_CH), row),
                  pl.BlockSpec((nb, GDN_V_W), row),
                  pl.BlockSpec((nb, AB_W), row),
                  pl.BlockSpec((CONV_W - 1, nb, CONV_CH), lambda i: (0, i, 0)),
                  state,
                  pl.BlockSpec((CONV_W, CONV_CH), fixed),
                  pl.BlockSpec((1, LANES), fixed),
                  pl.BlockSpec((1, LANES), fixed),
                  pl.BlockSpec((1, GDN_DV), fixed)],
        out_specs=[pl.BlockSpec((nb, GDN_V_W), row), state],
        out_shape=[jax.ShapeDtypeStruct((bs, GDN_V_W), F32),
                   jax.ShapeDtypeStruct(s.shape, F32)],
        compiler_params=pltpu.CompilerParams(dimension_semantics=("parallel",),
                                             vmem_limit_bytes=VMEM_LIMIT),
        name="gdn_step",
    )(c, z, ab, conv_state, s, wconv, alog, dtb, wg)


def _gdn(x, wn, wcz, tail, s0, wconv, alog, dtb, wg, tt):
    b, t, _ = x.shape
    assert t % tt == 0 and tt % GDN_CHUNK == 0
    rs = GDN_STACK * GDN_CHUNK
    nc, nu = tt // GDN_CHUNK, (tt // GDN_CHUNK) * (GDN_HEADS // GDN_STACK)
    seq = lambda i, j: (i, j, 0)
    fixed2 = lambda i, j: (0, 0)
    tail_map = (lambda i, j: (i, 0, 0)) if tail.shape[0] == b else (lambda i, j: (0, 0, 0))
    s0_map = (lambda i, j: (i, 0, 0, 0)) if s0.shape[0] == b else (lambda i, j: (0, 0, 0, 0))
    return pl.pallas_call(
        functools.partial(_gdn_kernel, tt=tt),
        grid=(b, t // tt),
        in_specs=[pl.BlockSpec((1, tt, D_MODEL), seq),
                  pl.BlockSpec((1, D_MODEL), fixed2),
                  pl.BlockSpec((D_MODEL, CONV_CH + GDN_V_W + AB_W), fixed2),
                  pl.BlockSpec((1, 8, CONV_CH), tail_map),
                  pl.BlockSpec((1, GDN_HEADS, GDN_DK, GDN_DV), s0_map),
                  pl.BlockSpec((CONV_W, CONV_CH), fixed2),
                  pl.BlockSpec((1, LANES), fixed2),
                  pl.BlockSpec((1, LANES), fixed2),
                  pl.BlockSpec((1, GDN_DV), fixed2)],
        out_specs=[pl.BlockSpec((1, tt, GDN_V_W), seq),
                   pl.BlockSpec((1, GDN_HEADS, GDN_DK, GDN_DV), lambda i, j: (i, 0, 0, 0)),
                   pl.BlockSpec((1, 8, CONV_CH), lambda i, j: (i, 0, 0))],
        out_shape=[jax.ShapeDtypeStruct((b, t, GDN_V_W), F32),
                   jax.ShapeDtypeStruct((b, GDN_HEADS, GDN_DK, GDN_DV), F32),
                   jax.ShapeDtypeStruct((b, 8, CONV_CH), F32)],
        scratch_shapes=[pltpu.VMEM((tt + 8, CONV_CH), F32),
                        pltpu.VMEM((tt, CONV_CH), F32),
                        pltpu.VMEM((tt, GDN_V_W), F32),
                        pltpu.VMEM((tt, AB_W), F32),
                        pltpu.VMEM((GDN_HEADS, GDN_DK, GDN_DV), F32),
                        pltpu.VMEM((nu, rs, GDN_DV), F32),
                        pltpu.VMEM((nu, 2 * rs, GDN_DK), BF16),
                        pltpu.VMEM((nu, rs, GDN_DK), BF16),
                        pltpu.VMEM((nu, rs, rs), BF16),
                        pltpu.VMEM((nc, 8, LANES), F32)],
        compiler_params=pltpu.CompilerParams(dimension_semantics=("parallel", "arbitrary"),
                                             vmem_limit_bytes=VMEM_LIMIT),
        name="gdn",
    )(x, wn, wcz, tail, s0, wconv, alog, dtb, wg)


def _kv_out_kernel(k_ref, v_ref, km_ref, vm_ref, ko_ref, vo_ref):
    t = k_ref.shape[1]
    for src, meta, dst in ((k_ref, km_ref, ko_ref), (v_ref, vm_ref, vo_ref)):
        for h in range(DA_HEADS):
            cols = slice(h * DA_DV, (h + 1) * DA_DV)
            dst[0, pl.ds(h, N_META, stride=DA_HEADS), :] = meta[:, cols]
            dst[0, pl.ds(N_META * DA_HEADS + h, t, stride=DA_HEADS), :] = src[0, :, cols]


def _kv_out(k, v, k_meta, v_meta):
    b, t, w = k.shape
    rows = (t + N_META) * DA_HEADS
    seq = pl.BlockSpec((1, t, w), lambda i: (i, 0, 0))
    meta = pl.BlockSpec((N_META, w), lambda i: (0, 0))
    out = pl.BlockSpec((1, rows, DA_DV), lambda i: (i, 0, 0))
    return pl.pallas_call(
        _kv_out_kernel,
        grid=(b,),
        in_specs=[seq, seq, meta, meta],
        out_specs=[out, out],
        out_shape=[jax.ShapeDtypeStruct((b, rows, DA_DV), F32)] * 2,
        compiler_params=pltpu.CompilerParams(dimension_semantics=("parallel",),
                                             vmem_limit_bytes=VMEM_LIMIT),
        name="kv_out",
    )(k, v, k_meta, v_meta)


def _lambda(lam_ref):
    lv = lam_ref[...]
    s1 = jnp.sum(lv[0:1, :] * lv[1:2, :], axis=-1, keepdims=True)
    s2 = jnp.sum(lv[2:3, :] * lv[3:4, :], axis=-1, keepdims=True)
    return jnp.exp(s1) - jnp.exp(s2) + LAM_INIT


def _split_components(q):
    lane = lax.broadcasted_iota(jnp.int32, q.shape, 1)
    first = (lane % DA_DV) < DA_DQK
    return jnp.where(first, q, 0.0).astype(BF16), jnp.where(first, 0.0, q).astype(BF16)


def _online_update(state, s, vb):
    m, l, acc = state
    m_new = jnp.maximum(m, jnp.max(s, axis=-1, keepdims=True))
    p = jnp.exp2(s - m_new)
    alpha = jnp.exp2(m - m_new)
    return (m_new, alpha * l + jnp.sum(p, axis=-1, keepdims=True),
            alpha * acc + jnp.dot(p.astype(BF16), vb, preferred_element_type=F32))


def _attn_prompt_kernel(q_ref, k_ref, v_ref, km_ref, vm_ref, lam_ref, wsub_ref, o_ref, *, bq, bk):
    qi = pl.program_id(2)
    heads = [slice(h * DA_DV, (h + 1) * DA_DV) for h in range(q_ref.shape[2] // DA_DV)]
    qs = [_split_components(q_ref[0, :, hs] * (DA_DQK ** -0.5 * math.log2(math.e))) for hs in heads]
    neg = jnp.full((bq, 1), -jnp.inf, F32)
    zero = jnp.zeros((bq, 1), F32)
    zacc = jnp.zeros((bq, DA_DV), F32)
    init = tuple(((neg, zero, zacc), (neg, zero, zacc)) for _ in heads)

    def block(kblk, vblk, mask, st):
        out = []
        for (q1, q2), hs, sth in zip(qs, heads, st):
            kb, vb = kblk[:, hs].astype(BF16), vblk[:, hs].astype(BF16)
            s1, s2 = _dot_nt(q1, kb), _dot_nt(q2, kb)
            if mask is not None:
                s1 = jnp.where(mask, s1, -jnp.inf)
                s2 = jnp.where(mask, s2, -jnp.inf)
            out.append((_online_update(sth[0], s1, vb), _online_update(sth[1], s2, vb)))
        return tuple(out)

    n_full = lax.shift_right_logical(qi * bq, bk.bit_length() - 1)

    def body(kb, st):
        r0 = pl.multiple_of(kb * bk, bk)
        return block(k_ref[0, pl.ds(r0, bk), :], v_ref[0, pl.ds(r0, bk), :], None, st)

    st = lax.fori_loop(0, n_full, body, init)
    r0 = pl.multiple_of(n_full * bk, bk)
    kcat = jnp.concatenate([km_ref[...], k_ref[0, pl.ds(r0, bk), :]], axis=0)
    vcat = jnp.concatenate([vm_ref[...], v_ref[0, pl.ds(r0, bk), :]], axis=0)
    ri = lax.broadcasted_iota(jnp.int32, (bq, LANES + bk), 0)
    ci = lax.broadcasted_iota(jnp.int32, (bq, LANES + bk), 1)
    seen = (ci < N_META) | ((ci >= LANES) & (ci - LANES <= ri + (qi * bq - n_full * bk)))
    st = block(kcat, vcat, seen, st)
    lam = _lambda(lam_ref)
    for hs, ((m1, l1, a1), (m2, l2, a2)) in zip(heads, st):
        o = a1 / l1 - lam * (a2 / l2)
        o_ref[0, :, hs] = _rms(o, wsub_ref[...]) * (1.0 - LAM_INIT)


def _attn_prompt(q, k, v, k_meta, v_meta, lam_vecs, wsub, bq, bk, hps):
    b, t, _ = q.shape
    assert t % bk == 0 and bk % bq == 0 and k_meta.shape[0] == LANES and DA_HEADS % hps == 0
    w = hps * DA_DV
    return pl.pallas_call(
        functools.partial(_attn_prompt_kernel, bq=bq, bk=bk),
        grid=(b, DA_HEADS // hps, t // bq),
        in_specs=[pl.BlockSpec((1, bq, w), lambda i, h, j: (i, j, h)),
                  pl.BlockSpec((1, t, w), lambda i, h, j: (i, 0, h)),
                  pl.BlockSpec((1, t, w), lambda i, h, j: (i, 0, h)),
                  pl.BlockSpec((LANES, w), lambda i, h, j: (0, h)),
                  pl.BlockSpec((LANES, w), lambda i, h, j: (0, h)),
                  pl.BlockSpec((4, DA_DQK), lambda i, h, j: (0, 0)),
                  pl.BlockSpec((1, DA_DV), lambda i, h, j: (0, 0))],
        out_specs=pl.BlockSpec((1, bq, w), lambda i, h, j: (i, j, h)),
        out_shape=jax.ShapeDtypeStruct((b, t, DA_V_W), F32),
        compiler_params=pltpu.CompilerParams(
            dimension_semantics=("parallel", "parallel", "arbitrary"),
            vmem_limit_bytes=VMEM_LIMIT),
        name="attn_prompt",
    )(q, k, v, k_meta, v_meta, lam_vecs, wsub)


def _attn_sample_kernel(pt_ref, q_ref, kn_ref, vn_ref, lam_ref, wsub_ref, *rest, pps):
    k_refs, v_refs = rest[:pps], rest[pps:2 * pps]
    o_ref, m_scr, l_scr, acc_scr = rest[2 * pps:]
    step = pl.program_id(1)
    nrow = 2 * DA_HEADS
    twice = lambda x: jnp.concatenate([x, x], axis=0)
    rowq = lax.broadcasted_iota(jnp.int32, (nrow, DA_DV), 0)
    laneq = lax.broadcasted_iota(jnp.int32, (nrow, DA_DV), 1)
    own_comp = (laneq // DA_DQK) == (rowq // DA_HEADS)
    qm = jnp.where(own_comp, twice(q_ref[0]) * (DA_DQK ** -0.5), 0.0)
    page_rows = k_refs[0].shape[1]
    rowp = lax.broadcasted_iota(jnp.int32, (nrow, page_rows * pps), 0)
    colp = lax.broadcasted_iota(jnp.int32, (nrow, page_rows * pps), 1)
    own_head = (colp % DA_HEADS) == (rowp % DA_HEADS)

    def rows3(x):
        hi = x.astype(BF16).astype(F32)
        mid = (x - hi).astype(BF16).astype(F32)
        return jnp.concatenate([hi, mid, x - hi - mid], axis=0).astype(BF16)

    def unrows3(r):
        return (r[0:nrow] + r[nrow:2 * nrow]) + r[2 * nrow:3 * nrow]

    def hi_lo(x):
        hi = x.astype(BF16)
        return hi, (x - hi.astype(F32)).astype(BF16)

    @pl.when(step == 0)
    def _():
        s_new = jnp.sum(qm * twice(kn_ref[0]), axis=-1, keepdims=True)
        m_scr[...] = s_new
        l_scr[...] = jnp.ones_like(s_new)
        acc_scr[...] = twice(vn_ref[0])

    q3 = rows3(qm)
    scores = []
    for r in k_refs:
        k_hi, k_lo = hi_lo(r[0])
        scores.append(unrows3(_dot_nt(q3, k_hi) + _dot_nt(q3, k_lo)))
    s = jnp.where(own_head, jnp.concatenate(scores, axis=1), -jnp.inf)
    m_old = m_scr[...]
    m_new = jnp.maximum(m_old, jnp.max(s, axis=-1, keepdims=True))
    p = jnp.exp(s - m_new)
    alpha = jnp.exp(m_old - m_new)
    p3 = rows3(p)
    pv = jnp.zeros((3 * nrow, DA_DV), F32)
    for j, r in enumerate(v_refs):
        v_hi, v_lo = hi_lo(r[0])
        pj = p3[:, j * page_rows:(j + 1) * page_rows]
        pv = pv + (jnp.dot(pj, v_hi, preferred_element_type=F32)
                   + jnp.dot(pj, v_lo, preferred_element_type=F32))
    m_scr[...] = m_new
    l_scr[...] = alpha * l_scr[...] + jnp.sum(p, axis=-1, keepdims=True)
    acc_scr[...] = alpha * acc_scr[...] + unrows3(pv)

    @pl.when(step == pl.num_programs(1) - 1)
    def _():
        w = acc_scr[...] / l_scr[...]
        o = w[0:DA_HEADS, :] - _lambda(lam_ref) * w[DA_HEADS:nrow, :]
        o_ref[0] = _rms(o, wsub_ref[...]) * (1.0 - LAM_INIT)


def _attn_sample(q, k_new, v_new, cache_k, cache_v, page_table, lam_vecs, wsub, pps):
    bs = q.shape[0]
    n_pages = page_table.shape[1]
    page_rows = cache_k.shape[1]
    assert n_pages % pps == 0
    tok = lambda i, s, pt: (i, 0, 0)
    fixed = lambda i, s, pt: (0, 0)

    def page_spec(j):
        return pl.BlockSpec((1, page_rows, DA_DV), lambda i, s, pt: (pt[i, s * pps + j], 0, 0))

    grid_spec = pltpu.PrefetchScalarGridSpec(
        num_scalar_prefetch=1,
        grid=(bs, n_pages // pps),
        in_specs=[pl.BlockSpec((1, DA_HEADS, DA_DV), tok),
                  pl.BlockSpec((1, DA_HEADS, DA_DV), tok),
                  pl.BlockSpec((1, DA_HEADS, DA_DV), tok),
                  pl.BlockSpec((4, DA_DQK), fixed),
                  pl.BlockSpec((1, DA_DV), fixed)]
                 + [page_spec(j) for j in range(pps)] * 2,
        out_specs=pl.BlockSpec((1, DA_HEADS, DA_DV), tok),
        scratch_shapes=[pltpu.VMEM((2 * DA_HEADS, 1), F32),
                        pltpu.VMEM((2 * DA_HEADS, 1), F32),
                        pltpu.VMEM((2 * DA_HEADS, DA_DV), F32)])
    return pl.pallas_call(
        functools.partial(_attn_sample_kernel, pps=pps),
        grid_spec=grid_spec,
        out_shape=jax.ShapeDtypeStruct((bs, DA_HEADS, DA_DV), F32),
        compiler_params=pltpu.CompilerParams(dimension_semantics=("parallel", "arbitrary"),
                                             vmem_limit_bytes=VMEM_LIMIT),
        name="attn_sample",
    )(page_table, q, k_new, v_new, lam_vecs, wsub, *([cache_k] * pps), *([cache_v] * pps))


def _route(logits):
    lane = lax.broadcasted_iota(jnp.int32, logits.shape, 1).astype(F32)
    big = float(LANES)
    ninf = -jnp.inf

    def top(mask):
        val = jnp.where(mask, logits, ninf)
        mx = jnp.max(val, axis=-1, keepdims=True)
        idx = jnp.min(jnp.where(mask & (val == mx), lane, big), axis=-1, keepdims=True)
        return mx, idx

    gmask = (lane >= N_EXPERTS) & (lane < N_EXPERTS + N_GROUPS)
    gmax, gidx = top(gmask)
    g_w = 1.0 / jnp.sum(jnp.where(gmask, jnp.exp(logits - gmax), 0.0), axis=-1, keepdims=True)
    g_sel = gidx - N_EXPERTS
    emask = (lane >= g_sel * EXPERTS_PER_GROUP) & (lane < (g_sel + 1) * EXPERTS_PER_GROUP)
    v1, i1 = top(emask)
    v2, i2 = top(emask & (lane != i1))
    e21 = jnp.exp(v2 - v1)
    p1 = 1.0 / (1.0 + e21)
    return jnp.where(lane == i1, p1 * g_w, 0.0) + jnp.where(lane == i2, e21 * p1 * g_w, 0.0), g_sel


def _mix_route(g_ref, a_ref, x_ref, wo_ref, wnf_ref, wr_ref, precise):
    mix = (_mm(g_ref[...], wo_ref[0:GDN_V_W, :], _NN, precise)
           + _mm(a_ref[...], wo_ref[GDN_V_W:, :], _NN, precise))
    x1 = x_ref[...] + mix
    xn = _rms(x1, wnf_ref[...])
    wr = wr_ref[...]
    if precise:
        logits = _dot_f32(xn, wr)
    else:
        x_hi, w_hi = xn.astype(BF16), wr.astype(BF16)
        x_lo, w_lo = (xn - x_hi.astype(F32)).astype(BF16), (wr - w_hi.astype(F32)).astype(BF16)
        logits = (jnp.dot(x_hi, w_hi, preferred_element_type=F32)
                  + (jnp.dot(x_hi, w_lo, preferred_element_type=F32)
                     + jnp.dot(x_lo, w_hi, preferred_element_type=F32)))
    comb, g_sel = _route(logits)
    return x1, xn, comb, g_sel


def _expert(xn, comb, e_lane, wg, wu, wd):
    hg = jnp.dot(xn, wg.astype(BF16), preferred_element_type=F32)
    hu = jnp.dot(xn, wu.astype(BF16), preferred_element_type=F32)
    lane = lax.broadcasted_iota(jnp.int32, comb.shape, 1)
    cw = jnp.sum(jnp.where(lane == e_lane, comb, 0.0), axis=-1, keepdims=True)
    hh = (hg * _sigmoid(hg)) * hu * cw
    return jnp.dot(hh.astype(BF16), wd.astype(BF16), preferred_element_type=F32)


def _ffn_kernel(g_ref, a_ref, x_ref, wo_ref, wnf_ref, wr_ref, wg_ref, wu_ref, wd_ref, wfin_ref,
                y_ref, x1_scr, xn_scr, comb_scr, acc_scr, *, precise):
    e = pl.program_id(1)

    @pl.when(e == 0)
    def _():
        x1, xn, comb, _ = _mix_route(g_ref, a_ref, x_ref, wo_ref, wnf_ref, wr_ref, precise)
        x1_scr[...] = x1
        xn_scr[...] = xn.astype(BF16)
        comb_scr[...] = comb
        acc_scr[...] = jnp.zeros_like(acc_scr)

    acc_scr[...] += _expert(xn_scr[...], comb_scr[...], e, wg_ref[0], wu_ref[0], wd_ref[0])

    @pl.when(e == pl.num_programs(1) - 1)
    def _():
        y_ref[...] = _rms(x1_scr[...] + acc_scr[...], wfin_ref[...])


def _ffn(g_out, a_out, x, wo, wnf, wr, wg, wu, wd, wfin, tm):
    n = x.shape[0]
    assert n % tm == 0
    row = lambda i, e: (i, 0)
    fixed = lambda i, e: (0, 0)
    exp = lambda i, e: (e, 0, 0)
    return pl.pallas_call(
        functools.partial(_ffn_kernel, precise=wo.dtype == F32),
        grid=(n // tm, N_EXPERTS),
        in_specs=[pl.BlockSpec((tm, GDN_V_W), row),
                  pl.BlockSpec((tm, DA_V_W), row),
                  pl.BlockSpec((tm, D_MODEL), row),
                  pl.BlockSpec((GDN_V_W + DA_V_W, D_MODEL), fixed),
                  pl.BlockSpec((1, D_MODEL), fixed),
                  pl.BlockSpec((D_MODEL, LANES), fixed),
                  pl.BlockSpec((1, D_MODEL, D_EXPERT), exp),
                  pl.BlockSpec((1, D_MODEL, D_EXPERT), exp),
                  pl.BlockSpec((1, D_EXPERT, D_MODEL), exp),
                  pl.BlockSpec((1, D_MODEL), fixed)],
        out_specs=pl.BlockSpec((tm, D_MODEL), row),
        out_shape=jax.ShapeDtypeStruct((n, D_MODEL), F32),
        scratch_shapes=[pltpu.VMEM((tm, D_MODEL), F32),
                        pltpu.VMEM((tm, D_MODEL), BF16),
                        pltpu.VMEM((tm, LANES), F32),
                        pltpu.VMEM((tm, D_MODEL), F32)],
        compiler_params=pltpu.CompilerParams(dimension_semantics=("parallel", "arbitrary"),
                                             vmem_limit_bytes=VMEM_LIMIT),
        name="ffn",
    )(g_out, a_out, x, wo, wnf, wr, wg, wu, wd, wfin)


ROW_W = D_MODEL + LANES
MOE_TILE = 512
ROW_TILE = 1024


def _ffn_pre_kernel(g_ref, a_ref, x_ref, wo_ref, wnf_ref, wr_ref, rows_ref, info_ref, cnt_ref,
                    cnt_scr):
    @pl.when(pl.program_id(0) == 0)
    def _():
        cnt_scr[...] = jnp.zeros_like(cnt_scr)

    x1, _, comb, g_sel = _mix_route(g_ref, a_ref, x_ref, wo_ref, wnf_ref, wr_ref, False)
    tm = x1.shape[0]
    lane = lax.broadcasted_iota(jnp.int32, (tm, LANES), 1).astype(F32)
    onehot = lane == g_sel
    ri = lax.broadcasted_iota(jnp.int32, (tm, tm), 0)
    ci = lax.broadcasted_iota(jnp.int32, (tm, tm), 1)
    earlier = jnp.dot((ci < ri).astype(BF16), onehot.astype(BF16), preferred_element_type=F32)
    rank = jnp.sum(jnp.where(onehot, earlier + cnt_scr[...], 0.0), axis=-1, keepdims=True)
    cnt_scr[...] += jnp.sum(onehot.astype(F32), axis=0, keepdims=True)
    cnt_ref[...] = cnt_scr[...]
    rows_ref[:, 0:D_MODEL] = x1
    rows_ref[:, D_MODEL:] = comb
    packed = jnp.where(lane == 0.0, g_sel, jnp.where(lane == 1.0, rank, 0.0))
    r8 = lax.broadcasted_iota(jnp.int32, (8, LANES), 0)
    l8 = lax.broadcasted_iota(jnp.int32, (8, LANES), 1)
    sel = (r8 == l8).astype(BF16)
    hi = packed.astype(BF16)
    info_ref[...] = _dot_nt(sel, hi) + _dot_nt(sel, packed - hi.astype(F32))


def _ffn_pre(g_out, a_out, x, wo, wnf, wr, tm):
    n = x.shape[0]
    assert n % tm == 0
    row = lambda i: (i, 0)
    fixed = lambda i: (0, 0)
    return pl.pallas_call(
        _ffn_pre_kernel,
        grid=(n // tm,),
        in_specs=[pl.BlockSpec((tm, GDN_V_W), row),
                  pl.BlockSpec((tm, DA_V_W), row),
                  pl.BlockSpec((tm, D_MODEL), row),
                  pl.BlockSpec((GDN_V_W + DA_V_W, D_MODEL), fixed),
                  pl.BlockSpec((1, D_MODEL), fixed),
                  pl.BlockSpec((D_MODEL, LANES), fixed)],
        out_specs=[pl.BlockSpec((tm, ROW_W), row),
                   pl.BlockSpec((8, tm), lambda i: (0, i)),
                   pl.BlockSpec((1, LANES), fixed)],
        out_shape=[jax.ShapeDtypeStruct((n, ROW_W), F32),
                   jax.ShapeDtypeStruct((8, n), F32),
                   jax.ShapeDtypeStruct((1, LANES), F32)],
        scratch_shapes=[pltpu.VMEM((1, LANES), F32)],
        compiler_params=pltpu.CompilerParams(dimension_semantics=("arbitrary",),
                                             vmem_limit_bytes=VMEM_LIMIT),
        name="ffn_pre",
    )(g_out, a_out, x, wo, wnf, wr)


def _copy_rows(idx_ref, tile_ref, hbm_ref, sem, scatter):
    tm = tile_ref.shape[0]
    base = pl.program_id(0) * tm

    def body(r, carry):
        there, here = hbm_ref.at[pl.ds(idx_ref[base + r], 1)], tile_ref.at[pl.ds(r, 1)]
        src, dst = (here, there) if scatter else (there, here)
        pltpu.make_async_copy(src, dst, sem).start()
        return carry

    lax.fori_loop(0, tm, body, 0, unroll=8)
    there, here = hbm_ref.at[pl.ds(0, tm)], tile_ref
    src, dst = (here, there) if scatter else (there, here)
    pltpu.make_async_copy(src, dst, sem).wait()


def _scatter_kernel(idx_ref, tile_ref, init_ref, dst_ref, sem):
    del init_ref
    _copy_rows(idx_ref, tile_ref, dst_ref, sem, True)


def _gather_kernel(idx_ref, src_ref, tile_ref, sem):
    _copy_rows(idx_ref, tile_ref, src_ref, sem, False)


def _move_rows(idx, src, n_out, scatter):
    n, w = idx.shape[0], src.shape[1]
    assert n % ROW_TILE == 0
    any_spec = pl.BlockSpec(memory_space=pl.ANY)
    tile_spec = pl.BlockSpec((ROW_TILE, w), lambda i, idx: (i, 0))
    if scatter:
        operands = (idx, src, jnp.zeros((n_out, w), src.dtype))
        in_specs, out_spec = [tile_spec, any_spec], any_spec
    else:
        operands = (idx, src)
        in_specs, out_spec = [any_spec], tile_spec
    return pl.pallas_call(
        _scatter_kernel if scatter else _gather_kernel,
        grid_spec=pltpu.PrefetchScalarGridSpec(
            num_scalar_prefetch=1, grid=(n // ROW_TILE,),
            in_specs=in_specs, out_specs=out_spec,
            scratch_shapes=[pltpu.SemaphoreType.DMA(())]),
        out_shape=jax.ShapeDtypeStruct((n_out, w), src.dtype),
        input_output_aliases={2: 0} if scatter else {},
        compiler_params=pltpu.CompilerParams(dimension_semantics=("arbitrary",)),
        name="scatter_rows" if scatter else "gather_rows",
    )(*operands)


def _moe_groups_kernel(tg_ref, nt_ref, rows_ref, wnf_ref, wg_ref, wu_ref, wd_ref, wfin_ref, y_ref,
                       wg_bf, wu_bf, wd_bf):
    j = pl.program_id(0)

    @pl.when((j == 0) | (tg_ref[j] != tg_ref[jnp.maximum(j - 1, 0)]))
    def _():
        for e in range(EXPERTS_PER_GROUP):
            wg_bf[e] = wg_ref[e].astype(BF16)
            wu_bf[e] = wu_ref[e].astype(BF16)
            wd_bf[e] = wd_ref[e].astype(BF16)

    @pl.when(j < nt_ref[0])
    def _():
        x1 = rows_ref[:, 0:D_MODEL]
        comb = rows_ref[:, D_MODEL:]
        xn = _rms(x1, wnf_ref[...]).astype(BF16)
        first = tg_ref[j] * EXPERTS_PER_GROUP
        acc = jnp.zeros_like(x1)
        for e in range(EXPERTS_PER_GROUP):
            acc = acc + _expert(xn, comb, first + e, wg_bf[e], wu_bf[e], wd_bf[e])
        y_ref[...] = _rms(x1 + acc, wfin_ref[...])

    @pl.when(j >= nt_ref[0])
    def _():
        y_ref[...] = jnp.zeros_like(y_ref)


def _moe_groups(tile_group, n_tiles_used, rows, wnf, wg, wu, wd, wfin):
    n_tiles = tile_group.shape[0]
    fixed = lambda j, tg, nt: (0, 0)
    grp = lambda j, tg, nt: (tg[j], 0, 0)
    in_shape, out_shape = (EXPERTS_PER_GROUP, D_MODEL, D_EXPERT), (EXPERTS_PER_GROUP, D_EXPERT, D_MODEL)
    once = pl.Buffered(1)
    return pl.pallas_call(
        _moe_groups_kernel,
        grid_spec=pltpu.PrefetchScalarGridSpec(
            num_scalar_prefetch=2, grid=(n_tiles,),
            in_specs=[pl.BlockSpec((MOE_TILE, ROW_W), lambda j, tg, nt: (j, 0)),
                      pl.BlockSpec((1, D_MODEL), fixed),
                      pl.BlockSpec(in_shape, grp, pipeline_mode=once),
                      pl.BlockSpec(in_shape, grp, pipeline_mode=once),
                      pl.BlockSpec(out_shape, grp, pipeline_mode=once),
                      pl.BlockSpec((1, D_MODEL), fixed)],
            out_specs=pl.BlockSpec((MOE_TILE, D_MODEL), lambda j, tg, nt: (j, 0)),
            scratch_shapes=[pltpu.VMEM(in_shape, BF16), pltpu.VMEM(in_shape, BF16),
                            pltpu.VMEM(out_shape, BF16)]),
        out_shape=jax.ShapeDtypeStruct((n_tiles * MOE_TILE, D_MODEL), F32),
        compiler_params=pltpu.CompilerParams(dimension_semantics=("arbitrary",),
                                             vmem_limit_bytes=VMEM_LIMIT),
        name="moe_groups",
    )(tile_group, n_tiles_used, rows, wnf, wg, wu, wd, wfin)


def _ffn_sorted(g_out, a_out, x, wo, wnf, wr, wg, wu, wd, wfin):
    n = x.shape[0]
    rows, info, cnt = _ffn_pre(g_out, a_out, x, wo, wnf, wr, 512)
    grp, rank = info[0].astype(jnp.int32), info[1].astype(jnp.int32)
    tiles = (cnt[0, :N_GROUPS].astype(jnp.int32) + MOE_TILE - 1) // MOE_TILE
    ends = jnp.cumsum(tiles)
    dest = (ends - tiles)[grp] * MOE_TILE + rank
    n_tiles = n // MOE_TILE + N_GROUPS
    tile_group = jnp.minimum(
        jnp.sum(jnp.arange(n_tiles, dtype=jnp.int32)[:, None] >= ends[None, :], axis=1),
        N_GROUPS - 1).astype(jnp.int32)
    rows_sorted = _move_rows(dest, rows, n_tiles * MOE_TILE, True)
    y_sorted = _moe_groups(tile_group, ends[N_GROUPS - 1:], rows_sorted, wnf, wg, wu, wd, wfin)
    return _move_rows(dest, y_sorted, n, False)


def kernel(x_prompt, x_sample, cache_k, cache_v, state_conv, state_gdn, page_table, meta_tokens,
           w_norm_mix, w_in, w_conv, a_log, dt_bias, w_gnorm, lambda_q1, lambda_k1, lambda_q2,
           lambda_k2, w_subln, w_out, w_norm_ffn, w_group, w_expert, w_gate, w_up, w_down, w_final):
    l = 0
    bp, seq = x_prompt.shape[0], x_prompt.shape[1]
    bs = x_sample.shape[0]
    n_pool, page = cache_k.shape[1], cache_k.shape[2]
    past = page_table.shape[1] * page
    C = GDN_CHUNK

    w_in_p = _prep_w_in(w_in[l])
    wn_mix = w_norm_mix[l][None, :]
    pad_lanes = lambda v: jnp.pad(v, (0, LANES - v.shape[0]))[None, :]
    alog, dtb = pad_lanes(a_log[l]), pad_lanes(dt_bias[l])
    wgn = w_gnorm[l][None, :]
    lam_vecs = jnp.stack([lambda_q1[l], lambda_k1[l], lambda_q2[l], lambda_k2[l]])
    wsub = w_subln[l][None, :]
    w_router = jnp.pad(jnp.concatenate([w_expert[l], w_group[l]], axis=1),
                       ((0, 0), (0, LANES - N_EXPERTS - N_GROUPS)))
    ffn_w = (w_norm_ffn[l][None, :], w_router, w_gate[l], w_up[l], w_down[l], w_final[None, :])

    pos_small = jnp.concatenate([jnp.arange(N_META), jnp.full((bs,), past)])
    rows_small = jnp.concatenate([meta_tokens, x_sample[:, 0, :]], axis=0)
    c_s, z_s, ab_s, q_s, k_s, v_s = _proj(rows_small, wn_mix, w_in_p, _rope_tables(pos_small),
                                           N_META + bs)
    xp = x_prompt.reshape(bp * seq, D_MODEL)
    n_gdn = CONV_CH + GDN_V_W
    w_qkv = w_in_p[:, n_gdn:n_gdn + 2 * DA_QK_W + DA_V_W].astype(BF16)
    w_cz = jnp.concatenate([w_in_p[:, :n_gdn], w_in_p[:, PROJ_PAD_W - AB_W:]], axis=1).astype(BF16)
    q_p, k_p, v_p = _proj_qkv(xp, wn_mix, w_qkv, _rope_tables(N_META + jnp.arange(seq)), 512)

    x_meta = jnp.pad(meta_tokens, ((C - N_META, 0), (0, 0)))[None]
    zero_tail = jnp.zeros((1, 8, CONV_CH), F32)
    zero_state = jnp.zeros((1, GDN_HEADS, GDN_DK, GDN_DV), F32)
    _, s_meta, tail_meta = _gdn(x_meta, wn_mix, w_cz, zero_tail, zero_state,
                                w_conv[l], alog, dtb, wgn, C)
    g_out, p_gdn, tail_p = _gdn(x_prompt, wn_mix, w_cz, tail_meta, s_meta,
                                w_conv[l], alog, dtb, wgn, 256)
    p_conv = tail_p[:, 8 - (CONV_W - 1):, :]

    p_k, p_v = _kv_out(k_p.reshape(bp, seq, DA_QK_W), v_p.reshape(bp, seq, DA_V_W),
                       k_s[:N_META], v_s[:N_META])
    meta_blk = lambda m: jnp.pad(m[:N_META], ((0, LANES - N_META), (0, 0)))
    a_out = _attn_prompt(q_p.reshape(bp, seq, DA_QK_W), k_p.reshape(bp, seq, DA_QK_W),
                         v_p.reshape(bp, seq, DA_V_W), meta_blk(k_s), meta_blk(v_s),
                         lam_vecs, wsub, 256, 512, 2)

    y_prompt = _ffn_sorted(g_out.reshape(bp * seq, GDN_V_W), a_out.reshape(bp * seq, DA_V_W), xp,
                           w_out[l].astype(BF16), *ffn_w).reshape(bp, seq, D_MODEL)

    sm = lambda a: a[N_META:]
    g_s, s_gdn = _gdn_step(sm(c_s), sm(z_s), sm(ab_s), jnp.swapaxes(state_conv[l], 0, 1),
                           state_gdn[l], w_conv[l], alog, dtb, wgn)
    s_conv = jnp.concatenate([state_conv[l][:, 1:, :], sm(c_s)[:, None, :]], axis=1)
    heads = lambda a: sm(a).reshape(bs, DA_HEADS, DA_DV)
    a_s = _attn_sample(heads(q_s), heads(k_s), heads(v_s),
                       cache_k[l].reshape(n_pool, page * DA_HEADS, 2 * DA_DQK),
                       cache_v[l].reshape(n_pool, page * DA_HEADS, DA_DV),
                       page_table, lam_vecs, wsub, 32)
    y_sample = _ffn(g_s, a_s.reshape(bs, DA_V_W), x_sample[:, 0, :], w_out[l], *ffn_w,
                    bs).reshape(bs, 1, D_MODEL)

    return (y_prompt, y_sample, p_conv[None], p_gdn[None],
            p_k.reshape(1, bp, seq + N_META, DA_HEADS, 2 * DA_DQK),
            p_v.reshape(1, bp, seq + N_META, DA_HEADS, DA_DV),
            s_conv[None], s_gdn[None],
            sm(k_s).reshape(1, bs, 1, DA_HEADS, 2 * DA_DQK),
            sm(v_s).reshape(1, bs, 1, DA_HEADS, DA_DV))
```

```python
import functools
import math

import jax
import jax.numpy as jnp
import numpy as np
from jax import lax
from jax.experimental import pallas as pl
from jax.experimental.pallas import tpu as pltpu

F32 = jnp.float32
BF16 = jnp.bfloat16

D_MODEL = 1024
N_META = 16
EPS = 1e-6
GDN_HEADS = 4
GDN_DK = 128
GDN_DV = 128
CONV_W = 4
GDN_QK_W = GDN_HEADS * GDN_DK
GDN_V_W = GDN_HEADS * GDN_DV
CONV_CH = 2 * GDN_QK_W + GDN_V_W
DA_HEADS = 4
DA_DQK = 64
DA_DV = 2 * DA_DQK
DA_QK_W = DA_HEADS * 2 * DA_DQK
DA_V_W = DA_HEADS * DA_DV
ROPE_DIM = DA_DQK // 4
ROPE_HALF = ROPE_DIM // 2
ROPE_THETA = 500000.0
N_GROUPS = 4
EXPERTS_PER_GROUP = 8
N_EXPERTS = N_GROUPS * EXPERTS_PER_GROUP
D_EXPERT = 256
LAM_INIT = 0.8 - 0.6 * math.exp(-0.3 * 0)

LANES = 128
GDN_CHUNK = 64
GDN_STACK = 2
AB_W = LANES
PROJ_PAD_W = CONV_CH + GDN_V_W + 2 * DA_QK_W + DA_V_W + AB_W
VMEM_LIMIT = 56 * 1024 * 1024


_NN = (((1,), (0,)), ((), ()))
_NT = (((1,), (1,)), ((), ()))
_TN = (((0,), (0,)), ((), ()))


def _mm(a, b, dims=_NN, precise=False):
    if precise:
        return lax.dot_general(a.astype(F32), b.astype(F32), dims, preferred_element_type=F32,
                               precision=lax.Precision.HIGHEST)
    return lax.dot_general(a.astype(BF16), b.astype(BF16), dims, preferred_element_type=F32)


def _dot_nt(a, b):
    return _mm(a, b, _NT)


def _dot_f32(a, b):
    return _mm(a, b, _NN, True)


def _dot_nt_f32(a, b):
    return _mm(a, b, _NT, True)


def _sigmoid(x):
    return 1.0 / (1.0 + jnp.exp(-x))


def _softplus(x):
    return jnp.maximum(x, 0.0) + jnp.log(1.0 + jnp.exp(-jnp.abs(x)))


def _rms(x, w):
    return x * lax.rsqrt(jnp.mean(x * x, axis=-1, keepdims=True) + EPS) * w


def _rope(x, cs, sn_lo, sn_hi):
    w = x.shape[1]
    cs = jnp.concatenate([cs] * DA_HEADS, axis=1)
    sn_lo = jnp.concatenate([sn_lo] * DA_HEADS, axis=1)
    sn_hi = jnp.concatenate([sn_hi] * DA_HEADS, axis=1)
    up = pltpu.roll(x, w - ROPE_HALF, 1)
    dn = pltpu.roll(x, ROPE_HALF, 1)
    return x * cs + up * sn_lo + dn * sn_hi


def _proj_kernel(x_ref, wn_ref, w_ref, cs_ref, snl_ref, snh_ref,
                 c_ref, z_ref, ab_ref, q_ref, k_ref, v_ref, *, precise):
    h = _rms(x_ref[...], wn_ref[...])
    h = h if precise else h.astype(BF16)
    o = 0
    c_ref[...] = _mm(h, w_ref[:, o:o + CONV_CH], _NN, precise)
    o += CONV_CH
    z_ref[...] = _mm(h, w_ref[:, o:o + GDN_V_W], _NN, precise)
    o += GDN_V_W
    q = _mm(h, w_ref[:, o:o + DA_QK_W], _NN, precise)
    o += DA_QK_W
    k = _mm(h, w_ref[:, o:o + DA_QK_W], _NN, precise)
    o += DA_QK_W
    v_ref[...] = _mm(h, w_ref[:, o:o + DA_V_W], _NN, precise)
    o += DA_V_W
    ab_ref[...] = _mm(h, w_ref[:, o:o + AB_W], _NN, precise)
    cs, snl, snh = cs_ref[...], snl_ref[...], snh_ref[...]
    q_ref[...] = _rope(q, cs, snl, snh)
    k_ref[...] = _rope(k, cs, snl, snh)


def _proj(x, wn, w, tabs, tm):
    n = x.shape[0]
    assert n % tm == 0
    nt = tabs[0].shape[0] // tm
    row = lambda i: (i, 0)
    fixed = lambda i: (0, 0)
    tab = lambda i: (i % nt, 0)
    widths = (CONV_CH, GDN_V_W, AB_W, DA_QK_W, DA_QK_W, DA_V_W)
    return pl.pallas_call(
        functools.partial(_proj_kernel, precise=w.dtype == F32),
        grid=(n // tm,),
        in_specs=[pl.BlockSpec((tm, D_MODEL), row),
                  pl.BlockSpec((1, D_MODEL), fixed),
                  pl.BlockSpec((D_MODEL, PROJ_PAD_W), fixed),
                  pl.BlockSpec((tm, LANES), tab),
                  pl.BlockSpec((tm, LANES), tab),
                  pl.BlockSpec((tm, LANES), tab)],
        out_specs=[pl.BlockSpec((tm, wd), row) for wd in widths],
        out_shape=[jax.ShapeDtypeStruct((n, wd), F32) for wd in widths],
        compiler_params=pltpu.CompilerParams(dimension_semantics=("parallel",),
                                             vmem_limit_bytes=VMEM_LIMIT),
        name="proj",
    )(x, wn, w, *tabs)


def _rope_tables(pos):
    inv_freq = ROPE_THETA ** (-jnp.arange(ROPE_HALF, dtype=F32) * 2.0 / ROPE_DIM)
    ang = pos.astype(F32)[:, None] * inv_freq[None, :]
    cos, sin = jnp.cos(ang), jnp.sin(ang)
    t = pos.shape[0]
    one = jnp.ones((t, DA_DQK - ROPE_DIM), F32)
    zero = jnp.zeros((t, DA_DQK - ROPE_DIM), F32)
    zh = jnp.zeros((t, ROPE_HALF), F32)
    cs = jnp.concatenate([cos, cos, one] * 2, axis=1)
    snl = jnp.concatenate([-sin, zh, zero] * 2, axis=1)
    snh = jnp.concatenate([zh, sin, zero] * 2, axis=1)
    return cs, snl, snh


def _prep_w_in(w_in):
    c, z, a, b, dq, dk, dv = jnp.split(
        w_in, np.cumsum([CONV_CH, GDN_V_W, GDN_HEADS, GDN_HEADS, DA_QK_W, DA_QK_W]).tolist(), axis=-1)
    pad = jnp.zeros((D_MODEL, AB_W - 2 * GDN_HEADS), w_in.dtype)
    return jnp.concatenate([c, z, dq, dk, dv, a, b, pad], axis=-1)


def _gdn_kernel(c_ref, z_ref, ab_ref, tail_ref, s0_ref, wconv_ref, alog_ref, dtb_ref, wg_ref,
                o_ref, sfin_ref, cbuf, ybuf, s_scr, u0_scr, wq_scr, kw_scr, qkd_scr, el_scr, *, tt):
    C, H = GDN_CHUNK, GDN_HEADS
    HS = GDN_STACK
    R = HS * C
    shift = C.bit_length() - 1
    t = pl.program_id(1)

    @pl.when(t == 0)
    def _():
        cbuf[0:8, :] = tail_ref[0]
        s_scr[...] = s0_ref[0]

    cbuf[8:8 + tt, :] = c_ref[0]
    acc = cbuf[8:8 + tt, :] * wconv_ref[CONV_W - 1:CONV_W, :]
    for i in range(CONV_W - 1):
        off = 8 - (CONV_W - 1) + i
        acc = acc + cbuf[off:off + tt, :] * wconv_ref[i:i + 1, :]
    ybuf[...] = acc * _sigmoid(acc)
    cbuf[0:8, :] = cbuf[tt:tt + 8, :]

    ii = lax.broadcasted_iota(jnp.int32, (R, R), 0)
    jj = lax.broadcasted_iota(jnp.int32, (R, R), 1)
    same_head = jnp.right_shift(ii, shift) == jnp.right_shift(jj, shift)
    strict = same_head & (ii > jj)
    causal = same_head & (ii >= jj)
    eye = (ii == jj).astype(F32)
    merge_masks = []
    s = 1
    while s < C:
        sh = s.bit_length() - 1
        merge_masks.append((jnp.right_shift(ii, sh + 1) == jnp.right_shift(jj, sh + 1))
                           & (jnp.right_shift(ii, sh) != jnp.right_shift(jj, sh)) & (ii > jj))
        s *= 2
    ci_ = lax.broadcasted_iota(jnp.int32, (C, C), 0)
    cj_ = lax.broadcasted_iota(jnp.int32, (C, C), 1)
    tri = (ci_ >= cj_).astype(F32)
    lane = lax.broadcasted_iota(jnp.int32, (R, LANES), 1)
    rhead = jnp.right_shift(lax.broadcasted_iota(jnp.int32, (R, LANES), 0), shift)
    ones8 = jnp.ones((8, LANES), F32)
    rep = lambda a: jnp.concatenate([a] * HS, axis=0)
    pick = lambda a, off: jnp.sum(jnp.where(lane == rhead + off, a, 0.0), axis=-1, keepdims=True)

    def stacked(rows, base, scale, h0):
        parts = []
        for h in range(h0, h0 + HS):
            x = ybuf[rows, base + h * GDN_DK:base + (h + 1) * GDN_DK]
            if scale is not None:
                x = x * lax.rsqrt(jnp.sum(x * x, axis=-1, keepdims=True) + EPS) * scale
            parts.append(x)
        return jnp.concatenate(parts, axis=0)

    n_stack = H // HS
    units = range((tt // C) * n_stack)
    low, rhs, qes = [], [], []
    for ci in range(tt // C):
        rows = slice(ci * C, (ci + 1) * C)
        ab = ab_ref[0, rows, :]
        gfull = -jnp.exp(alog_ref[...]) * _softplus(ab + dtb_ref[...])
        bfull = _sigmoid(ab)
        gcum = _dot_f32(tri, gfull)
        glast = gcum[C - 1:C, :]
        el_scr[ci] = jnp.broadcast_to(jnp.exp(glast), (8, LANES))
        for p in range(n_stack):
            un, h0 = ci * n_stack + p, p * HS
            g_m = jnp.where(lane == rhead + h0, rep(gcum), 0.0)
            g_col = jnp.sum(g_m, axis=-1, keepdims=True)
            g_row = _dot_nt_f32(ones8, g_m)[0:1, :]
            beta = pick(rep(bfull), H + h0)
            gl_col = pick(jnp.broadcast_to(glast, (R, LANES)), h0)
            decay = jnp.exp(jnp.where(causal, g_col - g_row, -jnp.inf))
            q = stacked(rows, 0, GDN_DK ** -0.5, h0)
            k = stacked(rows, GDN_QK_W, 1.0, h0)
            v = stacked(rows, 2 * GDN_QK_W, None, h0)
            kq = _mm(jnp.concatenate([k, q], axis=0), k, _NT)
            low.append(jnp.where(strict, kq[0:R] * decay * beta, 0.0))
            e_g = jnp.exp(g_col)
            rhs.append(jnp.concatenate([beta * v, (beta * e_g) * k], axis=1).astype(BF16))
            qes.append(q * e_g)
            kw_scr[un] = (k * jnp.exp(gl_col - g_col)).astype(BF16)
            qkd_scr[un] = jnp.where(causal, kq[R:2 * R] * decay, 0.0).astype(BF16)
    tinv = [eye - jnp.where(merge_masks[0], low[un], 0.0) for un in units]
    for m in merge_masks[1:]:
        half = [_mm(tinv[un], jnp.where(m, low[un], 0.0)) for un in units]
        tinv = [tinv[un] - _mm(half[un], tinv[un]) for un in units]
    for un in units:
        w2 = _mm(tinv[un], rhs[un])
        u0_scr[un] = w2[:, 0:GDN_DV]
        wk = w2[:, GDN_DV:]
        wq_scr[un] = jnp.concatenate(
            [x[h * C:(h + 1) * C] for h in range(HS) for x in (wk, qes[un])], axis=0).astype(BF16)

    st = [s_scr[h] for h in range(H)]
    for ci in range(tt // C):
        rows = slice(ci * C, (ci + 1) * C)
        el = el_scr[ci]
        for p in range(n_stack):
            un = ci * n_stack + p
            us, os_ = [], []
            for hh in range(HS):
                r = jnp.dot(wq_scr[un, 2 * hh * C:2 * (hh + 1) * C, :], st[p * HS + hh].astype(BF16),
                            preferred_element_type=F32)
                us.append(u0_scr[un, hh * C:(hh + 1) * C, :] - r[0:C])
                os_.append(r[C:2 * C])
            intra = jnp.dot(qkd_scr[un], jnp.concatenate(us, axis=0).astype(BF16),
                            preferred_element_type=F32)
            for hh in range(HS):
                h = p * HS + hh
                st[h] = (st[h] * el[0:1, h:h + 1]
                         + _mm(kw_scr[un, hh * C:(hh + 1) * C, :], us[hh], _TN))
                o = os_[hh] + intra[hh * C:(hh + 1) * C]
                zz = z_ref[0, rows, h * GDN_DV:(h + 1) * GDN_DV]
                o_ref[0, rows, h * GDN_DV:(h + 1) * GDN_DV] = (_rms(o, wg_ref[...])
                                                              * (zz * _sigmoid(zz)))
    for h in range(H):
        s_scr[h] = st[h]

    @pl.when(t == pl.num_programs(1) - 1)
    def _():
        sfin_ref[0] = s_scr[...]


def _gdn_step_kernel(c_ref, z_ref, ab_ref, cs_ref, s_ref, wconv_ref, alog_ref, dtb_ref, wg_ref,
                     o_ref, snew_ref, *, nb):
    H = GDN_HEADS
    acc = c_ref[...] * wconv_ref[CONV_W - 1:CONV_W, :]
    for i in range(CONV_W - 1):
        acc = acc + cs_ref[i] * wconv_ref[i:i + 1, :]
    y = acc * _sigmoid(acc)
    ab = ab_ref[...]
    e_g = jnp.exp(-jnp.exp(alog_ref[...]) * _softplus(ab + dtb_ref[...]))
    bfull = _sigmoid(ab)
    unit = lambda x: x * lax.rsqrt(jnp.sum(x * x, axis=-1, keepdims=True) + EPS)
    qs = [unit(y[:, h * GDN_DK:(h + 1) * GDN_DK]) * (GDN_DK ** -0.5) for h in range(H)]
    ks = [unit(y[:, GDN_QK_W + h * GDN_DK:GDN_QK_W + (h + 1) * GDN_DK]) for h in range(H)]
    vs = [y[:, 2 * GDN_QK_W + h * GDN_DV:2 * GDN_QK_W + (h + 1) * GDN_DV] for h in range(H)]
    cols = jnp.concatenate(ks + qs, axis=0).T
    for h in range(H):
        ks_rows, qs_rows = [], []
        for s in range(nb):
            state = s_ref[s, h]
            kcol = cols[:, h * nb + s:h * nb + s + 1]
            qcol = cols[:, (H + h) * nb + s:(H + h) * nb + s + 1]
            ks_rows.append(jnp.sum(kcol * state, axis=0, keepdims=True))
            qs_rows.append(jnp.sum(qcol * state, axis=0, keepdims=True))
        eg_h = e_g[:, h:h + 1]
        u = bfull[:, H + h:H + h + 1] * (vs[h] - eg_h * jnp.concatenate(ks_rows, axis=0))
        o = (eg_h * jnp.concatenate(qs_rows, axis=0)
             + jnp.sum(qs[h] * ks[h], axis=-1, keepdims=True) * u)
        eg_b = jnp.broadcast_to(eg_h, (nb, GDN_DV))
        for s in range(nb):
            kcol = cols[:, h * nb + s:h * nb + s + 1]
            snew_ref[s, h] = eg_b[s:s + 1, :] * s_ref[s, h] + kcol * u[s:s + 1, :]
        zz = z_ref[:, h * GDN_DV:(h + 1) * GDN_DV]
        o_ref[:, h * GDN_DV:(h + 1) * GDN_DV] = _rms(o, wg_ref[...]) * (zz * _sigmoid(zz))


def _gdn_step(c, z, ab, conv_state, s, wconv, alog, dtb, wg):
    bs = c.shape[0]
    nb = LANES // (2 * GDN_HEADS)
    assert bs % nb == 0
    row = lambda i: (i, 0)
    fixed = lambda i: (0, 0)
    state = pl.BlockSpec((nb, GDN_HEADS, GDN_DK, GDN_DV), lambda i: (i, 0, 0, 0))
    return pl.pallas_call(
        functools.partial(_gdn_step_kernel, nb=nb),
        grid=(bs // nb,),
        in_specs=[pl.BlockSpec((nb, CONV_CH), row),
                  pl.BlockSpec((nb, GDN_V_W), row),
                  pl.BlockSpec((nb, AB_W), row),
                  pl.BlockSpec((CONV_W - 1, nb, CONV_CH), lambda i: (0, i, 0)),
                  state,
                  pl.BlockSpec((CONV_W, CONV_CH), fixed),
                  pl.BlockSpec((1, LANES), fixed),
                  pl.BlockSpec((1, LANES), fixed),
                  pl.BlockSpec((1, GDN_DV), fixed)],
        out_specs=[pl.BlockSpec((nb, GDN_V_W), row), state],
        out_shape=[jax.ShapeDtypeStruct((bs, GDN_V_W), F32),
                   jax.ShapeDtypeStruct(s.shape, F32)],
        compiler_params=pltpu.CompilerParams(dimension_semantics=("parallel",),
                                             vmem_limit_bytes=VMEM_LIMIT),
        name="gdn_step",
    )(c, z, ab, conv_state, s, wconv, alog, dtb, wg)


def _gdn(c, z, ab, tail, s0, wconv, alog, dtb, wg, tt):
    b, t, _ = c.shape
    assert t % tt == 0 and tt % GDN_CHUNK == 0
    rs = GDN_STACK * GDN_CHUNK
    nc, nu = tt // GDN_CHUNK, (tt // GDN_CHUNK) * (GDN_HEADS // GDN_STACK)
    seq = lambda i, j: (i, j, 0)
    fixed2 = lambda i, j: (0, 0)
    tail_map = (lambda i, j: (i, 0, 0)) if tail.shape[0] == b else (lambda i, j: (0, 0, 0))
    s0_map = (lambda i, j: (i, 0, 0, 0)) if s0.shape[0] == b else (lambda i, j: (0, 0, 0, 0))
    return pl.pallas_call(
        functools.partial(_gdn_kernel, tt=tt),
        grid=(b, t // tt),
        in_specs=[pl.BlockSpec((1, tt, CONV_CH), seq),
                  pl.BlockSpec((1, tt, GDN_V_W), seq),
                  pl.BlockSpec((1, tt, AB_W), seq),
                  pl.BlockSpec((1, 8, CONV_CH), tail_map),
                  pl.BlockSpec((1, GDN_HEADS, GDN_DK, GDN_DV), s0_map),
                  pl.BlockSpec((CONV_W, CONV_CH), fixed2),
                  pl.BlockSpec((1, LANES), fixed2),
                  pl.BlockSpec((1, LANES), fixed2),
                  pl.BlockSpec((1, GDN_DV), fixed2)],
        out_specs=[pl.BlockSpec((1, tt, GDN_V_W), seq),
                   pl.BlockSpec((1, GDN_HEADS, GDN_DK, GDN_DV), lambda i, j: (i, 0, 0, 0))],
        out_shape=[jax.ShapeDtypeStruct((b, t, GDN_V_W), F32),
                   jax.ShapeDtypeStruct((b, GDN_HEADS, GDN_DK, GDN_DV), F32)],
        scratch_shapes=[pltpu.VMEM((tt + 8, CONV_CH), F32),
                        pltpu.VMEM((tt, CONV_CH), F32),
                        pltpu.VMEM((GDN_HEADS, GDN_DK, GDN_DV), F32),
                        pltpu.VMEM((nu, rs, GDN_DV), F32),
                        pltpu.VMEM((nu, 2 * rs, GDN_DK), BF16),
                        pltpu.VMEM((nu, rs, GDN_DK), BF16),
                        pltpu.VMEM((nu, rs, rs), BF16),
                        pltpu.VMEM((nc, 8, LANES), F32)],
        compiler_params=pltpu.CompilerParams(dimension_semantics=("parallel", "arbitrary"),
                                             vmem_limit_bytes=VMEM_LIMIT),
        name="gdn",
    )(c, z, ab, tail, s0, wconv, alog, dtb, wg)


def _kv_out_kernel(k_ref, v_ref, km_ref, vm_ref, ko_ref, vo_ref):
    t = k_ref.shape[1]
    for src, meta, dst in ((k_ref, km_ref, ko_ref), (v_ref, vm_ref, vo_ref)):
        for h in range(DA_HEADS):
            cols = slice(h * DA_DV, (h + 1) * DA_DV)
            dst[0, pl.ds(h, N_META, stride=DA_HEADS), :] = meta[:, cols]
            dst[0, pl.ds(N_META * DA_HEADS + h, t, stride=DA_HEADS), :] = src[0, :, cols]


def _kv_out(k, v, k_meta, v_meta):
    b, t, w = k.shape
    rows = (t + N_META) * DA_HEADS
    seq = pl.BlockSpec((1, t, w), lambda i: (i, 0, 0))
    meta = pl.BlockSpec((N_META, w), lambda i: (0, 0))
    out = pl.BlockSpec((1, rows, DA_DV), lambda i: (i, 0, 0))
    return pl.pallas_call(
        _kv_out_kernel,
        grid=(b,),
        in_specs=[seq, seq, meta, meta],
        out_specs=[out, out],
        out_shape=[jax.ShapeDtypeStruct((b, rows, DA_DV), F32)] * 2,
        compiler_params=pltpu.CompilerParams(dimension_semantics=("parallel",),
                                             vmem_limit_bytes=VMEM_LIMIT),
        name="kv_out",
    )(k, v, k_meta, v_meta)


def _lambda(lam_ref):
    lv = lam_ref[...]
    s1 = jnp.sum(lv[0:1, :] * lv[1:2, :], axis=-1, keepdims=True)
    s2 = jnp.sum(lv[2:3, :] * lv[3:4, :], axis=-1, keepdims=True)
    return jnp.exp(s1) - jnp.exp(s2) + LAM_INIT


def _split_components(q):
    lane = lax.broadcasted_iota(jnp.int32, q.shape, 1)
    first = (lane % DA_DV) < DA_DQK
    return jnp.where(first, q, 0.0).astype(BF16), jnp.where(first, 0.0, q).astype(BF16)


def _online_update(state, s, vb):
    m, l, acc = state
    m_new = jnp.maximum(m, jnp.max(s, axis=-1, keepdims=True))
    p = jnp.exp2(s - m_new)
    alpha = jnp.exp2(m - m_new)
    return (m_new, alpha * l + jnp.sum(p, axis=-1, keepdims=True),
            alpha * acc + jnp.dot(p.astype(BF16), vb, preferred_element_type=F32))


def _attn_prompt_kernel(q_ref, k_ref, v_ref, km_ref, vm_ref, lam_ref, wsub_ref, o_ref, *, bq, bk):
    qi = pl.program_id(2)
    heads = [slice(h * DA_DV, (h + 1) * DA_DV) for h in range(q_ref.shape[2] // DA_DV)]
    qs = [_split_components(q_ref[0, :, hs] * (DA_DQK ** -0.5 * math.log2(math.e))) for hs in heads]
    neg = jnp.full((bq, 1), -jnp.inf, F32)
    zero = jnp.zeros((bq, 1), F32)
    zacc = jnp.zeros((bq, DA_DV), F32)
    init = tuple(((neg, zero, zacc), (neg, zero, zacc)) for _ in heads)

    def block(kblk, vblk, mask, st):
        out = []
        for (q1, q2), hs, sth in zip(qs, heads, st):
            kb, vb = kblk[:, hs].astype(BF16), vblk[:, hs].astype(BF16)
            s1, s2 = _dot_nt(q1, kb), _dot_nt(q2, kb)
            if mask is not None:
                s1 = jnp.where(mask, s1, -jnp.inf)
                s2 = jnp.where(mask, s2, -jnp.inf)
            out.append((_online_update(sth[0], s1, vb), _online_update(sth[1], s2, vb)))
        return tuple(out)

    n_full = lax.shift_right_logical(qi * bq, bk.bit_length() - 1)

    def body(kb, st):
        r0 = pl.multiple_of(kb * bk, bk)
        return block(k_ref[0, pl.ds(r0, bk), :], v_ref[0, pl.ds(r0, bk), :], None, st)

    st = lax.fori_loop(0, n_full, body, init)
    r0 = pl.multiple_of(n_full * bk, bk)
    kcat = jnp.concatenate([km_ref[...], k_ref[0, pl.ds(r0, bk), :]], axis=0)
    vcat = jnp.concatenate([vm_ref[...], v_ref[0, pl.ds(r0, bk), :]], axis=0)
    ri = lax.broadcasted_iota(jnp.int32, (bq, LANES + bk), 0)
    ci = lax.broadcasted_iota(jnp.int32, (bq, LANES + bk), 1)
    seen = (ci < N_META) | ((ci >= LANES) & (ci - LANES <= ri + (qi * bq - n_full * bk)))
    st = block(kcat, vcat, seen, st)
    lam = _lambda(lam_ref)
    for hs, ((m1, l1, a1), (m2, l2, a2)) in zip(heads, st):
        o = a1 / l1 - lam * (a2 / l2)
        o_ref[0, :, hs] = _rms(o, wsub_ref[...]) * (1.0 - LAM_INIT)


def _attn_prompt(q, k, v, k_meta, v_meta, lam_vecs, wsub, bq, bk, hps):
    b, t, _ = q.shape
    assert t % bk == 0 and bk % bq == 0 and k_meta.shape[0] == LANES and DA_HEADS % hps == 0
    w = hps * DA_DV
    return pl.pallas_call(
        functools.partial(_attn_prompt_kernel, bq=bq, bk=bk),
        grid=(b, DA_HEADS // hps, t // bq),
        in_specs=[pl.BlockSpec((1, bq, w), lambda i, h, j: (i, j, h)),
                  pl.BlockSpec((1, t, w), lambda i, h, j: (i, 0, h)),
                  pl.BlockSpec((1, t, w), lambda i, h, j: (i, 0, h)),
                  pl.BlockSpec((LANES, w), lambda i, h, j: (0, h)),
                  pl.BlockSpec((LANES, w), lambda i, h, j: (0, h)),
                  pl.BlockSpec((4, DA_DQK), lambda i, h, j: (0, 0)),
                  pl.BlockSpec((1, DA_DV), lambda i, h, j: (0, 0))],
        out_specs=pl.BlockSpec((1, bq, w), lambda i, h, j: (i, j, h)),
        out_shape=jax.ShapeDtypeStruct((b, t, DA_V_W), F32),
        compiler_params=pltpu.CompilerParams(
            dimension_semantics=("parallel", "parallel", "arbitrary"),
            vmem_limit_bytes=VMEM_LIMIT),
        name="attn_prompt",
    )(q, k, v, k_meta, v_meta, lam_vecs, wsub)


def _attn_sample_kernel(pt_ref, q_ref, kn_ref, vn_ref, lam_ref, wsub_ref, *rest, pps):
    k_refs, v_refs = rest[:pps], rest[pps:2 * pps]
    o_ref, m_scr, l_scr, acc_scr = rest[2 * pps:]
    step = pl.program_id(1)
    nrow = 2 * DA_HEADS
    twice = lambda x: jnp.concatenate([x, x], axis=0)
    rowq = lax.broadcasted_iota(jnp.int32, (nrow, DA_DV), 0)
    laneq = lax.broadcasted_iota(jnp.int32, (nrow, DA_DV), 1)
    own_comp = (laneq // DA_DQK) == (rowq // DA_HEADS)
    qm = jnp.where(own_comp, twice(q_ref[0]) * (DA_DQK ** -0.5), 0.0)
    page_rows = k_refs[0].shape[1]
    rowp = lax.broadcasted_iota(jnp.int32, (nrow, page_rows * pps), 0)
    colp = lax.broadcasted_iota(jnp.int32, (nrow, page_rows * pps), 1)
    own_head = (colp % DA_HEADS) == (rowp % DA_HEADS)

    def rows3(x):
        hi = x.astype(BF16).astype(F32)
        mid = (x - hi).astype(BF16).astype(F32)
        return jnp.concatenate([hi, mid, x - hi - mid], axis=0).astype(BF16)

    def unrows3(r):
        return (r[0:nrow] + r[nrow:2 * nrow]) + r[2 * nrow:3 * nrow]

    def hi_lo(x):
        hi = x.astype(BF16)
        return hi, (x - hi.astype(F32)).astype(BF16)

    @pl.when(step == 0)
    def _():
        s_new = jnp.sum(qm * twice(kn_ref[0]), axis=-1, keepdims=True)
        m_scr[...] = s_new
        l_scr[...] = jnp.ones_like(s_new)
        acc_scr[...] = twice(vn_ref[0])

    q3 = rows3(qm)
    scores = []
    for r in k_refs:
        k_hi, k_lo = hi_lo(r[0])
        scores.append(unrows3(_dot_nt(q3, k_hi) + _dot_nt(q3, k_lo)))
    s = jnp.where(own_head, jnp.concatenate(scores, axis=1), -jnp.inf)
    m_old = m_scr[...]
    m_new = jnp.maximum(m_old, jnp.max(s, axis=-1, keepdims=True))
    p = jnp.exp(s - m_new)
    alpha = jnp.exp(m_old - m_new)
    p3 = rows3(p)
    pv = jnp.zeros((3 * nrow, DA_DV), F32)
    for j, r in enumerate(v_refs):
        v_hi, v_lo = hi_lo(r[0])
        pj = p3[:, j * page_rows:(j + 1) * page_rows]
        pv = pv + (jnp.dot(pj, v_hi, preferred_element_type=F32)
                   + jnp.dot(pj, v_lo, preferred_element_type=F32))
    m_scr[...] = m_new
    l_scr[...] = alpha * l_scr[...] + jnp.sum(p, axis=-1, keepdims=True)
    acc_scr[...] = alpha * acc_scr[...] + unrows3(pv)

    @pl.when(step == pl.num_programs(1) - 1)
    def _():
        w = acc_scr[...] / l_scr[...]
        o = w[0:DA_HEADS, :] - _lambda(lam_ref) * w[DA_HEADS:nrow, :]
        o_ref[0] = _rms(o, wsub_ref[...]) * (1.0 - LAM_INIT)


def _attn_sample(q, k_new, v_new, cache_k, cache_v, page_table, lam_vecs, wsub, pps):
    bs = q.shape[0]
    n_pages = page_table.shape[1]
    page_rows = cache_k.shape[1]
    assert n_pages % pps == 0
    tok = lambda i, s, pt: (i, 0, 0)
    fixed = lambda i, s, pt: (0, 0)

    def page_spec(j):
        return pl.BlockSpec((1, page_rows, DA_DV), lambda i, s, pt: (pt[i, s * pps + j], 0, 0))

    grid_spec = pltpu.PrefetchScalarGridSpec(
        num_scalar_prefetch=1,
        grid=(bs, n_pages // pps),
        in_specs=[pl.BlockSpec((1, DA_HEADS, DA_DV), tok),
                  pl.BlockSpec((1, DA_HEADS, DA_DV), tok),
                  pl.BlockSpec((1, DA_HEADS, DA_DV), tok),
                  pl.BlockSpec((4, DA_DQK), fixed),
                  pl.BlockSpec((1, DA_DV), fixed)]
                 + [page_spec(j) for j in range(pps)] * 2,
        out_specs=pl.BlockSpec((1, DA_HEADS, DA_DV), tok),
        scratch_shapes=[pltpu.VMEM((2 * DA_HEADS, 1), F32),
                        pltpu.VMEM((2 * DA_HEADS, 1), F32),
                        pltpu.VMEM((2 * DA_HEADS, DA_DV), F32)])
    return pl.pallas_call(
        functools.partial(_attn_sample_kernel, pps=pps),
        grid_spec=grid_spec,
        out_shape=jax.ShapeDtypeStruct((bs, DA_HEADS, DA_DV), F32),
        compiler_params=pltpu.CompilerParams(dimension_semantics=("parallel", "arbitrary"),
                                             vmem_limit_bytes=VMEM_LIMIT),
        name="attn_sample",
    )(page_table, q, k_new, v_new, lam_vecs, wsub, *([cache_k] * pps), *([cache_v] * pps))


def _route(logits):
    lane = lax.broadcasted_iota(jnp.int32, logits.shape, 1).astype(F32)
    big = float(LANES)
    ninf = -jnp.inf

    def top(mask):
        val = jnp.where(mask, logits, ninf)
        mx = jnp.max(val, axis=-1, keepdims=True)
        idx = jnp.min(jnp.where(mask & (val == mx), lane, big), axis=-1, keepdims=True)
        return mx, idx

    gmask = (lane >= N_EXPERTS) & (lane < N_EXPERTS + N_GROUPS)
    gmax, gidx = top(gmask)
    g_w = 1.0 / jnp.sum(jnp.where(gmask, jnp.exp(logits - gmax), 0.0), axis=-1, keepdims=True)
    g_sel = gidx - N_EXPERTS
    emask = (lane >= g_sel * EXPERTS_PER_GROUP) & (lane < (g_sel + 1) * EXPERTS_PER_GROUP)
    v1, i1 = top(emask)
    v2, i2 = top(emask & (lane != i1))
    e21 = jnp.exp(v2 - v1)
    p1 = 1.0 / (1.0 + e21)
    return jnp.where(lane == i1, p1 * g_w, 0.0) + jnp.where(lane == i2, e21 * p1 * g_w, 0.0), g_sel


def _mix_route(g_ref, a_ref, x_ref, wo_ref, wnf_ref, wr_ref, precise):
    mix = (_mm(g_ref[...], wo_ref[0:GDN_V_W, :], _NN, precise)
           + _mm(a_ref[...], wo_ref[GDN_V_W:, :], _NN, precise))
    x1 = x_ref[...] + mix
    xn = _rms(x1, wnf_ref[...])
    wr = wr_ref[...]
    if precise:
        logits = _dot_f32(xn, wr)
    else:
        x_hi, w_hi = xn.astype(BF16), wr.astype(BF16)
        x_lo, w_lo = (xn - x_hi.astype(F32)).astype(BF16), (wr - w_hi.astype(F32)).astype(BF16)
        logits = (jnp.dot(x_hi, w_hi, preferred_element_type=F32)
                  + (jnp.dot(x_hi, w_lo, preferred_element_type=F32)
                     + jnp.dot(x_lo, w_hi, preferred_element_type=F32)))
    comb, g_sel = _route(logits)
    return x1, xn, comb, g_sel


def _expert(xn, comb, e_lane, wg, wu, wd):
    hg = jnp.dot(xn, wg.astype(BF16), preferred_element_type=F32)
    hu = jnp.dot(xn, wu.astype(BF16), preferred_element_type=F32)
    lane = lax.broadcasted_iota(jnp.int32, comb.shape, 1)
    cw = jnp.sum(jnp.where(lane == e_lane, comb, 0.0), axis=-1, keepdims=True)
    hh = (hg * _sigmoid(hg)) * hu * cw
    return jnp.dot(hh.astype(BF16), wd.astype(BF16), preferred_element_type=F32)


def _ffn_kernel(g_ref, a_ref, x_ref, wo_ref, wnf_ref, wr_ref, wg_ref, wu_ref, wd_ref, wfin_ref,
                y_ref, x1_scr, xn_scr, comb_scr, acc_scr, *, precise):
    e = pl.program_id(1)

    @pl.when(e == 0)
    def _():
        x1, xn, comb, _ = _mix_route(g_ref, a_ref, x_ref, wo_ref, wnf_ref, wr_ref, precise)
        x1_scr[...] = x1
        xn_scr[...] = xn.astype(BF16)
        comb_scr[...] = comb
        acc_scr[...] = jnp.zeros_like(acc_scr)

    acc_scr[...] += _expert(xn_scr[...], comb_scr[...], e, wg_ref[0], wu_ref[0], wd_ref[0])

    @pl.when(e == pl.num_programs(1) - 1)
    def _():
        y_ref[...] = _rms(x1_scr[...] + acc_scr[...], wfin_ref[...])


def _ffn(g_out, a_out, x, wo, wnf, wr, wg, wu, wd, wfin, tm):
    n = x.shape[0]
    assert n % tm == 0
    row = lambda i, e: (i, 0)
    fixed = lambda i, e: (0, 0)
    exp = lambda i, e: (e, 0, 0)
    return pl.pallas_call(
        functools.partial(_ffn_kernel, precise=wo.dtype == F32),
        grid=(n // tm, N_EXPERTS),
        in_specs=[pl.BlockSpec((tm, GDN_V_W), row),
                  pl.BlockSpec((tm, DA_V_W), row),
                  pl.BlockSpec((tm, D_MODEL), row),
                  pl.BlockSpec((GDN_V_W + DA_V_W, D_MODEL), fixed),
                  pl.BlockSpec((1, D_MODEL), fixed),
                  pl.BlockSpec((D_MODEL, LANES), fixed),
                  pl.BlockSpec((1, D_MODEL, D_EXPERT), exp),
                  pl.BlockSpec((1, D_MODEL, D_EXPERT), exp),
                  pl.BlockSpec((1, D_EXPERT, D_MODEL), exp),
                  pl.BlockSpec((1, D_MODEL), fixed)],
        out_specs=pl.BlockSpec((tm, D_MODEL), row),
        out_shape=jax.ShapeDtypeStruct((n, D_MODEL), F32),
        scratch_shapes=[pltpu.VMEM((tm, D_MODEL), F32),
                        pltpu.VMEM((tm, D_MODEL), BF16),
                        pltpu.VMEM((tm, LANES), F32),
                        pltpu.VMEM((tm, D_MODEL), F32)],
        compiler_params=pltpu.CompilerParams(dimension_semantics=("parallel", "arbitrary"),
                                             vmem_limit_bytes=VMEM_LIMIT),
        name="ffn",
    )(g_out, a_out, x, wo, wnf, wr, wg, wu, wd, wfin)


ROW_W = D_MODEL + LANES
MOE_TILE = 512
ROW_TILE = 1024


def _ffn_pre_kernel(g_ref, a_ref, x_ref, wo_ref, wnf_ref, wr_ref, rows_ref, info_ref, cnt_ref,
                    cnt_scr):
    @pl.when(pl.program_id(0) == 0)
    def _():
        cnt_scr[...] = jnp.zeros_like(cnt_scr)

    x1, _, comb, g_sel = _mix_route(g_ref, a_ref, x_ref, wo_ref, wnf_ref, wr_ref, False)
    tm = x1.shape[0]
    lane = lax.broadcasted_iota(jnp.int32, (tm, LANES), 1).astype(F32)
    onehot = lane == g_sel
    ri = lax.broadcasted_iota(jnp.int32, (tm, tm), 0)
    ci = lax.broadcasted_iota(jnp.int32, (tm, tm), 1)
    earlier = jnp.dot((ci < ri).astype(BF16), onehot.astype(BF16), preferred_element_type=F32)
    rank = jnp.sum(jnp.where(onehot, earlier + cnt_scr[...], 0.0), axis=-1, keepdims=True)
    cnt_scr[...] += jnp.sum(onehot.astype(F32), axis=0, keepdims=True)
    cnt_ref[...] = cnt_scr[...]
    rows_ref[:, 0:D_MODEL] = x1
    rows_ref[:, D_MODEL:] = comb
    packed = jnp.where(lane == 0.0, g_sel, jnp.where(lane == 1.0, rank, 0.0))
    r8 = lax.broadcasted_iota(jnp.int32, (8, LANES), 0)
    l8 = lax.broadcasted_iota(jnp.int32, (8, LANES), 1)
    sel = (r8 == l8).astype(BF16)
    hi = packed.astype(BF16)
    info_ref[...] = _dot_nt(sel, hi) + _dot_nt(sel, packed - hi.astype(F32))


def _ffn_pre(g_out, a_out, x, wo, wnf, wr, tm):
    n = x.shape[0]
    assert n % tm == 0
    row = lambda i: (i, 0)
    fixed = lambda i: (0, 0)
    return pl.pallas_call(
        _ffn_pre_kernel,
        grid=(n // tm,),
        in_specs=[pl.BlockSpec((tm, GDN_V_W), row),
                  pl.BlockSpec((tm, DA_V_W), row),
                  pl.BlockSpec((tm, D_MODEL), row),
                  pl.BlockSpec((GDN_V_W + DA_V_W, D_MODEL), fixed),
                  pl.BlockSpec((1, D_MODEL), fixed),
                  pl.BlockSpec((D_MODEL, LANES), fixed)],
        out_specs=[pl.BlockSpec((tm, ROW_W), row),
                   pl.BlockSpec((8, tm), lambda i: (0, i)),
                   pl.BlockSpec((1, LANES), fixed)],
        out_shape=[jax.ShapeDtypeStruct((n, ROW_W), F32),
                   jax.ShapeDtypeStruct((8, n), F32),
                   jax.ShapeDtypeStruct((1, LANES), F32)],
        scratch_shapes=[pltpu.VMEM((1, LANES), F32)],
        compiler_params=pltpu.CompilerParams(dimension_semantics=("arbitrary",),
                                             vmem_limit_bytes=VMEM_LIMIT),
        name="ffn_pre",
    )(g_out, a_out, x, wo, wnf, wr)


def _copy_rows(idx_ref, tile_ref, hbm_ref, sem, scatter):
    tm = tile_ref.shape[0]
    base = pl.program_id(0) * tm

    def body(t, carry):
        for queue in range(2):
            r = 2 * t + queue
            there, here = hbm_ref.at[pl.ds(idx_ref[base + r], 1)], tile_ref.at[pl.ds(r, 1)]
            src, dst = (here, there) if scatter else (there, here)
            pltpu.make_async_copy(src, dst, sem).start(priority=queue)
        return carry

    lax.fori_loop(0, tm // 2, body, 0, unroll=4)
    there, here = hbm_ref.at[pl.ds(0, tm)], tile_ref
    src, dst = (here, there) if scatter else (there, here)
    pltpu.make_async_copy(src, dst, sem).wait()


def _scatter_kernel(idx_ref, tile_ref, init_ref, dst_ref, sem):
    del init_ref
    _copy_rows(idx_ref, tile_ref, dst_ref, sem, True)


def _gather_kernel(idx_ref, src_ref, tile_ref, sem):
    _copy_rows(idx_ref, tile_ref, src_ref, sem, False)


def _move_rows(idx, src, n_out, scatter):
    n, w = idx.shape[0], src.shape[1]
    assert n % ROW_TILE == 0
    any_spec = pl.BlockSpec(memory_space=pl.ANY)
    tile_spec = pl.BlockSpec((ROW_TILE, w), lambda i, idx: (i, 0))
    if scatter:
        operands = (idx, src, jnp.zeros((n_out, w), src.dtype))
        in_specs, out_spec = [tile_spec, any_spec], any_spec
    else:
        operands = (idx, src)
        in_specs, out_spec = [any_spec], tile_spec
    return pl.pallas_call(
        _scatter_kernel if scatter else _gather_kernel,
        grid_spec=pltpu.PrefetchScalarGridSpec(
            num_scalar_prefetch=1, grid=(n // ROW_TILE,),
            in_specs=in_specs, out_specs=out_spec,
            scratch_shapes=[pltpu.SemaphoreType.DMA(())]),
        out_shape=jax.ShapeDtypeStruct((n_out, w), src.dtype),
        input_output_aliases={2: 0} if scatter else {},
        compiler_params=pltpu.CompilerParams(dimension_semantics=("arbitrary",)),
        name="scatter_rows" if scatter else "gather_rows",
    )(*operands)


def _moe_groups_kernel(tg_ref, nt_ref, rows_ref, wnf_ref, wg_ref, wu_ref, wd_ref, wfin_ref, y_ref,
                       wg_bf, wu_bf, wd_bf):
    j = pl.program_id(0)

    @pl.when((j == 0) | (tg_ref[j] != tg_ref[jnp.maximum(j - 1, 0)]))
    def _():
        for e in range(EXPERTS_PER_GROUP):
            wg_bf[e] = wg_ref[e].astype(BF16)
            wu_bf[e] = wu_ref[e].astype(BF16)
            wd_bf[e] = wd_ref[e].astype(BF16)

    @pl.when(j < nt_ref[0])
    def _():
        x1 = rows_ref[:, 0:D_MODEL]
        comb = rows_ref[:, D_MODEL:]
        xn = _rms(x1, wnf_ref[...]).astype(BF16)
        first = tg_ref[j] * EXPERTS_PER_GROUP
        acc = jnp.zeros_like(x1)
        for e in range(EXPERTS_PER_GROUP):
            acc = acc + _expert(xn, comb, first + e, wg_bf[e], wu_bf[e], wd_bf[e])
        y_ref[...] = _rms(x1 + acc, wfin_ref[...])

    @pl.when(j >= nt_ref[0])
    def _():
        y_ref[...] = jnp.zeros_like(y_ref)


def _moe_groups(tile_group, n_tiles_used, rows, wnf, wg, wu, wd, wfin):
    n_tiles = tile_group.shape[0]
    fixed = lambda j, tg, nt: (0, 0)
    grp = lambda j, tg, nt: (tg[j], 0, 0)
    in_shape, out_shape = (EXPERTS_PER_GROUP, D_MODEL, D_EXPERT), (EXPERTS_PER_GROUP, D_EXPERT, D_MODEL)
    once = pl.Buffered(1)
    return pl.pallas_call(
        _moe_groups_kernel,
        grid_spec=pltpu.PrefetchScalarGridSpec(
            num_scalar_prefetch=2, grid=(n_tiles,),
            in_specs=[pl.BlockSpec((MOE_TILE, ROW_W), lambda j, tg, nt: (j, 0)),
                      pl.BlockSpec((1, D_MODEL), fixed),
                      pl.BlockSpec(in_shape, grp, pipeline_mode=once),
                      pl.BlockSpec(in_shape, grp, pipeline_mode=once),
                      pl.BlockSpec(out_shape, grp, pipeline_mode=once),
                      pl.BlockSpec((1, D_MODEL), fixed)],
            out_specs=pl.BlockSpec((MOE_TILE, D_MODEL), lambda j, tg, nt: (j, 0)),
            scratch_shapes=[pltpu.VMEM(in_shape, BF16), pltpu.VMEM(in_shape, BF16),
                            pltpu.VMEM(out_shape, BF16)]),
        out_shape=jax.ShapeDtypeStruct((n_tiles * MOE_TILE, D_MODEL), F32),
        compiler_params=pltpu.CompilerParams(dimension_semantics=("arbitrary",),
                                             vmem_limit_bytes=VMEM_LIMIT),
        name="moe_groups",
    )(tile_group, n_tiles_used, rows, wnf, wg, wu, wd, wfin)


def _ffn_sorted(g_out, a_out, x, wo, wnf, wr, wg, wu, wd, wfin):
    n = x.shape[0]
    rows, info, cnt = _ffn_pre(g_out, a_out, x, wo, wnf, wr, 512)
    grp, rank = info[0].astype(jnp.int32), info[1].astype(jnp.int32)
    tiles = (cnt[0, :N_GROUPS].astype(jnp.int32) + MOE_TILE - 1) // MOE_TILE
    ends = jnp.cumsum(tiles)
    dest = (ends - tiles)[grp] * MOE_TILE + rank
    n_tiles = n // MOE_TILE + N_GROUPS
    tile_group = jnp.minimum(
        jnp.sum(jnp.arange(n_tiles, dtype=jnp.int32)[:, None] >= ends[None, :], axis=1),
        N_GROUPS - 1).astype(jnp.int32)
    rows_sorted = _move_rows(dest, rows, n_tiles * MOE_TILE, True)
    y_sorted = _moe_groups(tile_group, ends[N_GROUPS - 1:], rows_sorted, wnf, wg, wu, wd, wfin)
    return _move_rows(dest, y_sorted, n, False)


def kernel(x_prompt, x_sample, cache_k, cache_v, state_conv, state_gdn, page_table, meta_tokens,
           w_norm_mix, w_in, w_conv, a_log, dt_bias, w_gnorm, lambda_q1, lambda_k1, lambda_q2,
           lambda_k2, w_subln, w_out, w_norm_ffn, w_group, w_expert, w_gate, w_up, w_down, w_final):
    l = 0
    bp, seq = x_prompt.shape[0], x_prompt.shape[1]
    bs = x_sample.shape[0]
    n_pool, page = cache_k.shape[1], cache_k.shape[2]
    past = page_table.shape[1] * page
    C = GDN_CHUNK

    w_in_p = _prep_w_in(w_in[l])
    wn_mix = w_norm_mix[l][None, :]
    pad_lanes = lambda v: jnp.pad(v, (0, LANES - v.shape[0]))[None, :]
    alog, dtb = pad_lanes(a_log[l]), pad_lanes(dt_bias[l])
    wgn = w_gnorm[l][None, :]
    lam_vecs = jnp.stack([lambda_q1[l], lambda_k1[l], lambda_q2[l], lambda_k2[l]])
    wsub = w_subln[l][None, :]
    w_router = jnp.pad(jnp.concatenate([w_expert[l], w_group[l]], axis=1),
                       ((0, 0), (0, LANES - N_EXPERTS - N_GROUPS)))
    ffn_w = (w_norm_ffn[l][None, :], w_router, w_gate[l], w_up[l], w_down[l], w_final[None, :])

    pos_small = jnp.concatenate([jnp.arange(N_META), jnp.full((bs,), past)])
    rows_small = jnp.concatenate([meta_tokens, x_sample[:, 0, :]], axis=0)
    c_s, z_s, ab_s, q_s, k_s, v_s = _proj(rows_small, wn_mix, w_in_p, _rope_tables(pos_small),
                                           N_META + bs)
    xp = x_prompt.reshape(bp * seq, D_MODEL)
    c_p, z_p, ab_p, q_p, k_p, v_p = _proj(xp, wn_mix, w_in_p.astype(BF16),
                                           _rope_tables(N_META + jnp.arange(seq)), 512)

    front = lambda a: jnp.pad(a[:N_META], ((C - N_META, 0), (0, 0)))[None]
    zero_tail = jnp.zeros((1, 8, CONV_CH), F32)
    zero_state = jnp.zeros((1, GDN_HEADS, GDN_DK, GDN_DV), F32)
    _, s_meta = _gdn(front(c_s), front(z_s), front(ab_s), zero_tail, zero_state,
                     w_conv[l], alog, dtb, wgn, C)
    tail_meta = jnp.pad(c_s[N_META - (CONV_W - 1):N_META], ((8 - (CONV_W - 1), 0), (0, 0)))[None]
    c_p3 = c_p.reshape(bp, seq, CONV_CH)
    g_out, p_gdn = _gdn(c_p3, z_p.reshape(bp, seq, GDN_V_W), ab_p.reshape(bp, seq, AB_W),
                        tail_meta, s_meta, w_conv[l], alog, dtb, wgn, 512)
    p_conv = c_p3[:, seq - (CONV_W - 1):, :]

    p_k, p_v = _kv_out(k_p.reshape(bp, seq, DA_QK_W), v_p.reshape(bp, seq, DA_V_W),
                       k_s[:N_META], v_s[:N_META])
    meta_blk = lambda m: jnp.pad(m[:N_META], ((0, LANES - N_META), (0, 0)))
    a_out = _attn_prompt(q_p.reshape(bp, seq, DA_QK_W), k_p.reshape(bp, seq, DA_QK_W),
                         v_p.reshape(bp, seq, DA_V_W), meta_blk(k_s), meta_blk(v_s),
                         lam_vecs, wsub, 256, 512, 2)

    y_prompt = _ffn_sorted(g_out.reshape(bp * seq, GDN_V_W), a_out.reshape(bp * seq, DA_V_W), xp,
                           w_out[l].astype(BF16), *ffn_w).reshape(bp, seq, D_MODEL)

    sm = lambda a: a[N_META:]
    g_s, s_gdn = _gdn_step(sm(c_s), sm(z_s), sm(ab_s), jnp.swapaxes(state_conv[l], 0, 1),
                           state_gdn[l], w_conv[l], alog, dtb, wgn)
    s_conv = jnp.concatenate([state_conv[l][:, 1:, :], sm(c_s)[:, None, :]], axis=1)
    heads = lambda a: sm(a).reshape(bs, DA_HEADS, DA_DV)
    a_s = _attn_sample(heads(q_s), heads(k_s), heads(v_s),
                       cache_k[l].reshape(n_pool, page * DA_HEADS, 2 * DA_DQK),
                       cache_v[l].reshape(n_pool, page * DA_HEADS, DA_DV),
                       page_table, lam_vecs, wsub, 32)
    y_sample = _ffn(g_s, a_s.reshape(bs, DA_V_W), x_sample[:, 0, :], w_out[l], *ffn_w,
                    bs).reshape(bs, 1, D_MODEL)

    return (y_prompt, y_sample, p_conv[None], p_gdn[None],
            p_k.reshape(1, bp, seq + N_META, DA_HEADS, 2 * DA_DQK),
            p_v.reshape(1, bp, seq + N_META, DA_HEADS, DA_DV),
            s_conv[None], s_gdn[None],
            sm(k_s).reshape(1, bs, 1, DA_HEADS, 2 * DA_DQK),
            sm(v_s).reshape(1, bs, 1, DA_HEADS, DA_DV))
```

```python
import functools
import math

import jax
import jax.numpy as jnp
import numpy as np
from jax import lax
from jax.experimental import pallas as pl
from jax.experimental.pallas import tpu as pltpu

F32 = jnp.float32
BF16 = jnp.bfloat16

D_MODEL = 1024
N_META = 16
EPS = 1e-6
GDN_HEADS = 4
GDN_DK = 128
GDN_DV = 128
CONV_W = 4
GDN_QK_W = GDN_HEADS * GDN_DK
GDN_V_W = GDN_HEADS * GDN_DV
CONV_CH = 2 * GDN_QK_W + GDN_V_W
DA_HEADS = 4
DA_DQK = 64
DA_DV = 2 * DA_DQK
DA_QK_W = DA_HEADS * 2 * DA_DQK
DA_V_W = DA_HEADS * DA_DV
ROPE_DIM = DA_DQK // 4
ROPE_HALF = ROPE_DIM // 2
ROPE_THETA = 500000.0
N_GROUPS = 4
EXPERTS_PER_GROUP = 8
N_EXPERTS = N_GROUPS * EXPERTS_PER_GROUP
D_EXPERT = 256
LAM_INIT = 0.8 - 0.6 * math.exp(-0.3 * 0)

LANES = 128
GDN_CHUNK = 64
GDN_STACK = 2
AB_W = LANES
PROJ_PAD_W = CONV_CH + GDN_V_W + 2 * DA_QK_W + DA_V_W + AB_W
VMEM_LIMIT = 56 * 1024 * 1024


_NN = (((1,), (0,)), ((), ()))
_NT = (((1,), (1,)), ((), ()))
_TN = (((0,), (0,)), ((), ()))


def _mm(a, b, dims=_NN, precise=False):
    if precise:
        return lax.dot_general(a.astype(F32), b.astype(F32), dims, preferred_element_type=F32,
                               precision=lax.Precision.HIGHEST)
    return lax.dot_general(a.astype(BF16), b.astype(BF16), dims, preferred_element_type=F32)


def _dot_nt(a, b):
    return _mm(a, b, _NT)


def _dot_f32(a, b):
    return _mm(a, b, _NN, True)


def _mask_dot(mask, x, dims):
    hi = x.astype(BF16)
    mid = (x - hi.astype(F32)).astype(BF16)
    lo = x - hi.astype(F32) - mid.astype(F32)
    m = mask.astype(BF16)
    return (_mm(m, hi, dims) + _mm(m, mid, dims)) + _mm(m, lo, dims)


def _sigmoid(x):
    return 1.0 / (1.0 + jnp.exp(-x))


def _softplus(x):
    return jnp.maximum(x, 0.0) + jnp.log(1.0 + jnp.exp(-jnp.abs(x)))


def _rms(x, w):
    return x * lax.rsqrt(jnp.mean(x * x, axis=-1, keepdims=True) + EPS) * w


def _rope(x, cs, sn_lo, sn_hi):
    w = x.shape[1]
    cs = jnp.concatenate([cs] * DA_HEADS, axis=1)
    sn_lo = jnp.concatenate([sn_lo] * DA_HEADS, axis=1)
    sn_hi = jnp.concatenate([sn_hi] * DA_HEADS, axis=1)
    up = pltpu.roll(x, w - ROPE_HALF, 1)
    dn = pltpu.roll(x, ROPE_HALF, 1)
    return x * cs + up * sn_lo + dn * sn_hi


def _proj_kernel(x_ref, wn_ref, w_ref, cs_ref, snl_ref, snh_ref,
                 c_ref, z_ref, ab_ref, q_ref, k_ref, v_ref, *, precise):
    h = _rms(x_ref[...], wn_ref[...])
    h = h if precise else h.astype(BF16)
    o = 0
    c_ref[...] = _mm(h, w_ref[:, o:o + CONV_CH], _NN, precise)
    o += CONV_CH
    z_ref[...] = _mm(h, w_ref[:, o:o + GDN_V_W], _NN, precise)
    o += GDN_V_W
    q = _mm(h, w_ref[:, o:o + DA_QK_W], _NN, precise)
    o += DA_QK_W
    k = _mm(h, w_ref[:, o:o + DA_QK_W], _NN, precise)
    o += DA_QK_W
    v_ref[...] = _mm(h, w_ref[:, o:o + DA_V_W], _NN, precise)
    o += DA_V_W
    ab_ref[...] = _mm(h, w_ref[:, o:o + AB_W], _NN, precise)
    cs, snl, snh = cs_ref[...], snl_ref[...], snh_ref[...]
    q_ref[...] = _rope(q, cs, snl, snh)
    k_ref[...] = _rope(k, cs, snl, snh)


def _proj(x, wn, w, tabs, tm):
    n = x.shape[0]
    assert n % tm == 0
    nt = tabs[0].shape[0] // tm
    row = lambda i: (i, 0)
    fixed = lambda i: (0, 0)
    tab = lambda i: (i % nt, 0)
    widths = (CONV_CH, GDN_V_W, AB_W, DA_QK_W, DA_QK_W, DA_V_W)
    return pl.pallas_call(
        functools.partial(_proj_kernel, precise=w.dtype == F32),
        grid=(n // tm,),
        in_specs=[pl.BlockSpec((tm, D_MODEL), row),
                  pl.BlockSpec((1, D_MODEL), fixed),
                  pl.BlockSpec((D_MODEL, PROJ_PAD_W), fixed),
                  pl.BlockSpec((tm, LANES), tab),
                  pl.BlockSpec((tm, LANES), tab),
                  pl.BlockSpec((tm, LANES), tab)],
        out_specs=[pl.BlockSpec((tm, wd), row) for wd in widths],
        out_shape=[jax.ShapeDtypeStruct((n, wd), F32) for wd in widths],
        compiler_params=pltpu.CompilerParams(dimension_semantics=("parallel",),
                                             vmem_limit_bytes=VMEM_LIMIT),
        name="proj",
    )(x, wn, w, *tabs)


def _rope_tables(pos):
    inv_freq = ROPE_THETA ** (-jnp.arange(ROPE_HALF, dtype=F32) * 2.0 / ROPE_DIM)
    ang = pos.astype(F32)[:, None] * inv_freq[None, :]
    cos, sin = jnp.cos(ang), jnp.sin(ang)
    t = pos.shape[0]
    one = jnp.ones((t, DA_DQK - ROPE_DIM), F32)
    zero = jnp.zeros((t, DA_DQK - ROPE_DIM), F32)
    zh = jnp.zeros((t, ROPE_HALF), F32)
    cs = jnp.concatenate([cos, cos, one] * 2, axis=1)
    snl = jnp.concatenate([-sin, zh, zero] * 2, axis=1)
    snh = jnp.concatenate([zh, sin, zero] * 2, axis=1)
    return cs, snl, snh


def _prep_w_in(w_in):
    c, z, a, b, dq, dk, dv = jnp.split(
        w_in, np.cumsum([CONV_CH, GDN_V_W, GDN_HEADS, GDN_HEADS, DA_QK_W, DA_QK_W]).tolist(), axis=-1)
    pad = jnp.zeros((D_MODEL, AB_W - 2 * GDN_HEADS), w_in.dtype)
    return jnp.concatenate([c, z, dq, dk, dv, a, b, pad], axis=-1)


def _gdn_kernel(c_ref, z_ref, ab_ref, tail_ref, s0_ref, wconv_ref, alog_ref, dtb_ref, wg_ref,
                o_ref, sfin_ref, cbuf, ybuf, s_scr, u0_scr, wq_scr, kw_scr, qkd_scr, el_scr, *, tt):
    C, H = GDN_CHUNK, GDN_HEADS
    HS = GDN_STACK
    R = HS * C
    shift = C.bit_length() - 1
    t = pl.program_id(1)

    @pl.when(t == 0)
    def _():
        cbuf[0:8, :] = tail_ref[0]
        s_scr[...] = s0_ref[0]

    cbuf[8:8 + tt, :] = c_ref[0]
    acc = cbuf[8:8 + tt, :] * wconv_ref[CONV_W - 1:CONV_W, :]
    for i in range(CONV_W - 1):
        off = 8 - (CONV_W - 1) + i
        acc = acc + cbuf[off:off + tt, :] * wconv_ref[i:i + 1, :]
    ybuf[...] = acc * _sigmoid(acc)
    cbuf[0:8, :] = cbuf[tt:tt + 8, :]

    ii = lax.broadcasted_iota(jnp.int32, (R, R), 0)
    jj = lax.broadcasted_iota(jnp.int32, (R, R), 1)
    same_head = jnp.right_shift(ii, shift) == jnp.right_shift(jj, shift)
    strict = same_head & (ii > jj)
    causal = same_head & (ii >= jj)
    eye = (ii == jj).astype(F32)
    merge_masks = []
    s = 1
    while s < C:
        sh = s.bit_length() - 1
        merge_masks.append((jnp.right_shift(ii, sh + 1) == jnp.right_shift(jj, sh + 1))
                           & (jnp.right_shift(ii, sh) != jnp.right_shift(jj, sh)) & (ii > jj))
        s *= 2
    ci_ = lax.broadcasted_iota(jnp.int32, (C, C), 0)
    cj_ = lax.broadcasted_iota(jnp.int32, (C, C), 1)
    tri = (ci_ >= cj_).astype(F32)
    lane = lax.broadcasted_iota(jnp.int32, (R, LANES), 1)
    rhead = jnp.right_shift(lax.broadcasted_iota(jnp.int32, (R, LANES), 0), shift)
    ones8 = jnp.ones((8, LANES), F32)
    rep = lambda a: jnp.concatenate([a] * HS, axis=0)
    pick = lambda a, off: jnp.sum(jnp.where(lane == rhead + off, a, 0.0), axis=-1, keepdims=True)

    def stacked(rows, base, scale, h0):
        parts = []
        for h in range(h0, h0 + HS):
            x = ybuf[rows, base + h * GDN_DK:base + (h + 1) * GDN_DK]
            if scale is not None:
                x = x * lax.rsqrt(jnp.sum(x * x, axis=-1, keepdims=True) + EPS) * scale
            parts.append(x)
        return jnp.concatenate(parts, axis=0)

    n_stack = H // HS
    units = range((tt // C) * n_stack)
    low, rhs, qes = [], [], []
    for ci in range(tt // C):
        rows = slice(ci * C, (ci + 1) * C)
        ab = ab_ref[0, rows, :]
        gfull = -jnp.exp(alog_ref[...]) * _softplus(ab + dtb_ref[...])
        bfull = _sigmoid(ab)
        gcum = _mask_dot(tri, gfull, _NN)
        glast = gcum[C - 1:C, :]
        el_scr[ci] = jnp.broadcast_to(jnp.exp(glast), (8, LANES))
        for p in range(n_stack):
            un, h0 = ci * n_stack + p, p * HS
            g_m = jnp.where(lane == rhead + h0, rep(gcum), 0.0)
            g_col = jnp.sum(g_m, axis=-1, keepdims=True)
            g_row = _mask_dot(ones8, g_m, _NT)[0:1, :]
            beta = pick(rep(bfull), H + h0)
            gl_col = pick(jnp.broadcast_to(glast, (R, LANES)), h0)
            decay = jnp.exp(jnp.where(causal, g_col - g_row, -jnp.inf))
            q = stacked(rows, 0, GDN_DK ** -0.5, h0)
            k = stacked(rows, GDN_QK_W, 1.0, h0)
            v = stacked(rows, 2 * GDN_QK_W, None, h0)
            kq = _mm(jnp.concatenate([k, q], axis=0), k, _NT)
            low.append(jnp.where(strict, kq[0:R] * decay * beta, 0.0))
            e_g = jnp.exp(g_col)
            rhs.append(jnp.concatenate([beta * v, (beta * e_g) * k], axis=1).astype(BF16))
            qes.append(q * e_g)
            kw_scr[un] = (k * jnp.exp(gl_col - g_col)).astype(BF16)
            qkd_scr[un] = jnp.where(causal, kq[R:2 * R] * decay, 0.0).astype(BF16)
    tinv = [eye - jnp.where(merge_masks[0], low[un], 0.0) for un in units]
    for m in merge_masks[1:]:
        half = [_mm(tinv[un], jnp.where(m, low[un], 0.0)) for un in units]
        tinv = [tinv[un] - _mm(half[un], tinv[un]) for un in units]
    for un in units:
        w2 = _mm(tinv[un], rhs[un])
        u0_scr[un] = w2[:, 0:GDN_DV]
        wk = w2[:, GDN_DV:]
        wq_scr[un] = jnp.concatenate(
            [x[h * C:(h + 1) * C] for h in range(HS) for x in (wk, qes[un])], axis=0).astype(BF16)

    st = [s_scr[h] for h in range(H)]
    for ci in range(tt // C):
        rows = slice(ci * C, (ci + 1) * C)
        el = el_scr[ci]
        for p in range(n_stack):
            un = ci * n_stack + p
            us, os_ = [], []
            for hh in range(HS):
                r = jnp.dot(wq_scr[un, 2 * hh * C:2 * (hh + 1) * C, :], st[p * HS + hh].astype(BF16),
                            preferred_element_type=F32)
                us.append(u0_scr[un, hh * C:(hh + 1) * C, :] - r[0:C])
                os_.append(r[C:2 * C])
            intra = jnp.dot(qkd_scr[un], jnp.concatenate(us, axis=0).astype(BF16),
                            preferred_element_type=F32)
            for hh in range(HS):
                h = p * HS + hh
                st[h] = (st[h] * el[0:1, h:h + 1]
                         + _mm(kw_scr[un, hh * C:(hh + 1) * C, :], us[hh], _TN))
                o = os_[hh] + intra[hh * C:(hh + 1) * C]
                zz = z_ref[0, rows, h * GDN_DV:(h + 1) * GDN_DV]
                o_ref[0, rows, h * GDN_DV:(h + 1) * GDN_DV] = (_rms(o, wg_ref[...])
                                                              * (zz * _sigmoid(zz)))
    for h in range(H):
        s_scr[h] = st[h]

    @pl.when(t == pl.num_programs(1) - 1)
    def _():
        sfin_ref[0] = s_scr[...]


def _gdn_step_kernel(c_ref, z_ref, ab_ref, cs_ref, s_ref, wconv_ref, alog_ref, dtb_ref, wg_ref,
                     o_ref, snew_ref, *, nb):
    H = GDN_HEADS
    acc = c_ref[...] * wconv_ref[CONV_W - 1:CONV_W, :]
    for i in range(CONV_W - 1):
        acc = acc + cs_ref[i] * wconv_ref[i:i + 1, :]
    y = acc * _sigmoid(acc)
    ab = ab_ref[...]
    e_g = jnp.exp(-jnp.exp(alog_ref[...]) * _softplus(ab + dtb_ref[...]))
    bfull = _sigmoid(ab)
    unit = lambda x: x * lax.rsqrt(jnp.sum(x * x, axis=-1, keepdims=True) + EPS)
    qs = [unit(y[:, h * GDN_DK:(h + 1) * GDN_DK]) * (GDN_DK ** -0.5) for h in range(H)]
    ks = [unit(y[:, GDN_QK_W + h * GDN_DK:GDN_QK_W + (h + 1) * GDN_DK]) for h in range(H)]
    vs = [y[:, 2 * GDN_QK_W + h * GDN_DV:2 * GDN_QK_W + (h + 1) * GDN_DV] for h in range(H)]
    cols = jnp.concatenate(ks + qs, axis=0).T
    for h in range(H):
        ks_rows, qs_rows = [], []
        for s in range(nb):
            state = s_ref[s, h]
            kcol = cols[:, h * nb + s:h * nb + s + 1]
            qcol = cols[:, (H + h) * nb + s:(H + h) * nb + s + 1]
            ks_rows.append(jnp.sum(kcol * state, axis=0, keepdims=True))
            qs_rows.append(jnp.sum(qcol * state, axis=0, keepdims=True))
        eg_h = e_g[:, h:h + 1]
        u = bfull[:, H + h:H + h + 1] * (vs[h] - eg_h * jnp.concatenate(ks_rows, axis=0))
        o = (eg_h * jnp.concatenate(qs_rows, axis=0)
             + jnp.sum(qs[h] * ks[h], axis=-1, keepdims=True) * u)
        eg_b = jnp.broadcast_to(eg_h, (nb, GDN_DV))
        for s in range(nb):
            kcol = cols[:, h * nb + s:h * nb + s + 1]
            snew_ref[s, h] = eg_b[s:s + 1, :] * s_ref[s, h] + kcol * u[s:s + 1, :]
        zz = z_ref[:, h * GDN_DV:(h + 1) * GDN_DV]
        o_ref[:, h * GDN_DV:(h + 1) * GDN_DV] = _rms(o, wg_ref[...]) * (zz * _sigmoid(zz))


def _gdn_step(c, z, ab, conv_state, s, wconv, alog, dtb, wg):
    bs = c.shape[0]
    nb = LANES // (2 * GDN_HEADS)
    assert bs % nb == 0
    row = lambda i: (i, 0)
    fixed = lambda i: (0, 0)
    state = pl.BlockSpec((nb, GDN_HEADS, GDN_DK, GDN_DV), lambda i: (i, 0, 0, 0))
    return pl.pallas_call(
        functools.partial(_gdn_step_kernel, nb=nb),
        grid=(bs // nb,),
        in_specs=[pl.BlockSpec((nb, CONV_CH), row),
                  pl.BlockSpec((nb, GDN_V_W), row),
                  pl.BlockSpec((nb, AB_W), row),
                  pl.BlockSpec((CONV_W - 1, nb, CONV_CH), lambda i: (0, i, 0)),
                  state,
                  pl.BlockSpec((CONV_W, CONV_CH), fixed),
                  pl.BlockSpec((1, LANES), fixed),
                  pl.BlockSpec((1, LANES), fixed),
                  pl.BlockSpec((1, GDN_DV), fixed)],
        out_specs=[pl.BlockSpec((nb, GDN_V_W), row), state],
        out_shape=[jax.ShapeDtypeStruct((bs, GDN_V_W), F32),
                   jax.ShapeDtypeStruct(s.shape, F32)],
        compiler_params=pltpu.CompilerParams(dimension_semantics=("parallel",),
                                             vmem_limit_bytes=VMEM_LIMIT),
        name="gdn_step",
    )(c, z, ab, conv_state, s, wconv, alog, dtb, wg)


def _gdn(c, z, ab, tail, s0, wconv, alog, dtb, wg, tt):
    b, t, _ = c.shape
    assert t % tt == 0 and tt % GDN_CHUNK == 0
    rs = GDN_STACK * GDN_CHUNK
    nc, nu = tt // GDN_CHUNK, (tt // GDN_CHUNK) * (GDN_HEADS // GDN_STACK)
    seq = lambda i, j: (i, j, 0)
    fixed2 = lambda i, j: (0, 0)
    tail_map = (lambda i, j: (i, 0, 0)) if tail.shape[0] == b else (lambda i, j: (0, 0, 0))
    s0_map = (lambda i, j: (i, 0, 0, 0)) if s0.shape[0] == b else (lambda i, j: (0, 0, 0, 0))
    return pl.pallas_call(
        functools.partial(_gdn_kernel, tt=tt),
        grid=(b, t // tt),
        in_specs=[pl.BlockSpec((1, tt, CONV_CH), seq),
                  pl.BlockSpec((1, tt, GDN_V_W), seq),
                  pl.BlockSpec((1, tt, AB_W), seq),
                  pl.BlockSpec((1, 8, CONV_CH), tail_map),
                  pl.BlockSpec((1, GDN_HEADS, GDN_DK, GDN_DV), s0_map),
                  pl.BlockSpec((CONV_W, CONV_CH), fixed2),
                  pl.BlockSpec((1, LANES), fixed2),
                  pl.BlockSpec((1, LANES), fixed2),
                  pl.BlockSpec((1, GDN_DV), fixed2)],
        out_specs=[pl.BlockSpec((1, tt, GDN_V_W), seq),
                   pl.BlockSpec((1, GDN_HEADS, GDN_DK, GDN_DV), lambda i, j: (i, 0, 0, 0))],
        out_shape=[jax.ShapeDtypeStruct((b, t, GDN_V_W), F32),
                   jax.ShapeDtypeStruct((b, GDN_HEADS, GDN_DK, GDN_DV), F32)],
        scratch_shapes=[pltpu.VMEM((tt + 8, CONV_CH), F32),
                        pltpu.VMEM((tt, CONV_CH), F32),
                        pltpu.VMEM((GDN_HEADS, GDN_DK, GDN_DV), F32),
                        pltpu.VMEM((nu, rs, GDN_DV), F32),
                        pltpu.VMEM((nu, 2 * rs, GDN_DK), BF16),
                        pltpu.VMEM((nu, rs, GDN_DK), BF16),
                        pltpu.VMEM((nu, rs, rs), BF16),
                        pltpu.VMEM((nc, 8, LANES), F32)],
        compiler_params=pltpu.CompilerParams(dimension_semantics=("parallel", "arbitrary"),
                                             vmem_limit_bytes=VMEM_LIMIT),
        name="gdn",
    )(c, z, ab, tail, s0, wconv, alog, dtb, wg)


def _kv_out_kernel(k_ref, v_ref, km_ref, vm_ref, ko_ref, vo_ref):
    t = k_ref.shape[1]
    for src, meta, dst in ((k_ref, km_ref, ko_ref), (v_ref, vm_ref, vo_ref)):
        for h in range(DA_HEADS):
            cols = slice(h * DA_DV, (h + 1) * DA_DV)
            dst[0, pl.ds(h, N_META, stride=DA_HEADS), :] = meta[:, cols]
            dst[0, pl.ds(N_META * DA_HEADS + h, t, stride=DA_HEADS), :] = src[0, :, cols]


def _kv_out(k, v, k_meta, v_meta):
    b, t, w = k.shape
    rows = (t + N_META) * DA_HEADS
    seq = pl.BlockSpec((1, t, w), lambda i: (i, 0, 0))
    meta = pl.BlockSpec((N_META, w), lambda i: (0, 0))
    out = pl.BlockSpec((1, rows, DA_DV), lambda i: (i, 0, 0))
    return pl.pallas_call(
        _kv_out_kernel,
        grid=(b,),
        in_specs=[seq, seq, meta, meta],
        out_specs=[out, out],
        out_shape=[jax.ShapeDtypeStruct((b, rows, DA_DV), F32)] * 2,
        compiler_params=pltpu.CompilerParams(dimension_semantics=("parallel",),
                                             vmem_limit_bytes=VMEM_LIMIT),
        name="kv_out",
    )(k, v, k_meta, v_meta)


def _lambda(lam_ref):
    lv = lam_ref[...]
    s1 = jnp.sum(lv[0:1, :] * lv[1:2, :], axis=-1, keepdims=True)
    s2 = jnp.sum(lv[2:3, :] * lv[3:4, :], axis=-1, keepdims=True)
    return jnp.exp(s1) - jnp.exp(s2) + LAM_INIT


def _split_components(q):
    lane = lax.broadcasted_iota(jnp.int32, q.shape, 1)
    first = (lane % DA_DV) < DA_DQK
    return jnp.where(first, q, 0.0).astype(BF16), jnp.where(first, 0.0, q).astype(BF16)


def _online_update(state, s, vb):
    m, l, acc = state
    m_new = jnp.maximum(m, jnp.max(s, axis=-1, keepdims=True))
    p = jnp.exp2(s - m_new)
    alpha = jnp.exp2(m - m_new)
    return (m_new, alpha * l + jnp.sum(p, axis=-1, keepdims=True),
            alpha * acc + jnp.dot(p.astype(BF16), vb, preferred_element_type=F32))


def _attn_prompt_kernel(q_ref, k_ref, v_ref, km_ref, vm_ref, lam_ref, wsub_ref, o_ref, *, bq, bk):
    qi = pl.program_id(2)
    heads = [slice(h * DA_DV, (h + 1) * DA_DV) for h in range(q_ref.shape[2] // DA_DV)]
    qs = [_split_components(q_ref[0, :, hs] * (DA_DQK ** -0.5 * math.log2(math.e))) for hs in heads]
    neg = jnp.full((bq, 1), -jnp.inf, F32)
    zero = jnp.zeros((bq, 1), F32)
    zacc = jnp.zeros((bq, DA_DV), F32)
    init = tuple(((neg, zero, zacc), (neg, zero, zacc)) for _ in heads)

    def block(kblk, vblk, mask, st):
        out = []
        for (q1, q2), hs, sth in zip(qs, heads, st):
            kb, vb = kblk[:, hs].astype(BF16), vblk[:, hs].astype(BF16)
            s1, s2 = _dot_nt(q1, kb), _dot_nt(q2, kb)
            if mask is not None:
                s1 = jnp.where(mask, s1, -jnp.inf)
                s2 = jnp.where(mask, s2, -jnp.inf)
            out.append((_online_update(sth[0], s1, vb), _online_update(sth[1], s2, vb)))
        return tuple(out)

    n_full = lax.shift_right_logical(qi * bq, bk.bit_length() - 1)

    def body(kb, st):
        r0 = pl.multiple_of(kb * bk, bk)
        return block(k_ref[0, pl.ds(r0, bk), :], v_ref[0, pl.ds(r0, bk), :], None, st)

    st = lax.fori_loop(0, n_full, body, init)
    r0 = pl.multiple_of(n_full * bk, bk)
    kcat = jnp.concatenate([km_ref[...], k_ref[0, pl.ds(r0, bk), :]], axis=0)
    vcat = jnp.concatenate([vm_ref[...], v_ref[0, pl.ds(r0, bk), :]], axis=0)
    ri = lax.broadcasted_iota(jnp.int32, (bq, LANES + bk), 0)
    ci = lax.broadcasted_iota(jnp.int32, (bq, LANES + bk), 1)
    seen = (ci < N_META) | ((ci >= LANES) & (ci - LANES <= ri + (qi * bq - n_full * bk)))
    st = block(kcat, vcat, seen, st)
    lam = _lambda(lam_ref)
    for hs, ((m1, l1, a1), (m2, l2, a2)) in zip(heads, st):
        o = a1 / l1 - lam * (a2 / l2)
        o_ref[0, :, hs] = _rms(o, wsub_ref[...]) * (1.0 - LAM_INIT)


def _attn_prompt(q, k, v, k_meta, v_meta, lam_vecs, wsub, bq, bk, hps):
    b, t, _ = q.shape
    assert t % bk == 0 and bk % bq == 0 and k_meta.shape[0] == LANES and DA_HEADS % hps == 0
    w = hps * DA_DV
    return pl.pallas_call(
        functools.partial(_attn_prompt_kernel, bq=bq, bk=bk),
        grid=(b, DA_HEADS // hps, t // bq),
        in_specs=[pl.BlockSpec((1, bq, w), lambda i, h, j: (i, j, h)),
                  pl.BlockSpec((1, t, w), lambda i, h, j: (i, 0, h)),
                  pl.BlockSpec((1, t, w), lambda i, h, j: (i, 0, h)),
                  pl.BlockSpec((LANES, w), lambda i, h, j: (0, h)),
                  pl.BlockSpec((LANES, w), lambda i, h, j: (0, h)),
                  pl.BlockSpec((4, DA_DQK), lambda i, h, j: (0, 0)),
                  pl.BlockSpec((1, DA_DV), lambda i, h, j: (0, 0))],
        out_specs=pl.BlockSpec((1, bq, w), lambda i, h, j: (i, j, h)),
        out_shape=jax.ShapeDtypeStruct((b, t, DA_V_W), F32),
        compiler_params=pltpu.CompilerParams(
            dimension_semantics=("parallel", "parallel", "arbitrary"),
            vmem_limit_bytes=VMEM_LIMIT),
        name="attn_prompt",
    )(q, k, v, k_meta, v_meta, lam_vecs, wsub)


def _attn_sample_kernel(pt_ref, q_ref, kn_ref, vn_ref, lam_ref, wsub_ref, *rest, pps):
    k_refs, v_refs = rest[:pps], rest[pps:2 * pps]
    o_ref, m_scr, l_scr, acc_scr = rest[2 * pps:]
    step = pl.program_id(1)
    nrow = 2 * DA_HEADS
    twice = lambda x: jnp.concatenate([x, x], axis=0)
    rowq = lax.broadcasted_iota(jnp.int32, (nrow, DA_DV), 0)
    laneq = lax.broadcasted_iota(jnp.int32, (nrow, DA_DV), 1)
    own_comp = (laneq // DA_DQK) == (rowq // DA_HEADS)
    qm = jnp.where(own_comp, twice(q_ref[0]) * (DA_DQK ** -0.5), 0.0)
    page_rows = k_refs[0].shape[1]
    rowp = lax.broadcasted_iota(jnp.int32, (nrow, page_rows * pps), 0)
    colp = lax.broadcasted_iota(jnp.int32, (nrow, page_rows * pps), 1)
    own_head = (colp % DA_HEADS) == (rowp % DA_HEADS)

    def rows3(x):
        hi = x.astype(BF16).astype(F32)
        mid = (x - hi).astype(BF16).astype(F32)
        return jnp.concatenate([hi, mid, x - hi - mid], axis=0).astype(BF16)

    def unrows3(r):
        return (r[0:nrow] + r[nrow:2 * nrow]) + r[2 * nrow:3 * nrow]

    def hi_lo(x):
        hi = x.astype(BF16)
        return hi, (x - hi.astype(F32)).astype(BF16)

    @pl.when(step == 0)
    def _():
        s_new = jnp.sum(qm * twice(kn_ref[0]), axis=-1, keepdims=True)
        m_scr[...] = s_new
        l_scr[...] = jnp.ones_like(s_new)
        acc_scr[...] = twice(vn_ref[0])

    q3 = rows3(qm)
    scores = []
    for r in k_refs:
        k_hi, k_lo = hi_lo(r[0])
        scores.append(unrows3(_dot_nt(q3, k_hi) + _dot_nt(q3, k_lo)))
    s = jnp.where(own_head, jnp.concatenate(scores, axis=1), -jnp.inf)
    m_old = m_scr[...]
    m_new = jnp.maximum(m_old, jnp.max(s, axis=-1, keepdims=True))
    p = jnp.exp(s - m_new)
    alpha = jnp.exp(m_old - m_new)
    p3 = rows3(p)
    pv = jnp.zeros((3 * nrow, DA_DV), F32)
    for j, r in enumerate(v_refs):
        v_hi, v_lo = hi_lo(r[0])
        pj = p3[:, j * page_rows:(j + 1) * page_rows]
        pv = pv + (jnp.dot(pj, v_hi, preferred_element_type=F32)
                   + jnp.dot(pj, v_lo, preferred_element_type=F32))
    m_scr[...] = m_new
    l_scr[...] = alpha * l_scr[...] + jnp.sum(p, axis=-1, keepdims=True)
    acc_scr[...] = alpha * acc_scr[...] + unrows3(pv)

    @pl.when(step == pl.num_programs(1) - 1)
    def _():
        w = acc_scr[...] / l_scr[...]
        o = w[0:DA_HEADS, :] - _lambda(lam_ref) * w[DA_HEADS:nrow, :]
        o_ref[0] = _rms(o, wsub_ref[...]) * (1.0 - LAM_INIT)


def _attn_sample(q, k_new, v_new, cache_k, cache_v, page_table, lam_vecs, wsub, pps):
    bs = q.shape[0]
    n_pages = page_table.shape[1]
    page_rows = cache_k.shape[1]
    assert n_pages % pps == 0
    tok = lambda i, s, pt: (i, 0, 0)
    fixed = lambda i, s, pt: (0, 0)

    def page_spec(j):
        return pl.BlockSpec((1, page_rows, DA_DV), lambda i, s, pt: (pt[i, s * pps + j], 0, 0))

    grid_spec = pltpu.PrefetchScalarGridSpec(
        num_scalar_prefetch=1,
        grid=(bs, n_pages // pps),
        in_specs=[pl.BlockSpec((1, DA_HEADS, DA_DV), tok),
                  pl.BlockSpec((1, DA_HEADS, DA_DV), tok),
                  pl.BlockSpec((1, DA_HEADS, DA_DV), tok),
                  pl.BlockSpec((4, DA_DQK), fixed),
                  pl.BlockSpec((1, DA_DV), fixed)]
                 + [page_spec(j) for j in range(pps)] * 2,
        out_specs=pl.BlockSpec((1, DA_HEADS, DA_DV), tok),
        scratch_shapes=[pltpu.VMEM((2 * DA_HEADS, 1), F32),
                        pltpu.VMEM((2 * DA_HEADS, 1), F32),
                        pltpu.VMEM((2 * DA_HEADS, DA_DV), F32)])
    return pl.pallas_call(
        functools.partial(_attn_sample_kernel, pps=pps),
        grid_spec=grid_spec,
        out_shape=jax.ShapeDtypeStruct((bs, DA_HEADS, DA_DV), F32),
        compiler_params=pltpu.CompilerParams(dimension_semantics=("parallel", "arbitrary"),
                                             vmem_limit_bytes=VMEM_LIMIT),
        name="attn_sample",
    )(page_table, q, k_new, v_new, lam_vecs, wsub, *([cache_k] * pps), *([cache_v] * pps))


def _route(logits):
    lane = lax.broadcasted_iota(jnp.int32, logits.shape, 1).astype(F32)
    big = float(LANES)
    ninf = -jnp.inf

    def top(mask):
        val = jnp.where(mask, logits, ninf)
        mx = jnp.max(val, axis=-1, keepdims=True)
        idx = jnp.min(jnp.where(mask & (val == mx), lane, big), axis=-1, keepdims=True)
        return mx, idx

    gmask = (lane >= N_EXPERTS) & (lane < N_EXPERTS + N_GROUPS)
    gmax, gidx = top(gmask)
    g_w = 1.0 / jnp.sum(jnp.where(gmask, jnp.exp(logits - gmax), 0.0), axis=-1, keepdims=True)
    g_sel = gidx - N_EXPERTS
    emask = (lane >= g_sel * EXPERTS_PER_GROUP) & (lane < (g_sel + 1) * EXPERTS_PER_GROUP)
    v1, i1 = top(emask)
    v2, i2 = top(emask & (lane != i1))
    e21 = jnp.exp(v2 - v1)
    p1 = 1.0 / (1.0 + e21)
    return jnp.where(lane == i1, p1 * g_w, 0.0) + jnp.where(lane == i2, e21 * p1 * g_w, 0.0), g_sel


def _mix_route(g_ref, a_ref, x_ref, wo_ref, wnf_ref, wr_ref, precise):
    mix = (_mm(g_ref[...], wo_ref[0:GDN_V_W, :], _NN, precise)
           + _mm(a_ref[...], wo_ref[GDN_V_W:, :], _NN, precise))
    x1 = x_ref[...] + mix
    xn = _rms(x1, wnf_ref[...])
    wr = wr_ref[...]
    if precise:
        logits = _dot_f32(xn, wr)
    else:
        x_hi, w_hi = xn.astype(BF16), wr.astype(BF16)
        x_lo, w_lo = (xn - x_hi.astype(F32)).astype(BF16), (wr - w_hi.astype(F32)).astype(BF16)
        both = jnp.dot(x_hi, jnp.concatenate([w_hi, w_lo], axis=1), preferred_element_type=F32)
        logits = (both[:, 0:LANES] + both[:, LANES:]) + jnp.dot(x_lo, w_hi,
                                                                 preferred_element_type=F32)
    comb, g_sel = _route(logits)
    return x1, xn, comb, g_sel


def _expert(xn, comb, e_lane, wg, wu, wd):
    hg = jnp.dot(xn, wg.astype(BF16), preferred_element_type=F32)
    hu = jnp.dot(xn, wu.astype(BF16), preferred_element_type=F32)
    lane = lax.broadcasted_iota(jnp.int32, comb.shape, 1)
    cw = jnp.sum(jnp.where(lane == e_lane, comb, 0.0), axis=-1, keepdims=True)
    hh = (hg * _sigmoid(hg)) * hu * cw
    return jnp.dot(hh.astype(BF16), wd.astype(BF16), preferred_element_type=F32)


def _ffn_kernel(g_ref, a_ref, x_ref, wo_ref, wnf_ref, wr_ref, wg_ref, wu_ref, wd_ref, wfin_ref,
                y_ref, x1_scr, xn_scr, comb_scr, acc_scr, *, precise):
    e = pl.program_id(1)

    @pl.when(e == 0)
    def _():
        x1, xn, comb, _ = _mix_route(g_ref, a_ref, x_ref, wo_ref, wnf_ref, wr_ref, precise)
        x1_scr[...] = x1
        xn_scr[...] = xn.astype(BF16)
        comb_scr[...] = comb
        acc_scr[...] = jnp.zeros_like(acc_scr)

    acc_scr[...] += _expert(xn_scr[...], comb_scr[...], e, wg_ref[0], wu_ref[0], wd_ref[0])

    @pl.when(e == pl.num_programs(1) - 1)
    def _():
        y_ref[...] = _rms(x1_scr[...] + acc_scr[...], wfin_ref[...])


def _ffn(g_out, a_out, x, wo, wnf, wr, wg, wu, wd, wfin, tm):
    n = x.shape[0]
    assert n % tm == 0
    row = lambda i, e: (i, 0)
    fixed = lambda i, e: (0, 0)
    exp = lambda i, e: (e, 0, 0)
    return pl.pallas_call(
        functools.partial(_ffn_kernel, precise=wo.dtype == F32),
        grid=(n // tm, N_EXPERTS),
        in_specs=[pl.BlockSpec((tm, GDN_V_W), row),
                  pl.BlockSpec((tm, DA_V_W), row),
                  pl.BlockSpec((tm, D_MODEL), row),
                  pl.BlockSpec((GDN_V_W + DA_V_W, D_MODEL), fixed),
                  pl.BlockSpec((1, D_MODEL), fixed),
                  pl.BlockSpec((D_MODEL, LANES), fixed),
                  pl.BlockSpec((1, D_MODEL, D_EXPERT), exp),
                  pl.BlockSpec((1, D_MODEL, D_EXPERT), exp),
                  pl.BlockSpec((1, D_EXPERT, D_MODEL), exp),
                  pl.BlockSpec((1, D_MODEL), fixed)],
        out_specs=pl.BlockSpec((tm, D_MODEL), row),
        out_shape=jax.ShapeDtypeStruct((n, D_MODEL), F32),
        scratch_shapes=[pltpu.VMEM((tm, D_MODEL), F32),
                        pltpu.VMEM((tm, D_MODEL), BF16),
                        pltpu.VMEM((tm, LANES), F32),
                        pltpu.VMEM((tm, D_MODEL), F32)],
        compiler_params=pltpu.CompilerParams(dimension_semantics=("parallel", "arbitrary"),
                                             vmem_limit_bytes=VMEM_LIMIT),
        name="ffn",
    )(g_out, a_out, x, wo, wnf, wr, wg, wu, wd, wfin)


ROW_W = D_MODEL + LANES
MOE_TILE = 512
ROW_TILE = 1024


def _ffn_pre_kernel(g_ref, a_ref, x_ref, wo_ref, wnf_ref, wr_ref, rows_ref, info_ref, cnt_ref,
                    cnt_scr):
    @pl.when(pl.program_id(0) == 0)
    def _():
        cnt_scr[...] = jnp.zeros_like(cnt_scr)

    x1, _, comb, g_sel = _mix_route(g_ref, a_ref, x_ref, wo_ref, wnf_ref, wr_ref, False)
    tm = x1.shape[0]
    lane = lax.broadcasted_iota(jnp.int32, (tm, LANES), 1).astype(F32)
    onehot = lane == g_sel
    ri = lax.broadcasted_iota(jnp.int32, (tm, tm), 0)
    ci = lax.broadcasted_iota(jnp.int32, (tm, tm), 1)
    earlier = jnp.dot((ci < ri).astype(BF16), onehot.astype(BF16), preferred_element_type=F32)
    rank = jnp.sum(jnp.where(onehot, earlier + cnt_scr[...], 0.0), axis=-1, keepdims=True)
    cnt_scr[...] += jnp.sum(onehot.astype(F32), axis=0, keepdims=True)
    cnt_ref[...] = cnt_scr[...]
    rows_ref[:, 0:D_MODEL] = x1
    rows_ref[:, D_MODEL:] = comb
    packed = jnp.where(lane == 0.0, g_sel, jnp.where(lane == 1.0, rank, 0.0))
    r8 = lax.broadcasted_iota(jnp.int32, (8, LANES), 0)
    l8 = lax.broadcasted_iota(jnp.int32, (8, LANES), 1)
    sel = (r8 == l8).astype(BF16)
    hi = packed.astype(BF16)
    info_ref[...] = _dot_nt(sel, hi) + _dot_nt(sel, packed - hi.astype(F32))


def _ffn_pre(g_out, a_out, x, wo, wnf, wr, tm):
    n = x.shape[0]
    assert n % tm == 0
    row = lambda i: (i, 0)
    fixed = lambda i: (0, 0)
    return pl.pallas_call(
        _ffn_pre_kernel,
        grid=(n // tm,),
        in_specs=[pl.BlockSpec((tm, GDN_V_W), row),
                  pl.BlockSpec((tm, DA_V_W), row),
                  pl.BlockSpec((tm, D_MODEL), row),
                  pl.BlockSpec((GDN_V_W + DA_V_W, D_MODEL), fixed),
                  pl.BlockSpec((1, D_MODEL), fixed),
                  pl.BlockSpec((D_MODEL, LANES), fixed)],
        out_specs=[pl.BlockSpec((tm, ROW_W), row),
                   pl.BlockSpec((8, tm), lambda i: (0, i)),
                   pl.BlockSpec((1, LANES), fixed)],
        out_shape=[jax.ShapeDtypeStruct((n, ROW_W), F32),
                   jax.ShapeDtypeStruct((8, n), F32),
                   jax.ShapeDtypeStruct((1, LANES), F32)],
        scratch_shapes=[pltpu.VMEM((1, LANES), F32)],
        compiler_params=pltpu.CompilerParams(dimension_semantics=("arbitrary",),
                                             vmem_limit_bytes=VMEM_LIMIT),
        name="ffn_pre",
    )(g_out, a_out, x, wo, wnf, wr)


def _copy_rows(idx_ref, tile_ref, hbm_ref, sem, scatter):
    tm = tile_ref.shape[0]
    base = pl.program_id(0) * tm

    def body(r, carry):
        there, here = hbm_ref.at[pl.ds(idx_ref[base + r], 1)], tile_ref.at[pl.ds(r, 1)]
        src, dst = (here, there) if scatter else (there, here)
        pltpu.make_async_copy(src, dst, sem).start()
        return carry

    lax.fori_loop(0, tm, body, 0, unroll=8)
    there, here = hbm_ref.at[pl.ds(0, tm)], tile_ref
    src, dst = (here, there) if scatter else (there, here)
    pltpu.make_async_copy(src, dst, sem).wait()


def _scatter_kernel(idx_ref, tile_ref, init_ref, dst_ref, sem):
    del init_ref
    _copy_rows(idx_ref, tile_ref, dst_ref, sem, True)


def _gather_kernel(idx_ref, src_ref, tile_ref, sem):
    _copy_rows(idx_ref, tile_ref, src_ref, sem, False)


def _move_rows(idx, src, n_out, scatter):
    n, w = idx.shape[0], src.shape[1]
    assert n % ROW_TILE == 0
    any_spec = pl.BlockSpec(memory_space=pl.ANY)
    tile_spec = pl.BlockSpec((ROW_TILE, w), lambda i, idx: (i, 0))
    if scatter:
        operands = (idx, src, jnp.zeros((n_out, w), src.dtype))
        in_specs, out_spec = [tile_spec, any_spec], any_spec
    else:
        operands = (idx, src)
        in_specs, out_spec = [any_spec], tile_spec
    return pl.pallas_call(
        _scatter_kernel if scatter else _gather_kernel,
        grid_spec=pltpu.PrefetchScalarGridSpec(
            num_scalar_prefetch=1, grid=(n // ROW_TILE,),
            in_specs=in_specs, out_specs=out_spec,
            scratch_shapes=[pltpu.SemaphoreType.DMA(())]),
        out_shape=jax.ShapeDtypeStruct((n_out, w), src.dtype),
        input_output_aliases={2: 0} if scatter else {},
        compiler_params=pltpu.CompilerParams(dimension_semantics=("arbitrary",)),
        name="scatter_rows" if scatter else "gather_rows",
    )(*operands)


def _moe_groups_kernel(tg_ref, nt_ref, rows_ref, wnf_ref, wg_ref, wu_ref, wd_ref, wfin_ref, y_ref,
                       wg_bf, wu_bf, wd_bf):
    j = pl.program_id(0)

    @pl.when((j == 0) | (tg_ref[j] != tg_ref[jnp.maximum(j - 1, 0)]))
    def _():
        for e in range(EXPERTS_PER_GROUP):
            wg_bf[e] = wg_ref[e].astype(BF16)
            wu_bf[e] = wu_ref[e].astype(BF16)
            wd_bf[e] = wd_ref[e].astype(BF16)

    @pl.when(j < nt_ref[0])
    def _():
        x1 = rows_ref[:, 0:D_MODEL]
        comb = rows_ref[:, D_MODEL:]
        xn = _rms(x1, wnf_ref[...]).astype(BF16)
        first = tg_ref[j] * EXPERTS_PER_GROUP
        acc = jnp.zeros_like(x1)
        for e in range(EXPERTS_PER_GROUP):
            acc = acc + _expert(xn, comb, first + e, wg_bf[e], wu_bf[e], wd_bf[e])
        y_ref[...] = _rms(x1 + acc, wfin_ref[...])

    @pl.when(j >= nt_ref[0])
    def _():
        y_ref[...] = jnp.zeros_like(y_ref)


def _moe_groups(tile_group, n_tiles_used, rows, wnf, wg, wu, wd, wfin):
    n_tiles = tile_group.shape[0]
    fixed = lambda j, tg, nt: (0, 0)
    grp = lambda j, tg, nt: (tg[j], 0, 0)
    in_shape, out_shape = (EXPERTS_PER_GROUP, D_MODEL, D_EXPERT), (EXPERTS_PER_GROUP, D_EXPERT, D_MODEL)
    once = pl.Buffered(1)
    return pl.pallas_call(
        _moe_groups_kernel,
        grid_spec=pltpu.PrefetchScalarGridSpec(
            num_scalar_prefetch=2, grid=(n_tiles,),
            in_specs=[pl.BlockSpec((MOE_TILE, ROW_W), lambda j, tg, nt: (j, 0)),
                      pl.BlockSpec((1, D_MODEL), fixed),
                      pl.BlockSpec(in_shape, grp, pipeline_mode=once),
                      pl.BlockSpec(in_shape, grp, pipeline_mode=once),
                      pl.BlockSpec(out_shape, grp, pipeline_mode=once),
                      pl.BlockSpec((1, D_MODEL), fixed)],
            out_specs=pl.BlockSpec((MOE_TILE, D_MODEL), lambda j, tg, nt: (j, 0)),
            scratch_shapes=[pltpu.VMEM(in_shape, BF16), pltpu.VMEM(in_shape, BF16),
                            pltpu.VMEM(out_shape, BF16)]),
        out_shape=jax.ShapeDtypeStruct((n_tiles * MOE_TILE, D_MODEL), F32),
        compiler_params=pltpu.CompilerParams(dimension_semantics=("arbitrary",),
                                             vmem_limit_bytes=VMEM_LIMIT),
        name="moe_groups",
    )(tile_group, n_tiles_used, rows, wnf, wg, wu, wd, wfin)


def _ffn_sorted(g_out, a_out, x, wo, wnf, wr, wg, wu, wd, wfin):
    n = x.shape[0]
    rows, info, cnt = _ffn_pre(g_out, a_out, x, wo, wnf, wr, 512)
    grp, rank = info[0].astype(jnp.int32), info[1].astype(jnp.int32)
    tiles = (cnt[0, :N_GROUPS].astype(jnp.int32) + MOE_TILE - 1) // MOE_TILE
    ends = jnp.cumsum(tiles)
    dest = (ends - tiles)[grp] * MOE_TILE + rank
    n_tiles = n // MOE_TILE + N_GROUPS
    tile_group = jnp.minimum(
        jnp.sum(jnp.arange(n_tiles, dtype=jnp.int32)[:, None] >= ends[None, :], axis=1),
        N_GROUPS - 1).astype(jnp.int32)
    rows_sorted = _move_rows(dest, rows, n_tiles * MOE_TILE, True)
    y_sorted = _moe_groups(tile_group, ends[N_GROUPS - 1:], rows_sorted, wnf, wg, wu, wd, wfin)
    return _move_rows(dest, y_sorted, n, False)


def kernel(x_prompt, x_sample, cache_k, cache_v, state_conv, state_gdn, page_table, meta_tokens,
           w_norm_mix, w_in, w_conv, a_log, dt_bias, w_gnorm, lambda_q1, lambda_k1, lambda_q2,
           lambda_k2, w_subln, w_out, w_norm_ffn, w_group, w_expert, w_gate, w_up, w_down, w_final):
    l = 0
    bp, seq = x_prompt.shape[0], x_prompt.shape[1]
    bs = x_sample.shape[0]
    n_pool, page = cache_k.shape[1], cache_k.shape[2]
    past = page_table.shape[1] * page
    C = GDN_CHUNK

    w_in_p = _prep_w_in(w_in[l])
    wn_mix = w_norm_mix[l][None, :]
    pad_lanes = lambda v: jnp.pad(v, (0, LANES - v.shape[0]))[None, :]
    alog, dtb = pad_lanes(a_log[l]), pad_lanes(dt_bias[l])
    wgn = w_gnorm[l][None, :]
    lam_vecs = jnp.stack([lambda_q1[l], lambda_k1[l], lambda_q2[l], lambda_k2[l]])
    wsub = w_subln[l][None, :]
    w_router = jnp.pad(jnp.concatenate([w_expert[l], w_group[l]], axis=1),
                       ((0, 0), (0, LANES - N_EXPERTS - N_GROUPS)))
    ffn_w = (w_norm_ffn[l][None, :], w_router, w_gate[l], w_up[l], w_down[l], w_final[None, :])

    pos_small = jnp.concatenate([jnp.arange(N_META), jnp.full((bs,), past)])
    rows_small = jnp.concatenate([meta_tokens, x_sample[:, 0, :]], axis=0)
    c_s, z_s, ab_s, q_s, k_s, v_s = _proj(rows_small, wn_mix, w_in_p, _rope_tables(pos_small),
                                           N_META + bs)
    xp = x_prompt.reshape(bp * seq, D_MODEL)
    c_p, z_p, ab_p, q_p, k_p, v_p = _proj(xp, wn_mix, w_in_p.astype(BF16),
                                           _rope_tables(N_META + jnp.arange(seq)), 512)

    front = lambda a: jnp.pad(a[:N_META], ((C - N_META, 0), (0, 0)))[None]
    zero_tail = jnp.zeros((1, 8, CONV_CH), F32)
    zero_state = jnp.zeros((1, GDN_HEADS, GDN_DK, GDN_DV), F32)
    _, s_meta = _gdn(front(c_s), front(z_s), front(ab_s), zero_tail, zero_state,
                     w_conv[l], alog, dtb, wgn, C)
    tail_meta = jnp.pad(c_s[N_META - (CONV_W - 1):N_META], ((8 - (CONV_W - 1), 0), (0, 0)))[None]
    c_p3 = c_p.reshape(bp, seq, CONV_CH)
    g_out, p_gdn = _gdn(c_p3, z_p.reshape(bp, seq, GDN_V_W), ab_p.reshape(bp, seq, AB_W),
                        tail_meta, s_meta, w_conv[l], alog, dtb, wgn, 512)
    p_conv = c_p3[:, seq - (CONV_W - 1):, :]

    p_k, p_v = _kv_out(k_p.reshape(bp, seq, DA_QK_W), v_p.reshape(bp, seq, DA_V_W),
                       k_s[:N_META], v_s[:N_META])
    meta_blk = lambda m: jnp.pad(m[:N_META], ((0, LANES - N_META), (0, 0)))
    a_out = _attn_prompt(q_p.reshape(bp, seq, DA_QK_W), k_p.reshape(bp, seq, DA_QK_W),
                         v_p.reshape(bp, seq, DA_V_W), meta_blk(k_s), meta_blk(v_s),
                         lam_vecs, wsub, 256, 512, 2)

    y_prompt = _ffn_sorted(g_out.reshape(bp * seq, GDN_V_W), a_out.reshape(bp * seq, DA_V_W), xp,
                           w_out[l].astype(BF16), *ffn_w).reshape(bp, seq, D_MODEL)

    sm = lambda a: a[N_META:]
    g_s, s_gdn = _gdn_step(sm(c_s), sm(z_s), sm(ab_s), jnp.swapaxes(state_conv[l], 0, 1),
                           state_gdn[l], w_conv[l], alog, dtb, wgn)
    s_conv = jnp.concatenate([state_conv[l][:, 1:, :], sm(c_s)[:, None, :]], axis=1)
    heads = lambda a: sm(a).reshape(bs, DA_HEADS, DA_DV)
    a_s = _attn_sample(heads(q_s), heads(k_s), heads(v_s),
                       cache_k[l].reshape(n_pool, page * DA_HEADS, 2 * DA_DQK),
                       cache_v[l].reshape(n_pool, page * DA_HEADS, DA_DV),
                       page_table, lam_vecs, wsub, 32)
    y_sample = _ffn(g_s, a_s.reshape(bs, DA_V_W), x_sample[:, 0, :], w_out[l], *ffn_w,
                    bs).reshape(bs, 1, D_MODEL)

    return (y_prompt, y_sample, p_conv[None], p_gdn[None],
            p_k.reshape(1, bp, seq + N_META, DA_HEADS, 2 * DA_DQK),
            p_v.reshape(1, bp, seq + N_META, DA_HEADS, DA_DV),
            s_conv[None], s_gdn[None],
            sm(k_s).reshape(1, bs, 1, DA_HEADS, 2 * DA_DQK),
            sm(v_s).reshape(1, bs, 1, DA_HEADS, DA_DV))
```

```python
import functools
import math

import jax
import jax.numpy as jnp
import numpy as np
from jax import lax
from jax.experimental import pallas as pl
from jax.experimental.pallas import tpu as pltpu

F32 = jnp.float32
BF16 = jnp.bfloat16

D_MODEL = 1024
N_META = 16
EPS = 1e-6
GDN_HEADS = 4
GDN_DK = 128
GDN_DV = 128
CONV_W = 4
GDN_QK_W = GDN_HEADS * GDN_DK
GDN_V_W = GDN_HEADS * GDN_DV
CONV_CH = 2 * GDN_QK_W + GDN_V_W
DA_HEADS = 4
DA_DQK = 64
DA_DV = 2 * DA_DQK
DA_QK_W = DA_HEADS * 2 * DA_DQK
DA_V_W = DA_HEADS * DA_DV
ROPE_DIM = DA_DQK // 4
ROPE_HALF = ROPE_DIM // 2
ROPE_THETA = 500000.0
N_GROUPS = 4
EXPERTS_PER_GROUP = 8
N_EXPERTS = N_GROUPS * EXPERTS_PER_GROUP
D_EXPERT = 256
LAM_INIT = 0.8 - 0.6 * math.exp(-0.3 * 0)

LANES = 128
GDN_CHUNK = 64
GDN_STACK = 2
AB_W = LANES
PROJ_PAD_W = CONV_CH + GDN_V_W + 2 * DA_QK_W + DA_V_W + AB_W
VMEM_LIMIT = 56 * 1024 * 1024

PROJ_TILE = 512
GDN_TILE = 512
ATTN_BQ, ATTN_BK = 256, 512
ATTN_HEADS_PER_STEP = 2
PAGES_PER_STEP = 32
FFN_PRE_TILE = 512


_NN = (((1,), (0,)), ((), ()))
_NT = (((1,), (1,)), ((), ()))
_TN = (((0,), (0,)), ((), ()))


def _mm(a, b, dims=_NN, precise=False):
    if precise:
        return lax.dot_general(a.astype(F32), b.astype(F32), dims, preferred_element_type=F32,
                               precision=lax.Precision.HIGHEST)
    return lax.dot_general(a.astype(BF16), b.astype(BF16), dims, preferred_element_type=F32)


def _dot_nt(a, b):
    return _mm(a, b, _NT)


def _dot_f32(a, b):
    return _mm(a, b, _NN, True)


def _mask_dot(mask, x, dims):
    hi = x.astype(BF16)
    mid = (x - hi.astype(F32)).astype(BF16)
    lo = x - hi.astype(F32) - mid.astype(F32)
    m = mask.astype(BF16)
    return (_mm(m, hi, dims) + _mm(m, mid, dims)) + _mm(m, lo, dims)


def _sigmoid(x):
    return 1.0 / (1.0 + jnp.exp(-x))


def _softplus(x):
    return jnp.maximum(x, 0.0) + jnp.log(1.0 + jnp.exp(-jnp.abs(x)))


def _rms(x, w):
    return x * lax.rsqrt(jnp.mean(x * x, axis=-1, keepdims=True) + EPS) * w


def _rope(x, cs, sn_lo, sn_hi):
    w = x.shape[1]
    cs = jnp.concatenate([cs] * DA_HEADS, axis=1)
    sn_lo = jnp.concatenate([sn_lo] * DA_HEADS, axis=1)
    sn_hi = jnp.concatenate([sn_hi] * DA_HEADS, axis=1)
    up = pltpu.roll(x, w - ROPE_HALF, 1)
    dn = pltpu.roll(x, ROPE_HALF, 1)
    return x * cs + up * sn_lo + dn * sn_hi


def _proj_kernel(x_ref, wn_ref, w_ref, cs_ref, snl_ref, snh_ref,
                 c_ref, z_ref, ab_ref, q_ref, k_ref, v_ref, *, precise):
    h = _rms(x_ref[...], wn_ref[...])
    h = h if precise else h.astype(BF16)
    o = 0
    c_ref[...] = _mm(h, w_ref[:, o:o + CONV_CH], _NN, precise)
    o += CONV_CH
    z_ref[...] = _mm(h, w_ref[:, o:o + GDN_V_W], _NN, precise)
    o += GDN_V_W
    q = _mm(h, w_ref[:, o:o + DA_QK_W], _NN, precise)
    o += DA_QK_W
    k = _mm(h, w_ref[:, o:o + DA_QK_W], _NN, precise)
    o += DA_QK_W
    v_ref[...] = _mm(h, w_ref[:, o:o + DA_V_W], _NN, precise)
    o += DA_V_W
    ab_ref[...] = _mm(h, w_ref[:, o:o + AB_W], _NN, precise)
    cs, snl, snh = cs_ref[...], snl_ref[...], snh_ref[...]
    q_ref[...] = _rope(q, cs, snl, snh)
    k_ref[...] = _rope(k, cs, snl, snh)


def _proj(x, wn, w, tabs, tm):
    n = x.shape[0]
    assert n % tm == 0
    nt = tabs[0].shape[0] // tm
    row = lambda i: (i, 0)
    fixed = lambda i: (0, 0)
    tab = lambda i: (i % nt, 0)
    widths = (CONV_CH, GDN_V_W, AB_W, DA_QK_W, DA_QK_W, DA_V_W)
    return pl.pallas_call(
        functools.partial(_proj_kernel, precise=w.dtype == F32),
        grid=(n // tm,),
        in_specs=[pl.BlockSpec((tm, D_MODEL), row),
                  pl.BlockSpec((1, D_MODEL), fixed),
                  pl.BlockSpec((D_MODEL, PROJ_PAD_W), fixed),
                  pl.BlockSpec((tm, LANES), tab),
                  pl.BlockSpec((tm, LANES), tab),
                  pl.BlockSpec((tm, LANES), tab)],
        out_specs=[pl.BlockSpec((tm, wd), row) for wd in widths],
        out_shape=[jax.ShapeDtypeStruct((n, wd), F32) for wd in widths],
        compiler_params=pltpu.CompilerParams(dimension_semantics=("parallel",),
                                             vmem_limit_bytes=VMEM_LIMIT),
        name="proj",
    )(x, wn, w, *tabs)


def _rope_tables(pos):
    inv_freq = ROPE_THETA ** (-jnp.arange(ROPE_HALF, dtype=F32) * 2.0 / ROPE_DIM)
    ang = pos.astype(F32)[:, None] * inv_freq[None, :]
    cos, sin = jnp.cos(ang), jnp.sin(ang)
    t = pos.shape[0]
    one = jnp.ones((t, DA_DQK - ROPE_DIM), F32)
    zero = jnp.zeros((t, DA_DQK - ROPE_DIM), F32)
    zh = jnp.zeros((t, ROPE_HALF), F32)
    cs = jnp.concatenate([cos, cos, one] * 2, axis=1)
    snl = jnp.concatenate([-sin, zh, zero] * 2, axis=1)
    snh = jnp.concatenate([zh, sin, zero] * 2, axis=1)
    return cs, snl, snh


def _prep_w_in(w_in):
    c, z, a, b, dq, dk, dv = jnp.split(
        w_in, np.cumsum([CONV_CH, GDN_V_W, GDN_HEADS, GDN_HEADS, DA_QK_W, DA_QK_W]).tolist(), axis=-1)
    pad = jnp.zeros((D_MODEL, AB_W - 2 * GDN_HEADS), w_in.dtype)
    return jnp.concatenate([c, z, dq, dk, dv, a, b, pad], axis=-1)


def _gdn_kernel(c_ref, z_ref, ab_ref, tail_ref, s0_ref, wconv_ref, alog_ref, dtb_ref, wg_ref,
                o_ref, sfin_ref, cbuf, ybuf, s_scr, u0_scr, wq_scr, kw_scr, qkd_scr, el_scr, *, tt):
    C, H = GDN_CHUNK, GDN_HEADS
    HS = GDN_STACK
    R = HS * C
    shift = C.bit_length() - 1
    t = pl.program_id(1)

    @pl.when(t == 0)
    def _():
        cbuf[0:8, :] = tail_ref[0]
        s_scr[...] = s0_ref[0]

    cbuf[8:8 + tt, :] = c_ref[0]
    acc = cbuf[8:8 + tt, :] * wconv_ref[CONV_W - 1:CONV_W, :]
    for i in range(CONV_W - 1):
        off = 8 - (CONV_W - 1) + i
        acc = acc + cbuf[off:off + tt, :] * wconv_ref[i:i + 1, :]
    ybuf[...] = acc * _sigmoid(acc)
    cbuf[0:8, :] = cbuf[tt:tt + 8, :]

    ii = lax.broadcasted_iota(jnp.int32, (R, R), 0)
    jj = lax.broadcasted_iota(jnp.int32, (R, R), 1)
    same_head = jnp.right_shift(ii, shift) == jnp.right_shift(jj, shift)
    strict = same_head & (ii > jj)
    causal = same_head & (ii >= jj)
    eye = (ii == jj).astype(F32)
    merge_masks = []
    s = 1
    while s < C:
        sh = s.bit_length() - 1
        merge_masks.append((jnp.right_shift(ii, sh + 1) == jnp.right_shift(jj, sh + 1))
                           & (jnp.right_shift(ii, sh) != jnp.right_shift(jj, sh)) & (ii > jj))
        s *= 2
    ci_ = lax.broadcasted_iota(jnp.int32, (C, C), 0)
    cj_ = lax.broadcasted_iota(jnp.int32, (C, C), 1)
    tri = (ci_ >= cj_).astype(F32)
    lane = lax.broadcasted_iota(jnp.int32, (R, LANES), 1)
    rhead = jnp.right_shift(lax.broadcasted_iota(jnp.int32, (R, LANES), 0), shift)
    ones8 = jnp.ones((8, LANES), F32)
    rep = lambda a: jnp.concatenate([a] * HS, axis=0)
    pick = lambda a, off: jnp.sum(jnp.where(lane == rhead + off, a, 0.0), axis=-1, keepdims=True)

    def stacked(rows, base, scale, h0):
        parts = []
        for h in range(h0, h0 + HS):
            x = ybuf[rows, base + h * GDN_DK:base + (h + 1) * GDN_DK]
            if scale is not None:
                x = x * lax.rsqrt(jnp.sum(x * x, axis=-1, keepdims=True) + EPS) * scale
            parts.append(x)
        return jnp.concatenate(parts, axis=0)

    n_stack = H // HS
    units = range((tt // C) * n_stack)
    low, rhs, qes = [], [], []
    for ci in range(tt // C):
        rows = slice(ci * C, (ci + 1) * C)
        ab = ab_ref[0, rows, :]
        gfull = -jnp.exp(alog_ref[...]) * _softplus(ab + dtb_ref[...])
        bfull = _sigmoid(ab)
        gcum = _mask_dot(tri, gfull, _NN)
        glast = gcum[C - 1:C, :]
        el_scr[ci] = jnp.broadcast_to(jnp.exp(glast), (8, LANES))
        for p in range(n_stack):
            un, h0 = ci * n_stack + p, p * HS
            g_m = jnp.where(lane == rhead + h0, rep(gcum), 0.0)
            g_col = jnp.sum(g_m, axis=-1, keepdims=True)
            g_row = _mask_dot(ones8, g_m, _NT)[0:1, :]
            beta = pick(rep(bfull), H + h0)
            gl_col = pick(jnp.broadcast_to(glast, (R, LANES)), h0)
            decay = jnp.exp(jnp.where(causal, g_col - g_row, -jnp.inf))
            q = stacked(rows, 0, GDN_DK ** -0.5, h0)
            k = stacked(rows, GDN_QK_W, 1.0, h0)
            v = stacked(rows, 2 * GDN_QK_W, None, h0)
            kq = _mm(jnp.concatenate([k, q], axis=0), k, _NT)
            low.append(jnp.where(strict, kq[0:R] * decay * beta, 0.0))
            e_g = jnp.exp(g_col)
            rhs.append(jnp.concatenate([beta * v, (beta * e_g) * k], axis=1).astype(BF16))
            qes.append(q * e_g)
            kw_scr[un] = (k * jnp.exp(gl_col - g_col)).astype(BF16)
            qkd_scr[un] = jnp.where(causal, kq[R:2 * R] * decay, 0.0).astype(BF16)
    tinv = [eye - jnp.where(merge_masks[0], low[un], 0.0) for un in units]
    for m in merge_masks[1:]:
        half = [_mm(tinv[un], jnp.where(m, low[un], 0.0)) for un in units]
        tinv = [tinv[un] - _mm(half[un], tinv[un]) for un in units]
    for un in units:
        w2 = _mm(tinv[un], rhs[un])
        u0_scr[un] = w2[:, 0:GDN_DV]
        wk = w2[:, GDN_DV:]
        wq_scr[un] = jnp.concatenate(
            [x[h * C:(h + 1) * C] for h in range(HS) for x in (wk, qes[un])], axis=0).astype(BF16)

    st = [s_scr[h] for h in range(H)]
    for ci in range(tt // C):
        rows = slice(ci * C, (ci + 1) * C)
        el = el_scr[ci]
        for p in range(n_stack):
            un = ci * n_stack + p
            us, os_ = [], []
            for hh in range(HS):
                r = jnp.dot(wq_scr[un, 2 * hh * C:2 * (hh + 1) * C, :], st[p * HS + hh].astype(BF16),
                            preferred_element_type=F32)
                us.append(u0_scr[un, hh * C:(hh + 1) * C, :] - r[0:C])
                os_.append(r[C:2 * C])
            intra = jnp.dot(qkd_scr[un], jnp.concatenate(us, axis=0).astype(BF16),
                            preferred_element_type=F32)
            for hh in range(HS):
                h = p * HS + hh
                st[h] = (st[h] * el[0:1, h:h + 1]
                         + _mm(kw_scr[un, hh * C:(hh + 1) * C, :], us[hh], _TN))
                o = os_[hh] + intra[hh * C:(hh + 1) * C]
                zz = z_ref[0, rows, h * GDN_DV:(h + 1) * GDN_DV]
                o_ref[0, rows, h * GDN_DV:(h + 1) * GDN_DV] = (_rms(o, wg_ref[...])
                                                              * (zz * _sigmoid(zz)))
    for h in range(H):
        s_scr[h] = st[h]

    @pl.when(t == pl.num_programs(1) - 1)
    def _():
        sfin_ref[0] = s_scr[...]


def _gdn_step_kernel(c_ref, z_ref, ab_ref, cs_ref, s_ref, wconv_ref, alog_ref, dtb_ref, wg_ref,
                     o_ref, snew_ref, *, nb):
    H = GDN_HEADS
    acc = c_ref[...] * wconv_ref[CONV_W - 1:CONV_W, :]
    for i in range(CONV_W - 1):
        acc = acc + cs_ref[i] * wconv_ref[i:i + 1, :]
    y = acc * _sigmoid(acc)
    ab = ab_ref[...]
    e_g = jnp.exp(-jnp.exp(alog_ref[...]) * _softplus(ab + dtb_ref[...]))
    bfull = _sigmoid(ab)
    unit = lambda x: x * lax.rsqrt(jnp.sum(x * x, axis=-1, keepdims=True) + EPS)
    qs = [unit(y[:, h * GDN_DK:(h + 1) * GDN_DK]) * (GDN_DK ** -0.5) for h in range(H)]
    ks = [unit(y[:, GDN_QK_W + h * GDN_DK:GDN_QK_W + (h + 1) * GDN_DK]) for h in range(H)]
    vs = [y[:, 2 * GDN_QK_W + h * GDN_DV:2 * GDN_QK_W + (h + 1) * GDN_DV] for h in range(H)]
    cols = jnp.concatenate(ks + qs, axis=0).T
    for h in range(H):
        ks_rows, qs_rows = [], []
        for s in range(nb):
            state = s_ref[s, h]
            kcol = cols[:, h * nb + s:h * nb + s + 1]
            qcol = cols[:, (H + h) * nb + s:(H + h) * nb + s + 1]
            ks_rows.append(jnp.sum(kcol * state, axis=0, keepdims=True))
            qs_rows.append(jnp.sum(qcol * state, axis=0, keepdims=True))
        eg_h = e_g[:, h:h + 1]
        u = bfull[:, H + h:H + h + 1] * (vs[h] - eg_h * jnp.concatenate(ks_rows, axis=0))
        o = (eg_h * jnp.concatenate(qs_rows, axis=0)
             + jnp.sum(qs[h] * ks[h], axis=-1, keepdims=True) * u)
        eg_b = jnp.broadcast_to(eg_h, (nb, GDN_DV))
        for s in range(nb):
            kcol = cols[:, h * nb + s:h * nb + s + 1]
            snew_ref[s, h] = eg_b[s:s + 1, :] * s_ref[s, h] + kcol * u[s:s + 1, :]
        zz = z_ref[:, h * GDN_DV:(h + 1) * GDN_DV]
        o_ref[:, h * GDN_DV:(h + 1) * GDN_DV] = _rms(o, wg_ref[...]) * (zz * _sigmoid(zz))


def _gdn_step(c, z, ab, conv_state, s, wconv, alog, dtb, wg):
    bs = c.shape[0]
    nb = LANES // (2 * GDN_HEADS)
    assert bs % nb == 0
    row = lambda i: (i, 0)
    fixed = lambda i: (0, 0)
    state = pl.BlockSpec((nb, GDN_HEADS, GDN_DK, GDN_DV), lambda i: (i, 0, 0, 0))
    return pl.pallas_call(
        functools.partial(_gdn_step_kernel, nb=nb),
        grid=(bs // nb,),
        in_specs=[pl.BlockSpec((nb, CONV_CH), row),
                  pl.BlockSpec((nb, GDN_V_W), row),
                  pl.BlockSpec((nb, AB_W), row),
                  pl.BlockSpec((CONV_W - 1, nb, CONV_CH), lambda i: (0, i, 0)),
                  state,
                  pl.BlockSpec((CONV_W, CONV_CH), fixed),
                  pl.BlockSpec((1, LANES), fixed),
                  pl.BlockSpec((1, LANES), fixed),
                  pl.BlockSpec((1, GDN_DV), fixed)],
        out_specs=[pl.BlockSpec((nb, GDN_V_W), row), state],
        out_shape=[jax.ShapeDtypeStruct((bs, GDN_V_W), F32),
                   jax.ShapeDtypeStruct(s.shape, F32)],
        compiler_params=pltpu.CompilerParams(dimension_semantics=("parallel",),
                                             vmem_limit_bytes=VMEM_LIMIT),
        name="gdn_step",
    )(c, z, ab, conv_state, s, wconv, alog, dtb, wg)


def _gdn(c, z, ab, tail, s0, wconv, alog, dtb, wg, tt):
    b, t, _ = c.shape
    assert t % tt == 0 and tt % GDN_CHUNK == 0
    rs = GDN_STACK * GDN_CHUNK
    nc, nu = tt // GDN_CHUNK, (tt // GDN_CHUNK) * (GDN_HEADS // GDN_STACK)
    seq = lambda i, j: (i, j, 0)
    fixed2 = lambda i, j: (0, 0)
    tail_map = (lambda i, j: (i, 0, 0)) if tail.shape[0] == b else (lambda i, j: (0, 0, 0))
    s0_map = (lambda i, j: (i, 0, 0, 0)) if s0.shape[0] == b else (lambda i, j: (0, 0, 0, 0))
    return pl.pallas_call(
        functools.partial(_gdn_kernel, tt=tt),
        grid=(b, t // tt),
        in_specs=[pl.BlockSpec((1, tt, CONV_CH), seq),
                  pl.BlockSpec((1, tt, GDN_V_W), seq),
                  pl.BlockSpec((1, tt, AB_W), seq),
                  pl.BlockSpec((1, 8, CONV_CH), tail_map),
                  pl.BlockSpec((1, GDN_HEADS, GDN_DK, GDN_DV), s0_map),
                  pl.BlockSpec((CONV_W, CONV_CH), fixed2),
                  pl.BlockSpec((1, LANES), fixed2),
                  pl.BlockSpec((1, LANES), fixed2),
                  pl.BlockSpec((1, GDN_DV), fixed2)],
        out_specs=[pl.BlockSpec((1, tt, GDN_V_W), seq),
                   pl.BlockSpec((1, GDN_HEADS, GDN_DK, GDN_DV), lambda i, j: (i, 0, 0, 0))],
        out_shape=[jax.ShapeDtypeStruct((b, t, GDN_V_W), F32),
                   jax.ShapeDtypeStruct((b, GDN_HEADS, GDN_DK, GDN_DV), F32)],
        scratch_shapes=[pltpu.VMEM((tt + 8, CONV_CH), F32),
                        pltpu.VMEM((tt, CONV_CH), F32),
                        pltpu.VMEM((GDN_HEADS, GDN_DK, GDN_DV), F32),
                        pltpu.VMEM((nu, rs, GDN_DV), F32),
                        pltpu.VMEM((nu, 2 * rs, GDN_DK), BF16),
                        pltpu.VMEM((nu, rs, GDN_DK), BF16),
                        pltpu.VMEM((nu, rs, rs), BF16),
                        pltpu.VMEM((nc, 8, LANES), F32)],
        compiler_params=pltpu.CompilerParams(dimension_semantics=("parallel", "arbitrary"),
                                             vmem_limit_bytes=VMEM_LIMIT),
        name="gdn",
    )(c, z, ab, tail, s0, wconv, alog, dtb, wg)


def _kv_out_kernel(k_ref, v_ref, km_ref, vm_ref, ko_ref, vo_ref):
    t = k_ref.shape[1]
    for src, meta, dst in ((k_ref, km_ref, ko_ref), (v_ref, vm_ref, vo_ref)):
        for h in range(DA_HEADS):
            cols = slice(h * DA_DV, (h + 1) * DA_DV)
            dst[0, pl.ds(h, N_META, stride=DA_HEADS), :] = meta[:, cols]
            dst[0, pl.ds(N_META * DA_HEADS + h, t, stride=DA_HEADS), :] = src[0, :, cols]


def _kv_out(k, v, k_meta, v_meta):
    b, t, w = k.shape
    rows = (t + N_META) * DA_HEADS
    seq = pl.BlockSpec((1, t, w), lambda i: (i, 0, 0))
    meta = pl.BlockSpec((N_META, w), lambda i: (0, 0))
    out = pl.BlockSpec((1, rows, DA_DV), lambda i: (i, 0, 0))
    return pl.pallas_call(
        _kv_out_kernel,
        grid=(b,),
        in_specs=[seq, seq, meta, meta],
        out_specs=[out, out],
        out_shape=[jax.ShapeDtypeStruct((b, rows, DA_DV), F32)] * 2,
        compiler_params=pltpu.CompilerParams(dimension_semantics=("parallel",),
                                             vmem_limit_bytes=VMEM_LIMIT),
        name="kv_out",
    )(k, v, k_meta, v_meta)


def _lambda(lam_ref):
    lv = lam_ref[...]
    s1 = jnp.sum(lv[0:1, :] * lv[1:2, :], axis=-1, keepdims=True)
    s2 = jnp.sum(lv[2:3, :] * lv[3:4, :], axis=-1, keepdims=True)
    return jnp.exp(s1) - jnp.exp(s2) + LAM_INIT


def _split_components(q):
    lane = lax.broadcasted_iota(jnp.int32, q.shape, 1)
    first = (lane % DA_DV) < DA_DQK
    return jnp.where(first, q, 0.0).astype(BF16), jnp.where(first, 0.0, q).astype(BF16)


def _online_update(state, s, vb):
    m, l, acc = state
    m_new = jnp.maximum(m, jnp.max(s, axis=-1, keepdims=True))
    p = jnp.exp2(s - m_new)
    alpha = jnp.exp2(m - m_new)
    return (m_new, alpha * l + jnp.sum(p, axis=-1, keepdims=True),
            alpha * acc + jnp.dot(p.astype(BF16), vb, preferred_element_type=F32))


def _attn_prompt_kernel(q_ref, k_ref, v_ref, km_ref, vm_ref, lam_ref, wsub_ref, o_ref, *, bq, bk):
    qi = pl.program_id(2)
    heads = [slice(h * DA_DV, (h + 1) * DA_DV) for h in range(q_ref.shape[2] // DA_DV)]
    qs = [_split_components(q_ref[0, :, hs] * (DA_DQK ** -0.5 * math.log2(math.e))) for hs in heads]
    neg = jnp.full((bq, 1), -jnp.inf, F32)
    zero = jnp.zeros((bq, 1), F32)
    zacc = jnp.zeros((bq, DA_DV), F32)
    init = tuple(((neg, zero, zacc), (neg, zero, zacc)) for _ in heads)

    def block(kblk, vblk, mask, st):
        out = []
        for (q1, q2), hs, sth in zip(qs, heads, st):
            kb, vb = kblk[:, hs].astype(BF16), vblk[:, hs].astype(BF16)
            s1, s2 = _dot_nt(q1, kb), _dot_nt(q2, kb)
            if mask is not None:
                s1 = jnp.where(mask, s1, -jnp.inf)
                s2 = jnp.where(mask, s2, -jnp.inf)
            out.append((_online_update(sth[0], s1, vb), _online_update(sth[1], s2, vb)))
        return tuple(out)

    n_full = lax.shift_right_logical(qi * bq, bk.bit_length() - 1)

    def body(kb, st):
        r0 = pl.multiple_of(kb * bk, bk)
        return block(k_ref[0, pl.ds(r0, bk), :], v_ref[0, pl.ds(r0, bk), :], None, st)

    st = lax.fori_loop(0, n_full, body, init)
    r0 = pl.multiple_of(n_full * bk, bk)
    kcat = jnp.concatenate([km_ref[...], k_ref[0, pl.ds(r0, bk), :]], axis=0)
    vcat = jnp.concatenate([vm_ref[...], v_ref[0, pl.ds(r0, bk), :]], axis=0)
    ri = lax.broadcasted_iota(jnp.int32, (bq, LANES + bk), 0)
    ci = lax.broadcasted_iota(jnp.int32, (bq, LANES + bk), 1)
    seen = (ci < N_META) | ((ci >= LANES) & (ci - LANES <= ri + (qi * bq - n_full * bk)))
    st = block(kcat, vcat, seen, st)
    lam = _lambda(lam_ref)
    for hs, ((m1, l1, a1), (m2, l2, a2)) in zip(heads, st):
        o = a1 / l1 - lam * (a2 / l2)
        o_ref[0, :, hs] = _rms(o, wsub_ref[...]) * (1.0 - LAM_INIT)


def _attn_prompt(q, k, v, k_meta, v_meta, lam_vecs, wsub, bq, bk, hps):
    b, t, _ = q.shape
    assert t % bk == 0 and bk % bq == 0 and k_meta.shape[0] == LANES and DA_HEADS % hps == 0
    w = hps * DA_DV
    return pl.pallas_call(
        functools.partial(_attn_prompt_kernel, bq=bq, bk=bk),
        grid=(b, DA_HEADS // hps, t // bq),
        in_specs=[pl.BlockSpec((1, bq, w), lambda i, h, j: (i, j, h)),
                  pl.BlockSpec((1, t, w), lambda i, h, j: (i, 0, h)),
                  pl.BlockSpec((1, t, w), lambda i, h, j: (i, 0, h)),
                  pl.BlockSpec((LANES, w), lambda i, h, j: (0, h)),
                  pl.BlockSpec((LANES, w), lambda i, h, j: (0, h)),
                  pl.BlockSpec((4, DA_DQK), lambda i, h, j: (0, 0)),
                  pl.BlockSpec((1, DA_DV), lambda i, h, j: (0, 0))],
        out_specs=pl.BlockSpec((1, bq, w), lambda i, h, j: (i, j, h)),
        out_shape=jax.ShapeDtypeStruct((b, t, DA_V_W), F32),
        compiler_params=pltpu.CompilerParams(
            dimension_semantics=("parallel", "parallel", "arbitrary"),
            vmem_limit_bytes=VMEM_LIMIT),
        name="attn_prompt",
    )(q, k, v, k_meta, v_meta, lam_vecs, wsub)


def _attn_sample_kernel(pt_ref, q_ref, kn_ref, vn_ref, lam_ref, wsub_ref, *rest, pps):
    k_refs, v_refs = rest[:pps], rest[pps:2 * pps]
    o_ref, m_scr, l_scr, acc_scr = rest[2 * pps:]
    step = pl.program_id(1)
    nrow = 2 * DA_HEADS
    twice = lambda x: jnp.concatenate([x, x], axis=0)
    rowq = lax.broadcasted_iota(jnp.int32, (nrow, DA_DV), 0)
    laneq = lax.broadcasted_iota(jnp.int32, (nrow, DA_DV), 1)
    own_comp = (laneq // DA_DQK) == (rowq // DA_HEADS)
    qm = jnp.where(own_comp, twice(q_ref[0]) * (DA_DQK ** -0.5), 0.0)
    page_rows = k_refs[0].shape[1]
    rowp = lax.broadcasted_iota(jnp.int32, (nrow, page_rows * pps), 0)
    colp = lax.broadcasted_iota(jnp.int32, (nrow, page_rows * pps), 1)
    own_head = (colp % DA_HEADS) == (rowp % DA_HEADS)

    def rows3(x):
        hi = x.astype(BF16).astype(F32)
        mid = (x - hi).astype(BF16).astype(F32)
        return jnp.concatenate([hi, mid, x - hi - mid], axis=0).astype(BF16)

    def unrows3(r):
        return (r[0:nrow] + r[nrow:2 * nrow]) + r[2 * nrow:3 * nrow]

    def hi_lo(x):
        hi = x.astype(BF16)
        return hi, (x - hi.astype(F32)).astype(BF16)

    @pl.when(step == 0)
    def _():
        s_new = jnp.sum(qm * twice(kn_ref[0]), axis=-1, keepdims=True)
        m_scr[...] = s_new
        l_scr[...] = jnp.ones_like(s_new)
        acc_scr[...] = twice(vn_ref[0])

    q3 = rows3(qm)
    scores = []
    for r in k_refs:
        k_hi, k_lo = hi_lo(r[0])
        scores.append(unrows3(_dot_nt(q3, k_hi) + _dot_nt(q3, k_lo)))
    s = jnp.where(own_head, jnp.concatenate(scores, axis=1), -jnp.inf)
    m_old = m_scr[...]
    m_new = jnp.maximum(m_old, jnp.max(s, axis=-1, keepdims=True))
    p = jnp.exp(s - m_new)
    alpha = jnp.exp(m_old - m_new)
    p3 = rows3(p)
    pv = jnp.zeros((3 * nrow, DA_DV), F32)
    for j, r in enumerate(v_refs):
        v_hi, v_lo = hi_lo(r[0])
        pj = p3[:, j * page_rows:(j + 1) * page_rows]
        pv = pv + (jnp.dot(pj, v_hi, preferred_element_type=F32)
                   + jnp.dot(pj, v_lo, preferred_element_type=F32))
    m_scr[...] = m_new
    l_scr[...] = alpha * l_scr[...] + jnp.sum(p, axis=-1, keepdims=True)
    acc_scr[...] = alpha * acc_scr[...] + unrows3(pv)

    @pl.when(step == pl.num_programs(1) - 1)
    def _():
        w = acc_scr[...] / l_scr[...]
        o = w[0:DA_HEADS, :] - _lambda(lam_ref) * w[DA_HEADS:nrow, :]
        o_ref[0] = _rms(o, wsub_ref[...]) * (1.0 - LAM_INIT)


def _attn_sample(q, k_new, v_new, cache_k, cache_v, page_table, lam_vecs, wsub, pps):
    bs = q.shape[0]
    n_pages = page_table.shape[1]
    page_rows = cache_k.shape[1]
    assert n_pages % pps == 0
    tok = lambda i, s, pt: (i, 0, 0)
    fixed = lambda i, s, pt: (0, 0)

    def page_spec(j):
        return pl.BlockSpec((1, page_rows, DA_DV), lambda i, s, pt: (pt[i, s * pps + j], 0, 0))

    grid_spec = pltpu.PrefetchScalarGridSpec(
        num_scalar_prefetch=1,
        grid=(bs, n_pages // pps),
        in_specs=[pl.BlockSpec((1, DA_HEADS, DA_DV), tok),
                  pl.BlockSpec((1, DA_HEADS, DA_DV), tok),
                  pl.BlockSpec((1, DA_HEADS, DA_DV), tok),
                  pl.BlockSpec((4, DA_DQK), fixed),
                  pl.BlockSpec((1, DA_DV), fixed)]
                 + [page_spec(j) for j in range(pps)] * 2,
        out_specs=pl.BlockSpec((1, DA_HEADS, DA_DV), tok),
        scratch_shapes=[pltpu.VMEM((2 * DA_HEADS, 1), F32),
                        pltpu.VMEM((2 * DA_HEADS, 1), F32),
                        pltpu.VMEM((2 * DA_HEADS, DA_DV), F32)])
    return pl.pallas_call(
        functools.partial(_attn_sample_kernel, pps=pps),
        grid_spec=grid_spec,
        out_shape=jax.ShapeDtypeStruct((bs, DA_HEADS, DA_DV), F32),
        compiler_params=pltpu.CompilerParams(dimension_semantics=("parallel", "arbitrary"),
                                             vmem_limit_bytes=VMEM_LIMIT),
        name="attn_sample",
    )(page_table, q, k_new, v_new, lam_vecs, wsub, *([cache_k] * pps), *([cache_v] * pps))


def _route(logits):
    lane = lax.broadcasted_iota(jnp.int32, logits.shape, 1).astype(F32)
    big = float(LANES)
    ninf = -jnp.inf

    def top(mask):
        val = jnp.where(mask, logits, ninf)
        mx = jnp.max(val, axis=-1, keepdims=True)
        idx = jnp.min(jnp.where(mask & (val == mx), lane, big), axis=-1, keepdims=True)
        return mx, idx

    gmask = (lane >= N_EXPERTS) & (lane < N_EXPERTS + N_GROUPS)
    gmax, gidx = top(gmask)
    g_w = 1.0 / jnp.sum(jnp.where(gmask, jnp.exp(logits - gmax), 0.0), axis=-1, keepdims=True)
    g_sel = gidx - N_EXPERTS
    emask = (lane >= g_sel * EXPERTS_PER_GROUP) & (lane < (g_sel + 1) * EXPERTS_PER_GROUP)
    v1, i1 = top(emask)
    v2, i2 = top(emask & (lane != i1))
    e21 = jnp.exp(v2 - v1)
    p1 = 1.0 / (1.0 + e21)
    return jnp.where(lane == i1, p1 * g_w, 0.0) + jnp.where(lane == i2, e21 * p1 * g_w, 0.0), g_sel


def _mix_route(g_ref, a_ref, x_ref, wo_ref, wnf_ref, wr_ref, precise):
    mix = (_mm(g_ref[...], wo_ref[0:GDN_V_W, :], _NN, precise)
           + _mm(a_ref[...], wo_ref[GDN_V_W:, :], _NN, precise))
    x1 = x_ref[...] + mix
    xn = _rms(x1, wnf_ref[...])
    wr = wr_ref[...]
    if precise:
        logits = _dot_f32(xn, wr)
    else:
        x_hi, w_hi = xn.astype(BF16), wr.astype(BF16)
        x_lo, w_lo = (xn - x_hi.astype(F32)).astype(BF16), (wr - w_hi.astype(F32)).astype(BF16)
        both = jnp.dot(x_hi, jnp.concatenate([w_hi, w_lo], axis=1), preferred_element_type=F32)
        logits = (both[:, 0:LANES] + both[:, LANES:]) + jnp.dot(x_lo, w_hi,
                                                                 preferred_element_type=F32)
    comb, g_sel = _route(logits)
    return x1, xn, comb, g_sel


def _expert(xn, comb, e_lane, wg, wu, wd):
    hg = jnp.dot(xn, wg.astype(BF16), preferred_element_type=F32)
    hu = jnp.dot(xn, wu.astype(BF16), preferred_element_type=F32)
    lane = lax.broadcasted_iota(jnp.int32, comb.shape, 1)
    cw = jnp.sum(jnp.where(lane == e_lane, comb, 0.0), axis=-1, keepdims=True)
    hh = (hg * _sigmoid(hg)) * hu * cw
    return jnp.dot(hh.astype(BF16), wd.astype(BF16), preferred_element_type=F32)


def _ffn_kernel(g_ref, a_ref, x_ref, wo_ref, wnf_ref, wr_ref, wg_ref, wu_ref, wd_ref, wfin_ref,
                y_ref, x1_scr, xn_scr, comb_scr, acc_scr, *, precise):
    e = pl.program_id(1)

    @pl.when(e == 0)
    def _():
        x1, xn, comb, _ = _mix_route(g_ref, a_ref, x_ref, wo_ref, wnf_ref, wr_ref, precise)
        x1_scr[...] = x1
        xn_scr[...] = xn.astype(BF16)
        comb_scr[...] = comb
        acc_scr[...] = jnp.zeros_like(acc_scr)

    acc_scr[...] += _expert(xn_scr[...], comb_scr[...], e, wg_ref[0], wu_ref[0], wd_ref[0])

    @pl.when(e == pl.num_programs(1) - 1)
    def _():
        y_ref[...] = _rms(x1_scr[...] + acc_scr[...], wfin_ref[...])


def _ffn(g_out, a_out, x, wo, wnf, wr, wg, wu, wd, wfin, tm):
    n = x.shape[0]
    assert n % tm == 0
    row = lambda i, e: (i, 0)
    fixed = lambda i, e: (0, 0)
    exp = lambda i, e: (e, 0, 0)
    return pl.pallas_call(
        functools.partial(_ffn_kernel, precise=wo.dtype == F32),
        grid=(n // tm, N_EXPERTS),
        in_specs=[pl.BlockSpec((tm, GDN_V_W), row),
                  pl.BlockSpec((tm, DA_V_W), row),
                  pl.BlockSpec((tm, D_MODEL), row),
                  pl.BlockSpec((GDN_V_W + DA_V_W, D_MODEL), fixed),
                  pl.BlockSpec((1, D_MODEL), fixed),
                  pl.BlockSpec((D_MODEL, LANES), fixed),
                  pl.BlockSpec((1, D_MODEL, D_EXPERT), exp),
                  pl.BlockSpec((1, D_MODEL, D_EXPERT), exp),
                  pl.BlockSpec((1, D_EXPERT, D_MODEL), exp),
                  pl.BlockSpec((1, D_MODEL), fixed)],
        out_specs=pl.BlockSpec((tm, D_MODEL), row),
        out_shape=jax.ShapeDtypeStruct((n, D_MODEL), F32),
        scratch_shapes=[pltpu.VMEM((tm, D_MODEL), F32),
                        pltpu.VMEM((tm, D_MODEL), BF16),
                        pltpu.VMEM((tm, LANES), F32),
                        pltpu.VMEM((tm, D_MODEL), F32)],
        compiler_params=pltpu.CompilerParams(dimension_semantics=("parallel", "arbitrary"),
                                             vmem_limit_bytes=VMEM_LIMIT),
        name="ffn",
    )(g_out, a_out, x, wo, wnf, wr, wg, wu, wd, wfin)


ROW_W = D_MODEL + LANES
MOE_TILE = 512
ROW_TILE = 2048


def _ffn_pre_kernel(g_ref, a_ref, x_ref, wo_ref, wnf_ref, wr_ref, rows_ref, info_ref, cnt_ref,
                    cnt_scr):
    @pl.when(pl.program_id(0) == 0)
    def _():
        cnt_scr[...] = jnp.zeros_like(cnt_scr)

    x1, _, comb, g_sel = _mix_route(g_ref, a_ref, x_ref, wo_ref, wnf_ref, wr_ref, False)
    tm = x1.shape[0]
    lane = lax.broadcasted_iota(jnp.int32, (tm, LANES), 1).astype(F32)
    onehot = lane == g_sel
    ri = lax.broadcasted_iota(jnp.int32, (tm, tm), 0)
    ci = lax.broadcasted_iota(jnp.int32, (tm, tm), 1)
    earlier = jnp.dot((ci < ri).astype(BF16), onehot.astype(BF16), preferred_element_type=F32)
    rank = jnp.sum(jnp.where(onehot, earlier + cnt_scr[...], 0.0), axis=-1, keepdims=True)
    cnt_scr[...] += jnp.sum(onehot.astype(F32), axis=0, keepdims=True)
    cnt_ref[...] = cnt_scr[...]
    rows_ref[:, 0:D_MODEL] = x1
    rows_ref[:, D_MODEL:] = comb
    packed = jnp.where(lane == 0.0, g_sel, jnp.where(lane == 1.0, rank, 0.0))
    r8 = lax.broadcasted_iota(jnp.int32, (8, LANES), 0)
    l8 = lax.broadcasted_iota(jnp.int32, (8, LANES), 1)
    sel = (r8 == l8).astype(BF16)
    hi = packed.astype(BF16)
    info_ref[...] = _dot_nt(sel, hi) + _dot_nt(sel, packed - hi.astype(F32))


def _ffn_pre(g_out, a_out, x, wo, wnf, wr, tm):
    n = x.shape[0]
    assert n % tm == 0
    row = lambda i: (i, 0)
    fixed = lambda i: (0, 0)
    return pl.pallas_call(
        _ffn_pre_kernel,
        grid=(n // tm,),
        in_specs=[pl.BlockSpec((tm, GDN_V_W), row),
                  pl.BlockSpec((tm, DA_V_W), row),
                  pl.BlockSpec((tm, D_MODEL), row),
                  pl.BlockSpec((GDN_V_W + DA_V_W, D_MODEL), fixed),
                  pl.BlockSpec((1, D_MODEL), fixed),
                  pl.BlockSpec((D_MODEL, LANES), fixed)],
        out_specs=[pl.BlockSpec((tm, ROW_W), row),
                   pl.BlockSpec((8, tm), lambda i: (0, i)),
                   pl.BlockSpec((1, LANES), fixed)],
        out_shape=[jax.ShapeDtypeStruct((n, ROW_W), F32),
                   jax.ShapeDtypeStruct((8, n), F32),
                   jax.ShapeDtypeStruct((1, LANES), F32)],
        scratch_shapes=[pltpu.VMEM((1, LANES), F32)],
        compiler_params=pltpu.CompilerParams(dimension_semantics=("arbitrary",),
                                             vmem_limit_bytes=VMEM_LIMIT),
        name="ffn_pre",
    )(g_out, a_out, x, wo, wnf, wr)


def _copy_rows(idx_ref, tile_ref, hbm_ref, sem, scatter):
    tm = tile_ref.shape[0]
    base = pl.program_id(0) * tm

    def body(r, carry):
        there, here = hbm_ref.at[pl.ds(idx_ref[base + r], 1)], tile_ref.at[pl.ds(r, 1)]
        src, dst = (here, there) if scatter else (there, here)
        pltpu.make_async_copy(src, dst, sem).start()
        return carry

    lax.fori_loop(0, tm, body, 0, unroll=8)
    there, here = hbm_ref.at[pl.ds(0, tm)], tile_ref
    src, dst = (here, there) if scatter else (there, here)
    pltpu.make_async_copy(src, dst, sem).wait()


def _scatter_kernel(idx_ref, tile_ref, init_ref, dst_ref, sem):
    del init_ref
    _copy_rows(idx_ref, tile_ref, dst_ref, sem, True)


def _gather_kernel(idx_ref, src_ref, tile_ref, sem):
    _copy_rows(idx_ref, tile_ref, src_ref, sem, False)


def _move_rows(idx, src, n_out, scatter):
    n, w = idx.shape[0], src.shape[1]
    assert n % ROW_TILE == 0
    any_spec = pl.BlockSpec(memory_space=pl.ANY)
    tile_spec = pl.BlockSpec((ROW_TILE, w), lambda i, idx: (i, 0))
    if scatter:
        operands = (idx, src, jnp.zeros((n_out, w), src.dtype))
        in_specs, out_spec = [tile_spec, any_spec], any_spec
    else:
        operands = (idx, src)
        in_specs, out_spec = [any_spec], tile_spec
    return pl.pallas_call(
        _scatter_kernel if scatter else _gather_kernel,
        grid_spec=pltpu.PrefetchScalarGridSpec(
            num_scalar_prefetch=1, grid=(n // ROW_TILE,),
            in_specs=in_specs, out_specs=out_spec,
            scratch_shapes=[pltpu.SemaphoreType.DMA(())]),
        out_shape=jax.ShapeDtypeStruct((n_out, w), src.dtype),
        input_output_aliases={2: 0} if scatter else {},
        compiler_params=pltpu.CompilerParams(dimension_semantics=("arbitrary",)),
        name="scatter_rows" if scatter else "gather_rows",
    )(*operands)


def _moe_groups_kernel(tg_ref, nt_ref, rows_ref, wnf_ref, wg_ref, wu_ref, wd_ref, wfin_ref, y_ref,
                       wg_bf, wu_bf, wd_bf):
    j = pl.program_id(0)

    @pl.when((j == 0) | (tg_ref[j] != tg_ref[jnp.maximum(j - 1, 0)]))
    def _():
        for e in range(EXPERTS_PER_GROUP):
            wg_bf[e] = wg_ref[e].astype(BF16)
            wu_bf[e] = wu_ref[e].astype(BF16)
            wd_bf[e] = wd_ref[e].astype(BF16)

    @pl.when(j < nt_ref[0])
    def _():
        x1 = rows_ref[:, 0:D_MODEL]
        comb = rows_ref[:, D_MODEL:]
        xn = _rms(x1, wnf_ref[...]).astype(BF16)
        first = tg_ref[j] * EXPERTS_PER_GROUP
        acc = jnp.zeros_like(x1)
        for e in range(EXPERTS_PER_GROUP):
            acc = acc + _expert(xn, comb, first + e, wg_bf[e], wu_bf[e], wd_bf[e])
        y_ref[...] = _rms(x1 + acc, wfin_ref[...])

    @pl.when(j >= nt_ref[0])
    def _():
        y_ref[...] = jnp.zeros_like(y_ref)


def _moe_groups(tile_group, n_tiles_used, rows, wnf, wg, wu, wd, wfin):
    n_tiles = tile_group.shape[0]
    fixed = lambda j, tg, nt: (0, 0)
    grp = lambda j, tg, nt: (tg[j], 0, 0)
    in_shape, out_shape = (EXPERTS_PER_GROUP, D_MODEL, D_EXPERT), (EXPERTS_PER_GROUP, D_EXPERT, D_MODEL)
    once = pl.Buffered(1)
    return pl.pallas_call(
        _moe_groups_kernel,
        grid_spec=pltpu.PrefetchScalarGridSpec(
            num_scalar_prefetch=2, grid=(n_tiles,),
            in_specs=[pl.BlockSpec((MOE_TILE, ROW_W), lambda j, tg, nt: (j, 0)),
                      pl.BlockSpec((1, D_MODEL), fixed),
                      pl.BlockSpec(in_shape, grp, pipeline_mode=once),
                      pl.BlockSpec(in_shape, grp, pipeline_mode=once),
                      pl.BlockSpec(out_shape, grp, pipeline_mode=once),
                      pl.BlockSpec((1, D_MODEL), fixed)],
            out_specs=pl.BlockSpec((MOE_TILE, D_MODEL), lambda j, tg, nt: (j, 0)),
            scratch_shapes=[pltpu.VMEM(in_shape, BF16), pltpu.VMEM(in_shape, BF16),
                            pltpu.VMEM(out_shape, BF16)]),
        out_shape=jax.ShapeDtypeStruct((n_tiles * MOE_TILE, D_MODEL), F32),
        compiler_params=pltpu.CompilerParams(dimension_semantics=("arbitrary",),
                                             vmem_limit_bytes=VMEM_LIMIT),
        name="moe_groups",
    )(tile_group, n_tiles_used, rows, wnf, wg, wu, wd, wfin)


def _ffn_sorted(g_out, a_out, x, wo, wnf, wr, wg, wu, wd, wfin):
    n = x.shape[0]
    rows, info, cnt = _ffn_pre(g_out, a_out, x, wo, wnf, wr, FFN_PRE_TILE)
    grp, rank = info[0].astype(jnp.int32), info[1].astype(jnp.int32)
    tiles = (cnt[0, :N_GROUPS].astype(jnp.int32) + MOE_TILE - 1) // MOE_TILE
    ends = jnp.cumsum(tiles)
    dest = (ends - tiles)[grp] * MOE_TILE + rank
    n_tiles = n // MOE_TILE + N_GROUPS
    tile_group = jnp.minimum(
        jnp.sum(jnp.arange(n_tiles, dtype=jnp.int32)[:, None] >= ends[None, :], axis=1),
        N_GROUPS - 1).astype(jnp.int32)
    rows_sorted = _move_rows(dest, rows, n_tiles * MOE_TILE, True)
    y_sorted = _moe_groups(tile_group, ends[N_GROUPS - 1:], rows_sorted, wnf, wg, wu, wd, wfin)
    return _move_rows(dest, y_sorted, n, False)


def kernel(x_prompt, x_sample, cache_k, cache_v, state_conv, state_gdn, page_table, meta_tokens,
           w_norm_mix, w_in, w_conv, a_log, dt_bias, w_gnorm, lambda_q1, lambda_k1, lambda_q2,
           lambda_k2, w_subln, w_out, w_norm_ffn, w_group, w_expert, w_gate, w_up, w_down, w_final):
    l = 0
    bp, seq = x_prompt.shape[0], x_prompt.shape[1]
    bs = x_sample.shape[0]
    n_pool, page = cache_k.shape[1], cache_k.shape[2]
    past = page_table.shape[1] * page
    C = GDN_CHUNK

    w_in_p = _prep_w_in(w_in[l])
    wn_mix = w_norm_mix[l][None, :]
    pad_lanes = lambda v: jnp.pad(v, (0, LANES - v.shape[0]))[None, :]
    alog, dtb = pad_lanes(a_log[l]), pad_lanes(dt_bias[l])
    wgn = w_gnorm[l][None, :]
    lam_vecs = jnp.stack([lambda_q1[l], lambda_k1[l], lambda_q2[l], lambda_k2[l]])
    wsub = w_subln[l][None, :]
    w_router = jnp.pad(jnp.concatenate([w_expert[l], w_group[l]], axis=1),
                       ((0, 0), (0, LANES - N_EXPERTS - N_GROUPS)))
    ffn_w = (w_norm_ffn[l][None, :], w_router, w_gate[l], w_up[l], w_down[l], w_final[None, :])

    pos_small = jnp.concatenate([jnp.arange(N_META), jnp.full((bs,), past)])
    rows_small = jnp.concatenate([meta_tokens, x_sample[:, 0, :]], axis=0)
    c_s, z_s, ab_s, q_s, k_s, v_s = _proj(rows_small, wn_mix, w_in_p, _rope_tables(pos_small),
                                           N_META + bs)
    xp = x_prompt.reshape(bp * seq, D_MODEL)
    c_p, z_p, ab_p, q_p, k_p, v_p = _proj(xp, wn_mix, w_in_p.astype(BF16),
                                           _rope_tables(N_META + jnp.arange(seq)), PROJ_TILE)

    front = lambda a: jnp.pad(a[:N_META], ((C - N_META, 0), (0, 0)))[None]
    zero_tail = jnp.zeros((1, 8, CONV_CH), F32)
    zero_state = jnp.zeros((1, GDN_HEADS, GDN_DK, GDN_DV), F32)
    _, s_meta = _gdn(front(c_s), front(z_s), front(ab_s), zero_tail, zero_state,
                     w_conv[l], alog, dtb, wgn, C)
    tail_meta = jnp.pad(c_s[N_META - (CONV_W - 1):N_META], ((8 - (CONV_W - 1), 0), (0, 0)))[None]
    c_p3 = c_p.reshape(bp, seq, CONV_CH)
    g_out, p_gdn = _gdn(c_p3, z_p.reshape(bp, seq, GDN_V_W), ab_p.reshape(bp, seq, AB_W),
                        tail_meta, s_meta, w_conv[l], alog, dtb, wgn, GDN_TILE)
    p_conv = c_p3[:, seq - (CONV_W - 1):, :]

    p_k, p_v = _kv_out(k_p.reshape(bp, seq, DA_QK_W), v_p.reshape(bp, seq, DA_V_W),
                       k_s[:N_META], v_s[:N_META])
    meta_blk = lambda m: jnp.pad(m[:N_META], ((0, LANES - N_META), (0, 0)))
    a_out = _attn_prompt(q_p.reshape(bp, seq, DA_QK_W), k_p.reshape(bp, seq, DA_QK_W),
                         v_p.reshape(bp, seq, DA_V_W), meta_blk(k_s), meta_blk(v_s),
                         lam_vecs, wsub, ATTN_BQ, ATTN_BK, ATTN_HEADS_PER_STEP)

    y_prompt = _ffn_sorted(g_out.reshape(bp * seq, GDN_V_W), a_out.reshape(bp * seq, DA_V_W), xp,
                           w_out[l].astype(BF16), *ffn_w).reshape(bp, seq, D_MODEL)

    sm = lambda a: a[N_META:]
    g_s, s_gdn = _gdn_step(sm(c_s), sm(z_s), sm(ab_s), jnp.swapaxes(state_conv[l], 0, 1),
                           state_gdn[l], w_conv[l], alog, dtb, wgn)
    s_conv = jnp.concatenate([state_conv[l][:, 1:, :], sm(c_s)[:, None, :]], axis=1)
    heads = lambda a: sm(a).reshape(bs, DA_HEADS, DA_DV)
    a_s = _attn_sample(heads(q_s), heads(k_s), heads(v_s),
                       cache_k[l].reshape(n_pool, page * DA_HEADS, 2 * DA_DQK),
                       cache_v[l].reshape(n_pool, page * DA_HEADS, DA_DV),
                       page_table, lam_vecs, wsub, PAGES_PER_STEP)
    y_sample = _ffn(g_s, a_s.reshape(bs, DA_V_W), x_sample[:, 0, :], w_out[l], *ffn_w,
                    bs).reshape(bs, 1, D_MODEL)

    return (y_prompt, y_sample, p_conv[None], p_gdn[None],
            p_k.reshape(1, bp, seq + N_META, DA_HEADS, 2 * DA_DQK),
            p_v.reshape(1, bp, seq + N_META, DA_HEADS, DA_DV),
            s_conv[None], s_gdn[None],
            sm(k_s).reshape(1, bs, 1, DA_HEADS, 2 * DA_DQK),
            sm(v_s).reshape(1, bs, 1, DA_HEADS, DA_DV))
```

```python
import functools
import math

import jax
import jax.numpy as jnp
import numpy as np
from jax import lax
from jax.experimental import pallas as pl
from jax.experimental.pallas import tpu as pltpu

F32 = jnp.float32
BF16 = jnp.bfloat16

D_MODEL = 1024
N_META = 16
EPS = 1e-6
GDN_HEADS = 4
GDN_DK = 128
GDN_DV = 128
CONV_W = 4
GDN_QK_W = GDN_HEADS * GDN_DK
GDN_V_W = GDN_HEADS * GDN_DV
CONV_CH = 2 * GDN_QK_W + GDN_V_W
DA_HEADS = 4
DA_DQK = 64
DA_DV = 2 * DA_DQK
DA_QK_W = DA_HEADS * 2 * DA_DQK
DA_V_W = DA_HEADS * DA_DV
ROPE_DIM = DA_DQK // 4
ROPE_HALF = ROPE_DIM // 2
ROPE_THETA = 500000.0
N_GROUPS = 4
EXPERTS_PER_GROUP = 8
N_EXPERTS = N_GROUPS * EXPERTS_PER_GROUP
D_EXPERT = 256
LAM_INIT = 0.8 - 0.6 * math.exp(-0.3 * 0)

LANES = 128
GDN_CHUNK = 64
GDN_STACK = 2
AB_W = LANES
PROJ_PAD_W = CONV_CH + GDN_V_W + 2 * DA_QK_W + DA_V_W + AB_W
VMEM_LIMIT = 56 * 1024 * 1024

PROJ_TILE = 512
GDN_TILE = 512
ATTN_BQ, ATTN_BK = 256, 512
ATTN_HEADS_PER_STEP = 2
PAGES_PER_STEP = 32
FFN_PRE_TILE = 512


_NN = (((1,), (0,)), ((), ()))
_NT = (((1,), (1,)), ((), ()))
_TN = (((0,), (0,)), ((), ()))


def _mm(a, b, dims=_NN, precise=False):
    if precise:
        return lax.dot_general(a.astype(F32), b.astype(F32), dims, preferred_element_type=F32,
                               precision=lax.Precision.HIGHEST)
    return lax.dot_general(a.astype(BF16), b.astype(BF16), dims, preferred_element_type=F32)


def _dot_nt(a, b):
    return _mm(a, b, _NT)


def _dot_f32(a, b):
    return _mm(a, b, _NN, True)


def _mask_dot(mask, x, dims):
    hi = x.astype(BF16)
    mid = (x - hi.astype(F32)).astype(BF16)
    lo = x - hi.astype(F32) - mid.astype(F32)
    m = mask.astype(BF16)
    return (_mm(m, hi, dims) + _mm(m, mid, dims)) + _mm(m, lo, dims)


def _sigmoid(x):
    return 1.0 / (1.0 + jnp.exp(-x))


def _softplus(x):
    return jnp.maximum(x, 0.0) + jnp.log(1.0 + jnp.exp(-jnp.abs(x)))


def _rms(x, w):
    return x * lax.rsqrt(jnp.mean(x * x, axis=-1, keepdims=True) + EPS) * w


def _rope(x, cs, sn_lo, sn_hi):
    w = x.shape[1]
    cs = jnp.concatenate([cs] * DA_HEADS, axis=1)
    sn_lo = jnp.concatenate([sn_lo] * DA_HEADS, axis=1)
    sn_hi = jnp.concatenate([sn_hi] * DA_HEADS, axis=1)
    up = pltpu.roll(x, w - ROPE_HALF, 1)
    dn = pltpu.roll(x, ROPE_HALF, 1)
    return x * cs + up * sn_lo + dn * sn_hi


def _proj_kernel(x_ref, wn_ref, w_ref, cs_ref, snl_ref, snh_ref,
                 c_ref, z_ref, ab_ref, q_ref, k_ref, v_ref, *, precise):
    h = _rms(x_ref[...], wn_ref[...])
    h = h if precise else h.astype(BF16)
    o = 0
    c_ref[...] = _mm(h, w_ref[:, o:o + CONV_CH], _NN, precise)
    o += CONV_CH
    z_ref[...] = _mm(h, w_ref[:, o:o + GDN_V_W], _NN, precise)
    o += GDN_V_W
    q = _mm(h, w_ref[:, o:o + DA_QK_W], _NN, precise)
    o += DA_QK_W
    k = _mm(h, w_ref[:, o:o + DA_QK_W], _NN, precise)
    o += DA_QK_W
    v_ref[...] = _mm(h, w_ref[:, o:o + DA_V_W], _NN, precise)
    o += DA_V_W
    ab_ref[...] = _mm(h, w_ref[:, o:o + AB_W], _NN, precise)
    cs, snl, snh = cs_ref[...], snl_ref[...], snh_ref[...]
    q_ref[...] = _rope(q, cs, snl, snh)
    k_ref[...] = _rope(k, cs, snl, snh)


def _proj(x, wn, w, tabs, tm):
    n = x.shape[0]
    assert n % tm == 0
    nt = tabs[0].shape[0] // tm
    row = lambda i: (i, 0)
    fixed = lambda i: (0, 0)
    tab = lambda i: (i % nt, 0)
    widths = (CONV_CH, GDN_V_W, AB_W, DA_QK_W, DA_QK_W, DA_V_W)
    return pl.pallas_call(
        functools.partial(_proj_kernel, precise=w.dtype == F32),
        grid=(n // tm,),
        in_specs=[pl.BlockSpec((tm, D_MODEL), row),
                  pl.BlockSpec((1, D_MODEL), fixed),
                  pl.BlockSpec((D_MODEL, PROJ_PAD_W), fixed),
                  pl.BlockSpec((tm, LANES), tab),
                  pl.BlockSpec((tm, LANES), tab),
                  pl.BlockSpec((tm, LANES), tab)],
        out_specs=[pl.BlockSpec((tm, wd), row) for wd in widths],
        out_shape=[jax.ShapeDtypeStruct((n, wd), F32) for wd in widths],
        compiler_params=pltpu.CompilerParams(dimension_semantics=("parallel",),
                                             vmem_limit_bytes=VMEM_LIMIT),
        name="proj",
    )(x, wn, w, *tabs)


def _rope_tables(pos):
    inv_freq = ROPE_THETA ** (-jnp.arange(ROPE_HALF, dtype=F32) * 2.0 / ROPE_DIM)
    ang = pos.astype(F32)[:, None] * inv_freq[None, :]
    cos, sin = jnp.cos(ang), jnp.sin(ang)
    t = pos.shape[0]
    one = jnp.ones((t, DA_DQK - ROPE_DIM), F32)
    zero = jnp.zeros((t, DA_DQK - ROPE_DIM), F32)
    zh = jnp.zeros((t, ROPE_HALF), F32)
    cs = jnp.concatenate([cos, cos, one] * 2, axis=1)
    snl = jnp.concatenate([-sin, zh, zero] * 2, axis=1)
    snh = jnp.concatenate([zh, sin, zero] * 2, axis=1)
    return cs, snl, snh


def _prep_w_in(w_in):
    c, z, a, b, dq, dk, dv = jnp.split(
        w_in, np.cumsum([CONV_CH, GDN_V_W, GDN_HEADS, GDN_HEADS, DA_QK_W, DA_QK_W]).tolist(), axis=-1)
    pad = jnp.zeros((D_MODEL, AB_W - 2 * GDN_HEADS), w_in.dtype)
    return jnp.concatenate([c, z, dq, dk, dv, a, b, pad], axis=-1)


def _gdn_kernel(c_ref, z_ref, ab_ref, tail_ref, s0_ref, wconv_ref, alog_ref, dtb_ref, wg_ref,
                o_ref, sfin_ref, cbuf, ybuf, s_scr, u0_scr, wq_scr, kw_scr, qkd_scr, el_scr, *, tt):
    C, H = GDN_CHUNK, GDN_HEADS
    HS = GDN_STACK
    R = HS * C
    shift = C.bit_length() - 1
    t = pl.program_id(1)

    @pl.when(t == 0)
    def _():
        cbuf[0:8, :] = tail_ref[0]
        s_scr[...] = s0_ref[0]

    cbuf[8:8 + tt, :] = c_ref[0]
    acc = cbuf[8:8 + tt, :] * wconv_ref[CONV_W - 1:CONV_W, :]
    for i in range(CONV_W - 1):
        off = 8 - (CONV_W - 1) + i
        acc = acc + cbuf[off:off + tt, :] * wconv_ref[i:i + 1, :]
    ybuf[...] = acc * _sigmoid(acc)
    cbuf[0:8, :] = cbuf[tt:tt + 8, :]

    ii = lax.broadcasted_iota(jnp.int32, (R, R), 0)
    jj = lax.broadcasted_iota(jnp.int32, (R, R), 1)
    same_head = jnp.right_shift(ii, shift) == jnp.right_shift(jj, shift)
    strict = same_head & (ii > jj)
    causal = same_head & (ii >= jj)
    eye = (ii == jj).astype(F32)
    merge_masks = []
    s = 1
    while s < C:
        sh = s.bit_length() - 1
        merge_masks.append((jnp.right_shift(ii, sh + 1) == jnp.right_shift(jj, sh + 1))
                           & (jnp.right_shift(ii, sh) != jnp.right_shift(jj, sh)) & (ii > jj))
        s *= 2
    ci_ = lax.broadcasted_iota(jnp.int32, (C, C), 0)
    cj_ = lax.broadcasted_iota(jnp.int32, (C, C), 1)
    tri = (ci_ >= cj_).astype(F32)
    lane = lax.broadcasted_iota(jnp.int32, (R, LANES), 1)
    rhead = jnp.right_shift(lax.broadcasted_iota(jnp.int32, (R, LANES), 0), shift)
    ones8 = jnp.ones((8, LANES), F32)
    rep = lambda a: jnp.concatenate([a] * HS, axis=0)
    pick = lambda a, off: jnp.sum(jnp.where(lane == rhead + off, a, 0.0), axis=-1, keepdims=True)

    def stacked(rows, base, scale, h0):
        parts = []
        for h in range(h0, h0 + HS):
            x = ybuf[rows, base + h * GDN_DK:base + (h + 1) * GDN_DK]
            if scale is not None:
                x = x * lax.rsqrt(jnp.sum(x * x, axis=-1, keepdims=True) + EPS) * scale
            parts.append(x)
        return jnp.concatenate(parts, axis=0)

    n_stack = H // HS
    units = range((tt // C) * n_stack)
    low, rhs, qes = [], [], []
    for ci in range(tt // C):
        rows = slice(ci * C, (ci + 1) * C)
        ab = ab_ref[0, rows, :]
        gfull = -jnp.exp(alog_ref[...]) * _softplus(ab + dtb_ref[...])
        bfull = _sigmoid(ab)
        gcum = _mask_dot(tri, gfull, _NN)
        glast = gcum[C - 1:C, :]
        el_scr[ci] = jnp.broadcast_to(jnp.exp(glast), (8, LANES))
        for p in range(n_stack):
            un, h0 = ci * n_stack + p, p * HS
            g_m = jnp.where(lane == rhead + h0, rep(gcum), 0.0)
            g_col = jnp.sum(g_m, axis=-1, keepdims=True)
            g_row = _mask_dot(ones8, g_m, _NT)[0:1, :]
            beta = pick(rep(bfull), H + h0)
            gl_col = pick(jnp.broadcast_to(glast, (R, LANES)), h0)
            decay = jnp.exp(jnp.where(causal, g_col - g_row, -jnp.inf))
            q = stacked(rows, 0, GDN_DK ** -0.5, h0)
            k = stacked(rows, GDN_QK_W, 1.0, h0)
            v = stacked(rows, 2 * GDN_QK_W, None, h0)
            kq = _mm(jnp.concatenate([k, q], axis=0), k, _NT)
            low.append(jnp.where(strict, kq[0:R] * decay * beta, 0.0))
            e_g = jnp.exp(g_col)
            rhs.append(jnp.concatenate([beta * v, (beta * e_g) * k], axis=1).astype(BF16))
            qes.append(q * e_g)
            kw_scr[un] = (k * jnp.exp(gl_col - g_col)).astype(BF16)
            qkd_scr[un] = jnp.where(causal, kq[R:2 * R] * decay, 0.0).astype(BF16)
    tinv = [eye - jnp.where(merge_masks[0], low[un], 0.0) for un in units]
    for m in merge_masks[1:]:
        half = [_mm(tinv[un], jnp.where(m, low[un], 0.0)) for un in units]
        tinv = [tinv[un] - _mm(half[un], tinv[un]) for un in units]
    for un in units:
        w2 = _mm(tinv[un], rhs[un])
        u0_scr[un] = w2[:, 0:GDN_DV]
        wk = w2[:, GDN_DV:]
        wq_scr[un] = jnp.concatenate(
            [x[h * C:(h + 1) * C] for h in range(HS) for x in (wk, qes[un])], axis=0).astype(BF16)

    st = [s_scr[h] for h in range(H)]
    for ci in range(tt // C):
        rows = slice(ci * C, (ci + 1) * C)
        el = el_scr[ci]
        for p in range(n_stack):
            un = ci * n_stack + p
            us, os_ = [], []
            for hh in range(HS):
                r = jnp.dot(wq_scr[un, 2 * hh * C:2 * (hh + 1) * C, :], st[p * HS + hh].astype(BF16),
                            preferred_element_type=F32)
                us.append(u0_scr[un, hh * C:(hh + 1) * C, :] - r[0:C])
                os_.append(r[C:2 * C])
            intra = jnp.dot(qkd_scr[un], jnp.concatenate(us, axis=0).astype(BF16),
                            preferred_element_type=F32)
            for hh in range(HS):
                h = p * HS + hh
                st[h] = (st[h] * el[0:1, h:h + 1]
                         + _mm(kw_scr[un, hh * C:(hh + 1) * C, :], us[hh], _TN))
                o = os_[hh] + intra[hh * C:(hh + 1) * C]
                zz = z_ref[0, rows, h * GDN_DV:(h + 1) * GDN_DV]
                o_ref[0, rows, h * GDN_DV:(h + 1) * GDN_DV] = (
                    _rms(o, wg_ref[...]) * (zz * _sigmoid(zz))).astype(o_ref.dtype)
    for h in range(H):
        s_scr[h] = st[h]

    @pl.when(t == pl.num_programs(1) - 1)
    def _():
        sfin_ref[0] = s_scr[...]


def _gdn_step_kernel(c_ref, z_ref, ab_ref, cs_ref, s_ref, wconv_ref, alog_ref, dtb_ref, wg_ref,
                     o_ref, snew_ref, *, nb):
    H = GDN_HEADS
    acc = c_ref[...] * wconv_ref[CONV_W - 1:CONV_W, :]
    for i in range(CONV_W - 1):
        acc = acc + cs_ref[i] * wconv_ref[i:i + 1, :]
    y = acc * _sigmoid(acc)
    ab = ab_ref[...]
    e_g = jnp.exp(-jnp.exp(alog_ref[...]) * _softplus(ab + dtb_ref[...]))
    bfull = _sigmoid(ab)
    unit = lambda x: x * lax.rsqrt(jnp.sum(x * x, axis=-1, keepdims=True) + EPS)
    qs = [unit(y[:, h * GDN_DK:(h + 1) * GDN_DK]) * (GDN_DK ** -0.5) for h in range(H)]
    ks = [unit(y[:, GDN_QK_W + h * GDN_DK:GDN_QK_W + (h + 1) * GDN_DK]) for h in range(H)]
    vs = [y[:, 2 * GDN_QK_W + h * GDN_DV:2 * GDN_QK_W + (h + 1) * GDN_DV] for h in range(H)]
    cols = jnp.concatenate(ks + qs, axis=0).T
    for h in range(H):
        ks_rows, qs_rows = [], []
        for s in range(nb):
            state = s_ref[s, h]
            kcol = cols[:, h * nb + s:h * nb + s + 1]
            qcol = cols[:, (H + h) * nb + s:(H + h) * nb + s + 1]
            ks_rows.append(jnp.sum(kcol * state, axis=0, keepdims=True))
            qs_rows.append(jnp.sum(qcol * state, axis=0, keepdims=True))
        eg_h = e_g[:, h:h + 1]
        u = bfull[:, H + h:H + h + 1] * (vs[h] - eg_h * jnp.concatenate(ks_rows, axis=0))
        o = (eg_h * jnp.concatenate(qs_rows, axis=0)
             + jnp.sum(qs[h] * ks[h], axis=-1, keepdims=True) * u)
        eg_b = jnp.broadcast_to(eg_h, (nb, GDN_DV))
        for s in range(nb):
            kcol = cols[:, h * nb + s:h * nb + s + 1]
            snew_ref[s, h] = eg_b[s:s + 1, :] * s_ref[s, h] + kcol * u[s:s + 1, :]
        zz = z_ref[:, h * GDN_DV:(h + 1) * GDN_DV]
        o_ref[:, h * GDN_DV:(h + 1) * GDN_DV] = _rms(o, wg_ref[...]) * (zz * _sigmoid(zz))


def _gdn_step(c, z, ab, conv_state, s, wconv, alog, dtb, wg):
    bs = c.shape[0]
    nb = LANES // (2 * GDN_HEADS)
    assert bs % nb == 0
    row = lambda i: (i, 0)
    fixed = lambda i: (0, 0)
    state = pl.BlockSpec((nb, GDN_HEADS, GDN_DK, GDN_DV), lambda i: (i, 0, 0, 0))
    return pl.pallas_call(
        functools.partial(_gdn_step_kernel, nb=nb),
        grid=(bs // nb,),
        in_specs=[pl.BlockSpec((nb, CONV_CH), row),
                  pl.BlockSpec((nb, GDN_V_W), row),
                  pl.BlockSpec((nb, AB_W), row),
                  pl.BlockSpec((CONV_W - 1, nb, CONV_CH), lambda i: (0, i, 0)),
                  state,
                  pl.BlockSpec((CONV_W, CONV_CH), fixed),
                  pl.BlockSpec((1, LANES), fixed),
                  pl.BlockSpec((1, LANES), fixed),
                  pl.BlockSpec((1, GDN_DV), fixed)],
        out_specs=[pl.BlockSpec((nb, GDN_V_W), row), state],
        out_shape=[jax.ShapeDtypeStruct((bs, GDN_V_W), F32),
                   jax.ShapeDtypeStruct(s.shape, F32)],
        compiler_params=pltpu.CompilerParams(dimension_semantics=("parallel",),
                                             vmem_limit_bytes=VMEM_LIMIT),
        name="gdn_step",
    )(c, z, ab, conv_state, s, wconv, alog, dtb, wg)


def _gdn(c, z, ab, tail, s0, wconv, alog, dtb, wg, tt):
    b, t, _ = c.shape
    assert t % tt == 0 and tt % GDN_CHUNK == 0
    rs = GDN_STACK * GDN_CHUNK
    nc, nu = tt // GDN_CHUNK, (tt // GDN_CHUNK) * (GDN_HEADS // GDN_STACK)
    seq = lambda i, j: (i, j, 0)
    fixed2 = lambda i, j: (0, 0)
    tail_map = (lambda i, j: (i, 0, 0)) if tail.shape[0] == b else (lambda i, j: (0, 0, 0))
    s0_map = (lambda i, j: (i, 0, 0, 0)) if s0.shape[0] == b else (lambda i, j: (0, 0, 0, 0))
    return pl.pallas_call(
        functools.partial(_gdn_kernel, tt=tt),
        grid=(b, t // tt),
        in_specs=[pl.BlockSpec((1, tt, CONV_CH), seq),
                  pl.BlockSpec((1, tt, GDN_V_W), seq),
                  pl.BlockSpec((1, tt, AB_W), seq),
                  pl.BlockSpec((1, 8, CONV_CH), tail_map),
                  pl.BlockSpec((1, GDN_HEADS, GDN_DK, GDN_DV), s0_map),
                  pl.BlockSpec((CONV_W, CONV_CH), fixed2),
                  pl.BlockSpec((1, LANES), fixed2),
                  pl.BlockSpec((1, LANES), fixed2),
                  pl.BlockSpec((1, GDN_DV), fixed2)],
        out_specs=[pl.BlockSpec((1, tt, GDN_V_W), seq),
                   pl.BlockSpec((1, GDN_HEADS, GDN_DK, GDN_DV), lambda i, j: (i, 0, 0, 0))],
        out_shape=[jax.ShapeDtypeStruct((b, t, GDN_V_W), BF16),
                   jax.ShapeDtypeStruct((b, GDN_HEADS, GDN_DK, GDN_DV), F32)],
        scratch_shapes=[pltpu.VMEM((tt + 8, CONV_CH), F32),
                        pltpu.VMEM((tt, CONV_CH), F32),
                        pltpu.VMEM((GDN_HEADS, GDN_DK, GDN_DV), F32),
                        pltpu.VMEM((nu, rs, GDN_DV), F32),
                        pltpu.VMEM((nu, 2 * rs, GDN_DK), BF16),
                        pltpu.VMEM((nu, rs, GDN_DK), BF16),
                        pltpu.VMEM((nu, rs, rs), BF16),
                        pltpu.VMEM((nc, 8, LANES), F32)],
        compiler_params=pltpu.CompilerParams(dimension_semantics=("parallel", "arbitrary"),
                                             vmem_limit_bytes=VMEM_LIMIT),
        name="gdn",
    )(c, z, ab, tail, s0, wconv, alog, dtb, wg)


def _kv_out_kernel(k_ref, v_ref, km_ref, vm_ref, ko_ref, vo_ref):
    t = k_ref.shape[1]
    for src, meta, dst in ((k_ref, km_ref, ko_ref), (v_ref, vm_ref, vo_ref)):
        for h in range(DA_HEADS):
            cols = slice(h * DA_DV, (h + 1) * DA_DV)
            dst[0, pl.ds(h, N_META, stride=DA_HEADS), :] = meta[:, cols]
            dst[0, pl.ds(N_META * DA_HEADS + h, t, stride=DA_HEADS), :] = src[0, :, cols]


def _kv_out(k, v, k_meta, v_meta):
    b, t, w = k.shape
    rows = (t + N_META) * DA_HEADS
    seq = pl.BlockSpec((1, t, w), lambda i: (i, 0, 0))
    meta = pl.BlockSpec((N_META, w), lambda i: (0, 0))
    out = pl.BlockSpec((1, rows, DA_DV), lambda i: (i, 0, 0))
    return pl.pallas_call(
        _kv_out_kernel,
        grid=(b,),
        in_specs=[seq, seq, meta, meta],
        out_specs=[out, out],
        out_shape=[jax.ShapeDtypeStruct((b, rows, DA_DV), F32)] * 2,
        compiler_params=pltpu.CompilerParams(dimension_semantics=("parallel",),
                                             vmem_limit_bytes=VMEM_LIMIT),
        name="kv_out",
    )(k, v, k_meta, v_meta)


def _lambda(lam_ref):
    lv = lam_ref[...]
    s1 = jnp.sum(lv[0:1, :] * lv[1:2, :], axis=-1, keepdims=True)
    s2 = jnp.sum(lv[2:3, :] * lv[3:4, :], axis=-1, keepdims=True)
    return jnp.exp(s1) - jnp.exp(s2) + LAM_INIT


def _split_components(q):
    lane = lax.broadcasted_iota(jnp.int32, q.shape, 1)
    first = (lane % DA_DV) < DA_DQK
    return jnp.where(first, q, 0.0).astype(BF16), jnp.where(first, 0.0, q).astype(BF16)


def _online_update(state, s, vb):
    m, l, acc = state
    m_new = jnp.maximum(m, jnp.max(s, axis=-1, keepdims=True))
    p = jnp.exp2(s - m_new)
    alpha = jnp.exp2(m - m_new)
    return (m_new, alpha * l + jnp.sum(p, axis=-1, keepdims=True),
            alpha * acc + jnp.dot(p.astype(BF16), vb, preferred_element_type=F32))


def _attn_prompt_kernel(q_ref, k_ref, v_ref, km_ref, vm_ref, lam_ref, wsub_ref, o_ref, *, bq, bk):
    qi = pl.program_id(2)
    heads = [slice(h * DA_DV, (h + 1) * DA_DV) for h in range(q_ref.shape[2] // DA_DV)]
    qs = [_split_components(q_ref[0, :, hs] * (DA_DQK ** -0.5 * math.log2(math.e))) for hs in heads]
    neg = jnp.full((bq, 1), -jnp.inf, F32)
    zero = jnp.zeros((bq, 1), F32)
    zacc = jnp.zeros((bq, DA_DV), F32)
    init = tuple(((neg, zero, zacc), (neg, zero, zacc)) for _ in heads)

    def block(kblk, vblk, mask, st):
        out = []
        for (q1, q2), hs, sth in zip(qs, heads, st):
            kb, vb = kblk[:, hs].astype(BF16), vblk[:, hs].astype(BF16)
            s1, s2 = _dot_nt(q1, kb), _dot_nt(q2, kb)
            if mask is not None:
                s1 = jnp.where(mask, s1, -jnp.inf)
                s2 = jnp.where(mask, s2, -jnp.inf)
            out.append((_online_update(sth[0], s1, vb), _online_update(sth[1], s2, vb)))
        return tuple(out)

    n_full = lax.shift_right_logical(qi * bq, bk.bit_length() - 1)

    def body(kb, st):
        r0 = pl.multiple_of(kb * bk, bk)
        return block(k_ref[0, pl.ds(r0, bk), :], v_ref[0, pl.ds(r0, bk), :], None, st)

    st = lax.fori_loop(0, n_full, body, init)
    r0 = pl.multiple_of(n_full * bk, bk)
    kcat = jnp.concatenate([km_ref[...], k_ref[0, pl.ds(r0, bk), :]], axis=0)
    vcat = jnp.concatenate([vm_ref[...], v_ref[0, pl.ds(r0, bk), :]], axis=0)
    ri = lax.broadcasted_iota(jnp.int32, (bq, LANES + bk), 0)
    ci = lax.broadcasted_iota(jnp.int32, (bq, LANES + bk), 1)
    seen = (ci < N_META) | ((ci >= LANES) & (ci - LANES <= ri + (qi * bq - n_full * bk)))
    st = block(kcat, vcat, seen, st)
    lam = _lambda(lam_ref)
    for hs, ((m1, l1, a1), (m2, l2, a2)) in zip(heads, st):
        o = a1 / l1 - lam * (a2 / l2)
        o_ref[0, :, hs] = (_rms(o, wsub_ref[...]) * (1.0 - LAM_INIT)).astype(o_ref.dtype)


def _attn_prompt(q, k, v, k_meta, v_meta, lam_vecs, wsub, bq, bk, hps):
    b, t, _ = q.shape
    assert t % bk == 0 and bk % bq == 0 and k_meta.shape[0] == LANES and DA_HEADS % hps == 0
    w = hps * DA_DV
    return pl.pallas_call(
        functools.partial(_attn_prompt_kernel, bq=bq, bk=bk),
        grid=(b, DA_HEADS // hps, t // bq),
        in_specs=[pl.BlockSpec((1, bq, w), lambda i, h, j: (i, j, h)),
                  pl.BlockSpec((1, t, w), lambda i, h, j: (i, 0, h)),
                  pl.BlockSpec((1, t, w), lambda i, h, j: (i, 0, h)),
                  pl.BlockSpec((LANES, w), lambda i, h, j: (0, h)),
                  pl.BlockSpec((LANES, w), lambda i, h, j: (0, h)),
                  pl.BlockSpec((4, DA_DQK), lambda i, h, j: (0, 0)),
                  pl.BlockSpec((1, DA_DV), lambda i, h, j: (0, 0))],
        out_specs=pl.BlockSpec((1, bq, w), lambda i, h, j: (i, j, h)),
        out_shape=jax.ShapeDtypeStruct((b, t, DA_V_W), BF16),
        compiler_params=pltpu.CompilerParams(
            dimension_semantics=("parallel", "parallel", "arbitrary"),
            vmem_limit_bytes=VMEM_LIMIT),
        name="attn_prompt",
    )(q, k, v, k_meta, v_meta, lam_vecs, wsub)


def _attn_sample_kernel(pt_ref, q_ref, kn_ref, vn_ref, lam_ref, wsub_ref, *rest, pps):
    k_refs, v_refs = rest[:pps], rest[pps:2 * pps]
    o_ref, m_scr, l_scr, acc_scr = rest[2 * pps:]
    step = pl.program_id(1)
    nrow = 2 * DA_HEADS
    twice = lambda x: jnp.concatenate([x, x], axis=0)
    rowq = lax.broadcasted_iota(jnp.int32, (nrow, DA_DV), 0)
    laneq = lax.broadcasted_iota(jnp.int32, (nrow, DA_DV), 1)
    own_comp = (laneq // DA_DQK) == (rowq // DA_HEADS)
    qm = jnp.where(own_comp, twice(q_ref[0]) * (DA_DQK ** -0.5), 0.0)
    page_rows = k_refs[0].shape[1]
    rowp = lax.broadcasted_iota(jnp.int32, (nrow, page_rows * pps), 0)
    colp = lax.broadcasted_iota(jnp.int32, (nrow, page_rows * pps), 1)
    own_head = (colp % DA_HEADS) == (rowp % DA_HEADS)

    def rows3(x):
        hi = x.astype(BF16).astype(F32)
        mid = (x - hi).astype(BF16).astype(F32)
        return jnp.concatenate([hi, mid, x - hi - mid], axis=0).astype(BF16)

    def unrows3(r):
        return (r[0:nrow] + r[nrow:2 * nrow]) + r[2 * nrow:3 * nrow]

    def hi_lo(x):
        hi = x.astype(BF16)
        return hi, (x - hi.astype(F32)).astype(BF16)

    @pl.when(step == 0)
    def _():
        s_new = jnp.sum(qm * twice(kn_ref[0]), axis=-1, keepdims=True)
        m_scr[...] = s_new
        l_scr[...] = jnp.ones_like(s_new)
        acc_scr[...] = twice(vn_ref[0])

    q3 = rows3(qm)
    scores = []
    for r in k_refs:
        k_hi, k_lo = hi_lo(r[0])
        scores.append(unrows3(_dot_nt(q3, k_hi) + _dot_nt(q3, k_lo)))
    s = jnp.where(own_head, jnp.concatenate(scores, axis=1), -jnp.inf)
    m_old = m_scr[...]
    m_new = jnp.maximum(m_old, jnp.max(s, axis=-1, keepdims=True))
    p = jnp.exp(s - m_new)
    alpha = jnp.exp(m_old - m_new)
    p3 = rows3(p)
    pv = jnp.zeros((3 * nrow, DA_DV), F32)
    for j, r in enumerate(v_refs):
        v_hi, v_lo = hi_lo(r[0])
        pj = p3[:, j * page_rows:(j + 1) * page_rows]
        pv = pv + (jnp.dot(pj, v_hi, preferred_element_type=F32)
                   + jnp.dot(pj, v_lo, preferred_element_type=F32))
    m_scr[...] = m_new
    l_scr[...] = alpha * l_scr[...] + jnp.sum(p, axis=-1, keepdims=True)
    acc_scr[...] = alpha * acc_scr[...] + unrows3(pv)

    @pl.when(step == pl.num_programs(1) - 1)
    def _():
        w = acc_scr[...] / l_scr[...]
        o = w[0:DA_HEADS, :] - _lambda(lam_ref) * w[DA_HEADS:nrow, :]
        o_ref[0] = _rms(o, wsub_ref[...]) * (1.0 - LAM_INIT)


def _attn_sample(q, k_new, v_new, cache_k, cache_v, page_table, lam_vecs, wsub, pps):
    bs = q.shape[0]
    n_pages = page_table.shape[1]
    page_rows = cache_k.shape[1]
    assert n_pages % pps == 0
    tok = lambda i, s, pt: (i, 0, 0)
    fixed = lambda i, s, pt: (0, 0)

    def page_spec(j):
        return pl.BlockSpec((1, page_rows, DA_DV), lambda i, s, pt: (pt[i, s * pps + j], 0, 0))

    grid_spec = pltpu.PrefetchScalarGridSpec(
        num_scalar_prefetch=1,
        grid=(bs, n_pages // pps),
        in_specs=[pl.BlockSpec((1, DA_HEADS, DA_DV), tok),
                  pl.BlockSpec((1, DA_HEADS, DA_DV), tok),
                  pl.BlockSpec((1, DA_HEADS, DA_DV), tok),
                  pl.BlockSpec((4, DA_DQK), fixed),
                  pl.BlockSpec((1, DA_DV), fixed)]
                 + [page_spec(j) for j in range(pps)] * 2,
        out_specs=pl.BlockSpec((1, DA_HEADS, DA_DV), tok),
        scratch_shapes=[pltpu.VMEM((2 * DA_HEADS, 1), F32),
                        pltpu.VMEM((2 * DA_HEADS, 1), F32),
                        pltpu.VMEM((2 * DA_HEADS, DA_DV), F32)])
    return pl.pallas_call(
        functools.partial(_attn_sample_kernel, pps=pps),
        grid_spec=grid_spec,
        out_shape=jax.ShapeDtypeStruct((bs, DA_HEADS, DA_DV), F32),
        compiler_params=pltpu.CompilerParams(dimension_semantics=("parallel", "arbitrary"),
                                             vmem_limit_bytes=VMEM_LIMIT),
        name="attn_sample",
    )(page_table, q, k_new, v_new, lam_vecs, wsub, *([cache_k] * pps), *([cache_v] * pps))


def _route(logits):
    lane = lax.broadcasted_iota(jnp.int32, logits.shape, 1).astype(F32)
    big = float(LANES)
    ninf = -jnp.inf

    def top(mask):
        val = jnp.where(mask, logits, ninf)
        mx = jnp.max(val, axis=-1, keepdims=True)
        idx = jnp.min(jnp.where(mask & (val == mx), lane, big), axis=-1, keepdims=True)
        return mx, idx

    gmask = (lane >= N_EXPERTS) & (lane < N_EXPERTS + N_GROUPS)
    gmax, gidx = top(gmask)
    g_w = 1.0 / jnp.sum(jnp.where(gmask, jnp.exp(logits - gmax), 0.0), axis=-1, keepdims=True)
    g_sel = gidx - N_EXPERTS
    emask = (lane >= g_sel * EXPERTS_PER_GROUP) & (lane < (g_sel + 1) * EXPERTS_PER_GROUP)
    v1, i1 = top(emask)
    v2, i2 = top(emask & (lane != i1))
    e21 = jnp.exp(v2 - v1)
    p1 = 1.0 / (1.0 + e21)
    return jnp.where(lane == i1, p1 * g_w, 0.0) + jnp.where(lane == i2, e21 * p1 * g_w, 0.0), g_sel


def _mix_route(g_ref, a_ref, x_ref, wo_ref, wnf_ref, wr_ref, precise):
    mix = (_mm(g_ref[...], wo_ref[0:GDN_V_W, :], _NN, precise)
           + _mm(a_ref[...], wo_ref[GDN_V_W:, :], _NN, precise))
    x1 = x_ref[...] + mix
    xn = _rms(x1, wnf_ref[...])
    wr = wr_ref[...]
    if precise:
        logits = _dot_f32(xn, wr)
    else:
        x_hi, w_hi = xn.astype(BF16), wr.astype(BF16)
        x_lo, w_lo = (xn - x_hi.astype(F32)).astype(BF16), (wr - w_hi.astype(F32)).astype(BF16)
        both = jnp.dot(x_hi, jnp.concatenate([w_hi, w_lo], axis=1), preferred_element_type=F32)
        logits = (both[:, 0:LANES] + both[:, LANES:]) + jnp.dot(x_lo, w_hi,
                                                                 preferred_element_type=F32)
    comb, g_sel = _route(logits)
    return x1, xn, comb, g_sel


def _expert(xn, comb, e_lane, wg, wu, wd):
    hg = jnp.dot(xn, wg.astype(BF16), preferred_element_type=F32)
    hu = jnp.dot(xn, wu.astype(BF16), preferred_element_type=F32)
    lane = lax.broadcasted_iota(jnp.int32, comb.shape, 1)
    cw = jnp.sum(jnp.where(lane == e_lane, comb, 0.0), axis=-1, keepdims=True)
    hh = (hg * _sigmoid(hg)) * hu * cw
    return jnp.dot(hh.astype(BF16), wd.astype(BF16), preferred_element_type=F32)


def _ffn_kernel(g_ref, a_ref, x_ref, wo_ref, wnf_ref, wr_ref, wg_ref, wu_ref, wd_ref, wfin_ref,
                y_ref, x1_scr, xn_scr, comb_scr, acc_scr, *, precise):
    e = pl.program_id(1)

    @pl.when(e == 0)
    def _():
        x1, xn, comb, _ = _mix_route(g_ref, a_ref, x_ref, wo_ref, wnf_ref, wr_ref, precise)
        x1_scr[...] = x1
        xn_scr[...] = xn.astype(BF16)
        comb_scr[...] = comb
        acc_scr[...] = jnp.zeros_like(acc_scr)

    acc_scr[...] += _expert(xn_scr[...], comb_scr[...], e, wg_ref[0], wu_ref[0], wd_ref[0])

    @pl.when(e == pl.num_programs(1) - 1)
    def _():
        y_ref[...] = _rms(x1_scr[...] + acc_scr[...], wfin_ref[...])


def _ffn(g_out, a_out, x, wo, wnf, wr, wg, wu, wd, wfin, tm):
    n = x.shape[0]
    assert n % tm == 0
    row = lambda i, e: (i, 0)
    fixed = lambda i, e: (0, 0)
    exp = lambda i, e: (e, 0, 0)
    return pl.pallas_call(
        functools.partial(_ffn_kernel, precise=wo.dtype == F32),
        grid=(n // tm, N_EXPERTS),
        in_specs=[pl.BlockSpec((tm, GDN_V_W), row),
                  pl.BlockSpec((tm, DA_V_W), row),
                  pl.BlockSpec((tm, D_MODEL), row),
                  pl.BlockSpec((GDN_V_W + DA_V_W, D_MODEL), fixed),
                  pl.BlockSpec((1, D_MODEL), fixed),
                  pl.BlockSpec((D_MODEL, LANES), fixed),
                  pl.BlockSpec((1, D_MODEL, D_EXPERT), exp),
                  pl.BlockSpec((1, D_MODEL, D_EXPERT), exp),
                  pl.BlockSpec((1, D_EXPERT, D_MODEL), exp),
                  pl.BlockSpec((1, D_MODEL), fixed)],
        out_specs=pl.BlockSpec((tm, D_MODEL), row),
        out_shape=jax.ShapeDtypeStruct((n, D_MODEL), F32),
        scratch_shapes=[pltpu.VMEM((tm, D_MODEL), F32),
                        pltpu.VMEM((tm, D_MODEL), BF16),
                        pltpu.VMEM((tm, LANES), F32),
                        pltpu.VMEM((tm, D_MODEL), F32)],
        compiler_params=pltpu.CompilerParams(dimension_semantics=("parallel", "arbitrary"),
                                             vmem_limit_bytes=VMEM_LIMIT),
        name="ffn",
    )(g_out, a_out, x, wo, wnf, wr, wg, wu, wd, wfin)


ROW_W = D_MODEL + LANES
MOE_TILE = 512
ROW_TILE = 2048


def _ffn_pre_kernel(g_ref, a_ref, x_ref, wo_ref, wnf_ref, wr_ref, rows_ref, info_ref, cnt_ref,
                    cnt_scr):
    @pl.when(pl.program_id(0) == 0)
    def _():
        cnt_scr[...] = jnp.zeros_like(cnt_scr)

    x1, _, comb, g_sel = _mix_route(g_ref, a_ref, x_ref, wo_ref, wnf_ref, wr_ref, False)
    tm = x1.shape[0]
    lane = lax.broadcasted_iota(jnp.int32, (tm, LANES), 1).astype(F32)
    onehot = lane == g_sel
    ri = lax.broadcasted_iota(jnp.int32, (tm, tm), 0)
    ci = lax.broadcasted_iota(jnp.int32, (tm, tm), 1)
    earlier = jnp.dot((ci < ri).astype(BF16), onehot.astype(BF16), preferred_element_type=F32)
    rank = jnp.sum(jnp.where(onehot, earlier + cnt_scr[...], 0.0), axis=-1, keepdims=True)
    cnt_scr[...] += jnp.sum(onehot.astype(F32), axis=0, keepdims=True)
    cnt_ref[...] = cnt_scr[...]
    rows_ref[:, 0:D_MODEL] = x1
    rows_ref[:, D_MODEL:] = comb
    packed = jnp.where(lane == 0.0, g_sel, jnp.where(lane == 1.0, rank, 0.0))
    r8 = lax.broadcasted_iota(jnp.int32, (8, LANES), 0)
    l8 = lax.broadcasted_iota(jnp.int32, (8, LANES), 1)
    sel = (r8 == l8).astype(BF16)
    hi = packed.astype(BF16)
    info_ref[...] = _dot_nt(sel, hi) + _dot_nt(sel, packed - hi.astype(F32))


def _ffn_pre(g_out, a_out, x, wo, wnf, wr, tm):
    n = x.shape[0]
    assert n % tm == 0
    row = lambda i: (i, 0)
    fixed = lambda i: (0, 0)
    return pl.pallas_call(
        _ffn_pre_kernel,
        grid=(n // tm,),
        in_specs=[pl.BlockSpec((tm, GDN_V_W), row),
                  pl.BlockSpec((tm, DA_V_W), row),
                  pl.BlockSpec((tm, D_MODEL), row),
                  pl.BlockSpec((GDN_V_W + DA_V_W, D_MODEL), fixed),
                  pl.BlockSpec((1, D_MODEL), fixed),
                  pl.BlockSpec((D_MODEL, LANES), fixed)],
        out_specs=[pl.BlockSpec((tm, ROW_W), row),
                   pl.BlockSpec((8, tm), lambda i: (0, i)),
                   pl.BlockSpec((1, LANES), fixed)],
        out_shape=[jax.ShapeDtypeStruct((n, ROW_W), F32),
                   jax.ShapeDtypeStruct((8, n), F32),
                   jax.ShapeDtypeStruct((1, LANES), F32)],
        scratch_shapes=[pltpu.VMEM((1, LANES), F32)],
        compiler_params=pltpu.CompilerParams(dimension_semantics=("arbitrary",),
                                             vmem_limit_bytes=VMEM_LIMIT),
        name="ffn_pre",
    )(g_out, a_out, x, wo, wnf, wr)


def _copy_rows(idx_ref, tile_ref, hbm_ref, sem, scatter):
    tm = tile_ref.shape[0]
    base = pl.program_id(0) * tm

    def body(r, carry):
        there, here = hbm_ref.at[pl.ds(idx_ref[base + r], 1)], tile_ref.at[pl.ds(r, 1)]
        src, dst = (here, there) if scatter else (there, here)
        pltpu.make_async_copy(src, dst, sem).start()
        return carry

    lax.fori_loop(0, tm, body, 0, unroll=8)
    there, here = hbm_ref.at[pl.ds(0, tm)], tile_ref
    src, dst = (here, there) if scatter else (there, here)
    pltpu.make_async_copy(src, dst, sem).wait()


def _scatter_kernel(idx_ref, tile_ref, init_ref, dst_ref, sem):
    del init_ref
    _copy_rows(idx_ref, tile_ref, dst_ref, sem, True)


def _gather_kernel(idx_ref, src_ref, tile_ref, sem):
    _copy_rows(idx_ref, tile_ref, src_ref, sem, False)


def _move_rows(idx, src, n_out, scatter):
    n, w = idx.shape[0], src.shape[1]
    assert n % ROW_TILE == 0
    any_spec = pl.BlockSpec(memory_space=pl.ANY)
    tile_spec = pl.BlockSpec((ROW_TILE, w), lambda i, idx: (i, 0))
    if scatter:
        operands = (idx, src, jnp.zeros((n_out, w), src.dtype))
        in_specs, out_spec = [tile_spec, any_spec], any_spec
    else:
        operands = (idx, src)
        in_specs, out_spec = [any_spec], tile_spec
    return pl.pallas_call(
        _scatter_kernel if scatter else _gather_kernel,
        grid_spec=pltpu.PrefetchScalarGridSpec(
            num_scalar_prefetch=1, grid=(n // ROW_TILE,),
            in_specs=in_specs, out_specs=out_spec,
            scratch_shapes=[pltpu.SemaphoreType.DMA(())]),
        out_shape=jax.ShapeDtypeStruct((n_out, w), src.dtype),
        input_output_aliases={2: 0} if scatter else {},
        compiler_params=pltpu.CompilerParams(dimension_semantics=("arbitrary",)),
        name="scatter_rows" if scatter else "gather_rows",
    )(*operands)


def _moe_groups_kernel(tg_ref, nt_ref, rows_ref, wnf_ref, wg_ref, wu_ref, wd_ref, wfin_ref, y_ref,
                       wg_bf, wu_bf, wd_bf):
    j = pl.program_id(0)

    @pl.when((j == 0) | (tg_ref[j] != tg_ref[jnp.maximum(j - 1, 0)]))
    def _():
        for e in range(EXPERTS_PER_GROUP):
            wg_bf[e] = wg_ref[e].astype(BF16)
            wu_bf[e] = wu_ref[e].astype(BF16)
            wd_bf[e] = wd_ref[e].astype(BF16)

    @pl.when(j < nt_ref[0])
    def _():
        x1 = rows_ref[:, 0:D_MODEL]
        comb = rows_ref[:, D_MODEL:]
        xn = _rms(x1, wnf_ref[...]).astype(BF16)
        first = tg_ref[j] * EXPERTS_PER_GROUP
        acc = jnp.zeros_like(x1)
        for e in range(EXPERTS_PER_GROUP):
            acc = acc + _expert(xn, comb, first + e, wg_bf[e], wu_bf[e], wd_bf[e])
        y_ref[...] = _rms(x1 + acc, wfin_ref[...])

    @pl.when(j >= nt_ref[0])
    def _():
        y_ref[...] = jnp.zeros_like(y_ref)


def _moe_groups(tile_group, n_tiles_used, rows, wnf, wg, wu, wd, wfin):
    n_tiles = tile_group.shape[0]
    fixed = lambda j, tg, nt: (0, 0)
    grp = lambda j, tg, nt: (tg[j], 0, 0)
    in_shape, out_shape = (EXPERTS_PER_GROUP, D_MODEL, D_EXPERT), (EXPERTS_PER_GROUP, D_EXPERT, D_MODEL)
    once = pl.Buffered(1)
    return pl.pallas_call(
        _moe_groups_kernel,
        grid_spec=pltpu.PrefetchScalarGridSpec(
            num_scalar_prefetch=2, grid=(n_tiles,),
            in_specs=[pl.BlockSpec((MOE_TILE, ROW_W), lambda j, tg, nt: (j, 0)),
                      pl.BlockSpec((1, D_MODEL), fixed),
                      pl.BlockSpec(in_shape, grp, pipeline_mode=once),
                      pl.BlockSpec(in_shape, grp, pipeline_mode=once),
                      pl.BlockSpec(out_shape, grp, pipeline_mode=once),
                      pl.BlockSpec((1, D_MODEL), fixed)],
            out_specs=pl.BlockSpec((MOE_TILE, D_MODEL), lambda j, tg, nt: (j, 0)),
            scratch_shapes=[pltpu.VMEM(in_shape, BF16), pltpu.VMEM(in_shape, BF16),
                            pltpu.VMEM(out_shape, BF16)]),
        out_shape=jax.ShapeDtypeStruct((n_tiles * MOE_TILE, D_MODEL), F32),
        compiler_params=pltpu.CompilerParams(dimension_semantics=("arbitrary",),
                                             vmem_limit_bytes=VMEM_LIMIT),
        name="moe_groups",
    )(tile_group, n_tiles_used, rows, wnf, wg, wu, wd, wfin)


def _ffn_sorted(g_out, a_out, x, wo, wnf, wr, wg, wu, wd, wfin):
    n = x.shape[0]
    rows, info, cnt = _ffn_pre(g_out, a_out, x, wo, wnf, wr, FFN_PRE_TILE)
    grp, rank = info[0].astype(jnp.int32), info[1].astype(jnp.int32)
    tiles = (cnt[0, :N_GROUPS].astype(jnp.int32) + MOE_TILE - 1) // MOE_TILE
    ends = jnp.cumsum(tiles)
    dest = (ends - tiles)[grp] * MOE_TILE + rank
    n_tiles = n // MOE_TILE + N_GROUPS
    tile_group = jnp.minimum(
        jnp.sum(jnp.arange(n_tiles, dtype=jnp.int32)[:, None] >= ends[None, :], axis=1),
        N_GROUPS - 1).astype(jnp.int32)
    rows_sorted = _move_rows(dest, rows, n_tiles * MOE_TILE, True)
    y_sorted = _moe_groups(tile_group, ends[N_GROUPS - 1:], rows_sorted, wnf, wg, wu, wd, wfin)
    return _move_rows(dest, y_sorted, n, False)


def kernel(x_prompt, x_sample, cache_k, cache_v, state_conv, state_gdn, page_table, meta_tokens,
           w_norm_mix, w_in, w_conv, a_log, dt_bias, w_gnorm, lambda_q1, lambda_k1, lambda_q2,
           lambda_k2, w_subln, w_out, w_norm_ffn, w_group, w_expert, w_gate, w_up, w_down, w_final):
    l = 0
    bp, seq = x_prompt.shape[0], x_prompt.shape[1]
    bs = x_sample.shape[0]
    n_pool, page = cache_k.shape[1], cache_k.shape[2]
    past = page_table.shape[1] * page
    C = GDN_CHUNK

    w_in_p = _prep_w_in(w_in[l])
    wn_mix = w_norm_mix[l][None, :]
    pad_lanes = lambda v: jnp.pad(v, (0, LANES - v.shape[0]))[None, :]
    alog, dtb = pad_lanes(a_log[l]), pad_lanes(dt_bias[l])
    wgn = w_gnorm[l][None, :]
    lam_vecs = jnp.stack([lambda_q1[l], lambda_k1[l], lambda_q2[l], lambda_k2[l]])
    wsub = w_subln[l][None, :]
    w_router = jnp.pad(jnp.concatenate([w_expert[l], w_group[l]], axis=1),
                       ((0, 0), (0, LANES - N_EXPERTS - N_GROUPS)))
    ffn_w = (w_norm_ffn[l][None, :], w_router, w_gate[l], w_up[l], w_down[l], w_final[None, :])

    pos_small = jnp.concatenate([jnp.arange(N_META), jnp.full((bs,), past)])
    rows_small = jnp.concatenate([meta_tokens, x_sample[:, 0, :]], axis=0)
    c_s, z_s, ab_s, q_s, k_s, v_s = _proj(rows_small, wn_mix, w_in_p, _rope_tables(pos_small),
                                           N_META + bs)
    xp = x_prompt.reshape(bp * seq, D_MODEL)
    c_p, z_p, ab_p, q_p, k_p, v_p = _proj(xp, wn_mix, w_in_p.astype(BF16),
                                           _rope_tables(N_META + jnp.arange(seq)), PROJ_TILE)

    front = lambda a: jnp.pad(a[:N_META], ((C - N_META, 0), (0, 0)))[None]
    zero_tail = jnp.zeros((1, 8, CONV_CH), F32)
    zero_state = jnp.zeros((1, GDN_HEADS, GDN_DK, GDN_DV), F32)
    _, s_meta = _gdn(front(c_s), front(z_s), front(ab_s), zero_tail, zero_state,
                     w_conv[l], alog, dtb, wgn, C)
    tail_meta = jnp.pad(c_s[N_META - (CONV_W - 1):N_META], ((8 - (CONV_W - 1), 0), (0, 0)))[None]
    c_p3 = c_p.reshape(bp, seq, CONV_CH)
    g_out, p_gdn = _gdn(c_p3, z_p.reshape(bp, seq, GDN_V_W), ab_p.reshape(bp, seq, AB_W),
                        tail_meta, s_meta, w_conv[l], alog, dtb, wgn, GDN_TILE)
    p_conv = c_p3[:, seq - (CONV_W - 1):, :]

    p_k, p_v = _kv_out(k_p.reshape(bp, seq, DA_QK_W), v_p.reshape(bp, seq, DA_V_W),
                       k_s[:N_META], v_s[:N_META])
    meta_blk = lambda m: jnp.pad(m[:N_META], ((0, LANES - N_META), (0, 0)))
    a_out = _attn_prompt(q_p.reshape(bp, seq, DA_QK_W), k_p.reshape(bp, seq, DA_QK_W),
                         v_p.reshape(bp, seq, DA_V_W), meta_blk(k_s), meta_blk(v_s),
                         lam_vecs, wsub, ATTN_BQ, ATTN_BK, ATTN_HEADS_PER_STEP)

    y_prompt = _ffn_sorted(g_out.reshape(bp * seq, GDN_V_W), a_out.reshape(bp * seq, DA_V_W), xp,
                           w_out[l].astype(BF16), *ffn_w).reshape(bp, seq, D_MODEL)

    sm = lambda a: a[N_META:]
    g_s, s_gdn = _gdn_step(sm(c_s), sm(z_s), sm(ab_s), jnp.swapaxes(state_conv[l], 0, 1),
                           state_gdn[l], w_conv[l], alog, dtb, wgn)
    s_conv = jnp.concatenate([state_conv[l][:, 1:, :], sm(c_s)[:, None, :]], axis=1)
    heads = lambda a: sm(a).reshape(bs, DA_HEADS, DA_DV)
    a_s = _attn_sample(heads(q_s), heads(k_s), heads(v_s),
                       cache_k[l].reshape(n_pool, page * DA_HEADS, 2 * DA_DQK),
                       cache_v[l].reshape(n_pool, page * DA_HEADS, DA_DV),
                       page_table, lam_vecs, wsub, PAGES_PER_STEP)
    y_sample = _ffn(g_s, a_s.reshape(bs, DA_V_W), x_sample[:, 0, :], w_out[l], *ffn_w,
                    bs).reshape(bs, 1, D_MODEL)

    return (y_prompt, y_sample, p_conv[None], p_gdn[None],
            p_k.reshape(1, bp, seq + N_META, DA_HEADS, 2 * DA_DQK),
            p_v.reshape(1, bp, seq + N_META, DA_HEADS, DA_DV),
            s_conv[None], s_gdn[None],
            sm(k_s).reshape(1, bs, 1, DA_HEADS, 2 * DA_DQK),
            sm(v_s).reshape(1, bs, 1, DA_HEADS, DA_DV))
```

```python
import functools
import math

import jax
import jax.numpy as jnp
import numpy as np
from jax import lax
from jax.experimental import pallas as pl
from jax.experimental.pallas import tpu as pltpu

F32 = jnp.float32
BF16 = jnp.bfloat16
FP8 = jnp.float8_e4m3fn
FP8_MAX = 448.0
LO_SCALE = 256.0

D_MODEL = 1024
N_META = 16
EPS = 1e-6
GDN_HEADS = 4
GDN_DK = 128
GDN_DV = 128
CONV_W = 4
GDN_QK_W = GDN_HEADS * GDN_DK
GDN_V_W = GDN_HEADS * GDN_DV
CONV_CH = 2 * GDN_QK_W + GDN_V_W
DA_HEADS = 4
DA_DQK = 64
DA_DV = 2 * DA_DQK
DA_QK_W = DA_HEADS * 2 * DA_DQK
DA_V_W = DA_HEADS * DA_DV
ROPE_DIM = DA_DQK // 4
ROPE_HALF = ROPE_DIM // 2
ROPE_THETA = 500000.0
N_GROUPS = 4
EXPERTS_PER_GROUP = 8
N_EXPERTS = N_GROUPS * EXPERTS_PER_GROUP
D_EXPERT = 256
LAM_INIT = 0.8 - 0.6 * math.exp(-0.3 * 0)

LANES = 128
GDN_CHUNK = 64
GDN_STACK = 2
AB_W = LANES
PROJ_PAD_W = CONV_CH + GDN_V_W + 2 * DA_QK_W + DA_V_W + AB_W
VMEM_LIMIT = 56 * 1024 * 1024

PROJ_TILE = 512
GDN_TILE = 512
ATTN_BQ, ATTN_BK = 256, 512
ATTN_HEADS_PER_STEP = 2
PAGES_PER_STEP = 32
FFN_PRE_TILE = 512


_NN = (((1,), (0,)), ((), ()))
_NT = (((1,), (1,)), ((), ()))
_TN = (((0,), (0,)), ((), ()))


def _mm(a, b, dims=_NN, precise=False):
    if precise:
        return lax.dot_general(a.astype(F32), b.astype(F32), dims, preferred_element_type=F32,
                               precision=lax.Precision.HIGHEST)
    return lax.dot_general(a.astype(BF16), b.astype(BF16), dims, preferred_element_type=F32)


def _dot_nt(a, b):
    return _mm(a, b, _NT)


def _dot_f32(a, b):
    return _mm(a, b, _NN, True)


def _mask_dot(mask, x, dims):
    hi = x.astype(BF16)
    mid = (x - hi.astype(F32)).astype(BF16)
    lo = x - hi.astype(F32) - mid.astype(F32)
    m = mask.astype(BF16)
    return (_mm(m, hi, dims) + _mm(m, mid, dims)) + _mm(m, lo, dims)


def _sigmoid(x):
    return 1.0 / (1.0 + jnp.exp(-x))


def _softplus(x):
    return jnp.maximum(x, 0.0) + jnp.log(1.0 + jnp.exp(-jnp.abs(x)))


def _rms(x, w):
    return x * lax.rsqrt(jnp.mean(x * x, axis=-1, keepdims=True) + EPS) * w


def _rope(x, cs, sn_lo, sn_hi):
    w = x.shape[1]
    cs = jnp.concatenate([cs] * DA_HEADS, axis=1)
    sn_lo = jnp.concatenate([sn_lo] * DA_HEADS, axis=1)
    sn_hi = jnp.concatenate([sn_hi] * DA_HEADS, axis=1)
    up = pltpu.roll(x, w - ROPE_HALF, 1)
    dn = pltpu.roll(x, ROPE_HALF, 1)
    return x * cs + up * sn_lo + dn * sn_hi


def _proj_kernel(x_ref, wn_ref, w_ref, cs_ref, snl_ref, snh_ref,
                 c_ref, z_ref, ab_ref, q_ref, k_ref, v_ref, *, precise):
    h = _rms(x_ref[...], wn_ref[...])
    h = h if precise else h.astype(BF16)
    o = 0
    c_ref[...] = _mm(h, w_ref[:, o:o + CONV_CH], _NN, precise)
    o += CONV_CH
    z_ref[...] = _mm(h, w_ref[:, o:o + GDN_V_W], _NN, precise)
    o += GDN_V_W
    q = _mm(h, w_ref[:, o:o + DA_QK_W], _NN, precise)
    o += DA_QK_W
    k = _mm(h, w_ref[:, o:o + DA_QK_W], _NN, precise)
    o += DA_QK_W
    v_ref[...] = _mm(h, w_ref[:, o:o + DA_V_W], _NN, precise)
    o += DA_V_W
    ab_ref[...] = _mm(h, w_ref[:, o:o + AB_W], _NN, precise)
    cs, snl, snh = cs_ref[...], snl_ref[...], snh_ref[...]
    q_ref[...] = _rope(q, cs, snl, snh)
    k_ref[...] = _rope(k, cs, snl, snh)


def _proj(x, wn, w, tabs, tm):
    n = x.shape[0]
    assert n % tm == 0
    nt = tabs[0].shape[0] // tm
    row = lambda i: (i, 0)
    fixed = lambda i: (0, 0)
    tab = lambda i: (i % nt, 0)
    widths = (CONV_CH, GDN_V_W, AB_W, DA_QK_W, DA_QK_W, DA_V_W)
    return pl.pallas_call(
        functools.partial(_proj_kernel, precise=w.dtype == F32),
        grid=(n // tm,),
        in_specs=[pl.BlockSpec((tm, D_MODEL), row),
                  pl.BlockSpec((1, D_MODEL), fixed),
                  pl.BlockSpec((D_MODEL, PROJ_PAD_W), fixed),
                  pl.BlockSpec((tm, LANES), tab),
                  pl.BlockSpec((tm, LANES), tab),
                  pl.BlockSpec((tm, LANES), tab)],
        out_specs=[pl.BlockSpec((tm, wd), row) for wd in widths],
        out_shape=[jax.ShapeDtypeStruct((n, wd), F32) for wd in widths],
        compiler_params=pltpu.CompilerParams(dimension_semantics=("parallel",),
                                             vmem_limit_bytes=VMEM_LIMIT),
        name="proj",
    )(x, wn, w, *tabs)


def _rope_tables(pos):
    inv_freq = ROPE_THETA ** (-jnp.arange(ROPE_HALF, dtype=F32) * 2.0 / ROPE_DIM)
    ang = pos.astype(F32)[:, None] * inv_freq[None, :]
    cos, sin = jnp.cos(ang), jnp.sin(ang)
    t = pos.shape[0]
    one = jnp.ones((t, DA_DQK - ROPE_DIM), F32)
    zero = jnp.zeros((t, DA_DQK - ROPE_DIM), F32)
    zh = jnp.zeros((t, ROPE_HALF), F32)
    cs = jnp.concatenate([cos, cos, one] * 2, axis=1)
    snl = jnp.concatenate([-sin, zh, zero] * 2, axis=1)
    snh = jnp.concatenate([zh, sin, zero] * 2, axis=1)
    return cs, snl, snh


def _prep_w_in(w_in):
    c, z, a, b, dq, dk, dv = jnp.split(
        w_in, np.cumsum([CONV_CH, GDN_V_W, GDN_HEADS, GDN_HEADS, DA_QK_W, DA_QK_W]).tolist(), axis=-1)
    pad = jnp.zeros((D_MODEL, AB_W - 2 * GDN_HEADS), w_in.dtype)
    return jnp.concatenate([c, z, dq, dk, dv, a, b, pad], axis=-1)


def _gdn_kernel(c_ref, z_ref, ab_ref, tail_ref, s0_ref, wconv_ref, alog_ref, dtb_ref, wg_ref,
                o_ref, sfin_ref, cbuf, ybuf, s_scr, u0_scr, wq_scr, kw_scr, qkd_scr, el_scr, *, tt):
    C, H = GDN_CHUNK, GDN_HEADS
    HS = GDN_STACK
    R = HS * C
    shift = C.bit_length() - 1
    t = pl.program_id(1)

    @pl.when(t == 0)
    def _():
        cbuf[0:8, :] = tail_ref[0]
        s_scr[...] = s0_ref[0]

    cbuf[8:8 + tt, :] = c_ref[0]
    acc = cbuf[8:8 + tt, :] * wconv_ref[CONV_W - 1:CONV_W, :]
    for i in range(CONV_W - 1):
        off = 8 - (CONV_W - 1) + i
        acc = acc + cbuf[off:off + tt, :] * wconv_ref[i:i + 1, :]
    ybuf[...] = acc * _sigmoid(acc)
    cbuf[0:8, :] = cbuf[tt:tt + 8, :]

    ii = lax.broadcasted_iota(jnp.int32, (R, R), 0)
    jj = lax.broadcasted_iota(jnp.int32, (R, R), 1)
    same_head = jnp.right_shift(ii, shift) == jnp.right_shift(jj, shift)
    strict = same_head & (ii > jj)
    causal = same_head & (ii >= jj)
    eye = (ii == jj).astype(F32)
    merge_masks = []
    s = 1
    while s < C:
        sh = s.bit_length() - 1
        merge_masks.append((jnp.right_shift(ii, sh + 1) == jnp.right_shift(jj, sh + 1))
                           & (jnp.right_shift(ii, sh) != jnp.right_shift(jj, sh)) & (ii > jj))
        s *= 2
    ci_ = lax.broadcasted_iota(jnp.int32, (C, C), 0)
    cj_ = lax.broadcasted_iota(jnp.int32, (C, C), 1)
    tri = (ci_ >= cj_).astype(F32)
    lane = lax.broadcasted_iota(jnp.int32, (R, LANES), 1)
    rhead = jnp.right_shift(lax.broadcasted_iota(jnp.int32, (R, LANES), 0), shift)
    ones8 = jnp.ones((8, LANES), F32)
    rep = lambda a: jnp.concatenate([a] * HS, axis=0)
    pick = lambda a, off: jnp.sum(jnp.where(lane == rhead + off, a, 0.0), axis=-1, keepdims=True)

    def stacked(rows, base, scale, h0):
        parts = []
        for h in range(h0, h0 + HS):
            x = ybuf[rows, base + h * GDN_DK:base + (h + 1) * GDN_DK]
            if scale is not None:
                x = x * lax.rsqrt(jnp.sum(x * x, axis=-1, keepdims=True) + EPS) * scale
            parts.append(x)
        return jnp.concatenate(parts, axis=0)

    n_stack = H // HS
    units = range((tt // C) * n_stack)
    low, rhs, qes = [], [], []
    for ci in range(tt // C):
        rows = slice(ci * C, (ci + 1) * C)
        ab = ab_ref[0, rows, :]
        gfull = -jnp.exp(alog_ref[...]) * _softplus(ab + dtb_ref[...])
        bfull = _sigmoid(ab)
        gcum = _mask_dot(tri, gfull, _NN)
        glast = gcum[C - 1:C, :]
        el_scr[ci] = jnp.broadcast_to(jnp.exp(glast), (8, LANES))
        for p in range(n_stack):
            un, h0 = ci * n_stack + p, p * HS
            g_m = jnp.where(lane == rhead + h0, rep(gcum), 0.0)
            g_col = jnp.sum(g_m, axis=-1, keepdims=True)
            g_row = _mask_dot(ones8, g_m, _NT)[0:1, :]
            beta = pick(rep(bfull), H + h0)
            gl_col = pick(jnp.broadcast_to(glast, (R, LANES)), h0)
            decay = jnp.exp(jnp.where(causal, g_col - g_row, -jnp.inf))
            q = stacked(rows, 0, GDN_DK ** -0.5, h0)
            k = stacked(rows, GDN_QK_W, 1.0, h0)
            v = stacked(rows, 2 * GDN_QK_W, None, h0)
            kq = _mm(jnp.concatenate([k, q], axis=0), k, _NT)
            low.append(jnp.where(strict, kq[0:R] * decay * beta, 0.0))
            e_g = jnp.exp(g_col)
            rhs.append(jnp.concatenate([beta * v, (beta * e_g) * k], axis=1).astype(BF16))
            qes.append(q * e_g)
            kw_scr[un] = (k * jnp.exp(gl_col - g_col)).astype(BF16)
            qkd_scr[un] = jnp.where(causal, kq[R:2 * R] * decay, 0.0).astype(BF16)
    tinv = [eye - jnp.where(merge_masks[0], low[un], 0.0) for un in units]
    for m in merge_masks[1:]:
        half = [_mm(tinv[un], jnp.where(m, low[un], 0.0)) for un in units]
        tinv = [tinv[un] - _mm(half[un], tinv[un]) for un in units]
    for un in units:
        w2 = _mm(tinv[un], rhs[un])
        u0_scr[un] = w2[:, 0:GDN_DV]
        wk = w2[:, GDN_DV:]
        wq_scr[un] = jnp.concatenate(
            [x[h * C:(h + 1) * C] for h in range(HS) for x in (wk, qes[un])], axis=0).astype(BF16)

    st = [s_scr[h] for h in range(H)]
    for ci in range(tt // C):
        rows = slice(ci * C, (ci + 1) * C)
        el = el_scr[ci]
        for p in range(n_stack):
            un = ci * n_stack + p
            us, os_ = [], []
            for hh in range(HS):
                r = jnp.dot(wq_scr[un, 2 * hh * C:2 * (hh + 1) * C, :], st[p * HS + hh].astype(BF16),
                            preferred_element_type=F32)
                us.append(u0_scr[un, hh * C:(hh + 1) * C, :] - r[0:C])
                os_.append(r[C:2 * C])
            intra = jnp.dot(qkd_scr[un], jnp.concatenate(us, axis=0).astype(BF16),
                            preferred_element_type=F32)
            for hh in range(HS):
                h = p * HS + hh
                st[h] = (st[h] * el[0:1, h:h + 1]
                         + _mm(kw_scr[un, hh * C:(hh + 1) * C, :], us[hh], _TN))
                o = os_[hh] + intra[hh * C:(hh + 1) * C]
                zz = z_ref[0, rows, h * GDN_DV:(h + 1) * GDN_DV]
                o_ref[0, rows, h * GDN_DV:(h + 1) * GDN_DV] = (
                    _rms(o, wg_ref[...]) * (zz * _sigmoid(zz))).astype(o_ref.dtype)
    for h in range(H):
        s_scr[h] = st[h]

    @pl.when(t == pl.num_programs(1) - 1)
    def _():
        sfin_ref[0] = s_scr[...]


def _gdn_step_kernel(c_ref, z_ref, ab_ref, cs_ref, s_ref, wconv_ref, alog_ref, dtb_ref, wg_ref,
                     o_ref, snew_ref, *, nb):
    H = GDN_HEADS
    acc = c_ref[...] * wconv_ref[CONV_W - 1:CONV_W, :]
    for i in range(CONV_W - 1):
        acc = acc + cs_ref[i] * wconv_ref[i:i + 1, :]
    y = acc * _sigmoid(acc)
    ab = ab_ref[...]
    e_g = jnp.exp(-jnp.exp(alog_ref[...]) * _softplus(ab + dtb_ref[...]))
    bfull = _sigmoid(ab)
    unit = lambda x: x * lax.rsqrt(jnp.sum(x * x, axis=-1, keepdims=True) + EPS)
    qs = [unit(y[:, h * GDN_DK:(h + 1) * GDN_DK]) * (GDN_DK ** -0.5) for h in range(H)]
    ks = [unit(y[:, GDN_QK_W + h * GDN_DK:GDN_QK_W + (h + 1) * GDN_DK]) for h in range(H)]
    vs = [y[:, 2 * GDN_QK_W + h * GDN_DV:2 * GDN_QK_W + (h + 1) * GDN_DV] for h in range(H)]
    cols = jnp.concatenate(ks + qs, axis=0).T
    for h in range(H):
        ks_rows, qs_rows = [], []
        for s in range(nb):
            state = s_ref[s, h]
            kcol = cols[:, h * nb + s:h * nb + s + 1]
            qcol = cols[:, (H + h) * nb + s:(H + h) * nb + s + 1]
            ks_rows.append(jnp.sum(kcol * state, axis=0, keepdims=True))
            qs_rows.append(jnp.sum(qcol * state, axis=0, keepdims=True))
        eg_h = e_g[:, h:h + 1]
        u = bfull[:, H + h:H + h + 1] * (vs[h] - eg_h * jnp.concatenate(ks_rows, axis=0))
        o = (eg_h * jnp.concatenate(qs_rows, axis=0)
             + jnp.sum(qs[h] * ks[h], axis=-1, keepdims=True) * u)
        eg_b = jnp.broadcast_to(eg_h, (nb, GDN_DV))
        for s in range(nb):
            kcol = cols[:, h * nb + s:h * nb + s + 1]
            snew_ref[s, h] = eg_b[s:s + 1, :] * s_ref[s, h] + kcol * u[s:s + 1, :]
        zz = z_ref[:, h * GDN_DV:(h + 1) * GDN_DV]
        o_ref[:, h * GDN_DV:(h + 1) * GDN_DV] = _rms(o, wg_ref[...]) * (zz * _sigmoid(zz))


def _gdn_step(c, z, ab, conv_state, s, wconv, alog, dtb, wg):
    bs = c.shape[0]
    nb = LANES // (2 * GDN_HEADS)
    assert bs % nb == 0
    row = lambda i: (i, 0)
    fixed = lambda i: (0, 0)
    state = pl.BlockSpec((nb, GDN_HEADS, GDN_DK, GDN_DV), lambda i: (i, 0, 0, 0))
    return pl.pallas_call(
        functools.partial(_gdn_step_kernel, nb=nb),
        grid=(bs // nb,),
        in_specs=[pl.BlockSpec((nb, CONV_CH), row),
                  pl.BlockSpec((nb, GDN_V_W), row),
                  pl.BlockSpec((nb, AB_W), row),
                  pl.BlockSpec((CONV_W - 1, nb, CONV_CH), lambda i: (0, i, 0)),
                  state,
                  pl.BlockSpec((CONV_W, CONV_CH), fixed),
                  pl.BlockSpec((1, LANES), fixed),
                  pl.BlockSpec((1, LANES), fixed),
                  pl.BlockSpec((1, GDN_DV), fixed)],
        out_specs=[pl.BlockSpec((nb, GDN_V_W), row), state],
        out_shape=[jax.ShapeDtypeStruct((bs, GDN_V_W), F32),
                   jax.ShapeDtypeStruct(s.shape, F32)],
        compiler_params=pltpu.CompilerParams(dimension_semantics=("parallel",),
                                             vmem_limit_bytes=VMEM_LIMIT),
        name="gdn_step",
    )(c, z, ab, conv_state, s, wconv, alog, dtb, wg)


def _gdn(c, z, ab, tail, s0, wconv, alog, dtb, wg, tt):
    b, t, _ = c.shape
    assert t % tt == 0 and tt % GDN_CHUNK == 0
    rs = GDN_STACK * GDN_CHUNK
    nc, nu = tt // GDN_CHUNK, (tt // GDN_CHUNK) * (GDN_HEADS // GDN_STACK)
    seq = lambda i, j: (i, j, 0)
    fixed2 = lambda i, j: (0, 0)
    tail_map = (lambda i, j: (i, 0, 0)) if tail.shape[0] == b else (lambda i, j: (0, 0, 0))
    s0_map = (lambda i, j: (i, 0, 0, 0)) if s0.shape[0] == b else (lambda i, j: (0, 0, 0, 0))
    return pl.pallas_call(
        functools.partial(_gdn_kernel, tt=tt),
        grid=(b, t // tt),
        in_specs=[pl.BlockSpec((1, tt, CONV_CH), seq),
                  pl.BlockSpec((1, tt, GDN_V_W), seq),
                  pl.BlockSpec((1, tt, AB_W), seq),
                  pl.BlockSpec((1, 8, CONV_CH), tail_map),
                  pl.BlockSpec((1, GDN_HEADS, GDN_DK, GDN_DV), s0_map),
                  pl.BlockSpec((CONV_W, CONV_CH), fixed2),
                  pl.BlockSpec((1, LANES), fixed2),
                  pl.BlockSpec((1, LANES), fixed2),
                  pl.BlockSpec((1, GDN_DV), fixed2)],
        out_specs=[pl.BlockSpec((1, tt, GDN_V_W), seq),
                   pl.BlockSpec((1, GDN_HEADS, GDN_DK, GDN_DV), lambda i, j: (i, 0, 0, 0))],
        out_shape=[jax.ShapeDtypeStruct((b, t, GDN_V_W), BF16),
                   jax.ShapeDtypeStruct((b, GDN_HEADS, GDN_DK, GDN_DV), F32)],
        scratch_shapes=[pltpu.VMEM((tt + 8, CONV_CH), F32),
                        pltpu.VMEM((tt, CONV_CH), F32),
                        pltpu.VMEM((GDN_HEADS, GDN_DK, GDN_DV), F32),
                        pltpu.VMEM((nu, rs, GDN_DV), F32),
                        pltpu.VMEM((nu, 2 * rs, GDN_DK), BF16),
                        pltpu.VMEM((nu, rs, GDN_DK), BF16),
                        pltpu.VMEM((nu, rs, rs), BF16),
                        pltpu.VMEM((nc, 8, LANES), F32)],
        compiler_params=pltpu.CompilerParams(dimension_semantics=("parallel", "arbitrary"),
                                             vmem_limit_bytes=VMEM_LIMIT),
        name="gdn",
    )(c, z, ab, tail, s0, wconv, alog, dtb, wg)


def _kv_out_kernel(k_ref, v_ref, km_ref, vm_ref, ko_ref, vo_ref):
    t = k_ref.shape[1]
    for src, meta, dst in ((k_ref, km_ref, ko_ref), (v_ref, vm_ref, vo_ref)):
        for h in range(DA_HEADS):
            cols = slice(h * DA_DV, (h + 1) * DA_DV)
            dst[0, pl.ds(h, N_META, stride=DA_HEADS), :] = meta[:, cols]
            dst[0, pl.ds(N_META * DA_HEADS + h, t, stride=DA_HEADS), :] = src[0, :, cols]


def _kv_out(k, v, k_meta, v_meta):
    b, t, w = k.shape
    rows = (t + N_META) * DA_HEADS
    seq = pl.BlockSpec((1, t, w), lambda i: (i, 0, 0))
    meta = pl.BlockSpec((N_META, w), lambda i: (0, 0))
    out = pl.BlockSpec((1, rows, DA_DV), lambda i: (i, 0, 0))
    return pl.pallas_call(
        _kv_out_kernel,
        grid=(b,),
        in_specs=[seq, seq, meta, meta],
        out_specs=[out, out],
        out_shape=[jax.ShapeDtypeStruct((b, rows, DA_DV), F32)] * 2,
        compiler_params=pltpu.CompilerParams(dimension_semantics=("parallel",),
                                             vmem_limit_bytes=VMEM_LIMIT),
        name="kv_out",
    )(k, v, k_meta, v_meta)


def _lambda(lam_ref):
    lv = lam_ref[...]
    s1 = jnp.sum(lv[0:1, :] * lv[1:2, :], axis=-1, keepdims=True)
    s2 = jnp.sum(lv[2:3, :] * lv[3:4, :], axis=-1, keepdims=True)
    return jnp.exp(s1) - jnp.exp(s2) + LAM_INIT


def _split_components(q):
    lane = lax.broadcasted_iota(jnp.int32, q.shape, 1)
    first = (lane % DA_DV) < DA_DQK
    return jnp.where(first, q, 0.0).astype(BF16), jnp.where(first, 0.0, q).astype(BF16)


def _online_update(state, s, vb):
    m, l, acc = state
    m_new = jnp.maximum(m, jnp.max(s, axis=-1, keepdims=True))
    p = jnp.exp2(s - m_new)
    alpha = jnp.exp2(m - m_new)
    return (m_new, alpha * l + jnp.sum(p, axis=-1, keepdims=True),
            alpha * acc + jnp.dot(p.astype(BF16), vb, preferred_element_type=F32))


def _attn_prompt_kernel(q_ref, k_ref, v_ref, km_ref, vm_ref, lam_ref, wsub_ref, o_ref, *, bq, bk):
    qi = pl.program_id(2)
    heads = [slice(h * DA_DV, (h + 1) * DA_DV) for h in range(q_ref.shape[2] // DA_DV)]
    qs = [_split_components(q_ref[0, :, hs] * (DA_DQK ** -0.5 * math.log2(math.e))) for hs in heads]
    neg = jnp.full((bq, 1), -jnp.inf, F32)
    zero = jnp.zeros((bq, 1), F32)
    zacc = jnp.zeros((bq, DA_DV), F32)
    init = tuple(((neg, zero, zacc), (neg, zero, zacc)) for _ in heads)

    def block(kblk, vblk, mask, st):
        out = []
        for (q1, q2), hs, sth in zip(qs, heads, st):
            kb, vb = kblk[:, hs].astype(BF16), vblk[:, hs].astype(BF16)
            s1, s2 = _dot_nt(q1, kb), _dot_nt(q2, kb)
            if mask is not None:
                s1 = jnp.where(mask, s1, -jnp.inf)
                s2 = jnp.where(mask, s2, -jnp.inf)
            out.append((_online_update(sth[0], s1, vb), _online_update(sth[1], s2, vb)))
        return tuple(out)

    n_full = lax.shift_right_logical(qi * bq, bk.bit_length() - 1)

    def body(kb, st):
        r0 = pl.multiple_of(kb * bk, bk)
        return block(k_ref[0, pl.ds(r0, bk), :], v_ref[0, pl.ds(r0, bk), :], None, st)

    st = lax.fori_loop(0, n_full, body, init)
    r0 = pl.multiple_of(n_full * bk, bk)
    kcat = jnp.concatenate([km_ref[...], k_ref[0, pl.ds(r0, bk), :]], axis=0)
    vcat = jnp.concatenate([vm_ref[...], v_ref[0, pl.ds(r0, bk), :]], axis=0)
    ri = lax.broadcasted_iota(jnp.int32, (bq, LANES + bk), 0)
    ci = lax.broadcasted_iota(jnp.int32, (bq, LANES + bk), 1)
    seen = (ci < N_META) | ((ci >= LANES) & (ci - LANES <= ri + (qi * bq - n_full * bk)))
    st = block(kcat, vcat, seen, st)
    lam = _lambda(lam_ref)
    for hs, ((m1, l1, a1), (m2, l2, a2)) in zip(heads, st):
        o = a1 / l1 - lam * (a2 / l2)
        o_ref[0, :, hs] = (_rms(o, wsub_ref[...]) * (1.0 - LAM_INIT)).astype(o_ref.dtype)


def _attn_prompt(q, k, v, k_meta, v_meta, lam_vecs, wsub, bq, bk, hps):
    b, t, _ = q.shape
    assert t % bk == 0 and bk % bq == 0 and k_meta.shape[0] == LANES and DA_HEADS % hps == 0
    w = hps * DA_DV
    return pl.pallas_call(
        functools.partial(_attn_prompt_kernel, bq=bq, bk=bk),
        grid=(b, DA_HEADS // hps, t // bq),
        in_specs=[pl.BlockSpec((1, bq, w), lambda i, h, j: (i, j, h)),
                  pl.BlockSpec((1, t, w), lambda i, h, j: (i, 0, h)),
                  pl.BlockSpec((1, t, w), lambda i, h, j: (i, 0, h)),
                  pl.BlockSpec((LANES, w), lambda i, h, j: (0, h)),
                  pl.BlockSpec((LANES, w), lambda i, h, j: (0, h)),
                  pl.BlockSpec((4, DA_DQK), lambda i, h, j: (0, 0)),
                  pl.BlockSpec((1, DA_DV), lambda i, h, j: (0, 0))],
        out_specs=pl.BlockSpec((1, bq, w), lambda i, h, j: (i, j, h)),
        out_shape=jax.ShapeDtypeStruct((b, t, DA_V_W), BF16),
        compiler_params=pltpu.CompilerParams(
            dimension_semantics=("parallel", "parallel", "arbitrary"),
            vmem_limit_bytes=VMEM_LIMIT),
        name="attn_prompt",
    )(q, k, v, k_meta, v_meta, lam_vecs, wsub)


def _attn_sample_kernel(pt_ref, q_ref, kn_ref, vn_ref, lam_ref, wsub_ref, *rest, pps):
    k_refs, v_refs = rest[:pps], rest[pps:2 * pps]
    o_ref, m_scr, l_scr, acc_scr = rest[2 * pps:]
    step = pl.program_id(1)
    nrow = 2 * DA_HEADS
    twice = lambda x: jnp.concatenate([x, x], axis=0)
    rowq = lax.broadcasted_iota(jnp.int32, (nrow, DA_DV), 0)
    laneq = lax.broadcasted_iota(jnp.int32, (nrow, DA_DV), 1)
    own_comp = (laneq // DA_DQK) == (rowq // DA_HEADS)
    qm = jnp.where(own_comp, twice(q_ref[0]) * (DA_DQK ** -0.5), 0.0)
    page_rows = k_refs[0].shape[1]
    rowp = lax.broadcasted_iota(jnp.int32, (nrow, page_rows * pps), 0)
    colp = lax.broadcasted_iota(jnp.int32, (nrow, page_rows * pps), 1)
    own_head = (colp % DA_HEADS) == (rowp % DA_HEADS)

    def rows3(x):
        hi = x.astype(BF16).astype(F32)
        mid = (x - hi).astype(BF16).astype(F32)
        return jnp.concatenate([hi, mid, x - hi - mid], axis=0).astype(BF16)

    def unrows3(r):
        return (r[0:nrow] + r[nrow:2 * nrow]) + r[2 * nrow:3 * nrow]

    def rows2_fp8(x):
        a = x.astype(FP8).astype(F32)
        return jnp.concatenate([a, x - a], axis=0).astype(FP8)

    def unrows2(r):
        return (r[0:nrow] + r[nrow:2 * nrow]) * (1.0 / LO_SCALE)

    def hi_lo(x):
        hi = x.astype(BF16)
        lo = jnp.clip((x - hi.astype(F32)) * LO_SCALE, -FP8_MAX, FP8_MAX)
        return hi, lo.astype(FP8)

    def mm8(a, b, dims):
        return lax.dot_general(a, b, dims, preferred_element_type=F32)

    @pl.when(step == 0)
    def _():
        s_new = jnp.sum(qm * twice(kn_ref[0]), axis=-1, keepdims=True)
        m_scr[...] = s_new
        l_scr[...] = jnp.ones_like(s_new)
        acc_scr[...] = twice(vn_ref[0])

    q3, q8 = rows3(qm), rows2_fp8(qm)
    scores = []
    for r in k_refs:
        k_hi, k_lo = hi_lo(r[0])
        scores.append(unrows3(_dot_nt(q3, k_hi)) + unrows2(mm8(q8, k_lo, _NT)))
    s = jnp.where(own_head, jnp.concatenate(scores, axis=1), -jnp.inf)
    m_old = m_scr[...]
    m_new = jnp.maximum(m_old, jnp.max(s, axis=-1, keepdims=True))
    p = jnp.exp(s - m_new)
    alpha = jnp.exp(m_old - m_new)
    p3, p8 = rows3(p), rows2_fp8(p)
    pv_hi = jnp.zeros((3 * nrow, DA_DV), F32)
    pv_lo = jnp.zeros((2 * nrow, DA_DV), F32)
    for j, r in enumerate(v_refs):
        v_hi, v_lo = hi_lo(r[0])
        cols = slice(j * page_rows, (j + 1) * page_rows)
        pv_hi = pv_hi + jnp.dot(p3[:, cols], v_hi, preferred_element_type=F32)
        pv_lo = pv_lo + mm8(p8[:, cols], v_lo, _NN)
    m_scr[...] = m_new
    l_scr[...] = alpha * l_scr[...] + jnp.sum(p, axis=-1, keepdims=True)
    acc_scr[...] = alpha * acc_scr[...] + (unrows3(pv_hi) + unrows2(pv_lo))

    @pl.when(step == pl.num_programs(1) - 1)
    def _():
        w = acc_scr[...] / l_scr[...]
        o = w[0:DA_HEADS, :] - _lambda(lam_ref) * w[DA_HEADS:nrow, :]
        o_ref[0] = _rms(o, wsub_ref[...]) * (1.0 - LAM_INIT)


def _attn_sample(q, k_new, v_new, cache_k, cache_v, page_table, lam_vecs, wsub, pps):
    bs = q.shape[0]
    n_pages = page_table.shape[1]
    page_rows = cache_k.shape[1]
    assert n_pages % pps == 0
    tok = lambda i, s, pt: (i, 0, 0)
    fixed = lambda i, s, pt: (0, 0)

    def page_spec(j):
        return pl.BlockSpec((1, page_rows, DA_DV), lambda i, s, pt: (pt[i, s * pps + j], 0, 0))

    grid_spec = pltpu.PrefetchScalarGridSpec(
        num_scalar_prefetch=1,
        grid=(bs, n_pages // pps),
        in_specs=[pl.BlockSpec((1, DA_HEADS, DA_DV), tok),
                  pl.BlockSpec((1, DA_HEADS, DA_DV), tok),
                  pl.BlockSpec((1, DA_HEADS, DA_DV), tok),
                  pl.BlockSpec((4, DA_DQK), fixed),
                  pl.BlockSpec((1, DA_DV), fixed)]
                 + [page_spec(j) for j in range(pps)] * 2,
        out_specs=pl.BlockSpec((1, DA_HEADS, DA_DV), tok),
        scratch_shapes=[pltpu.VMEM((2 * DA_HEADS, 1), F32),
                        pltpu.VMEM((2 * DA_HEADS, 1), F32),
                        pltpu.VMEM((2 * DA_HEADS, DA_DV), F32)])
    return pl.pallas_call(
        functools.partial(_attn_sample_kernel, pps=pps),
        grid_spec=grid_spec,
        out_shape=jax.ShapeDtypeStruct((bs, DA_HEADS, DA_DV), F32),
        compiler_params=pltpu.CompilerParams(dimension_semantics=("parallel", "arbitrary"),
                                             vmem_limit_bytes=VMEM_LIMIT),
        name="attn_sample",
    )(page_table, q, k_new, v_new, lam_vecs, wsub, *([cache_k] * pps), *([cache_v] * pps))


def _route(logits):
    lane = lax.broadcasted_iota(jnp.int32, logits.shape, 1).astype(F32)
    big = float(LANES)
    ninf = -jnp.inf

    def top(mask):
        val = jnp.where(mask, logits, ninf)
        mx = jnp.max(val, axis=-1, keepdims=True)
        idx = jnp.min(jnp.where(mask & (val == mx), lane, big), axis=-1, keepdims=True)
        return mx, idx

    gmask = (lane >= N_EXPERTS) & (lane < N_EXPERTS + N_GROUPS)
    gmax, gidx = top(gmask)
    g_w = 1.0 / jnp.sum(jnp.where(gmask, jnp.exp(logits - gmax), 0.0), axis=-1, keepdims=True)
    g_sel = gidx - N_EXPERTS
    emask = (lane >= g_sel * EXPERTS_PER_GROUP) & (lane < (g_sel + 1) * EXPERTS_PER_GROUP)
    v1, i1 = top(emask)
    v2, i2 = top(emask & (lane != i1))
    e21 = jnp.exp(v2 - v1)
    p1 = 1.0 / (1.0 + e21)
    return jnp.where(lane == i1, p1 * g_w, 0.0) + jnp.where(lane == i2, e21 * p1 * g_w, 0.0), g_sel


def _mix_route(g_ref, a_ref, x_ref, wo_ref, wnf_ref, wr_ref, precise):
    mix = (_mm(g_ref[...], wo_ref[0:GDN_V_W, :], _NN, precise)
           + _mm(a_ref[...], wo_ref[GDN_V_W:, :], _NN, precise))
    x1 = x_ref[...] + mix
    xn = _rms(x1, wnf_ref[...])
    wr = wr_ref[...]
    if precise:
        logits = _dot_f32(xn, wr)
    else:
        x_hi, w_hi = xn.astype(BF16), wr.astype(BF16)
        x_lo, w_lo = (xn - x_hi.astype(F32)).astype(BF16), (wr - w_hi.astype(F32)).astype(BF16)
        both = jnp.dot(x_hi, jnp.concatenate([w_hi, w_lo], axis=1), preferred_element_type=F32)
        logits = (both[:, 0:LANES] + both[:, LANES:]) + jnp.dot(x_lo, w_hi,
                                                                 preferred_element_type=F32)
    comb, g_sel = _route(logits)
    return x1, xn, comb, g_sel


def _expert(xn, comb, e_lane, wg, wu, wd):
    hg = jnp.dot(xn, wg.astype(BF16), preferred_element_type=F32)
    hu = jnp.dot(xn, wu.astype(BF16), preferred_element_type=F32)
    lane = lax.broadcasted_iota(jnp.int32, comb.shape, 1)
    cw = jnp.sum(jnp.where(lane == e_lane, comb, 0.0), axis=-1, keepdims=True)
    hh = (hg * _sigmoid(hg)) * hu * cw
    return jnp.dot(hh.astype(BF16), wd.astype(BF16), preferred_element_type=F32)


def _ffn_kernel(g_ref, a_ref, x_ref, wo_ref, wnf_ref, wr_ref, wg_ref, wu_ref, wd_ref, wfin_ref,
                y_ref, x1_scr, xn_scr, comb_scr, acc_scr, *, precise):
    e = pl.program_id(1)

    @pl.when(e == 0)
    def _():
        x1, xn, comb, _ = _mix_route(g_ref, a_ref, x_ref, wo_ref, wnf_ref, wr_ref, precise)
        x1_scr[...] = x1
        xn_scr[...] = xn.astype(BF16)
        comb_scr[...] = comb
        acc_scr[...] = jnp.zeros_like(acc_scr)

    acc_scr[...] += _expert(xn_scr[...], comb_scr[...], e, wg_ref[0], wu_ref[0], wd_ref[0])

    @pl.when(e == pl.num_programs(1) - 1)
    def _():
        y_ref[...] = _rms(x1_scr[...] + acc_scr[...], wfin_ref[...])


def _ffn(g_out, a_out, x, wo, wnf, wr, wg, wu, wd, wfin, tm):
    n = x.shape[0]
    assert n % tm == 0
    row = lambda i, e: (i, 0)
    fixed = lambda i, e: (0, 0)
    exp = lambda i, e: (e, 0, 0)
    return pl.pallas_call(
        functools.partial(_ffn_kernel, precise=wo.dtype == F32),
        grid=(n // tm, N_EXPERTS),
        in_specs=[pl.BlockSpec((tm, GDN_V_W), row),
                  pl.BlockSpec((tm, DA_V_W), row),
                  pl.BlockSpec((tm, D_MODEL), row),
                  pl.BlockSpec((GDN_V_W + DA_V_W, D_MODEL), fixed),
                  pl.BlockSpec((1, D_MODEL), fixed),
                  pl.BlockSpec((D_MODEL, LANES), fixed),
                  pl.BlockSpec((1, D_MODEL, D_EXPERT), exp),
                  pl.BlockSpec((1, D_MODEL, D_EXPERT), exp),
                  pl.BlockSpec((1, D_EXPERT, D_MODEL), exp),
                  pl.BlockSpec((1, D_MODEL), fixed)],
        out_specs=pl.BlockSpec((tm, D_MODEL), row),
        out_shape=jax.ShapeDtypeStruct((n, D_MODEL), F32),
        scratch_shapes=[pltpu.VMEM((tm, D_MODEL), F32),
                        pltpu.VMEM((tm, D_MODEL), BF16),
                        pltpu.VMEM((tm, LANES), F32),
                        pltpu.VMEM((tm, D_MODEL), F32)],
        compiler_params=pltpu.CompilerParams(dimension_semantics=("parallel", "arbitrary"),
                                             vmem_limit_bytes=VMEM_LIMIT),
        name="ffn",
    )(g_out, a_out, x, wo, wnf, wr, wg, wu, wd, wfin)


ROW_W = D_MODEL + LANES
MOE_TILE = 512
ROW_TILE = 2048


def _ffn_pre_kernel(g_ref, a_ref, x_ref, wo_ref, wnf_ref, wr_ref, rows_ref, info_ref, cnt_ref,
                    cnt_scr):
    @pl.when(pl.program_id(0) == 0)
    def _():
        cnt_scr[...] = jnp.zeros_like(cnt_scr)

    x1, _, comb, g_sel = _mix_route(g_ref, a_ref, x_ref, wo_ref, wnf_ref, wr_ref, False)
    tm = x1.shape[0]
    lane = lax.broadcasted_iota(jnp.int32, (tm, LANES), 1).astype(F32)
    onehot = lane == g_sel
    ri = lax.broadcasted_iota(jnp.int32, (tm, tm), 0)
    ci = lax.broadcasted_iota(jnp.int32, (tm, tm), 1)
    earlier = jnp.dot((ci < ri).astype(BF16), onehot.astype(BF16), preferred_element_type=F32)
    rank = jnp.sum(jnp.where(onehot, earlier + cnt_scr[...], 0.0), axis=-1, keepdims=True)
    cnt_scr[...] += jnp.sum(onehot.astype(F32), axis=0, keepdims=True)
    cnt_ref[...] = cnt_scr[...]
    rows_ref[:, 0:D_MODEL] = x1
    rows_ref[:, D_MODEL:] = comb
    packed = jnp.where(lane == 0.0, g_sel, jnp.where(lane == 1.0, rank, 0.0))
    r8 = lax.broadcasted_iota(jnp.int32, (8, LANES), 0)
    l8 = lax.broadcasted_iota(jnp.int32, (8, LANES), 1)
    sel = (r8 == l8).astype(BF16)
    hi = packed.astype(BF16)
    info_ref[...] = _dot_nt(sel, hi) + _dot_nt(sel, packed - hi.astype(F32))


def _ffn_pre(g_out, a_out, x, wo, wnf, wr, tm):
    n = x.shape[0]
    assert n % tm == 0
    row = lambda i: (i, 0)
    fixed = lambda i: (0, 0)
    return pl.pallas_call(
        _ffn_pre_kernel,
        grid=(n // tm,),
        in_specs=[pl.BlockSpec((tm, GDN_V_W), row),
                  pl.BlockSpec((tm, DA_V_W), row),
                  pl.BlockSpec((tm, D_MODEL), row),
                  pl.BlockSpec((GDN_V_W + DA_V_W, D_MODEL), fixed),
                  pl.BlockSpec((1, D_MODEL), fixed),
                  pl.BlockSpec((D_MODEL, LANES), fixed)],
        out_specs=[pl.BlockSpec((tm, ROW_W), row),
                   pl.BlockSpec((8, tm), lambda i: (0, i)),
                   pl.BlockSpec((1, LANES), fixed)],
        out_shape=[jax.ShapeDtypeStruct((n, ROW_W), F32),
                   jax.ShapeDtypeStruct((8, n), F32),
                   jax.ShapeDtypeStruct((1, LANES), F32)],
        scratch_shapes=[pltpu.VMEM((1, LANES), F32)],
        compiler_params=pltpu.CompilerParams(dimension_semantics=("arbitrary",),
                                             vmem_limit_bytes=VMEM_LIMIT),
        name="ffn_pre",
    )(g_out, a_out, x, wo, wnf, wr)


def _copy_rows(idx_ref, tile_ref, hbm_ref, sem, scatter):
    tm = tile_ref.shape[0]
    base = pl.program_id(0) * tm

    def body(r, carry):
        there, here = hbm_ref.at[pl.ds(idx_ref[base + r], 1)], tile_ref.at[pl.ds(r, 1)]
        src, dst = (here, there) if scatter else (there, here)
        pltpu.make_async_copy(src, dst, sem).start()
        return carry

    lax.fori_loop(0, tm, body, 0, unroll=8)
    there, here = hbm_ref.at[pl.ds(0, tm)], tile_ref
    src, dst = (here, there) if scatter else (there, here)
    pltpu.make_async_copy(src, dst, sem).wait()


def _scatter_kernel(idx_ref, tile_ref, init_ref, dst_ref, sem):
    del init_ref
    _copy_rows(idx_ref, tile_ref, dst_ref, sem, True)


def _gather_kernel(idx_ref, src_ref, tile_ref, sem):
    _copy_rows(idx_ref, tile_ref, src_ref, sem, False)


def _move_rows(idx, src, n_out, scatter):
    n, w = idx.shape[0], src.shape[1]
    assert n % ROW_TILE == 0
    any_spec = pl.BlockSpec(memory_space=pl.ANY)
    tile_spec = pl.BlockSpec((ROW_TILE, w), lambda i, idx: (i, 0))
    if scatter:
        operands = (idx, src, jnp.zeros((n_out, w), src.dtype))
        in_specs, out_spec = [tile_spec, any_spec], any_spec
    else:
        operands = (idx, src)
        in_specs, out_spec = [any_spec], tile_spec
    return pl.pallas_call(
        _scatter_kernel if scatter else _gather_kernel,
        grid_spec=pltpu.PrefetchScalarGridSpec(
            num_scalar_prefetch=1, grid=(n // ROW_TILE,),
            in_specs=in_specs, out_specs=out_spec,
            scratch_shapes=[pltpu.SemaphoreType.DMA(())]),
        out_shape=jax.ShapeDtypeStruct((n_out, w), src.dtype),
        input_output_aliases={2: 0} if scatter else {},
        compiler_params=pltpu.CompilerParams(dimension_semantics=("arbitrary",)),
        name="scatter_rows" if scatter else "gather_rows",
    )(*operands)


def _moe_groups_kernel(tg_ref, nt_ref, rows_ref, wnf_ref, wg_ref, wu_ref, wd_ref, wfin_ref, y_ref,
                       wg_bf, wu_bf, wd_bf):
    j = pl.program_id(0)

    @pl.when((j == 0) | (tg_ref[j] != tg_ref[jnp.maximum(j - 1, 0)]))
    def _():
        for e in range(EXPERTS_PER_GROUP):
            wg_bf[e] = wg_ref[e].astype(BF16)
            wu_bf[e] = wu_ref[e].astype(BF16)
            wd_bf[e] = wd_ref[e].astype(BF16)

    @pl.when(j < nt_ref[0])
    def _():
        x1 = rows_ref[:, 0:D_MODEL]
        comb = rows_ref[:, D_MODEL:]
        xn = _rms(x1, wnf_ref[...]).astype(BF16)
        first = tg_ref[j] * EXPERTS_PER_GROUP
        acc = jnp.zeros_like(x1)
        for e in range(EXPERTS_PER_GROUP):
            acc = acc + _expert(xn, comb, first + e, wg_bf[e], wu_bf[e], wd_bf[e])
        y_ref[...] = _rms(x1 + acc, wfin_ref[...])

    @pl.when(j >= nt_ref[0])
    def _():
        y_ref[...] = jnp.zeros_like(y_ref)


def _moe_groups(tile_group, n_tiles_used, rows, wnf, wg, wu, wd, wfin):
    n_tiles = tile_group.shape[0]
    fixed = lambda j, tg, nt: (0, 0)
    grp = lambda j, tg, nt: (tg[j], 0, 0)
    in_shape, out_shape = (EXPERTS_PER_GROUP, D_MODEL, D_EXPERT), (EXPERTS_PER_GROUP, D_EXPERT, D_MODEL)
    once = pl.Buffered(1)
    return pl.pallas_call(
        _moe_groups_kernel,
        grid_spec=pltpu.PrefetchScalarGridSpec(
            num_scalar_prefetch=2, grid=(n_tiles,),
            in_specs=[pl.BlockSpec((MOE_TILE, ROW_W), lambda j, tg, nt: (j, 0)),
                      pl.BlockSpec((1, D_MODEL), fixed),
                      pl.BlockSpec(in_shape, grp, pipeline_mode=once),
                      pl.BlockSpec(in_shape, grp, pipeline_mode=once),
                      pl.BlockSpec(out_shape, grp, pipeline_mode=once),
                      pl.BlockSpec((1, D_MODEL), fixed)],
            out_specs=pl.BlockSpec((MOE_TILE, D_MODEL), lambda j, tg, nt: (j, 0)),
            scratch_shapes=[pltpu.VMEM(in_shape, BF16), pltpu.VMEM(in_shape, BF16),
                            pltpu.VMEM(out_shape, BF16)]),
        out_shape=jax.ShapeDtypeStruct((n_tiles * MOE_TILE, D_MODEL), F32),
        compiler_params=pltpu.CompilerParams(dimension_semantics=("arbitrary",),
                                             vmem_limit_bytes=VMEM_LIMIT),
        name="moe_groups",
    )(tile_group, n_tiles_used, rows, wnf, wg, wu, wd, wfin)


def _ffn_sorted(g_out, a_out, x, wo, wnf, wr, wg, wu, wd, wfin):
    n = x.shape[0]
    rows, info, cnt = _ffn_pre(g_out, a_out, x, wo, wnf, wr, FFN_PRE_TILE)
    grp, rank = info[0].astype(jnp.int32), info[1].astype(jnp.int32)
    tiles = (cnt[0, :N_GROUPS].astype(jnp.int32) + MOE_TILE - 1) // MOE_TILE
    ends = jnp.cumsum(tiles)
    dest = (ends - tiles)[grp] * MOE_TILE + rank
    n_tiles = n // MOE_TILE + N_GROUPS
    tile_group = jnp.minimum(
        jnp.sum(jnp.arange(n_tiles, dtype=jnp.int32)[:, None] >= ends[None, :], axis=1),
        N_GROUPS - 1).astype(jnp.int32)
    rows_sorted = _move_rows(dest, rows, n_tiles * MOE_TILE, True)
    y_sorted = _moe_groups(tile_group, ends[N_GROUPS - 1:], rows_sorted, wnf, wg, wu, wd, wfin)
    return _move_rows(dest, y_sorted, n, False)


def kernel(x_prompt, x_sample, cache_k, cache_v, state_conv, state_gdn, page_table, meta_tokens,
           w_norm_mix, w_in, w_conv, a_log, dt_bias, w_gnorm, lambda_q1, lambda_k1, lambda_q2,
           lambda_k2, w_subln, w_out, w_norm_ffn, w_group, w_expert, w_gate, w_up, w_down, w_final):
    l = 0
    bp, seq = x_prompt.shape[0], x_prompt.shape[1]
    bs = x_sample.shape[0]
    n_pool, page = cache_k.shape[1], cache_k.shape[2]
    past = page_table.shape[1] * page
    C = GDN_CHUNK

    w_in_p = _prep_w_in(w_in[l])
    wn_mix = w_norm_mix[l][None, :]
    pad_lanes = lambda v: jnp.pad(v, (0, LANES - v.shape[0]))[None, :]
    alog, dtb = pad_lanes(a_log[l]), pad_lanes(dt_bias[l])
    wgn = w_gnorm[l][None, :]
    lam_vecs = jnp.stack([lambda_q1[l], lambda_k1[l], lambda_q2[l], lambda_k2[l]])
    wsub = w_subln[l][None, :]
    w_router = jnp.pad(jnp.concatenate([w_expert[l], w_group[l]], axis=1),
                       ((0, 0), (0, LANES - N_EXPERTS - N_GROUPS)))
    ffn_w = (w_norm_ffn[l][None, :], w_router, w_gate[l], w_up[l], w_down[l], w_final[None, :])

    pos_small = jnp.concatenate([jnp.arange(N_META), jnp.full((bs,), past)])
    rows_small = jnp.concatenate([meta_tokens, x_sample[:, 0, :]], axis=0)
    c_s, z_s, ab_s, q_s, k_s, v_s = _proj(rows_small, wn_mix, w_in_p, _rope_tables(pos_small),
                                           N_META + bs)
    xp = x_prompt.reshape(bp * seq, D_MODEL)
    c_p, z_p, ab_p, q_p, k_p, v_p = _proj(xp, wn_mix, w_in_p.astype(BF16),
                                           _rope_tables(N_META + jnp.arange(seq)), PROJ_TILE)

    front = lambda a: jnp.pad(a[:N_META], ((C - N_META, 0), (0, 0)))[None]
    zero_tail = jnp.zeros((1, 8, CONV_CH), F32)
    zero_state = jnp.zeros((1, GDN_HEADS, GDN_DK, GDN_DV), F32)
    _, s_meta = _gdn(front(c_s), front(z_s), front(ab_s), zero_tail, zero_state,
                     w_conv[l], alog, dtb, wgn, C)
    tail_meta = jnp.pad(c_s[N_META - (CONV_W - 1):N_META], ((8 - (CONV_W - 1), 0), (0, 0)))[None]
    c_p3 = c_p.reshape(bp, seq, CONV_CH)
    g_out, p_gdn = _gdn(c_p3, z_p.reshape(bp, seq, GDN_V_W), ab_p.reshape(bp, seq, AB_W),
                        tail_meta, s_meta, w_conv[l], alog, dtb, wgn, GDN_TILE)
    p_conv = c_p3[:, seq - (CONV_W - 1):, :]

    p_k, p_v = _kv_out(k_p.reshape(bp, seq, DA_QK_W), v_p.reshape(bp, seq, DA_V_W),
                       k_s[:N_META], v_s[:N_META])
    meta_blk = lambda m: jnp.pad(m[:N_META], ((0, LANES - N_META), (0, 0)))
    a_out = _attn_prompt(q_p.reshape(bp, seq, DA_QK_W), k_p.reshape(bp, seq, DA_QK_W),
                         v_p.reshape(bp, seq, DA_V_W), meta_blk(k_s), meta_blk(v_s),
                         lam_vecs, wsub, ATTN_BQ, ATTN_BK, ATTN_HEADS_PER_STEP)

    y_prompt = _ffn_sorted(g_out.reshape(bp * seq, GDN_V_W), a_out.reshape(bp * seq, DA_V_W), xp,
                           w_out[l].astype(BF16), *ffn_w).reshape(bp, seq, D_MODEL)

    sm = lambda a: a[N_META:]
    g_s, s_gdn = _gdn_step(sm(c_s), sm(z_s), sm(ab_s), jnp.swapaxes(state_conv[l], 0, 1),
                           state_gdn[l], w_conv[l], alog, dtb, wgn)
    s_conv = jnp.concatenate([state_conv[l][:, 1:, :], sm(c_s)[:, None, :]], axis=1)
    heads = lambda a: sm(a).reshape(bs, DA_HEADS, DA_DV)
    a_s = _attn_sample(heads(q_s), heads(k_s), heads(v_s),
                       cache_k[l].reshape(n_pool, page * DA_HEADS, 2 * DA_DQK),
                       cache_v[l].reshape(n_pool, page * DA_HEADS, DA_DV),
                       page_table, lam_vecs, wsub, PAGES_PER_STEP)
    y_sample = _ffn(g_s, a_s.reshape(bs, DA_V_W), x_sample[:, 0, :], w_out[l], *ffn_w,
                    bs).reshape(bs, 1, D_MODEL)

    return (y_prompt, y_sample, p_conv[None], p_gdn[None],
            p_k.reshape(1, bp, seq + N_META, DA_HEADS, 2 * DA_DQK),
            p_v.reshape(1, bp, seq + N_META, DA_HEADS, DA_DV),
            s_conv[None], s_gdn[None],
            sm(k_s).reshape(1, bs, 1, DA_HEADS, 2 * DA_DQK),
            sm(v_s).reshape(1, bs, 1, DA_HEADS, DA_DV))
```
